```python
import jax, jax.numpy as jnp
from jax import lax
import numpy as np

D_MODEL = 1024
BATCH = 8
SEQ = 16384
DEPTH = 4

CHUNK = 64
N_MIXERS = 2
N_A = (DEPTH + 1) // 2
N_B = DEPTH // 2
CONV_WIDTH = 31
CONV_INNER = D_MODEL
POOL_INNER = D_MODEL
POOL_WINDOWS = (2, 4, 8, 16)
POOL_GROUPS = len(POOL_WINDOWS)
POOL_GC = POOL_INNER // POOL_GROUPS
RMS_EPS = 1e-6
LN_EPS = 1e-5

kernel_name = "hybrid_conformer_conv_multiscale_pool_trunk"


def rmsnorm(x, g):
    xf = x.astype(jnp.float32)
    y = xf * lax.rsqrt(jnp.mean(xf * xf, axis=-1, keepdims=True) + RMS_EPS)
    return (y * g.astype(jnp.float32)).astype(x.dtype)


def layernorm(x, g, b):
    xf = x.astype(jnp.float32)
    mu = jnp.mean(xf, axis=-1, keepdims=True)
    var = jnp.mean(jnp.square(xf - mu), axis=-1, keepdims=True)
    y = (xf - mu) * lax.rsqrt(var + LN_EPS)
    return (y * g.astype(jnp.float32) + b.astype(jnp.float32)).astype(x.dtype)


def conformer_conv_branch(h, w_in, dw, dw_b, ln_g, ln_b, w_out):
    p = jnp.einsum('bsd,de->bse', h, w_in)
    a, b, z = jnp.split(p, 3, axis=-1)
    u = a * jax.nn.sigmoid(b)
    u = lax.conv_general_dilated(
        u, dw[:, None, :].astype(u.dtype), window_strides=(1,),
        padding=[(CONV_WIDTH - 1, 0)],
        dimension_numbers=('NWC', 'WIO', 'NWC'),
        feature_group_count=CONV_INNER) + dw_b
    u = layernorm(u, ln_g, ln_b)
    u = jax.nn.silu(u) * jax.nn.silu(z)
    return jnp.einsum('bse,ed->bsd', u, w_out)


def multiscale_pool_branch(h, w_in, w_grp, b_grp, scale, w_out):
    p = jnp.einsum('bsd,de->bse', h, w_in)
    u, z = jnp.split(p, 2, axis=-1)
    S = u.shape[1]
    uf = u.astype(jnp.float32)
    cs = jnp.pad(jnp.cumsum(uf, axis=1), ((0, 0), (1, 0), (0, 0)))
    pos = jnp.arange(S, dtype=jnp.int32) + 1
    outs = []
    for g, w in enumerate(POOL_WINDOWS):
        sl = slice(g * POOL_GC, (g + 1) * POOL_GC)
        csg = cs[:, :, sl]
        upper = csg[:, 1:]
        lower = jnp.pad(csg[:, :S + 1 - w], ((0, 0), (w - 1, 0), (0, 0)))
        count = jnp.minimum(pos, w).astype(jnp.float32)[None, :, None]
        d = (upper - lower) / count - uf[:, :, sl]
        outs.append(jnp.einsum('bsc,cd->bsd', d.astype(u.dtype), w_grp[g]))
    y = (jnp.concatenate(outs, axis=-1) + b_grp) * scale
    y = y * jax.nn.silu(z)
    return jnp.einsum('bse,ed->bsd', y, w_out)


def _fwd_setup_inputs(seed: int = 0) -> dict:
    key = jax.random.key(seed)
    ks = jax.random.split(key, 16)
    D, EA, EB = D_MODEL, CONV_INNER, POOL_INNER
    nrm = jax.random.normal
    f32 = jnp.float32
    return {
        "x": nrm(ks[0], (BATCH, SEQ, D), f32),
        "norm_g": 1.0 + 0.05 * nrm(ks[1], (DEPTH, D), f32),
        "final_g": 1.0 + 0.05 * nrm(ks[2], (D,), f32),
        "conv_w_in": nrm(ks[3], (N_A, D, 3 * EA), f32) * D ** -0.5,
        "conv_dw": nrm(ks[4], (N_A, CONV_WIDTH, EA), f32) * CONV_WIDTH ** -0.5,
        "conv_dw_b": 0.02 * nrm(ks[5], (N_A, EA), f32),
        "conv_ln_g": 1.0 + 0.05 * nrm(ks[6], (N_A, EA), f32),
        "conv_ln_b": 0.02 * nrm(ks[7], (N_A, EA), f32),
        "conv_w_out": nrm(ks[8], (N_A, EA, D), f32) * EA ** -0.5,
        "pool_w_in": nrm(ks[9], (N_B, D, 2 * EB), f32) * D ** -0.5,
        "pool_w_grp": nrm(ks[10], (N_B, POOL_GROUPS, POOL_GC, POOL_GC), f32) * POOL_GC ** -0.5,
        "pool_b_grp": 0.02 * nrm(ks[11], (N_B, EB), f32),
        "pool_scale": 1.0 + 0.1 * nrm(ks[12], (N_B, EB), f32),
        "pool_w_out": nrm(ks[13], (N_B, EB, D), f32) * EB ** -0.5,
    }


def _fwd_reference(x, norm_g, final_g, conv_w_in, conv_dw, conv_dw_b, conv_ln_g,
              conv_ln_b, conv_w_out, pool_w_in, pool_w_grp, pool_b_grp,
              pool_scale, pool_w_out):
    h = x
    for i in range(DEPTH):
        hn = rmsnorm(h, norm_g[i])
        j = i // N_MIXERS
        if i % N_MIXERS == 0:
            y = conformer_conv_branch(hn, conv_w_in[j], conv_dw[j], conv_dw_b[j],
                                      conv_ln_g[j], conv_ln_b[j], conv_w_out[j])
        else:
            y = multiscale_pool_branch(hn, pool_w_in[j], pool_w_grp[j], pool_b_grp[j],
                                       pool_scale[j], pool_w_out[j])
        h = h + y
    return rmsnorm(h, final_g)


import jax as _jax
import jax.numpy as _jnp

TWIN_FORMAT = 'train_step'
FWD_PARAMS = ['x', 'norm_g', 'final_g', 'conv_w_in', 'conv_dw', 'conv_dw_b', 'conv_ln_g', 'conv_ln_b', 'conv_w_out', 'pool_w_in', 'pool_w_grp', 'pool_b_grp', 'pool_scale', 'pool_w_out']
TWIN_WEIGHTS = ['norm_g', 'final_g', 'conv_w_in', 'conv_dw', 'conv_dw_b', 'conv_ln_g', 'conv_ln_b', 'conv_w_out', 'pool_w_in', 'pool_w_grp', 'pool_b_grp', 'pool_scale', 'pool_w_out']
TWIN_DIFF_INPUT = 'x'
TWIN_INPUTS = ['x', 'norm_g', 'final_g', 'conv_w_in', 'conv_dw', 'conv_dw_b', 'conv_ln_g', 'conv_ln_b', 'conv_w_out', 'pool_w_in', 'pool_w_grp', 'pool_b_grp', 'pool_scale', 'pool_w_out', 'loss_target', 'm_norm_g', 'm_final_g', 'm_conv_w_in', 'm_conv_dw', 'm_conv_dw_b', 'm_conv_ln_g', 'm_conv_ln_b', 'm_conv_w_out', 'm_pool_w_in', 'm_pool_w_grp', 'm_pool_b_grp', 'm_pool_scale', 'm_pool_w_out', 'v_norm_g', 'v_final_g', 'v_conv_w_in', 'v_conv_dw', 'v_conv_dw_b', 'v_conv_ln_g', 'v_conv_ln_b', 'v_conv_w_out', 'v_pool_w_in', 'v_pool_w_grp', 'v_pool_b_grp', 'v_pool_scale', 'v_pool_w_out']
TWIN_OUTPUTS = ['loss', 'grad_x', 'grad_norm_g', 'grad_final_g', 'grad_conv_w_in', 'grad_conv_dw', 'grad_conv_dw_b', 'grad_conv_ln_g', 'grad_conv_ln_b', 'grad_conv_w_out', 'grad_pool_w_in', 'grad_pool_w_grp', 'grad_pool_b_grp', 'grad_pool_scale', 'grad_pool_w_out', 'delta_norm_g', 'delta_final_g', 'delta_conv_w_in', 'delta_conv_dw', 'delta_conv_dw_b', 'delta_conv_ln_g', 'delta_conv_ln_b', 'delta_conv_w_out', 'delta_pool_w_in', 'delta_pool_w_grp', 'delta_pool_b_grp', 'delta_pool_scale', 'delta_pool_w_out', 'new_m_norm_g', 'new_m_final_g', 'new_m_conv_w_in', 'new_m_conv_dw', 'new_m_conv_dw_b', 'new_m_conv_ln_g', 'new_m_conv_ln_b', 'new_m_conv_w_out', 'new_m_pool_w_in', 'new_m_pool_w_grp', 'new_m_pool_b_grp', 'new_m_pool_scale', 'new_m_pool_w_out', 'new_v_norm_g', 'new_v_final_g', 'new_v_conv_w_in', 'new_v_conv_dw', 'new_v_conv_dw_b', 'new_v_conv_ln_g', 'new_v_conv_ln_b', 'new_v_conv_w_out', 'new_v_pool_w_in', 'new_v_pool_w_grp', 'new_v_pool_b_grp', 'new_v_pool_scale', 'new_v_pool_w_out']
TWIN_LEAF_KINDS = {'loss': 'loss', 'grad_x': 'grad_x', 'grad_norm_g': 'grad_w', 'grad_final_g': 'grad_w', 'grad_conv_w_in': 'grad_w', 'grad_conv_dw': 'grad_w', 'grad_conv_dw_b': 'grad_w', 'grad_conv_ln_g': 'grad_w', 'grad_conv_ln_b': 'grad_w', 'grad_conv_w_out': 'grad_w', 'grad_pool_w_in': 'grad_w', 'grad_pool_w_grp': 'grad_w', 'grad_pool_b_grp': 'grad_w', 'grad_pool_scale': 'grad_w', 'grad_pool_w_out': 'grad_w', 'delta_norm_g': 'delta_w', 'delta_final_g': 'delta_w', 'delta_conv_w_in': 'delta_w', 'delta_conv_dw': 'delta_w', 'delta_conv_dw_b': 'delta_w', 'delta_conv_ln_g': 'delta_w', 'delta_conv_ln_b': 'delta_w', 'delta_conv_w_out': 'delta_w', 'delta_pool_w_in': 'delta_w', 'delta_pool_w_grp': 'delta_w', 'delta_pool_b_grp': 'delta_w', 'delta_pool_scale': 'delta_w', 'delta_pool_w_out': 'delta_w', 'new_m_norm_g': 'new_m', 'new_m_final_g': 'new_m', 'new_m_conv_w_in': 'new_m', 'new_m_conv_dw': 'new_m', 'new_m_conv_dw_b': 'new_m', 'new_m_conv_ln_g': 'new_m', 'new_m_conv_ln_b': 'new_m', 'new_m_conv_w_out': 'new_m', 'new_m_pool_w_in': 'new_m', 'new_m_pool_w_grp': 'new_m', 'new_m_pool_b_grp': 'new_m', 'new_m_pool_scale': 'new_m', 'new_m_pool_w_out': 'new_m', 'new_v_norm_g': 'new_v', 'new_v_final_g': 'new_v', 'new_v_conv_w_in': 'new_v', 'new_v_conv_dw': 'new_v', 'new_v_conv_dw_b': 'new_v', 'new_v_conv_ln_g': 'new_v', 'new_v_conv_ln_b': 'new_v', 'new_v_conv_w_out': 'new_v', 'new_v_pool_w_in': 'new_v', 'new_v_pool_w_grp': 'new_v', 'new_v_pool_b_grp': 'new_v', 'new_v_pool_scale': 'new_v', 'new_v_pool_w_out': 'new_v'}


def _forward(args):
    return _fwd_reference(*[args[k] for k in FWD_PARAMS])


def _output_shape():
    def fwd():
        inp = _fwd_setup_inputs(0)
        return _fwd_reference(*[inp[k] for k in FWD_PARAMS])
    out = _jax.eval_shape(fwd)
    return out.shape, out.dtype

N_MICROBATCH = 1
ADAM_LR = 0.001
ADAM_B1 = 0.9
ADAM_B2 = 0.999
ADAM_EPS = 1e-08
ADAM_WD = 0.01
ADAM_STEP = 10
PER_EXAMPLE_BATCH_AXIS = {'x': 0, 'loss_target': 0}
SHARED_INPUTS = []
_WEIGHT_DTYPES = {'norm_g': _jnp.float32, 'final_g': _jnp.float32, 'conv_w_in': _jnp.float32, 'conv_dw': _jnp.float32, 'conv_dw_b': _jnp.float32, 'conv_ln_g': _jnp.float32, 'conv_ln_b': _jnp.float32, 'conv_w_out': _jnp.float32, 'pool_w_in': _jnp.float32, 'pool_w_grp': _jnp.float32, 'pool_b_grp': _jnp.float32, 'pool_scale': _jnp.float32, 'pool_w_out': _jnp.float32}
MOMENT_SCALE = {'norm_g': 2.199919e-01, 'final_g': 1.279825e+02, 'conv_w_in': 1.158885e-01, 'conv_dw': 1.367257e-01, 'conv_dw_b': 2.874815e-01, 'conv_ln_g': 1.631779e-01, 'conv_ln_b': 1.365885e-01, 'conv_w_out': 1.327960e-01, 'pool_w_in': 1.643572e-01, 'pool_w_grp': 1.618405e-01, 'pool_b_grp': 2.006902e-01, 'pool_scale': 1.751634e-01, 'pool_w_out': 1.653354e-01}


def _to_microbatches(a, axis):
    t = _jnp.moveaxis(a, axis, 0)
    t = t.reshape((N_MICROBATCH, t.shape[0] // N_MICROBATCH) + t.shape[1:])
    return _jnp.moveaxis(t, 1, axis + 1)


def setup_inputs(seed: int = 0) -> dict:
    inp = _fwd_setup_inputs(seed)
    key = _jax.random.fold_in(_jax.random.key(seed), 7919)
    shape, _ = _output_shape()
    out = dict(inp)
    out["loss_target"] = _jax.random.normal(_jax.random.fold_in(key, 0), shape, _jnp.float32)
    for i, name in enumerate(TWIN_WEIGHTS):
        w = inp[name].astype(_jnp.float32)
        if MOMENT_SCALE is None:
            s = _jnp.sqrt(_jnp.mean(_jnp.square(w)) + 1e-30)
        else:
            s = MOMENT_SCALE[name]
        km, kv = _jax.random.split(_jax.random.fold_in(key, i + 1))
        out[name] = w
        out["m_" + name] = s * _jax.random.normal(km, w.shape, _jnp.float32)
        out["v_" + name] = (s * s) * _jax.random.uniform(kv, w.shape, _jnp.float32, 0.5, 1.5)
    if N_MICROBATCH > 1:
        for name, axis in PER_EXAMPLE_BATCH_AXIS.items():
            out[name] = _to_microbatches(out[name], axis)
    return {'x': out['x'], 'norm_g': out['norm_g'], 'final_g': out['final_g'], 'conv_w_in': out['conv_w_in'], 'conv_dw': out['conv_dw'], 'conv_dw_b': out['conv_dw_b'], 'conv_ln_g': out['conv_ln_g'], 'conv_ln_b': out['conv_ln_b'], 'conv_w_out': out['conv_w_out'], 'pool_w_in': out['pool_w_in'], 'pool_w_grp': out['pool_w_grp'], 'pool_b_grp': out['pool_b_grp'], 'pool_scale': out['pool_scale'], 'pool_w_out': out['pool_w_out'], 'loss_target': out['loss_target'], 'm_norm_g': out['m_norm_g'], 'm_final_g': out['m_final_g'], 'm_conv_w_in': out['m_conv_w_in'], 'm_conv_dw': out['m_conv_dw'], 'm_conv_dw_b': out['m_conv_dw_b'], 'm_conv_ln_g': out['m_conv_ln_g'], 'm_conv_ln_b': out['m_conv_ln_b'], 'm_conv_w_out': out['m_conv_w_out'], 'm_pool_w_in': out['m_pool_w_in'], 'm_pool_w_grp': out['m_pool_w_grp'], 'm_pool_b_grp': out['m_pool_b_grp'], 'm_pool_scale': out['m_pool_scale'], 'm_pool_w_out': out['m_pool_w_out'], 'v_norm_g': out['v_norm_g'], 'v_final_g': out['v_final_g'], 'v_conv_w_in': out['v_conv_w_in'], 'v_conv_dw': out['v_conv_dw'], 'v_conv_dw_b': out['v_conv_dw_b'], 'v_conv_ln_g': out['v_conv_ln_g'], 'v_conv_ln_b': out['v_conv_ln_b'], 'v_conv_w_out': out['v_conv_w_out'], 'v_pool_w_in': out['v_pool_w_in'], 'v_pool_w_grp': out['v_pool_w_grp'], 'v_pool_b_grp': out['v_pool_b_grp'], 'v_pool_scale': out['v_pool_scale'], 'v_pool_w_out': out['v_pool_w_out']}


def _loss(weights, diff, rest, loss_target):
    with _jax.named_scope("forward"):
        args = {**rest, TWIN_DIFF_INPUT: diff, **{k: w.astype(_WEIGHT_DTYPES[k]) for k, w in weights.items()}}
        y = _forward(args)
    with _jax.named_scope("loss_head"):
        err = _jnp.square(y.astype(_jnp.float32) - loss_target)
        return 0.5 * _jnp.sum(_jnp.mean(err, axis=-1)) if err.ndim else 0.5 * err


def _adamw(w, g, m, v):
    m = ADAM_B1 * m + (1.0 - ADAM_B1) * g
    v = ADAM_B2 * v + (1.0 - ADAM_B2) * _jnp.square(g)
    m_hat = m / (1.0 - ADAM_B1 ** ADAM_STEP)
    v_hat = v / (1.0 - ADAM_B2 ** ADAM_STEP)
    delta = -ADAM_LR * (m_hat / (_jnp.sqrt(v_hat) + ADAM_EPS) + ADAM_WD * w)
    return delta, m, v


def reference(x, norm_g, final_g, conv_w_in, conv_dw, conv_dw_b, conv_ln_g, conv_ln_b, conv_w_out, pool_w_in, pool_w_grp, pool_b_grp, pool_scale, pool_w_out, loss_target, m_norm_g, m_final_g, m_conv_w_in, m_conv_dw, m_conv_dw_b, m_conv_ln_g, m_conv_ln_b, m_conv_w_out, m_pool_w_in, m_pool_w_grp, m_pool_b_grp, m_pool_scale, m_pool_w_out, v_norm_g, v_final_g, v_conv_w_in, v_conv_dw, v_conv_dw_b, v_conv_ln_g, v_conv_ln_b, v_conv_w_out, v_pool_w_in, v_pool_w_grp, v_pool_b_grp, v_pool_scale, v_pool_w_out):
    given = dict(x=x, norm_g=norm_g, final_g=final_g, conv_w_in=conv_w_in, conv_dw=conv_dw, conv_dw_b=conv_dw_b, conv_ln_g=conv_ln_g, conv_ln_b=conv_ln_b, conv_w_out=conv_w_out, pool_w_in=pool_w_in, pool_w_grp=pool_w_grp, pool_b_grp=pool_b_grp, pool_scale=pool_scale, pool_w_out=pool_w_out, loss_target=loss_target, m_norm_g=m_norm_g, m_final_g=m_final_g, m_conv_w_in=m_conv_w_in, m_conv_dw=m_conv_dw, m_conv_dw_b=m_conv_dw_b, m_conv_ln_g=m_conv_ln_g, m_conv_ln_b=m_conv_ln_b, m_conv_w_out=m_conv_w_out, m_pool_w_in=m_pool_w_in, m_pool_w_grp=m_pool_w_grp, m_pool_b_grp=m_pool_b_grp, m_pool_scale=m_pool_scale, m_pool_w_out=m_pool_w_out, v_norm_g=v_norm_g, v_final_g=v_final_g, v_conv_w_in=v_conv_w_in, v_conv_dw=v_conv_dw, v_conv_dw_b=v_conv_dw_b, v_conv_ln_g=v_conv_ln_g, v_conv_ln_b=v_conv_ln_b, v_conv_w_out=v_conv_w_out, v_pool_w_in=v_pool_w_in, v_pool_w_grp=v_pool_w_grp, v_pool_b_grp=v_pool_b_grp, v_pool_scale=v_pool_scale, v_pool_w_out=v_pool_w_out)
    weights = {n: given[n] for n in TWIN_WEIGHTS}
    shared = {n: given[n] for n in SHARED_INPUTS}
    per_example = {n: given[n] for n in ['x']}
    grad_fn = _jax.value_and_grad(_loss, argnums=(0, 1))

    def one_microbatch(ex, loss_target):
        ex = dict(ex)
        diff = ex.pop(TWIN_DIFF_INPUT)
        return grad_fn(weights, diff, {**shared, **ex}, loss_target)

    if N_MICROBATCH == 1:
        loss, (grad_w, grad_x) = one_microbatch(per_example, given["loss_target"])
    else:
        def body(carry, xs):
            loss_sum, grad_sum = carry
            l_k, (gw_k, gx_k) = one_microbatch(xs[0], xs[1])
            with _jax.named_scope("update"):
                return (loss_sum + l_k, _jax.tree.map(_jnp.add, grad_sum, gw_k)), gx_k

        init = (_jnp.zeros((), _jnp.float32), _jax.tree.map(_jnp.zeros_like, weights))
        (loss, grad_w), grad_x = _jax.lax.scan(body, init, (per_example, given["loss_target"]))
    with _jax.named_scope("update"):
        delta_w, new_m, new_v = {}, {}, {}
        for n in TWIN_WEIGHTS:
            delta_w[n], new_m[n], new_v[n] = _adamw(weights[n], grad_w[n], given["m_" + n], given["v_" + n])
    return (loss, grad_x, *[grad_w[n] for n in TWIN_WEIGHTS], *[delta_w[n] for n in TWIN_WEIGHTS],
            *[new_m[n] for n in TWIN_WEIGHTS], *[new_v[n] for n in TWIN_WEIGHTS])
```

```python
import functools

import jax
import jax.numpy as jnp
from jax import lax
from jax.experimental import pallas as pl
from jax.experimental.pallas import tpu as pltpu

F32 = jnp.float32
BF16 = jnp.bfloat16
MESH = pl.DeviceIdType.MESH

RMS_EPS = 1e-6
LN_EPS = 1e-5
CONV_K = 31
HALO = 32
PHALO = 16
POOL_WINDOWS = (2, 4, 8, 16)
N_CHIPS = 4
LANES = 128
ROW_CHUNK = 32
TOKEN_TILE = 512
VMEM_LIMIT = 56 * 1024 * 1024

ADAM_LR = 0.001
ADAM_B1 = 0.9
ADAM_B2 = 0.999
ADAM_EPS = 1e-08
ADAM_WD = 0.01
ADAM_STEP = 10


def _params(*sem):
    return pltpu.CompilerParams(dimension_semantics=sem, vmem_limit_bytes=VMEM_LIMIT)


def _sig(v):
    return jax.nn.sigmoid(v)


def _dsilu(v, sv):
    return sv * (1.0 + v * (1.0 - sv))


def _tile(t):
    return min(TOKEN_TILE, t)


def _rms_matmul(h, g, w4, name):
    t, d = h.shape
    nk = w4.shape[-1]
    tm = _tile(t)

    def body(h_ref, g_ref, w_ref, p_ref, hn_ref):
        hh = h_ref[...]
        r = lax.rsqrt(jnp.mean(hh * hh, axis=-1, keepdims=True) + RMS_EPS)
        hn = (hh * r * g_ref[...]).astype(BF16)
        hn_ref[...] = hn
        for k in range(N_CHIPS):
            p_ref[:, k * nk:(k + 1) * nk] = jnp.dot(hn, w_ref[k], preferred_element_type=F32).astype(BF16)

    return pl.pallas_call(
        body, name=name, grid=(t // tm,),
        in_specs=[pl.BlockSpec((tm, d), lambda i: (i, 0)),
                  pl.BlockSpec((1, d), lambda i: (0, 0)),
                  pl.BlockSpec((N_CHIPS, d, nk), lambda i: (0, 0, 0))],
        out_specs=[pl.BlockSpec((tm, N_CHIPS * nk), lambda i: (i, 0)),
                   pl.BlockSpec((tm, d), lambda i: (i, 0))],
        out_shape=[jax.ShapeDtypeStruct((t, N_CHIPS * nk), BF16), jax.ShapeDtypeStruct((t, d), BF16)],
        compiler_params=_params("parallel"),
    )(h, g, w4)


def _conv_taps(r):
    return [(q, 8 * q + r) for q in range(5) if 8 * q + r <= HALO]


def _conv_mid_fwd(p, dw, dwb, lg, lb):
    t = p.shape[0]
    e = p.shape[1] // 3
    tm = _tile(t)
    rc = ROW_CHUNK
    n = rc + HALO

    def body(p_ref, dw_ref, dwb_ref, lg_ref, lb_ref, s_ref, c_ref, ubuf, cbuf):
        i = pl.program_id(0)

        @pl.when(i == 0)
        def _():
            ubuf[0:HALO, :] = jnp.zeros((HALO, e), F32)

        def glu(rci, carry):
            base = pl.multiple_of(rci * rc, rc)
            a = p_ref[pl.ds(base, rc), 0:e].astype(F32)
            b = p_ref[pl.ds(base, rc), e:2 * e].astype(F32)
            ubuf[pl.ds(HALO + base, rc), :] = a * _sig(b)
            return carry

        lax.fori_loop(0, tm // rc, glu, 0)

        def chunk(rci, carry):
            base = pl.multiple_of(rci * rc, rc)
            for lc in range(e // LANES):
                lanes = slice(lc * LANES, (lc + 1) * LANES)
                blk = ubuf[pl.ds(base, n), lanes]
                acc = jnp.broadcast_to(dwb_ref[:, lanes], (rc, LANES))
                for r in range(8):
                    sh = blk if r == 0 else pltpu.roll(blk, n - r, 0)
                    for q, o in _conv_taps(r):
                        k = o - 2
                        if 0 <= k < CONV_K:
                            acc = acc + dw_ref[k:k + 1, lanes] * sh[8 * q:8 * q + rc]
                cbuf[pl.ds(base, rc), lanes] = acc
            c = cbuf[pl.ds(base, rc), :]
            mu = jnp.mean(c, axis=-1, keepdims=True)
            cc = c - mu
            var = jnp.mean(cc * cc, axis=-1, keepdims=True)
            ln = cc * lax.rsqrt(var + LN_EPS) * lg_ref[...] + lb_ref[...]
            z = p_ref[pl.ds(base, rc), 2 * e:3 * e].astype(F32)
            s = (ln * _sig(ln)) * (z * _sig(z))
            s_ref[pl.ds(base, rc), :] = s.astype(BF16)
            c_ref[pl.ds(base, rc), :] = c.astype(BF16)
            return carry

        lax.fori_loop(0, tm // rc, chunk, 0)
        ubuf[0:HALO, :] = ubuf[tm:tm + HALO, :]

    vec = pl.BlockSpec((1, e), lambda i: (0, 0))
    return pl.pallas_call(
        body, name="conv_mid_fwd", grid=(t // tm,),
        in_specs=[pl.BlockSpec((tm, 3 * e), lambda i: (i, 0)),
                  pl.BlockSpec((CONV_K, e), lambda i: (0, 0)), vec, vec, vec],
        out_specs=[pl.BlockSpec((tm, e), lambda i: (i, 0)), pl.BlockSpec((tm, e), lambda i: (i, 0))],
        out_shape=[jax.ShapeDtypeStruct((t, e), BF16), jax.ShapeDtypeStruct((t, e), BF16)],
        scratch_shapes=[pltpu.VMEM((tm + HALO, e), F32), pltpu.VMEM((tm, e), F32)],
        compiler_params=_params("arbitrary"),
    )(p, dw, dwb, lg, lb)


def _pool_group(lc, e):
    return (lc * LANES) // (e // len(POOL_WINDOWS))


def _pool_inv_count(row0, rows, w):
    tpos = row0 + lax.broadcasted_iota(jnp.int32, (rows, 1), 0)
    return 1.0 / jnp.minimum(tpos + 1, w).astype(F32)


def _pool_window_dev(ubuf, base, rc, e, row0, dbuf):
    n = rc + PHALO
    for lc in range(e // LANES):
        lanes = slice(lc * LANES, (lc + 1) * LANES)
        g = _pool_group(lc, e)
        w = POOL_WINDOWS[g]
        blk = ubuf[pl.ds(base, n), lanes]
        acc = blk
        step = 1
        while step < w:
            acc = acc + pltpu.roll(acc, step, 0)
            step *= 2
        win = acc[PHALO:n]
        tok = blk[PHALO:n]
        dbuf[pl.ds(base, rc), lanes] = win * _pool_inv_count(row0 + base, rc, w) - tok


def _pool_mid_fwd(p, wg, bg, sc):
    t = p.shape[0]
    e = p.shape[1] // 2
    gc = e // len(POOL_WINDOWS)
    tm = _tile(t)
    rc = ROW_CHUNK

    def body(p_ref, wg_ref, bg_ref, sc_ref, s_ref, ubuf, dbuf):
        i = pl.program_id(0)

        @pl.when(i == 0)
        def _():
            ubuf[0:PHALO, :] = jnp.zeros((PHALO, e), F32)

        ubuf[PHALO:PHALO + tm, :] = p_ref[:, 0:e].astype(F32)

        def chunk(rci, carry):
            base = pl.multiple_of(rci * rc, rc)
            _pool_window_dev(ubuf, base, rc, e, i * tm, dbuf)
            return carry

        lax.fori_loop(0, tm // rc, chunk, 0)
        ubuf[0:PHALO, :] = ubuf[tm:tm + PHALO, :]

        for g in range(len(POOL_WINDOWS)):
            cols = slice(g * gc, (g + 1) * gc)
            yg = jnp.dot(dbuf[:, cols].astype(BF16), wg_ref[g], preferred_element_type=F32)
            z = p_ref[:, e + g * gc:e + (g + 1) * gc].astype(F32)
            s = ((yg + bg_ref[:, cols]) * sc_ref[:, cols]) * (z * _sig(z))
            s_ref[:, cols] = s.astype(BF16)

    vec = pl.BlockSpec((1, e), lambda i: (0, 0))
    return pl.pallas_call(
        body, name="pool_mid_fwd", grid=(t // tm,),
        in_specs=[pl.BlockSpec((tm, 2 * e), lambda i: (i, 0)),
                  pl.BlockSpec((len(POOL_WINDOWS), gc, gc), lambda i: (0, 0, 0)), vec, vec],
        out_specs=pl.BlockSpec((tm, e), lambda i: (i, 0)),
        out_shape=jax.ShapeDtypeStruct((t, e), BF16),
        scratch_shapes=[pltpu.VMEM((tm + PHALO, e), F32), pltpu.VMEM((tm, e), F32)],
        compiler_params=_params("arbitrary"),
    )(p, wg, bg, sc)


def _matmul_res(h, s, w):
    t, d = h.shape
    e = s.shape[1]
    tm = _tile(t)

    def body(h_ref, s_ref, w_ref, o_ref):
        o_ref[...] = h_ref[...] + jnp.dot(s_ref[...], w_ref[...], preferred_element_type=F32)

    return pl.pallas_call(
        body, name="matmul_res", grid=(t // tm,),
        in_specs=[pl.BlockSpec((tm, d), lambda i: (i, 0)), pl.BlockSpec((tm, e), lambda i: (i, 0)),
                  pl.BlockSpec((e, d), lambda i: (0, 0))],
        out_specs=pl.BlockSpec((tm, d), lambda i: (i, 0)),
        out_shape=jax.ShapeDtypeStruct((t, d), F32),
        compiler_params=_params("parallel"),
    )(h, s, w)


def _loss_head(h, fg, tgt):
    t, d = h.shape
    tm = _tile(t)

    def body(h_ref, g_ref, t_ref, dh_ref, loss_ref, dg_ref):
        i = pl.program_id(0)

        @pl.when(i == 0)
        def _():
            loss_ref[...] = jnp.zeros_like(loss_ref)
            dg_ref[...] = jnp.zeros_like(dg_ref)

        hh = h_ref[...]
        r = lax.rsqrt(jnp.mean(hh * hh, axis=-1, keepdims=True) + RMS_EPS)
        hhat = hh * r
        err = hhat * g_ref[...] - t_ref[...]
        per_tok = jnp.mean(err * err, axis=-1, keepdims=True)
        loss_ref[...] += 0.5 * jnp.sum(per_tok, axis=0, keepdims=True)
        dy = err * (1.0 / d)
        tt = dy * g_ref[...]
        dh_ref[...] = r * (tt - hhat * jnp.mean(tt * hhat, axis=-1, keepdims=True))
        dg_ref[...] += jnp.sum(dy * hhat, axis=0, keepdims=True)

    return pl.pallas_call(
        body, name="loss_head", grid=(t // tm,),
        in_specs=[pl.BlockSpec((tm, d), lambda i: (i, 0)), pl.BlockSpec((1, d), lambda i: (0, 0)),
                  pl.BlockSpec((tm, d), lambda i: (i, 0))],
        out_specs=[pl.BlockSpec((tm, d), lambda i: (i, 0)), pl.BlockSpec((1, LANES), lambda i: (0, 0)),
                   pl.BlockSpec((1, d), lambda i: (0, 0))],
        out_shape=[jax.ShapeDtypeStruct((t, d), F32), jax.ShapeDtypeStruct((1, LANES), F32),
                   jax.ShapeDtypeStruct((1, d), F32)],
        compiler_params=_params("arbitrary"),
    )(h, fg, tgt)


def _ds_matmul(dy, w):
    t, d = dy.shape
    e = w.shape[0]
    tm = _tile(t)

    def body(dy_ref, w_ref, ds_ref):
        ds_ref[...] = lax.dot_general(dy_ref[...].astype(BF16), w_ref[...], (((1,), (1,)), ((), ())),
                                      preferred_element_type=F32).astype(BF16)

    return pl.pallas_call(
        body, name="ds_matmul", grid=(t // tm,),
        in_specs=[pl.BlockSpec((tm, d), lambda i: (i, 0)), pl.BlockSpec((e, d), lambda i: (0, 0))],
        out_specs=pl.BlockSpec((tm, e), lambda i: (i, 0)),
        out_shape=jax.ShapeDtypeStruct((t, e), BF16),
        compiler_params=_params("parallel"),
    )(dy, w)


def _conv_mid_bwd(p, c, ds, dw, lg, lb):
    t = p.shape[0]
    e = p.shape[1] // 3
    tm = _tile(t)
    nt = t // tm
    rc = ROW_CHUNK
    n = rc + HALO
    hb = tm // HALO

    def body(p_ref, ph_ref, c_ref, ds_ref, dw_ref, lg_ref, lb_ref,
             dp_ref, dlg_ref, dlb_ref, ddwb_ref, ddw_ref, ubuf, dcbuf, dubuf, ddw_acc):
        i = pl.program_id(0)
        ti = nt - 1 - i

        @pl.when(i == 0)
        def _():
            dcbuf[tm:tm + HALO, :] = jnp.zeros((HALO, e), F32)
            dlg_ref[...] = jnp.zeros_like(dlg_ref)
            dlb_ref[...] = jnp.zeros_like(dlb_ref)
            ddwb_ref[...] = jnp.zeros_like(ddwb_ref)
            ddw_acc[...] = jnp.zeros_like(ddw_acc)

        ha = ph_ref[:, 0:e].astype(F32)
        hbb = ph_ref[:, e:2 * e].astype(F32)
        ubuf[0:HALO, :] = jnp.where(ti > 0, ha * _sig(hbb), 0.0)

        def front(rci, carry):
            slg, slb, sdwb = carry
            base = pl.multiple_of(rci * rc, rc)
            rows = pl.ds(base, rc)
            a = p_ref[rows, 0:e].astype(F32)
            b = p_ref[rows, e:2 * e].astype(F32)
            ubuf[pl.ds(HALO + base, rc), :] = a * _sig(b)
            cv = c_ref[rows, :].astype(F32)
            mu = jnp.mean(cv, axis=-1, keepdims=True)
            cc = cv - mu
            var = jnp.mean(cc * cc, axis=-1, keepdims=True)
            rs = lax.rsqrt(var + LN_EPS)
            nn = cc * rs
            ln = nn * lg_ref[...] + lb_ref[...]
            z = p_ref[rows, 2 * e:3 * e].astype(F32)
            sz = _sig(z)
            sl = _sig(ln)
            dsv = ds_ref[rows, :].astype(F32)
            dln = dsv * (z * sz) * _dsilu(ln, sl)
            dz = dsv * (ln * sl) * _dsilu(z, sz)
            dp_ref[rows, 2 * e:3 * e] = dz.astype(BF16)
            dn = dln * lg_ref[...]
            dc = rs * (dn - jnp.mean(dn, axis=-1, keepdims=True)
                       - nn * jnp.mean(dn * nn, axis=-1, keepdims=True))
            dcbuf[rows, :] = dc
            return (slg + jnp.sum(dln * nn, axis=0, keepdims=True),
                    slb + jnp.sum(dln, axis=0, keepdims=True),
                    sdwb + jnp.sum(dc, axis=0, keepdims=True))

        zero = jnp.zeros((1, e), F32)
        slg, slb, sdwb = lax.fori_loop(0, tm // rc, front, (zero, zero, zero))
        dlg_ref[...] += slg
        dlb_ref[...] += slb
        ddwb_ref[...] += sdwb

        def chunk(rci, carry):
            base = pl.multiple_of(rci * rc, rc)
            for lc in range(e // LANES):
                lanes = slice(lc * LANES, (lc + 1) * LANES)
                dcx = dcbuf[pl.ds(base, n), lanes]
                ux = ubuf[pl.ds(base, n), lanes]
                dc0 = dcx[0:rc]
                du = jnp.zeros((rc, LANES), F32)
                for r in range(8):
                    dsh = dcx if r == 0 else pltpu.roll(dcx, n - r, 0)
                    ush = ux if r == 0 else pltpu.roll(ux, n - r, 0)
                    for q, o in _conv_taps(r):
                        if o <= CONV_K - 1:
                            k = CONV_K - 1 - o
                            du = du + dw_ref[k:k + 1, lanes] * dsh[8 * q:8 * q + rc]
                        k = o - 2
                        if 0 <= k < CONV_K:
                            prod = dc0 * ush[8 * q:8 * q + rc]
                            part = prod[0:8]
                            for v in range(1, rc // 8):
                                part = part + prod[8 * v:8 * v + 8]
                            ddw_acc[k, :, lanes] += part
                dubuf[pl.ds(base, rc), lanes] = du
            rows = pl.ds(base, rc)
            a = p_ref[rows, 0:e].astype(F32)
            b = p_ref[rows, e:2 * e].astype(F32)
            sb = _sig(b)
            duv = dubuf[rows, :]
            dp_ref[rows, 0:e] = (duv * sb).astype(BF16)
            dp_ref[rows, e:2 * e] = (duv * a * sb * (1.0 - sb)).astype(BF16)
            return carry

        lax.fori_loop(0, tm // rc, chunk, 0)
        dcbuf[tm:tm + HALO, :] = dcbuf[0:HALO, :]

        @pl.when(i == nt - 1)
        def _():
            for k in range(CONV_K):
                ddw_ref[k:k + 1, :] = jnp.sum(ddw_acc[k], axis=0, keepdims=True)

    vec = pl.BlockSpec((1, e), lambda i: (0, 0))
    rev = lambda i: (nt - 1 - i, 0)
    halo = lambda i: (jnp.maximum((nt - 1 - i) * hb - 1, 0), 0)
    return pl.pallas_call(
        body, name="conv_mid_bwd", grid=(nt,),
        in_specs=[pl.BlockSpec((tm, 3 * e), rev), pl.BlockSpec((HALO, 3 * e), halo),
                  pl.BlockSpec((tm, e), rev), pl.BlockSpec((tm, e), rev),
                  pl.BlockSpec((CONV_K, e), lambda i: (0, 0)), vec, vec],
        out_specs=[pl.BlockSpec((tm, 3 * e), rev), vec, vec, vec,
                   pl.BlockSpec((CONV_K, e), lambda i: (0, 0))],
        out_shape=[jax.ShapeDtypeStruct((t, 3 * e), BF16), jax.ShapeDtypeStruct((1, e), F32),
                   jax.ShapeDtypeStruct((1, e), F32), jax.ShapeDtypeStruct((1, e), F32),
                   jax.ShapeDtypeStruct((CONV_K, e), F32)],
        scratch_shapes=[pltpu.VMEM((tm + HALO, e), F32), pltpu.VMEM((tm + HALO, e), F32),
                        pltpu.VMEM((tm, e), F32), pltpu.VMEM((CONV_K, 8, e), F32)],
        compiler_params=_params("arbitrary"),
    )(p, p, c, ds, dw, lg, lb)


def _pool_mid_bwd(p, ds, wg, bg, sc):
    t = p.shape[0]
    e = p.shape[1] // 2
    ng = len(POOL_WINDOWS)
    gc = e // ng
    tm = _tile(t)
    nt = t // tm
    rc = ROW_CHUNK
    hb = tm // PHALO

    def body(p_ref, ph_ref, ds_ref, wg_ref, bg_ref, sc_ref,
             dp_ref, dwg_ref, dbg_ref, dsc_ref, ubuf, dbuf, ebuf, ddbuf):
        i = pl.program_id(0)
        ti = nt - 1 - i

        @pl.when(i == 0)
        def _():
            ebuf[tm:tm + PHALO, :] = jnp.zeros((PHALO, e), F32)
            dwg_ref[...] = jnp.zeros_like(dwg_ref)
            dbg_ref[...] = jnp.zeros_like(dbg_ref)
            dsc_ref[...] = jnp.zeros_like(dsc_ref)

        ubuf[0:PHALO, :] = jnp.where(ti > 0, ph_ref[:, 0:e].astype(F32), 0.0)
        ubuf[PHALO:PHALO + tm, :] = p_ref[:, 0:e].astype(F32)

        def recompute(rci, carry):
            base = pl.multiple_of(rci * rc, rc)
            _pool_window_dev(ubuf, base, rc, e, ti * tm, dbuf)
            return carry

        lax.fori_loop(0, tm // rc, recompute, 0)

        for g in range(ng):
            cols = slice(g * gc, (g + 1) * gc)
            dg = dbuf[:, cols].astype(BF16)
            q = jnp.dot(dg, wg_ref[g], preferred_element_type=F32) + bg_ref[:, cols]
            z = p_ref[:, e + g * gc:e + (g + 1) * gc].astype(F32)
            sz = _sig(z)
            dsv = ds_ref[:, cols].astype(F32)
            dz = dsv * (q * sc_ref[:, cols]) * _dsilu(z, sz)
            dp_ref[:, e + g * gc:e + (g + 1) * gc] = dz.astype(BF16)
            dy2 = dsv * (z * sz)
            dsc_ref[:, cols] += jnp.sum(dy2 * q, axis=0, keepdims=True)
            dq = dy2 * sc_ref[:, cols]
            dbg_ref[:, cols] += jnp.sum(dq, axis=0, keepdims=True)
            dqb = dq.astype(BF16)
            dwg_ref[g] += lax.dot_general(dg, dqb, (((0,), (0,)), ((), ())), preferred_element_type=F32)
            ddbuf[:, cols] = lax.dot_general(dqb, wg_ref[g], (((1,), (1,)), ((), ())),
                                             preferred_element_type=F32)

        def scale(rci, carry):
            base = pl.multiple_of(rci * rc, rc)
            for lc in range(e // LANES):
                lanes = slice(lc * LANES, (lc + 1) * LANES)
                w = POOL_WINDOWS[_pool_group(lc, e)]
                ebuf[pl.ds(base, rc), lanes] = (ddbuf[pl.ds(base, rc), lanes]
                                                * _pool_inv_count(ti * tm + base, rc, w))
            return carry

        lax.fori_loop(0, tm // rc, scale, 0)

        def chunk(rci, carry):
            base = pl.multiple_of(rci * rc, rc)
            n = rc + PHALO
            for lc in range(e // LANES):
                lanes = slice(lc * LANES, (lc + 1) * LANES)
                w = POOL_WINDOWS[_pool_group(lc, e)]
                acc = ebuf[pl.ds(base, n), lanes]
                step = 1
                while step < w:
                    acc = acc + pltpu.roll(acc, n - step, 0)
                    step *= 2
                du = acc[0:rc] - ddbuf[pl.ds(base, rc), lanes]
                dp_ref[pl.ds(base, rc), lanes] = du.astype(BF16)
            return carry

        lax.fori_loop(0, tm // rc, chunk, 0)
        ebuf[tm:tm + PHALO, :] = ebuf[0:PHALO, :]

    vec = pl.BlockSpec((1, e), lambda i: (0, 0))
    rev = lambda i: (nt - 1 - i, 0)
    halo = lambda i: (jnp.maximum((nt - 1 - i) * hb - 1, 0), 0)
    wspec = pl.BlockSpec((ng, gc, gc), lambda i: (0, 0, 0))
    return pl.pallas_call(
        body, name="pool_mid_bwd", grid=(nt,),
        in_specs=[pl.BlockSpec((tm, 2 * e), rev), pl.BlockSpec((PHALO, 2 * e), halo),
                  pl.BlockSpec((tm, e), rev), wspec, vec, vec],
        out_specs=[pl.BlockSpec((tm, 2 * e), rev), wspec, vec, vec],
        out_shape=[jax.ShapeDtypeStruct((t, 2 * e), BF16), jax.ShapeDtypeStruct((ng, gc, gc), F32),
                   jax.ShapeDtypeStruct((1, e), F32), jax.ShapeDtypeStruct((1, e), F32)],
        scratch_shapes=[pltpu.VMEM((tm + PHALO, e), F32), pltpu.VMEM((tm, e), F32),
                        pltpu.VMEM((tm + PHALO, e), F32), pltpu.VMEM((tm, e), F32)],
        compiler_params=_params("arbitrary"),
    )(p, p, ds, wg, bg, sc)


def _dhn_rms_bwd(dp, w4, h, g, dh_out, name):
    t, d = h.shape
    nk = w4.shape[-1]
    tm = _tile(t)

    def body(dp_ref, w_ref, h_ref, g_ref, dho_ref, dh_ref, dg_ref):
        i = pl.program_id(0)

        @pl.when(i == 0)
        def _():
            dg_ref[...] = jnp.zeros_like(dg_ref)

        dhn = jnp.zeros((tm, d), F32)
        for k in range(N_CHIPS):
            dhn = dhn + lax.dot_general(dp_ref[:, k * nk:(k + 1) * nk], w_ref[k], (((1,), (1,)), ((), ())),
                                        preferred_element_type=F32)
        hh = h_ref[...]
        r = lax.rsqrt(jnp.mean(hh * hh, axis=-1, keepdims=True) + RMS_EPS)
        hhat = hh * r
        tt = dhn * g_ref[...]
        dh_ref[...] = dho_ref[...] + r * (tt - hhat * jnp.mean(tt * hhat, axis=-1, keepdims=True))
        dg_ref[...] += jnp.sum(dhn * hhat, axis=0, keepdims=True)

    return pl.pallas_call(
        body, name=name, grid=(t // tm,),
        in_specs=[pl.BlockSpec((tm, N_CHIPS * nk), lambda i: (i, 0)),
                  pl.BlockSpec((N_CHIPS, d, nk), lambda i: (0, 0, 0)),
                  pl.BlockSpec((tm, d), lambda i: (i, 0)), pl.BlockSpec((1, d), lambda i: (0, 0)),
                  pl.BlockSpec((tm, d), lambda i: (i, 0))],
        out_specs=[pl.BlockSpec((tm, d), lambda i: (i, 0)), pl.BlockSpec((1, d), lambda i: (0, 0))],
        out_shape=[jax.ShapeDtypeStruct((t, d), F32), jax.ShapeDtypeStruct((1, d), F32)],
        compiler_params=_params("arbitrary"),
    )(dp, w4, h, g, dh_out)


def _wgrad(a, b, nblk, name):
    t, m = a.shape
    nn = b.shape[1] // nblk
    tk = _tile(t)

    def body(a_ref, b_ref, o_ref):
        @pl.when(pl.program_id(1) == 0)
        def _():
            o_ref[...] = jnp.zeros_like(o_ref)

        o_ref[...] += lax.dot_general(a_ref[...].astype(BF16), b_ref[...].astype(BF16),
                                      (((0,), (0,)), ((), ())), preferred_element_type=F32)

    return pl.pallas_call(
        body, name=name, grid=(nblk, t // tk),
        in_specs=[pl.BlockSpec((tk, m), lambda j, i: (i, 0)), pl.BlockSpec((tk, nn), lambda j, i: (i, j))],
        out_specs=pl.BlockSpec((None, m, nn), lambda j, i: (j, 0, 0)),
        out_shape=jax.ShapeDtypeStruct((nblk, m, nn), F32),
        compiler_params=_params("parallel", "arbitrary"),
    )(a, b)


def _rows2d(shape):
    rows = 1
    for s in shape[:-1]:
        rows *= s
    return rows, shape[-1]


def _row_tile(rows):
    for cand in (512, 256, 128, 64, 32, 16, 8):
        if rows % cand == 0:
            return cand
    return rows


def _add_pair(a2, sel, r1, name):
    _, rows, cols = a2.shape
    tr = _row_tile(rows)

    def body(sel_ref, a_ref, r_ref, o_ref):
        o_ref[...] = a_ref[...] + r_ref[...]

    return pl.pallas_call(
        body, name=name,
        grid_spec=pltpu.PrefetchScalarGridSpec(
            num_scalar_prefetch=1, grid=(rows // tr,),
            in_specs=[pl.BlockSpec((None, tr, cols), lambda i, s: (s[0], i, 0)),
                      pl.BlockSpec((tr, cols), lambda i, s: (i, 0))],
            out_specs=pl.BlockSpec((tr, cols), lambda i, s: (i, 0))),
        out_shape=jax.ShapeDtypeStruct((rows, cols), F32),
        compiler_params=_params("parallel"),
    )(sel, a2, r1)


def _add_four(s1, r2, chip_core, name):
    _, rows, cols = s1.shape
    tr = _row_tile(rows)

    def body(sel_ref, s_ref, r_ref, o_ref):
        o_ref[...] = ((s_ref[...] + r_ref[0]) + r_ref[1]) + r_ref[2]

    return pl.pallas_call(
        body, name=name,
        grid_spec=pltpu.PrefetchScalarGridSpec(
            num_scalar_prefetch=1, grid=(rows // tr,),
            in_specs=[pl.BlockSpec((None, tr, cols), lambda i, s: (s[0], i, 0)),
                      pl.BlockSpec((3, tr, cols), lambda i, s: (0, i, 0))],
            out_specs=pl.BlockSpec((None, tr, cols), lambda i, s: (s[1], i, 0))),
        out_shape=jax.ShapeDtypeStruct((2, rows, cols), F32),
        compiler_params=_params("parallel"),
    )(chip_core, s1, r2)


def _adamw(w, g, m, v, name):
    rows, cols = w.shape
    tr = _row_tile(rows)

    def body(w_ref, g_ref, m_ref, v_ref, d_ref, m2_ref, v2_ref):
        gg = g_ref[...]
        m2 = ADAM_B1 * m_ref[...] + (1.0 - ADAM_B1) * gg
        v2 = ADAM_B2 * v_ref[...] + (1.0 - ADAM_B2) * (gg * gg)
        m_hat = m2 / (1.0 - ADAM_B1 ** ADAM_STEP)
        v_hat = v2 / (1.0 - ADAM_B2 ** ADAM_STEP)
        d_ref[...] = -ADAM_LR * (m_hat / (jnp.sqrt(v_hat) + ADAM_EPS) + ADAM_WD * w_ref[...])
        m2_ref[...] = m2
        v2_ref[...] = v2

    spec = pl.BlockSpec((tr, cols), lambda i: (i, 0))
    shp = jax.ShapeDtypeStruct((rows, cols), F32)
    return pl.pallas_call(
        body, name=name, grid=(rows // tr,),
        in_specs=[spec, spec, spec, spec], out_specs=[spec, spec, spec], out_shape=[shp, shp, shp],
        compiler_params=_params("parallel"),
    )(w, g, m, v)


ANY = pl.BlockSpec(memory_space=pl.ANY)


def _place():
    x, y, c = lax.axis_index("x"), lax.axis_index("y"), lax.axis_index("c")
    chips = [(1 - x, y), (x, 1 - y), (1 - x, 1 - y)]
    return x, y, c, chips


def _allgather_weights(shards):
    n = len(shards)

    def body(*refs):
        ins, outs = refs[:n], refs[n:2 * n]
        send_ici, recv_ici, send_d2d, recv_d2d, loc_sem = refs[2 * n:]
        x, y, c, chips = _place()
        k0 = 2 * x + y
        sib = (x, y, 1 - c)

        locs = [pltpu.make_async_copy(ins[a], outs[a].at[k0], loc_sem.at[a]) for a in range(n)]
        for cp in locs:
            cp.start()

        def ici(a, r, src_chip, target):
            return pltpu.make_async_remote_copy(
                src_ref=ins[a].at[c], dst_ref=outs[a].at[src_chip, c],
                send_sem=send_ici.at[a * 3 + r], recv_sem=recv_ici.at[a * 3 + r],
                device_id=target, device_id_type=MESH)

        def d2d(a, r, src_chip, layer):
            return pltpu.make_async_remote_copy(
                src_ref=outs[a].at[src_chip, layer], dst_ref=outs[a].at[src_chip, layer],
                send_sem=send_d2d.at[a * 3 + r], recv_sem=recv_d2d.at[a * 3 + r],
                device_id=sib, device_id_type=MESH)

        first = [ici(a, r, k0, (cx, cy, c)) for a in range(n) for r, (cx, cy) in enumerate(chips)]
        for cp in first:
            cp.start()
        passed = []
        for a in range(n):
            for r, (cx, cy) in enumerate(chips):
                ici(a, r, 2 * cx + cy, (cx, cy, c)).wait_recv()
                cp = d2d(a, r, 2 * cx + cy, c)
                cp.start()
                passed.append(cp)
        for a in range(n):
            for r, (cx, cy) in enumerate(chips):
                d2d(a, r, 2 * cx + cy, 1 - c).wait_recv()
        for cp in first + passed:
            cp.wait_send()
        for cp in locs:
            cp.wait()

    return pl.pallas_call(
        body, name="allgather_weights",
        in_specs=[ANY] * n, out_specs=[ANY] * n,
        out_shape=[jax.ShapeDtypeStruct((N_CHIPS,) + s.shape, s.dtype) for s in shards],
        scratch_shapes=[pltpu.SemaphoreType.DMA((3 * n,)), pltpu.SemaphoreType.DMA((3 * n,)),
                        pltpu.SemaphoreType.DMA((3 * n,)), pltpu.SemaphoreType.DMA((3 * n,)),
                        pltpu.SemaphoreType.DMA((n,))],
    )(*shards)


def _swap_layers(grads):
    n = len(grads)

    def body(*refs):
        ins, outs = refs[:n], refs[n:2 * n]
        send_sem, recv_sem = refs[2 * n:]
        x, y, c, _ = _place()
        cps = [pltpu.make_async_remote_copy(
            src_ref=ins[a].at[1 - c], dst_ref=outs[a], send_sem=send_sem.at[a], recv_sem=recv_sem.at[a],
            device_id=(x, y, 1 - c), device_id_type=MESH) for a in range(n)]
        for cp in cps:
            cp.start()
        for cp in cps:
            cp.wait()

    return pl.pallas_call(
        body, name="swap_layers",
        in_specs=[ANY] * n, out_specs=[ANY] * n,
        out_shape=[jax.ShapeDtypeStruct(g.shape[1:], g.dtype) for g in grads],
        scratch_shapes=[pltpu.SemaphoreType.DMA((n,)), pltpu.SemaphoreType.DMA((n,))],
    )(*grads)


def _scatter_chips(sums):
    n = len(sums)

    def body(*refs):
        ins, outs = refs[:n], refs[n:2 * n]
        send_sem, recv_sem = refs[2 * n:]
        x, y, c, chips = _place()
        cps = [pltpu.make_async_remote_copy(
            src_ref=ins[a].at[2 * cx + cy], dst_ref=outs[a].at[r],
            send_sem=send_sem.at[a * 3 + r], recv_sem=recv_sem.at[a * 3 + r],
            device_id=(cx, cy, c), device_id_type=MESH)
            for a in range(n) for r, (cx, cy) in enumerate(chips)]
        for cp in cps:
            cp.start()
        for cp in cps:
            cp.wait()

    return pl.pallas_call(
        body, name="scatter_chips",
        in_specs=[ANY] * n, out_specs=[ANY] * n,
        out_shape=[jax.ShapeDtypeStruct((3,) + s.shape[1:], s.dtype) for s in sums],
        scratch_shapes=[pltpu.SemaphoreType.DMA((3 * n,)), pltpu.SemaphoreType.DMA((3 * n,))],
    )(*sums)


def _share_halves(halves):
    n = len(halves)

    def body(*refs):
        ins, outs = refs[:n], refs[n:2 * n]
        send_sem, recv_sem = refs[2 * n:]
        x, y, c, _ = _place()
        cps = [pltpu.make_async_remote_copy(
            src_ref=outs[a].at[c], dst_ref=outs[a].at[c], send_sem=send_sem.at[a], recv_sem=recv_sem.at[a],
            device_id=(x, y, 1 - c), device_id_type=MESH) for a in range(n)]
        for cp in cps:
            cp.start()
        for cp in cps:
            cp.wait()

    return pl.pallas_call(
        body, name="share_halves",
        in_specs=[ANY] * n, out_specs=[ANY] * n,
        out_shape=[jax.ShapeDtypeStruct(h.shape, h.dtype) for h in halves],
        input_output_aliases={a: a for a in range(n)},
        scratch_shapes=[pltpu.SemaphoreType.DMA((n,)), pltpu.SemaphoreType.DMA((n,))],
    )(*halves)


N_DEV = 8


def _allreduce_small(v):
    m, nc = v.shape

    def body(x_ref, out_ref, gat, send_sems, recv_sems, local_sem):
        x, y, c, chips = _place()
        me, sib = (x, y, c), (x, y, 1 - c)

        def rows(px, py, pc):
            return gat.at[pl.ds((4 * px + 2 * py + pc) * m, m), :]

        def copy(k, block, to, src=None):
            return pltpu.make_async_remote_copy(
                src_ref=rows(*block) if src is None else src, dst_ref=rows(*block),
                send_sem=send_sems.at[k], recv_sem=recv_sems.at[k], device_id=to, device_id_type=MESH)

        mine = pltpu.make_async_copy(x_ref, rows(*me), local_sem)
        mine.start()
        first = [copy(0, me, sib, src=x_ref)]
        first += [copy(1 + j, me, (*chip, c), src=x_ref) for j, chip in enumerate(chips)]
        for cp in first:
            cp.start()
        passed = [copy(4 + j, (*chip, c), sib) for j, chip in enumerate(chips)]
        for j, chip in enumerate(chips):
            copy(1 + j, (*chip, c), me).wait_recv()
            passed[j].start()
        copy(0, sib, me).wait_recv()
        for j, chip in enumerate(chips):
            copy(4 + j, (*chip, 1 - c), me).wait_recv()
        for cp in first + passed:
            cp.wait_send()
        mine.wait()
        acc = gat[0:m, :]
        for dev in range(1, N_DEV):
            acc = acc + gat[dev * m:(dev + 1) * m, :]
        out_ref[...] = acc

    return pl.pallas_call(
        body, name="allreduce_small",
        in_specs=[pl.BlockSpec(memory_space=pltpu.VMEM)],
        out_specs=pl.BlockSpec(memory_space=pltpu.VMEM),
        out_shape=jax.ShapeDtypeStruct((m, nc), F32),
        scratch_shapes=[pltpu.VMEM((N_DEV * m, nc), F32), pltpu.SemaphoreType.DMA((7,)),
                        pltpu.SemaphoreType.DMA((7,)), pltpu.SemaphoreType.DMA],
        compiler_params=pltpu.CompilerParams(vmem_limit_bytes=VMEM_LIMIT),
    )(v)


def _pad_rows(a, rows):
    return jnp.pad(a, ((0, rows - a.shape[0]), (0, 0)))


def kernel(x, norm_g, final_g, conv_w_in, conv_dw, conv_dw_b, conv_ln_g, conv_ln_b, conv_w_out, pool_w_in, pool_w_grp, pool_b_grp, pool_scale, pool_w_out, loss_target, m_norm_g, m_final_g, m_conv_w_in, m_conv_dw, m_conv_dw_b, m_conv_ln_g, m_conv_ln_b, m_conv_w_out, m_pool_w_in, m_pool_w_grp, m_pool_b_grp, m_pool_scale, m_pool_w_out, v_norm_g, v_final_g, v_conv_w_in, v_conv_dw, v_conv_dw_b, v_conv_ln_g, v_conv_ln_b, v_conv_w_out, v_pool_w_in, v_pool_w_grp, v_pool_b_grp, v_pool_scale, v_pool_w_out):
    t, d = x.shape[1], x.shape[2]
    e = conv_w_out.shape[2]
    ng = len(POOL_WINDOWS)
    gc = e // ng
    gcs = pool_w_grp.shape[2]
    ck = conv_dw.shape[1]
    es = conv_dw.shape[2]
    xi, yi, ci = lax.axis_index("x"), lax.axis_index("y"), lax.axis_index("c")
    chip = 2 * xi + yi

    small_rows = ck + 2
    small_pad = -(-small_rows // 8) * 8
    small = jnp.concatenate([conv_dw, pool_b_grp[:, None, :], pool_scale[:, None, :],
                             jnp.zeros((2, small_pad - small_rows, es), F32)], axis=1)
    g_cwi, g_cwo, g_pwi, g_pwg, g_pwo, g_small = _allgather_weights(
        [conv_w_in.astype(BF16), conv_w_out.astype(BF16), pool_w_in.astype(BF16),
         pool_w_grp.astype(BF16), pool_w_out.astype(BF16), small])
    smallf = jnp.transpose(g_small, (1, 2, 0, 3)).reshape(2, small_pad, N_CHIPS * es)
    wg_full = jnp.transpose(g_pwg, (1, 2, 0, 3, 4)).reshape(2, ng, gc, gc)

    h = x.reshape(t, d)
    tgt = loss_target.reshape(t, d)
    hs, saved = [], []
    for layer in range(4):
        j = layer // 2
        hs.append(h)
        gvec = norm_g[layer][None, :]
        if layer % 2 == 0:
            w_in4 = g_cwi[:, j]
            w_out = g_cwo[:, j].reshape(e, d)
            p, hn = _rms_matmul(h, gvec, w_in4, "rms_matmul_conv")
            dw_full = smallf[j, 0:ck]
            s, c = _conv_mid_fwd(p, dw_full, conv_dw_b[j][None, :], conv_ln_g[j][None, :],
                                 conv_ln_b[j][None, :])
            saved.append((p, hn, s, c, w_in4, w_out, dw_full))
        else:
            w_in4 = g_pwi[:, j]
            w_out = g_pwo[:, j].reshape(e, d)
            p, hn = _rms_matmul(h, gvec, w_in4, "rms_matmul_pool")
            bg_full = smallf[j, ck:ck + 1]
            sc_full = smallf[j, ck + 1:ck + 2]
            s = _pool_mid_fwd(p, wg_full[j], bg_full, sc_full)
            saved.append((p, hn, s, None, w_in4, w_out, (wg_full[j], bg_full, sc_full)))
        h = _matmul_res(h, s, w_out)

    dh, loss_part, dfg = _loss_head(h, final_g[None, :], tgt)
    loss = lax.psum(loss_part[0, 0], ("x", "y", "c"))

    dng = [None] * 4
    g_conv = [None, None]
    g_pool = [None, None]
    for layer in (3, 2, 1, 0):
        j = layer // 2
        p, hn, s, c, w_in4, w_out, extra = saved[layer]
        gvec = norm_g[layer][None, :]
        ds = _ds_matmul(dh, w_out)
        dw_out = _wgrad(s, dh, 1, "wgrad_out")[0].reshape(N_CHIPS, e // N_CHIPS, d)
        if layer % 2 == 0:
            dp, dlg, dlb, ddwb, ddw = _conv_mid_bwd(p, c, ds, extra, conv_ln_g[j][None, :],
                                                    conv_ln_b[j][None, :])
            dw_in = _wgrad(hn, dp, N_CHIPS, "wgrad_in_conv")
            dh, dng[layer] = _dhn_rms_bwd(dp, w_in4, hs[layer], gvec, dh, "dhn_rms_bwd_conv")
            g_conv[j] = (dw_in, dw_out, dlg, dlb, ddwb, ddw)
        else:
            wg, bg_full, sc_full = extra
            dp, dwg, dbg, dsc = _pool_mid_bwd(p, ds, wg, bg_full, sc_full)
            dw_in = _wgrad(hn, dp, N_CHIPS, "wgrad_in_pool")
            dh, dng[layer] = _dhn_rms_bwd(dp, w_in4, hs[layer], gvec, dh, "dhn_rms_bwd_pool")
            dwg4 = jnp.transpose(dwg.reshape(ng, N_CHIPS, gcs, gc), (1, 0, 2, 3))
            g_pool[j] = (dw_in, dw_out, dwg4, dbg, dsc)
    grad_x = dh.reshape(x.shape)

    big = [jnp.stack([g_conv[0][0], g_conv[1][0]]),
           jnp.stack([g_conv[0][1], g_conv[1][1]]),
           jnp.stack([g_pool[0][0], g_pool[1][0]]),
           jnp.stack([g_pool[0][2], g_pool[1][2]]),
           jnp.stack([g_pool[0][1], g_pool[1][1]])]
    names = ["cwi", "cwo", "pwi", "pwg", "pwo"]
    recv1 = _swap_layers(big)
    sel_c = jnp.reshape(ci, (1,)).astype(jnp.int32)
    sel_kc = jnp.stack([chip, ci]).astype(jnp.int32)
    sums = []
    for a, r1, nm in zip(big, recv1, names):
        rows, cols = _rows2d(a.shape[1:])
        sums.append(_add_pair(a.reshape(2, rows, cols), sel_c, r1.reshape(rows, cols),
                              "add_pair_" + nm).reshape(a.shape[1:]))
    recv2 = _scatter_chips(sums)
    halves = []
    for s1, r2, nm in zip(sums, recv2, names):
        rows, cols = _rows2d(s1.shape[1:])
        halves.append(_add_four(s1.reshape(N_CHIPS, rows, cols), r2.reshape(3, rows, cols), sel_kc,
                                "add_four_" + nm).reshape((2,) + s1.shape[1:]))
    g_cwi_f, g_cwo_f, g_pwi_f, g_pwg_f, g_pwo_f = _share_halves(halves)

    rows_list = [dng[0], dng[1], dng[2], dng[3], dfg,
                 g_conv[0][4], g_conv[1][4], g_conv[0][2], g_conv[1][2], g_conv[0][3], g_conv[1][3],
                 g_pool[0][3], g_pool[1][3], g_pool[0][4], g_pool[1][4], g_conv[0][5], g_conv[1][5]]
    slab = jnp.concatenate(rows_list, axis=0)
    nrows = slab.shape[0]
    slab = _pad_rows(slab, -(-nrows // 8) * 8)
    tot = _allreduce_small(slab)
    g_norm_g = tot[0:4]
    g_final_g = tot[4]
    g_dwb = tot[5:7]
    g_lng = tot[7:9]
    g_lnb = tot[9:11]
    g_bg = lax.dynamic_slice_in_dim(tot[11:13], chip * es, es, axis=1)
    g_sc = lax.dynamic_slice_in_dim(tot[13:15], chip * es, es, axis=1)
    g_dw = lax.dynamic_slice_in_dim(tot[15:15 + 2 * ck].reshape(2, ck, e), chip * es, es, axis=2)

    def adam_nd(w, g, m, v, nm):
        rows, cols = _rows2d(w.shape)
        outs = _adamw(w.reshape(rows, cols), g.reshape(rows, cols), m.reshape(rows, cols),
                      v.reshape(rows, cols), "adamw_" + nm)
        return [o.reshape(w.shape) for o in outs]

    res = {}
    res["conv_w_in"] = (g_cwi_f, *adam_nd(conv_w_in, g_cwi_f, m_conv_w_in, v_conv_w_in, "cwi"))
    res["conv_w_out"] = (g_cwo_f, *adam_nd(conv_w_out, g_cwo_f, m_conv_w_out, v_conv_w_out, "cwo"))
    res["pool_w_in"] = (g_pwi_f, *adam_nd(pool_w_in, g_pwi_f, m_pool_w_in, v_pool_w_in, "pwi"))
    res["pool_w_grp"] = (g_pwg_f, *adam_nd(pool_w_grp, g_pwg_f, m_pool_w_grp, v_pool_w_grp, "pwg"))
    res["pool_w_out"] = (g_pwo_f, *adam_nd(pool_w_out, g_pwo_f, m_pool_w_out, v_pool_w_out, "pwo"))

    def pack(parts, rows_to):
        return _pad_rows(jnp.concatenate([q.reshape(-1, q.shape[-1]) for q in parts], axis=0), rows_to)

    rep_w = [norm_g, final_g[None, :], conv_dw_b, conv_ln_g, conv_ln_b]
    rep_g = [g_norm_g, g_final_g[None, :], g_dwb, g_lng, g_lnb]
    rep_m = [m_norm_g, m_final_g[None, :], m_conv_dw_b, m_conv_ln_g, m_conv_ln_b]
    rep_v = [v_norm_g, v_final_g[None, :], v_conv_dw_b, v_conv_ln_g, v_conv_ln_b]
    rep = _adamw(pack(rep_w, 16), pack(rep_g, 16), pack(rep_m, 16), pack(rep_v, 16), "adamw_rep")
    rep_names = ["norm_g", "final_g", "conv_dw_b", "conv_ln_g", "conv_ln_b"]
    rep_rows = [(0, 4), (4, 5), (5, 7), (7, 9), (9, 11)]
    for nm, (lo, hi), gq, wq in zip(rep_names, rep_rows, rep_g, rep_w):
        shape = (d,) if nm == "final_g" else wq.shape
        res[nm] = (gq.reshape(shape), *[o[lo:hi].reshape(shape) for o in rep])

    sh_w = [conv_dw, pool_b_grp, pool_scale]
    sh_g = [g_dw, g_bg, g_sc]
    sh_m = [m_conv_dw, m_pool_b_grp, m_pool_scale]
    sh_v = [v_conv_dw, v_pool_b_grp, v_pool_scale]
    sh_total = 2 * ck + 4
    sh_pad = -(-sh_total // 8) * 8
    shd = _adamw(pack(sh_w, sh_pad), pack(sh_g, sh_pad), pack(sh_m, sh_pad), pack(sh_v, sh_pad), "adamw_shard")
    sh_names = ["conv_dw", "pool_b_grp", "pool_scale"]
    sh_rows = [(0, 2 * ck), (2 * ck, 2 * ck + 2), (2 * ck + 2, 2 * ck + 4)]
    for nm, (lo, hi), gq, wq in zip(sh_names, sh_rows, sh_g, sh_w):
        res[nm] = (gq.reshape(wq.shape), *[o[lo:hi].reshape(wq.shape) for o in shd])

    order = ["norm_g", "final_g", "conv_w_in", "conv_dw", "conv_dw_b", "conv_ln_g", "conv_ln_b", "conv_w_out",
             "pool_w_in", "pool_w_grp", "pool_b_grp", "pool_scale", "pool_w_out"]
    outs = [loss, grad_x]
    for part in range(4):
        outs += [res[nm][part] for nm in order]
    return tuple(outs)
```

```python
import functools

import jax
import jax.numpy as jnp
from jax import lax
from jax.experimental import pallas as pl
from jax.experimental.pallas import tpu as pltpu

F32 = jnp.float32
BF16 = jnp.bfloat16
MESH = pl.DeviceIdType.MESH

RMS_EPS = 1e-6
LN_EPS = 1e-5
CONV_K = 31
HALO = 32
PHALO = 16
POOL_WINDOWS = (2, 4, 8, 16)
N_CHIPS = 4
LANES = 128
ROW_CHUNK = 32
FIR_BLOCK = 16
TOKEN_TILE = 512
VMEM_LIMIT = 56 * 1024 * 1024

ADAM_LR = 0.001
ADAM_B1 = 0.9
ADAM_B2 = 0.999
ADAM_EPS = 1e-08
ADAM_WD = 0.01
ADAM_STEP = 10


def _params(*sem):
    return pltpu.CompilerParams(dimension_semantics=sem, vmem_limit_bytes=VMEM_LIMIT)


def _sig(v):
    return 0.5 * jnp.tanh(0.5 * v) + 0.5


def _dsilu(v, sv):
    return sv * (1.0 + v * (1.0 - sv))


def _tile(t):
    return min(TOKEN_TILE, t)


def _rms_matmul(h, g, w4, name):
    t, d = h.shape
    nk = w4.shape[-1]
    tm = _tile(t)

    def body(h_ref, g_ref, w_ref, p_ref, hn_ref):
        hh = h_ref[...]
        r = lax.rsqrt(jnp.mean(hh * hh, axis=-1, keepdims=True) + RMS_EPS)
        hn = (hh * r * g_ref[...]).astype(BF16)
        hn_ref[...] = hn
        for k in range(N_CHIPS):
            p_ref[:, k * nk:(k + 1) * nk] = jnp.dot(hn, w_ref[k], preferred_element_type=F32).astype(BF16)

    return pl.pallas_call(
        body, name=name, grid=(t // tm,),
        in_specs=[pl.BlockSpec((tm, d), lambda i: (i, 0)),
                  pl.BlockSpec((1, d), lambda i: (0, 0)),
                  pl.BlockSpec((N_CHIPS, d, nk), lambda i: (0, 0, 0))],
        out_specs=[pl.BlockSpec((tm, N_CHIPS * nk), lambda i: (i, 0)),
                   pl.BlockSpec((tm, d), lambda i: (i, 0))],
        out_shape=[jax.ShapeDtypeStruct((t, N_CHIPS * nk), BF16), jax.ShapeDtypeStruct((t, d), BF16)],
        compiler_params=_params("parallel"),
    )(h, g, w4)


def _to_token_tiles(ref, tok0, rows, val, ng):
    for j in range(ng):
        ref[pl.ds(tok0 * ng + j, rows, stride=ng), :] = val[:, j * LANES:(j + 1) * LANES]


def _from_token_tiles(ref, tok0, rows, ng):
    return jnp.concatenate([ref[pl.ds(tok0 * ng + j, rows, stride=ng), :] for j in range(ng)], axis=1)


def _conv_mid_fwd(p, dw3, dwb3, lg, lb):
    t = p.shape[0]
    e = p.shape[1] // 3
    ng = e // LANES
    tm = _tile(t)
    rc = ROW_CHUNK
    fb = FIR_BLOCK

    def body(p_ref, dw_ref, dwb_ref, lg_ref, lb_ref, s_ref, c_ref, u3, c3):
        i = pl.program_id(0)

        @pl.when(i == 0)
        def _():
            u3[0:HALO * ng, :] = jnp.zeros((HALO * ng, LANES), F32)

        def glu(rci, carry):
            base = pl.multiple_of(rci * rc, rc)
            a = p_ref[pl.ds(base, rc), 0:e].astype(F32)
            b = p_ref[pl.ds(base, rc), e:2 * e].astype(F32)
            _to_token_tiles(u3, HALO + base, rc, a * _sig(b), ng)
            return carry

        lax.fori_loop(0, tm // rc, glu, 0)

        def fir(bi, carry):
            t0 = bi * fb
            def x(q):
                return u3[pl.ds(pl.multiple_of((t0 + HALO - (CONV_K - 1) + q) * ng, ng), ng), :]

            xs = [x(q) for q in range(fb - 1)]
            accs = [dwb_ref[...]] * fb
            for k in range(CONV_K):
                wk = dw_ref[k * ng:(k + 1) * ng, :]
                xs.append(x(k + fb - 1))
                accs = [accs[q] + wk * xs[q + k] for q in range(fb)]
            for q in range(fb):
                c3[pl.ds(pl.multiple_of((t0 + q) * ng, ng), ng), :] = accs[q]
            return carry

        lax.fori_loop(0, tm // fb, fir, 0)
        u3[0:HALO * ng, :] = u3[tm * ng:(tm + HALO) * ng, :]

        def chunk(rci, carry):
            base = pl.multiple_of(rci * rc, rc)
            c = _from_token_tiles(c3, base, rc, ng)
            mu = jnp.mean(c, axis=-1, keepdims=True)
            cc = c - mu
            var = jnp.mean(cc * cc, axis=-1, keepdims=True)
            ln = cc * lax.rsqrt(var + LN_EPS) * lg_ref[...] + lb_ref[...]
            z = p_ref[pl.ds(base, rc), 2 * e:3 * e].astype(F32)
            s = (ln * _sig(ln)) * (z * _sig(z))
            s_ref[pl.ds(base, rc), :] = s.astype(BF16)
            c_ref[pl.ds(base, rc), :] = c.astype(BF16)
            return carry

        lax.fori_loop(0, tm // rc, chunk, 0, unroll=2)

    vec = pl.BlockSpec((1, e), lambda i: (0, 0))
    return pl.pallas_call(
        body, name="conv_mid_fwd", grid=(t // tm,),
        in_specs=[pl.BlockSpec((tm, 3 * e), lambda i: (i, 0)),
                  pl.BlockSpec((CONV_K * ng, LANES), lambda i: (0, 0)),
                  pl.BlockSpec((ng, LANES), lambda i: (0, 0)), vec, vec],
        out_specs=[pl.BlockSpec((tm, e), lambda i: (i, 0)), pl.BlockSpec((tm, e), lambda i: (i, 0))],
        out_shape=[jax.ShapeDtypeStruct((t, e), BF16), jax.ShapeDtypeStruct((t, e), BF16)],
        scratch_shapes=[pltpu.VMEM(((tm + HALO) * ng, LANES), F32), pltpu.VMEM((tm * ng, LANES), F32)],
        compiler_params=_params("arbitrary"),
    )(p, dw3, dwb3, lg, lb)


def _pool_group(lc, e):
    return (lc * LANES) // (e // len(POOL_WINDOWS))


def _pool_inv_count(row0, rows, w):
    tpos = row0 + lax.broadcasted_iota(jnp.int32, (rows, 1), 0)
    return 1.0 / jnp.minimum(tpos + 1, w).astype(F32)


def _pool_window_dev(ubuf, base, rc, e, row0, dbuf):
    n = rc + PHALO
    for lc in range(e // LANES):
        lanes = slice(lc * LANES, (lc + 1) * LANES)
        g = _pool_group(lc, e)
        w = POOL_WINDOWS[g]
        blk = ubuf[pl.ds(base, n), lanes]
        acc = blk
        step = 1
        while step < w:
            acc = acc + pltpu.roll(acc, step, 0)
            step *= 2
        win = acc[PHALO:n]
        tok = blk[PHALO:n]
        dbuf[pl.ds(base, rc), lanes] = win * _pool_inv_count(row0 + base, rc, w) - tok


def _pool_mid_fwd(p, wg, bg, sc):
    t = p.shape[0]
    e = p.shape[1] // 2
    gc = e // len(POOL_WINDOWS)
    tm = _tile(t)
    rc = ROW_CHUNK

    def body(p_ref, wg_ref, bg_ref, sc_ref, s_ref, ubuf, dbuf):
        i = pl.program_id(0)

        @pl.when(i == 0)
        def _():
            ubuf[0:PHALO, :] = jnp.zeros((PHALO, e), F32)

        ubuf[PHALO:PHALO + tm, :] = p_ref[:, 0:e].astype(F32)

        def chunk(rci, carry):
            base = pl.multiple_of(rci * rc, rc)
            _pool_window_dev(ubuf, base, rc, e, i * tm, dbuf)
            return carry

        lax.fori_loop(0, tm // rc, chunk, 0)
        ubuf[0:PHALO, :] = ubuf[tm:tm + PHALO, :]

        for g in range(len(POOL_WINDOWS)):
            cols = slice(g * gc, (g + 1) * gc)
            yg = jnp.dot(dbuf[:, cols].astype(BF16), wg_ref[g], preferred_element_type=F32)
            z = p_ref[:, e + g * gc:e + (g + 1) * gc].astype(F32)
            s = ((yg + bg_ref[:, cols]) * sc_ref[:, cols]) * (z * _sig(z))
            s_ref[:, cols] = s.astype(BF16)

    vec = pl.BlockSpec((1, e), lambda i: (0, 0))
    return pl.pallas_call(
        body, name="pool_mid_fwd", grid=(t // tm,),
        in_specs=[pl.BlockSpec((tm, 2 * e), lambda i: (i, 0)),
                  pl.BlockSpec((len(POOL_WINDOWS), gc, gc), lambda i: (0, 0, 0)), vec, vec],
        out_specs=pl.BlockSpec((tm, e), lambda i: (i, 0)),
        out_shape=jax.ShapeDtypeStruct((t, e), BF16),
        scratch_shapes=[pltpu.VMEM((tm + PHALO, e), F32), pltpu.VMEM((tm, e), F32)],
        compiler_params=_params("arbitrary"),
    )(p, wg, bg, sc)


def _matmul_res(h, s, w):
    t, d = h.shape
    e = s.shape[1]
    tm = _tile(t)

    def body(h_ref, s_ref, w_ref, o_ref):
        o_ref[...] = h_ref[...] + jnp.dot(s_ref[...], w_ref[...], preferred_element_type=F32)

    return pl.pallas_call(
        body, name="matmul_res", grid=(t // tm,),
        in_specs=[pl.BlockSpec((tm, d), lambda i: (i, 0)), pl.BlockSpec((tm, e), lambda i: (i, 0)),
                  pl.BlockSpec((e, d), lambda i: (0, 0))],
        out_specs=pl.BlockSpec((tm, d), lambda i: (i, 0)),
        out_shape=jax.ShapeDtypeStruct((t, d), F32),
        compiler_params=_params("parallel"),
    )(h, s, w)


def _loss_head(h, fg, tgt):
    t, d = h.shape
    tm = _tile(t)

    def body(h_ref, g_ref, t_ref, dh_ref, loss_ref, dg_ref):
        i = pl.program_id(0)

        @pl.when(i == 0)
        def _():
            loss_ref[...] = jnp.zeros_like(loss_ref)
            dg_ref[...] = jnp.zeros_like(dg_ref)

        hh = h_ref[...]
        r = lax.rsqrt(jnp.mean(hh * hh, axis=-1, keepdims=True) + RMS_EPS)
        hhat = hh * r
        err = hhat * g_ref[...] - t_ref[...]
        per_tok = jnp.mean(err * err, axis=-1, keepdims=True)
        loss_ref[...] += 0.5 * jnp.sum(per_tok, axis=0, keepdims=True)
        dy = err * (1.0 / d)
        tt = dy * g_ref[...]
        dh_ref[...] = r * (tt - hhat * jnp.mean(tt * hhat, axis=-1, keepdims=True))
        dg_ref[...] += jnp.sum(dy * hhat, axis=0, keepdims=True)

    return pl.pallas_call(
        body, name="loss_head", grid=(t // tm,),
        in_specs=[pl.BlockSpec((tm, d), lambda i: (i, 0)), pl.BlockSpec((1, d), lambda i: (0, 0)),
                  pl.BlockSpec((tm, d), lambda i: (i, 0))],
        out_specs=[pl.BlockSpec((tm, d), lambda i: (i, 0)), pl.BlockSpec((1, LANES), lambda i: (0, 0)),
                   pl.BlockSpec((1, d), lambda i: (0, 0))],
        out_shape=[jax.ShapeDtypeStruct((t, d), F32), jax.ShapeDtypeStruct((1, LANES), F32),
                   jax.ShapeDtypeStruct((1, d), F32)],
        compiler_params=_params("arbitrary"),
    )(h, fg, tgt)


def _ds_matmul(dy, w):
    t, d = dy.shape
    e = w.shape[0]
    tm = _tile(t)

    def body(dy_ref, w_ref, ds_ref):
        ds_ref[...] = lax.dot_general(dy_ref[...].astype(BF16), w_ref[...], (((1,), (1,)), ((), ())),
                                      preferred_element_type=F32).astype(BF16)

    return pl.pallas_call(
        body, name="ds_matmul", grid=(t // tm,),
        in_specs=[pl.BlockSpec((tm, d), lambda i: (i, 0)), pl.BlockSpec((e, d), lambda i: (0, 0))],
        out_specs=pl.BlockSpec((tm, e), lambda i: (i, 0)),
        out_shape=jax.ShapeDtypeStruct((t, e), BF16),
        compiler_params=_params("parallel"),
    )(dy, w)


def _conv_mid_bwd(p, c, ds, dw3, lg, lb):
    t = p.shape[0]
    e = p.shape[1] // 3
    ng = e // LANES
    tm = _tile(t)
    nt = t // tm
    rc = ROW_CHUNK
    fb = FIR_BLOCK
    hb = tm // HALO

    def body(p_ref, ph_ref, c_ref, ds_ref, dw_ref, lg_ref, lb_ref,
             dp_ref, dlg_ref, dlb_ref, ddwb_ref, ddw_ref, u3, dc3, du3):
        i = pl.program_id(0)
        ti = nt - 1 - i

        @pl.when(i == 0)
        def _():
            dc3[tm * ng:(tm + HALO) * ng, :] = jnp.zeros((HALO * ng, LANES), F32)
            dlg_ref[...] = jnp.zeros_like(dlg_ref)
            dlb_ref[...] = jnp.zeros_like(dlb_ref)
            ddwb_ref[...] = jnp.zeros_like(ddwb_ref)
            ddw_ref[...] = jnp.zeros_like(ddw_ref)

        ha = ph_ref[:, 0:e].astype(F32)
        hbb = ph_ref[:, e:2 * e].astype(F32)
        _to_token_tiles(u3, 0, HALO, jnp.where(ti > 0, ha * _sig(hbb), 0.0), ng)

        def front(rci, carry):
            slg, slb, sdwb = carry
            base = pl.multiple_of(rci * rc, rc)
            rows = pl.ds(base, rc)
            a = p_ref[rows, 0:e].astype(F32)
            b = p_ref[rows, e:2 * e].astype(F32)
            _to_token_tiles(u3, HALO + base, rc, a * _sig(b), ng)
            cv = c_ref[rows, :].astype(F32)
            mu = jnp.mean(cv, axis=-1, keepdims=True)
            cc = cv - mu
            var = jnp.mean(cc * cc, axis=-1, keepdims=True)
            rs = lax.rsqrt(var + LN_EPS)
            nn = cc * rs
            ln = nn * lg_ref[...] + lb_ref[...]
            z = p_ref[rows, 2 * e:3 * e].astype(F32)
            sz = _sig(z)
            sl = _sig(ln)
            dsv = ds_ref[rows, :].astype(F32)
            dln = dsv * (z * sz) * _dsilu(ln, sl)
            dz = dsv * (ln * sl) * _dsilu(z, sz)
            dp_ref[rows, 2 * e:3 * e] = dz.astype(BF16)
            dn = dln * lg_ref[...]
            dc = rs * (dn - jnp.mean(dn, axis=-1, keepdims=True)
                       - nn * jnp.mean(dn * nn, axis=-1, keepdims=True))
            _to_token_tiles(dc3, base, rc, dc, ng)
            return (slg + jnp.sum(dln * nn, axis=0, keepdims=True),
                    slb + jnp.sum(dln, axis=0, keepdims=True),
                    sdwb + jnp.sum(dc, axis=0, keepdims=True))

        zero = jnp.zeros((1, e), F32)
        slg, slb, sdwb = lax.fori_loop(0, tm // rc, front, (zero, zero, zero), unroll=2)
        dlg_ref[...] += slg
        dlb_ref[...] += slb
        ddwb_ref[...] += sdwb

        def fir(bi, carry):
            t0 = bi * fb

            def dcs(q):
                return dc3[pl.ds(pl.multiple_of((t0 + q) * ng, ng), ng), :]

            def us(q):
                return u3[pl.ds(pl.multiple_of((t0 + HALO - (CONV_K - 1) + q) * ng, ng), ng), :]

            xs = [dcs(q) for q in range(fb - 1)]
            accs = [None] * fb
            for j in range(CONV_K):
                wk = dw_ref[(CONV_K - 1 - j) * ng:(CONV_K - j) * ng, :]
                xs.append(dcs(j + fb - 1))
                accs = [wk * xs[q + j] if accs[q] is None else accs[q] + wk * xs[q + j] for q in range(fb)]
            for q in range(fb):
                du3[pl.ds(pl.multiple_of((t0 + q) * ng, ng), ng), :] = accs[q]
            own = xs[0:fb]
            ys = [us(q) for q in range(fb - 1)]
            for k in range(CONV_K):
                ys.append(us(k + fb - 1))
                prods = [own[q] * ys[q + k] for q in range(fb)]
                while len(prods) > 1:
                    prods = [prods[2 * v] + prods[2 * v + 1] for v in range(len(prods) // 2)]
                ddw_ref[k * ng:(k + 1) * ng, :] += prods[0]
            return carry

        lax.fori_loop(0, tm // fb, fir, 0)
        dc3[tm * ng:(tm + HALO) * ng, :] = dc3[0:HALO * ng, :]

        def back(rci, carry):
            base = pl.multiple_of(rci * rc, rc)
            rows = pl.ds(base, rc)
            a = p_ref[rows, 0:e].astype(F32)
            b = p_ref[rows, e:2 * e].astype(F32)
            sb = _sig(b)
            duv = _from_token_tiles(du3, base, rc, ng)
            dp_ref[rows, 0:e] = (duv * sb).astype(BF16)
            dp_ref[rows, e:2 * e] = (duv * a * sb * (1.0 - sb)).astype(BF16)
            return carry

        lax.fori_loop(0, tm // rc, back, 0, unroll=2)

    vec = pl.BlockSpec((1, e), lambda i: (0, 0))
    taps = pl.BlockSpec((CONV_K * ng, LANES), lambda i: (0, 0))
    rev = lambda i: (nt - 1 - i, 0)
    halo = lambda i: (jnp.maximum((nt - 1 - i) * hb - 1, 0), 0)
    return pl.pallas_call(
        body, name="conv_mid_bwd", grid=(nt,),
        in_specs=[pl.BlockSpec((tm, 3 * e), rev), pl.BlockSpec((HALO, 3 * e), halo),
                  pl.BlockSpec((tm, e), rev), pl.BlockSpec((tm, e), rev), taps, vec, vec],
        out_specs=[pl.BlockSpec((tm, 3 * e), rev), vec, vec, vec, taps],
        out_shape=[jax.ShapeDtypeStruct((t, 3 * e), BF16), jax.ShapeDtypeStruct((1, e), F32),
                   jax.ShapeDtypeStruct((1, e), F32), jax.ShapeDtypeStruct((1, e), F32),
                   jax.ShapeDtypeStruct((CONV_K * ng, LANES), F32)],
        scratch_shapes=[pltpu.VMEM(((tm + HALO) * ng, LANES), F32), pltpu.VMEM(((tm + HALO) * ng, LANES), F32),
                        pltpu.VMEM((tm * ng, LANES), F32)],
        compiler_params=_params("arbitrary"),
    )(p, p, c, ds, dw3, lg, lb)


def _pool_mid_bwd(p, ds, wg, bg, sc):
    t = p.shape[0]
    e = p.shape[1] // 2
    ng = len(POOL_WINDOWS)
    gc = e // ng
    tm = _tile(t)
    nt = t // tm
    rc = ROW_CHUNK
    hb = tm // PHALO

    def body(p_ref, ph_ref, ds_ref, wg_ref, bg_ref, sc_ref,
             dp_ref, dwg_ref, dbg_ref, dsc_ref, ubuf, dbuf, ebuf, ddbuf):
        i = pl.program_id(0)
        ti = nt - 1 - i

        @pl.when(i == 0)
        def _():
            ebuf[tm:tm + PHALO, :] = jnp.zeros((PHALO, e), F32)
            dwg_ref[...] = jnp.zeros_like(dwg_ref)
            dbg_ref[...] = jnp.zeros_like(dbg_ref)
            dsc_ref[...] = jnp.zeros_like(dsc_ref)

        ubuf[0:PHALO, :] = jnp.where(ti > 0, ph_ref[:, 0:e].astype(F32), 0.0)
        ubuf[PHALO:PHALO + tm, :] = p_ref[:, 0:e].astype(F32)

        def recompute(rci, carry):
            base = pl.multiple_of(rci * rc, rc)
            _pool_window_dev(ubuf, base, rc, e, ti * tm, dbuf)
            return carry

        lax.fori_loop(0, tm // rc, recompute, 0)

        for g in range(ng):
            cols = slice(g * gc, (g + 1) * gc)
            dg = dbuf[:, cols].astype(BF16)
            q = jnp.dot(dg, wg_ref[g], preferred_element_type=F32) + bg_ref[:, cols]
            z = p_ref[:, e + g * gc:e + (g + 1) * gc].astype(F32)
            sz = _sig(z)
            dsv = ds_ref[:, cols].astype(F32)
            dz = dsv * (q * sc_ref[:, cols]) * _dsilu(z, sz)
            dp_ref[:, e + g * gc:e + (g + 1) * gc] = dz.astype(BF16)
            dy2 = dsv * (z * sz)
            dsc_ref[:, cols] += jnp.sum(dy2 * q, axis=0, keepdims=True)
            dq = dy2 * sc_ref[:, cols]
            dbg_ref[:, cols] += jnp.sum(dq, axis=0, keepdims=True)
            dqb = dq.astype(BF16)
            dwg_ref[g] += lax.dot_general(dg, dqb, (((0,), (0,)), ((), ())), preferred_element_type=F32)
            ddbuf[:, cols] = lax.dot_general(dqb, wg_ref[g], (((1,), (1,)), ((), ())),
                                             preferred_element_type=F32)

        def scale(rci, carry):
            base = pl.multiple_of(rci * rc, rc)
            for lc in range(e // LANES):
                lanes = slice(lc * LANES, (lc + 1) * LANES)
                w = POOL_WINDOWS[_pool_group(lc, e)]
                ebuf[pl.ds(base, rc), lanes] = (ddbuf[pl.ds(base, rc), lanes]
                                                * _pool_inv_count(ti * tm + base, rc, w))
            return carry

        lax.fori_loop(0, tm // rc, scale, 0)

        def chunk(rci, carry):
            base = pl.multiple_of(rci * rc, rc)
            n = rc + PHALO
            for lc in range(e // LANES):
                lanes = slice(lc * LANES, (lc + 1) * LANES)
                w = POOL_WINDOWS[_pool_group(lc, e)]
                acc = ebuf[pl.ds(base, n), lanes]
                step = 1
                while step < w:
                    acc = acc + pltpu.roll(acc, n - step, 0)
                    step *= 2
                du = acc[0:rc] - ddbuf[pl.ds(base, rc), lanes]
                dp_ref[pl.ds(base, rc), lanes] = du.astype(BF16)
            return carry

        lax.fori_loop(0, tm // rc, chunk, 0)
        ebuf[tm:tm + PHALO, :] = ebuf[0:PHALO, :]

    vec = pl.BlockSpec((1, e), lambda i: (0, 0))
    rev = lambda i: (nt - 1 - i, 0)
    halo = lambda i: (jnp.maximum((nt - 1 - i) * hb - 1, 0), 0)
    wspec = pl.BlockSpec((ng, gc, gc), lambda i: (0, 0, 0))
    return pl.pallas_call(
        body, name="pool_mid_bwd", grid=(nt,),
        in_specs=[pl.BlockSpec((tm, 2 * e), rev), pl.BlockSpec((PHALO, 2 * e), halo),
                  pl.BlockSpec((tm, e), rev), wspec, vec, vec],
        out_specs=[pl.BlockSpec((tm, 2 * e), rev), wspec, vec, vec],
        out_shape=[jax.ShapeDtypeStruct((t, 2 * e), BF16), jax.ShapeDtypeStruct((ng, gc, gc), F32),
                   jax.ShapeDtypeStruct((1, e), F32), jax.ShapeDtypeStruct((1, e), F32)],
        scratch_shapes=[pltpu.VMEM((tm + PHALO, e), F32), pltpu.VMEM((tm, e), F32),
                        pltpu.VMEM((tm + PHALO, e), F32), pltpu.VMEM((tm, e), F32)],
        compiler_params=_params("arbitrary"),
    )(p, p, ds, wg, bg, sc)


def _dhn_rms_bwd(dp, w4, h, g, dh_out, name):
    t, d = h.shape
    nk = w4.shape[-1]
    tm = _tile(t)

    def body(dp_ref, w_ref, h_ref, g_ref, dho_ref, dh_ref, dg_ref):
        i = pl.program_id(0)

        @pl.when(i == 0)
        def _():
            dg_ref[...] = jnp.zeros_like(dg_ref)

        dhn = jnp.zeros((tm, d), F32)
        for k in range(N_CHIPS):
            dhn = dhn + lax.dot_general(dp_ref[:, k * nk:(k + 1) * nk], w_ref[k], (((1,), (1,)), ((), ())),
                                        preferred_element_type=F32)
        hh = h_ref[...]
        r = lax.rsqrt(jnp.mean(hh * hh, axis=-1, keepdims=True) + RMS_EPS)
        hhat = hh * r
        tt = dhn * g_ref[...]
        dh_ref[...] = dho_ref[...] + r * (tt - hhat * jnp.mean(tt * hhat, axis=-1, keepdims=True))
        dg_ref[...] += jnp.sum(dhn * hhat, axis=0, keepdims=True)

    return pl.pallas_call(
        body, name=name, grid=(t // tm,),
        in_specs=[pl.BlockSpec((tm, N_CHIPS * nk), lambda i: (i, 0)),
                  pl.BlockSpec((N_CHIPS, d, nk), lambda i: (0, 0, 0)),
                  pl.BlockSpec((tm, d), lambda i: (i, 0)), pl.BlockSpec((1, d), lambda i: (0, 0)),
                  pl.BlockSpec((tm, d), lambda i: (i, 0))],
        out_specs=[pl.BlockSpec((tm, d), lambda i: (i, 0)), pl.BlockSpec((1, d), lambda i: (0, 0))],
        out_shape=[jax.ShapeDtypeStruct((t, d), F32), jax.ShapeDtypeStruct((1, d), F32)],
        compiler_params=_params("arbitrary"),
    )(dp, w4, h, g, dh_out)


def _wgrad(a, b, nblk, name):
    t, m = a.shape
    nn = b.shape[1] // nblk
    tk = _tile(t)

    def body(a_ref, b_ref, o_ref):
        @pl.when(pl.program_id(1) == 0)
        def _():
            o_ref[...] = jnp.zeros_like(o_ref)

        o_ref[...] += lax.dot_general(a_ref[...].astype(BF16), b_ref[...].astype(BF16),
                                      (((0,), (0,)), ((), ())), preferred_element_type=F32)

    return pl.pallas_call(
        body, name=name, grid=(nblk, t // tk),
        in_specs=[pl.BlockSpec((tk, m), lambda j, i: (i, 0)), pl.BlockSpec((tk, nn), lambda j, i: (i, j))],
        out_specs=pl.BlockSpec((None, m, nn), lambda j, i: (j, 0, 0)),
        out_shape=jax.ShapeDtypeStruct((nblk, m, nn), F32),
        compiler_params=_params("parallel", "arbitrary"),
    )(a, b)


def _rows2d(shape):
    rows = 1
    for s in shape[:-1]:
        rows *= s
    return rows, shape[-1]


def _row_tile(rows):
    for cand in (512, 256, 128, 64, 32, 16, 8):
        if rows % cand == 0:
            return cand
    return rows


def _add_pair(a2, sel, r1, name):
    _, rows, cols = a2.shape
    tr = _row_tile(rows)

    def body(sel_ref, a_ref, r_ref, o_ref):
        o_ref[...] = (a_ref[...] + r_ref[...]).astype(BF16)

    return pl.pallas_call(
        body, name=name,
        grid_spec=pltpu.PrefetchScalarGridSpec(
            num_scalar_prefetch=1, grid=(rows // tr,),
            in_specs=[pl.BlockSpec((None, tr, cols), lambda i, s: (s[0], i, 0)),
                      pl.BlockSpec((tr, cols), lambda i, s: (i, 0))],
            out_specs=pl.BlockSpec((tr, cols), lambda i, s: (i, 0))),
        out_shape=jax.ShapeDtypeStruct((rows, cols), BF16),
        compiler_params=_params("parallel"),
    )(sel, a2, r1)


def _add_four(s1, r2, chip_core, name):
    _, rows, cols = s1.shape
    tr = _row_tile(rows)

    def body(sel_ref, s_ref, r_ref, o_ref):
        o_ref[...] = ((s_ref[...].astype(F32) + r_ref[0].astype(F32)) + r_ref[1].astype(F32)) + r_ref[2].astype(F32)

    return pl.pallas_call(
        body, name=name,
        grid_spec=pltpu.PrefetchScalarGridSpec(
            num_scalar_prefetch=1, grid=(rows // tr,),
            in_specs=[pl.BlockSpec((None, tr, cols), lambda i, s: (s[0], i, 0)),
                      pl.BlockSpec((3, tr, cols), lambda i, s: (0, i, 0))],
            out_specs=pl.BlockSpec((None, tr, cols), lambda i, s: (s[1], i, 0))),
        out_shape=jax.ShapeDtypeStruct((2, rows, cols), F32),
        compiler_params=_params("parallel"),
    )(chip_core, s1, r2)


def _adamw(w, g, m, v, name):
    rows, cols = w.shape
    tr = _row_tile(rows)

    def body(w_ref, g_ref, m_ref, v_ref, d_ref, m2_ref, v2_ref):
        gg = g_ref[...]
        m2 = ADAM_B1 * m_ref[...] + (1.0 - ADAM_B1) * gg
        v2 = ADAM_B2 * v_ref[...] + (1.0 - ADAM_B2) * (gg * gg)
        m_hat = m2 / (1.0 - ADAM_B1 ** ADAM_STEP)
        v_hat = v2 / (1.0 - ADAM_B2 ** ADAM_STEP)
        d_ref[...] = -ADAM_LR * (m_hat / (jnp.sqrt(v_hat) + ADAM_EPS) + ADAM_WD * w_ref[...])
        m2_ref[...] = m2
        v2_ref[...] = v2

    spec = pl.BlockSpec((tr, cols), lambda i: (i, 0))
    shp = jax.ShapeDtypeStruct((rows, cols), F32)
    return pl.pallas_call(
        body, name=name, grid=(rows // tr,),
        in_specs=[spec, spec, spec, spec], out_specs=[spec, spec, spec], out_shape=[shp, shp, shp],
        compiler_params=_params("parallel"),
    )(w, g, m, v)


ANY = pl.BlockSpec(memory_space=pl.ANY)


def _place():
    x, y, c = lax.axis_index("x"), lax.axis_index("y"), lax.axis_index("c")
    chips = [(1 - x, y), (x, 1 - y), (1 - x, 1 - y)]
    return x, y, c, chips


def _allgather_weights(shards):
    n = len(shards)

    def body(*refs):
        ins, outs = refs[:n], refs[n:2 * n]
        send_ici, recv_ici, send_d2d, recv_d2d, loc_sem = refs[2 * n:]
        x, y, c, chips = _place()
        k0 = 2 * x + y
        sib = (x, y, 1 - c)

        locs = [pltpu.make_async_copy(ins[a], outs[a].at[k0], loc_sem.at[a]) for a in range(n)]
        for cp in locs:
            cp.start()

        def ici(a, r, src_chip, target):
            return pltpu.make_async_remote_copy(
                src_ref=ins[a].at[c], dst_ref=outs[a].at[src_chip, c],
                send_sem=send_ici.at[a * 3 + r], recv_sem=recv_ici.at[a * 3 + r],
                device_id=target, device_id_type=MESH)

        def d2d(a, r, src_chip, layer):
            return pltpu.make_async_remote_copy(
                src_ref=outs[a].at[src_chip, layer], dst_ref=outs[a].at[src_chip, layer],
                send_sem=send_d2d.at[a * 3 + r], recv_sem=recv_d2d.at[a * 3 + r],
                device_id=sib, device_id_type=MESH)

        first = [ici(a, r, k0, (cx, cy, c)) for a in range(n) for r, (cx, cy) in enumerate(chips)]
        for cp in first:
            cp.start()
        passed = []
        for a in range(n):
            for r, (cx, cy) in enumerate(chips):
                ici(a, r, 2 * cx + cy, (cx, cy, c)).wait_recv()
                cp = d2d(a, r, 2 * cx + cy, c)
                cp.start()
                passed.append(cp)
        for a in range(n):
            for r, (cx, cy) in enumerate(chips):
                d2d(a, r, 2 * cx + cy, 1 - c).wait_recv()
        for cp in first + passed:
            cp.wait_send()
        for cp in locs:
            cp.wait()

    return pl.pallas_call(
        body, name="allgather_weights",
        in_specs=[ANY] * n, out_specs=[ANY] * n,
        out_shape=[jax.ShapeDtypeStruct((N_CHIPS,) + s.shape, s.dtype) for s in shards],
        scratch_shapes=[pltpu.SemaphoreType.DMA((3 * n,)), pltpu.SemaphoreType.DMA((3 * n,)),
                        pltpu.SemaphoreType.DMA((3 * n,)), pltpu.SemaphoreType.DMA((3 * n,)),
                        pltpu.SemaphoreType.DMA((n,))],
    )(*shards)


def _swap_layers(grads):
    n = len(grads)

    def body(*refs):
        ins, outs = refs[:n], refs[n:2 * n]
        send_sem, recv_sem = refs[2 * n:]
        x, y, c, _ = _place()
        cps = [pltpu.make_async_remote_copy(
            src_ref=ins[a].at[1 - c], dst_ref=outs[a], send_sem=send_sem.at[a], recv_sem=recv_sem.at[a],
            device_id=(x, y, 1 - c), device_id_type=MESH) for a in range(n)]
        for cp in cps:
            cp.start()
        for cp in cps:
            cp.wait()

    return pl.pallas_call(
        body, name="swap_layers",
        in_specs=[ANY] * n, out_specs=[ANY] * n,
        out_shape=[jax.ShapeDtypeStruct(g.shape[1:], g.dtype) for g in grads],
        scratch_shapes=[pltpu.SemaphoreType.DMA((n,)), pltpu.SemaphoreType.DMA((n,))],
    )(*grads)


def _scatter_chips(sums):
    n = len(sums)

    def body(*refs):
        ins, outs = refs[:n], refs[n:2 * n]
        send_sem, recv_sem = refs[2 * n:]
        x, y, c, chips = _place()
        cps = [pltpu.make_async_remote_copy(
            src_ref=ins[a].at[2 * cx + cy], dst_ref=outs[a].at[r],
            send_sem=send_sem.at[a * 3 + r], recv_sem=recv_sem.at[a * 3 + r],
            device_id=(cx, cy, c), device_id_type=MESH)
            for a in range(n) for r, (cx, cy) in enumerate(chips)]
        for cp in cps:
            cp.start()
        for cp in cps:
            cp.wait()

    return pl.pallas_call(
        body, name="scatter_chips",
        in_specs=[ANY] * n, out_specs=[ANY] * n,
        out_shape=[jax.ShapeDtypeStruct((3,) + s.shape[1:], s.dtype) for s in sums],
        scratch_shapes=[pltpu.SemaphoreType.DMA((3 * n,)), pltpu.SemaphoreType.DMA((3 * n,))],
    )(*sums)


def _share_halves(halves):
    n = len(halves)

    def body(*refs):
        ins, outs = refs[:n], refs[n:2 * n]
        send_sem, recv_sem = refs[2 * n:]
        x, y, c, _ = _place()
        cps = [pltpu.make_async_remote_copy(
            src_ref=outs[a].at[c], dst_ref=outs[a].at[c], send_sem=send_sem.at[a], recv_sem=recv_sem.at[a],
            device_id=(x, y, 1 - c), device_id_type=MESH) for a in range(n)]
        for cp in cps:
            cp.start()
        for cp in cps:
            cp.wait()

    return pl.pallas_call(
        body, name="share_halves",
        in_specs=[ANY] * n, out_specs=[ANY] * n,
        out_shape=[jax.ShapeDtypeStruct(h.shape, h.dtype) for h in halves],
        input_output_aliases={a: a for a in range(n)},
        scratch_shapes=[pltpu.SemaphoreType.DMA((n,)), pltpu.SemaphoreType.DMA((n,))],
    )(*halves)


N_DEV = 8


def _allreduce_small(v):
    m, nc = v.shape

    def body(x_ref, out_ref, gat, send_sems, recv_sems, local_sem):
        x, y, c, chips = _place()
        me, sib = (x, y, c), (x, y, 1 - c)

        def rows(px, py, pc):
            return gat.at[pl.ds((4 * px + 2 * py + pc) * m, m), :]

        def copy(k, block, to, src=None):
            return pltpu.make_async_remote_copy(
                src_ref=rows(*block) if src is None else src, dst_ref=rows(*block),
                send_sem=send_sems.at[k], recv_sem=recv_sems.at[k], device_id=to, device_id_type=MESH)

        mine = pltpu.make_async_copy(x_ref, rows(*me), local_sem)
        mine.start()
        first = [copy(0, me, sib, src=x_ref)]
        first += [copy(1 + j, me, (*chip, c), src=x_ref) for j, chip in enumerate(chips)]
        for cp in first:
            cp.start()
        passed = [copy(4 + j, (*chip, c), sib) for j, chip in enumerate(chips)]
        for j, chip in enumerate(chips):
            copy(1 + j, (*chip, c), me).wait_recv()
            passed[j].start()
        copy(0, sib, me).wait_recv()
        for j, chip in enumerate(chips):
            copy(4 + j, (*chip, 1 - c), me).wait_recv()
        for cp in first + passed:
            cp.wait_send()
        mine.wait()
        acc = gat[0:m, :]
        for dev in range(1, N_DEV):
            acc = acc + gat[dev * m:(dev + 1) * m, :]
        out_ref[...] = acc

    return pl.pallas_call(
        body, name="allreduce_small",
        in_specs=[pl.BlockSpec(memory_space=pltpu.VMEM)],
        out_specs=pl.BlockSpec(memory_space=pltpu.VMEM),
        out_shape=jax.ShapeDtypeStruct((m, nc), F32),
        scratch_shapes=[pltpu.VMEM((N_DEV * m, nc), F32), pltpu.SemaphoreType.DMA((7,)),
                        pltpu.SemaphoreType.DMA((7,)), pltpu.SemaphoreType.DMA],
        compiler_params=pltpu.CompilerParams(vmem_limit_bytes=VMEM_LIMIT),
    )(v)


def _pad_rows(a, rows):
    return jnp.pad(a, ((0, rows - a.shape[0]), (0, 0)))


def kernel(x, norm_g, final_g, conv_w_in, conv_dw, conv_dw_b, conv_ln_g, conv_ln_b, conv_w_out, pool_w_in, pool_w_grp, pool_b_grp, pool_scale, pool_w_out, loss_target, m_norm_g, m_final_g, m_conv_w_in, m_conv_dw, m_conv_dw_b, m_conv_ln_g, m_conv_ln_b, m_conv_w_out, m_pool_w_in, m_pool_w_grp, m_pool_b_grp, m_pool_scale, m_pool_w_out, v_norm_g, v_final_g, v_conv_w_in, v_conv_dw, v_conv_dw_b, v_conv_ln_g, v_conv_ln_b, v_conv_w_out, v_pool_w_in, v_pool_w_grp, v_pool_b_grp, v_pool_scale, v_pool_w_out):
    t, d = x.shape[1], x.shape[2]
    e = conv_w_out.shape[2]
    ng = len(POOL_WINDOWS)
    gc = e // ng
    gcs = pool_w_grp.shape[2]
    ck = conv_dw.shape[1]
    es = conv_dw.shape[2]
    xi, yi, ci = lax.axis_index("x"), lax.axis_index("y"), lax.axis_index("c")
    chip = 2 * xi + yi

    small_rows = ck + 2
    small_pad = -(-small_rows // 8) * 8
    small = jnp.concatenate([conv_dw, pool_b_grp[:, None, :], pool_scale[:, None, :],
                             jnp.zeros((2, small_pad - small_rows, es), F32)], axis=1)
    g_cwi, g_cwo, g_pwi, g_pwg, g_pwo, g_small = _allgather_weights(
        [conv_w_in.astype(BF16), conv_w_out.astype(BF16), pool_w_in.astype(BF16),
         pool_w_grp.astype(BF16), pool_w_out.astype(BF16), small])
    smallf = jnp.transpose(g_small, (1, 2, 0, 3)).reshape(2, small_pad, N_CHIPS * es)
    wg_full = jnp.transpose(g_pwg, (1, 2, 0, 3, 4)).reshape(2, ng, gc, gc)

    h = x.reshape(t, d)
    tgt = loss_target.reshape(t, d)
    hs, saved = [], []
    for layer in range(4):
        j = layer // 2
        hs.append(h)
        gvec = norm_g[layer][None, :]
        if layer % 2 == 0:
            w_in4 = g_cwi[:, j]
            w_out = g_cwo[:, j].reshape(e, d)
            p, hn = _rms_matmul(h, gvec, w_in4, "rms_matmul_conv")
            dw_full = smallf[j, 0:ck]
            s, c = _conv_mid_fwd(p, dw_full.reshape(-1, LANES), conv_dw_b[j].reshape(-1, LANES),
                                 conv_ln_g[j][None, :], conv_ln_b[j][None, :])
            saved.append((p, hn, s, c, w_in4, w_out, dw_full))
        else:
            w_in4 = g_pwi[:, j]
            w_out = g_pwo[:, j].reshape(e, d)
            p, hn = _rms_matmul(h, gvec, w_in4, "rms_matmul_pool")
            bg_full = smallf[j, ck:ck + 1]
            sc_full = smallf[j, ck + 1:ck + 2]
            s = _pool_mid_fwd(p, wg_full[j], bg_full, sc_full)
            saved.append((p, hn, s, None, w_in4, w_out, (wg_full[j], bg_full, sc_full)))
        h = _matmul_res(h, s, w_out)

    dh, loss_part, dfg = _loss_head(h, final_g[None, :], tgt)
    loss = lax.psum(loss_part[0, 0], ("x", "y", "c"))

    dng = [None] * 4
    g_conv = [None, None]
    g_pool = [None, None]
    for layer in (3, 2, 1, 0):
        j = layer // 2
        p, hn, s, c, w_in4, w_out, extra = saved[layer]
        gvec = norm_g[layer][None, :]
        ds = _ds_matmul(dh, w_out)
        dw_out = _wgrad(s, dh, 1, "wgrad_out")[0].reshape(N_CHIPS, e // N_CHIPS, d)
        if layer % 2 == 0:
            dp, dlg, dlb, ddwb, ddw3 = _conv_mid_bwd(p, c, ds, extra.reshape(-1, LANES),
                                                     conv_ln_g[j][None, :], conv_ln_b[j][None, :])
            ddw = ddw3.reshape(ck, e)
            dw_in = _wgrad(hn, dp, N_CHIPS, "wgrad_in_conv")
            dh, dng[layer] = _dhn_rms_bwd(dp, w_in4, hs[layer], gvec, dh, "dhn_rms_bwd_conv")
            g_conv[j] = (dw_in, dw_out, dlg, dlb, ddwb, ddw)
        else:
            wg, bg_full, sc_full = extra
            dp, dwg, dbg, dsc = _pool_mid_bwd(p, ds, wg, bg_full, sc_full)
            dw_in = _wgrad(hn, dp, N_CHIPS, "wgrad_in_pool")
            dh, dng[layer] = _dhn_rms_bwd(dp, w_in4, hs[layer], gvec, dh, "dhn_rms_bwd_pool")
            dwg4 = jnp.transpose(dwg.reshape(ng, N_CHIPS, gcs, gc), (1, 0, 2, 3))
            g_pool[j] = (dw_in, dw_out, dwg4, dbg, dsc)
    grad_x = dh.reshape(x.shape)

    big = [jnp.stack([g_conv[0][0], g_conv[1][0]]),
           jnp.stack([g_conv[0][1], g_conv[1][1]]),
           jnp.stack([g_pool[0][0], g_pool[1][0]]),
           jnp.stack([g_pool[0][2], g_pool[1][2]]),
           jnp.stack([g_pool[0][1], g_pool[1][1]])]
    names = ["cwi", "cwo", "pwi", "pwg", "pwo"]
    recv1 = _swap_layers(big)
    sel_c = jnp.reshape(ci, (1,)).astype(jnp.int32)
    sel_kc = jnp.stack([chip, ci]).astype(jnp.int32)
    sums = []
    for a, r1, nm in zip(big, recv1, names):
        rows, cols = _rows2d(a.shape[1:])
        sums.append(_add_pair(a.reshape(2, rows, cols), sel_c, r1.reshape(rows, cols),
                              "add_pair_" + nm).reshape(a.shape[1:]))
    recv2 = _scatter_chips(sums)
    halves = []
    for s1, r2, nm in zip(sums, recv2, names):
        rows, cols = _rows2d(s1.shape[1:])
        halves.append(_add_four(s1.reshape(N_CHIPS, rows, cols), r2.reshape(3, rows, cols), sel_kc,
                                "add_four_" + nm).reshape((2,) + s1.shape[1:]))
    g_cwi_f, g_cwo_f, g_pwi_f, g_pwg_f, g_pwo_f = _share_halves(halves)

    rows_list = [dng[0], dng[1], dng[2], dng[3], dfg,
                 g_conv[0][4], g_conv[1][4], g_conv[0][2], g_conv[1][2], g_conv[0][3], g_conv[1][3],
                 g_pool[0][3], g_pool[1][3], g_pool[0][4], g_pool[1][4], g_conv[0][5], g_conv[1][5]]
    slab = jnp.concatenate(rows_list, axis=0)
    nrows = slab.shape[0]
    slab = _pad_rows(slab, -(-nrows // 8) * 8)
    tot = _allreduce_small(slab)
    g_norm_g = tot[0:4]
    g_final_g = tot[4]
    g_dwb = tot[5:7]
    g_lng = tot[7:9]
    g_lnb = tot[9:11]
    g_bg = lax.dynamic_slice_in_dim(tot[11:13], chip * es, es, axis=1)
    g_sc = lax.dynamic_slice_in_dim(tot[13:15], chip * es, es, axis=1)
    g_dw = lax.dynamic_slice_in_dim(tot[15:15 + 2 * ck].reshape(2, ck, e), chip * es, es, axis=2)

    def adam_nd(w, g, m, v, nm):
        rows, cols = _rows2d(w.shape)
        outs = _adamw(w.reshape(rows, cols), g.reshape(rows, cols), m.reshape(rows, cols),
                      v.reshape(rows, cols), "adamw_" + nm)
        return [o.reshape(w.shape) for o in outs]

    res = {}
    res["conv_w_in"] = (g_cwi_f, *adam_nd(conv_w_in, g_cwi_f, m_conv_w_in, v_conv_w_in, "cwi"))
    res["conv_w_out"] = (g_cwo_f, *adam_nd(conv_w_out, g_cwo_f, m_conv_w_out, v_conv_w_out, "cwo"))
    res["pool_w_in"] = (g_pwi_f, *adam_nd(pool_w_in, g_pwi_f, m_pool_w_in, v_pool_w_in, "pwi"))
    res["pool_w_grp"] = (g_pwg_f, *adam_nd(pool_w_grp, g_pwg_f, m_pool_w_grp, v_pool_w_grp, "pwg"))
    res["pool_w_out"] = (g_pwo_f, *adam_nd(pool_w_out, g_pwo_f, m_pool_w_out, v_pool_w_out, "pwo"))

    def pack(parts, rows_to):
        return _pad_rows(jnp.concatenate([q.reshape(-1, q.shape[-1]) for q in parts], axis=0), rows_to)

    rep_w = [norm_g, final_g[None, :], conv_dw_b, conv_ln_g, conv_ln_b]
    rep_g = [g_norm_g, g_final_g[None, :], g_dwb, g_lng, g_lnb]
    rep_m = [m_norm_g, m_final_g[None, :], m_conv_dw_b, m_conv_ln_g, m_conv_ln_b]
    rep_v = [v_norm_g, v_final_g[None, :], v_conv_dw_b, v_conv_ln_g, v_conv_ln_b]
    rep = _adamw(pack(rep_w, 16), pack(rep_g, 16), pack(rep_m, 16), pack(rep_v, 16), "adamw_rep")
    rep_names = ["norm_g", "final_g", "conv_dw_b", "conv_ln_g", "conv_ln_b"]
    rep_rows = [(0, 4), (4, 5), (5, 7), (7, 9), (9, 11)]
    for nm, (lo, hi), gq, wq in zip(rep_names, rep_rows, rep_g, rep_w):
        shape = (d,) if nm == "final_g" else wq.shape
        res[nm] = (gq.reshape(shape), *[o[lo:hi].reshape(shape) for o in rep])

    sh_w = [conv_dw, pool_b_grp, pool_scale]
    sh_g = [g_dw, g_bg, g_sc]
    sh_m = [m_conv_dw, m_pool_b_grp, m_pool_scale]
    sh_v = [v_conv_dw, v_pool_b_grp, v_pool_scale]
    sh_total = 2 * ck + 4
    sh_pad = -(-sh_total // 8) * 8
    shd = _adamw(pack(sh_w, sh_pad), pack(sh_g, sh_pad), pack(sh_m, sh_pad), pack(sh_v, sh_pad), "adamw_shard")
    sh_names = ["conv_dw", "pool_b_grp", "pool_scale"]
    sh_rows = [(0, 2 * ck), (2 * ck, 2 * ck + 2), (2 * ck + 2, 2 * ck + 4)]
    for nm, (lo, hi), gq, wq in zip(sh_names, sh_rows, sh_g, sh_w):
        res[nm] = (gq.reshape(wq.shape), *[o[lo:hi].reshape(wq.shape) for o in shd])

    order = ["norm_g", "final_g", "conv_w_in", "conv_dw", "conv_dw_b", "conv_ln_g", "conv_ln_b", "conv_w_out",
             "pool_w_in", "pool_w_grp", "pool_b_grp", "pool_scale", "pool_w_out"]
    outs = [loss, grad_x]
    for part in range(4):
        outs += [res[nm][part] for nm in order]
    return tuple(outs)
```

```python
import functools

import jax
import jax.numpy as jnp
from jax import lax
from jax.experimental import pallas as pl
from jax.experimental.pallas import tpu as pltpu

F32 = jnp.float32
BF16 = jnp.bfloat16
MESH = pl.DeviceIdType.MESH

RMS_EPS = 1e-6
LN_EPS = 1e-5
CONV_K = 31
HALO = 32
PHALO = 16
POOL_WINDOWS = (2, 4, 8, 16)
N_CHIPS = 4
LANES = 128
ROW_CHUNK = 32
FIR_BLOCK = 16
TOKEN_TILE = 512
MATMUL_TILE = 1024
WGRAD_TILE = 2048
VMEM_LIMIT = 56 * 1024 * 1024

ADAM_LR = 0.001
ADAM_B1 = 0.9
ADAM_B2 = 0.999
ADAM_EPS = 1e-08
ADAM_WD = 0.01
ADAM_STEP = 10


def _params(*sem):
    return pltpu.CompilerParams(dimension_semantics=sem, vmem_limit_bytes=VMEM_LIMIT)


def _sig(v):
    return 0.5 * jnp.tanh(0.5 * v) + 0.5


def _dsilu(v, sv):
    return sv * (1.0 + v * (1.0 - sv))


def _resident(shape):
    return pl.BlockSpec(shape, lambda *_: (0,) * len(shape), pipeline_mode=pl.Buffered(1))


def _tile(t):
    return min(TOKEN_TILE, t)


def _rms_matmul(h, g, w4, name):
    t, d = h.shape
    nk = w4.shape[-1]
    tm = min(MATMUL_TILE, t)

    def body(h_ref, g_ref, w_ref, p_ref, hn_ref):
        hh = h_ref[...]
        r = lax.rsqrt(jnp.mean(hh * hh, axis=-1, keepdims=True) + RMS_EPS)
        hn = (hh * r * g_ref[...]).astype(BF16)
        hn_ref[...] = hn
        for k in range(N_CHIPS):
            p_ref[:, k * nk:(k + 1) * nk] = jnp.dot(hn, w_ref[k], preferred_element_type=F32).astype(BF16)

    return pl.pallas_call(
        body, name=name, grid=(t // tm,),
        in_specs=[pl.BlockSpec((tm, d), lambda i: (i, 0)),
                  pl.BlockSpec((1, d), lambda i: (0, 0)),
                  _resident((N_CHIPS, d, nk))],
        out_specs=[pl.BlockSpec((tm, N_CHIPS * nk), lambda i: (i, 0)),
                   pl.BlockSpec((tm, d), lambda i: (i, 0))],
        out_shape=[jax.ShapeDtypeStruct((t, N_CHIPS * nk), BF16), jax.ShapeDtypeStruct((t, d), BF16)],
        compiler_params=_params("parallel"),
    )(h, g, w4)


def _to_token_tiles(ref, tok0, rows, val, ng):
    for j in range(ng):
        ref[pl.ds(tok0 * ng + j, rows, stride=ng), :] = val[:, j * LANES:(j + 1) * LANES]


def _from_token_tiles(ref, tok0, rows, ng):
    return jnp.concatenate([ref[pl.ds(tok0 * ng + j, rows, stride=ng), :] for j in range(ng)], axis=1)


def _conv_mid_fwd(p, dw3, dwb3, lg, lb):
    t = p.shape[0]
    e = p.shape[1] // 3
    ng = e // LANES
    tm = _tile(t)
    rc = ROW_CHUNK
    fb = FIR_BLOCK

    def body(p_ref, dw_ref, dwb_ref, lg_ref, lb_ref, s_ref, c_ref, u3, c3):
        i = pl.program_id(0)

        @pl.when(i == 0)
        def _():
            u3[0:HALO * ng, :] = jnp.zeros((HALO * ng, LANES), F32)

        def glu(rci, carry):
            base = pl.multiple_of(rci * rc, rc)
            a = p_ref[pl.ds(base, rc), 0:e].astype(F32)
            b = p_ref[pl.ds(base, rc), e:2 * e].astype(F32)
            _to_token_tiles(u3, HALO + base, rc, a * _sig(b), ng)
            return carry

        lax.fori_loop(0, tm // rc, glu, 0)

        def fir(bi, carry):
            t0 = bi * fb
            def x(q):
                return u3[pl.ds(pl.multiple_of((t0 + HALO - (CONV_K - 1) + q) * ng, ng), ng), :]

            xs = [x(q) for q in range(fb - 1)]
            accs = [dwb_ref[...]] * fb
            for k in range(CONV_K):
                wk = dw_ref[k * ng:(k + 1) * ng, :]
                xs.append(x(k + fb - 1))
                accs = [accs[q] + wk * xs[q + k] for q in range(fb)]
            for q in range(fb):
                c3[pl.ds(pl.multiple_of((t0 + q) * ng, ng), ng), :] = accs[q]
            return carry

        lax.fori_loop(0, tm // fb, fir, 0)
        u3[0:HALO * ng, :] = u3[tm * ng:(tm + HALO) * ng, :]

        def chunk(rci, carry):
            base = pl.multiple_of(rci * rc, rc)
            c = _from_token_tiles(c3, base, rc, ng)
            mu = jnp.mean(c, axis=-1, keepdims=True)
            cc = c - mu
            var = jnp.mean(cc * cc, axis=-1, keepdims=True)
            ln = cc * lax.rsqrt(var + LN_EPS) * lg_ref[...] + lb_ref[...]
            z = p_ref[pl.ds(base, rc), 2 * e:3 * e].astype(F32)
            s = (ln * _sig(ln)) * (z * _sig(z))
            s_ref[pl.ds(base, rc), :] = s.astype(BF16)
            c_ref[pl.ds(base, rc), :] = c.astype(BF16)
            return carry

        lax.fori_loop(0, tm // rc, chunk, 0, unroll=2)

    vec = pl.BlockSpec((1, e), lambda i: (0, 0))
    return pl.pallas_call(
        body, name="conv_mid_fwd", grid=(t // tm,),
        in_specs=[pl.BlockSpec((tm, 3 * e), lambda i: (i, 0)),
                  pl.BlockSpec((CONV_K * ng, LANES), lambda i: (0, 0)),
                  pl.BlockSpec((ng, LANES), lambda i: (0, 0)), vec, vec],
        out_specs=[pl.BlockSpec((tm, e), lambda i: (i, 0)), pl.BlockSpec((tm, e), lambda i: (i, 0))],
        out_shape=[jax.ShapeDtypeStruct((t, e), BF16), jax.ShapeDtypeStruct((t, e), BF16)],
        scratch_shapes=[pltpu.VMEM(((tm + HALO) * ng, LANES), F32), pltpu.VMEM((tm * ng, LANES), F32)],
        compiler_params=_params("arbitrary"),
    )(p, dw3, dwb3, lg, lb)


def _pool_group(lc, e):
    return (lc * LANES) // (e // len(POOL_WINDOWS))


def _pool_inv_count(row0, rows, w):
    tpos = row0 + lax.broadcasted_iota(jnp.int32, (rows, 1), 0)
    return 1.0 / jnp.minimum(tpos + 1, w).astype(F32)


def _pool_window_dev(ubuf, base, rc, e, row0, dbuf):
    n = rc + PHALO
    for lc in range(e // LANES):
        lanes = slice(lc * LANES, (lc + 1) * LANES)
        g = _pool_group(lc, e)
        w = POOL_WINDOWS[g]
        blk = ubuf[pl.ds(base, n), lanes]
        acc = blk
        step = 1
        while step < w:
            acc = acc + pltpu.roll(acc, step, 0)
            step *= 2
        win = acc[PHALO:n]
        tok = blk[PHALO:n]
        dbuf[pl.ds(base, rc), lanes] = win * _pool_inv_count(row0 + base, rc, w) - tok


def _pool_mid_fwd(p, wg, bg, sc):
    t = p.shape[0]
    e = p.shape[1] // 2
    gc = e // len(POOL_WINDOWS)
    tm = _tile(t)
    rc = ROW_CHUNK

    def body(p_ref, wg_ref, bg_ref, sc_ref, s_ref, ubuf, dbuf):
        i = pl.program_id(0)

        @pl.when(i == 0)
        def _():
            ubuf[0:PHALO, :] = jnp.zeros((PHALO, e), F32)

        ubuf[PHALO:PHALO + tm, :] = p_ref[:, 0:e].astype(F32)

        def chunk(rci, carry):
            base = pl.multiple_of(rci * rc, rc)
            _pool_window_dev(ubuf, base, rc, e, i * tm, dbuf)
            return carry

        lax.fori_loop(0, tm // rc, chunk, 0)
        ubuf[0:PHALO, :] = ubuf[tm:tm + PHALO, :]

        for g in range(len(POOL_WINDOWS)):
            cols = slice(g * gc, (g + 1) * gc)
            yg = jnp.dot(dbuf[:, cols].astype(BF16), wg_ref[g], preferred_element_type=F32)
            z = p_ref[:, e + g * gc:e + (g + 1) * gc].astype(F32)
            s = ((yg + bg_ref[:, cols]) * sc_ref[:, cols]) * (z * _sig(z))
            s_ref[:, cols] = s.astype(BF16)

    vec = pl.BlockSpec((1, e), lambda i: (0, 0))
    return pl.pallas_call(
        body, name="pool_mid_fwd", grid=(t // tm,),
        in_specs=[pl.BlockSpec((tm, 2 * e), lambda i: (i, 0)),
                  pl.BlockSpec((len(POOL_WINDOWS), gc, gc), lambda i: (0, 0, 0)), vec, vec],
        out_specs=pl.BlockSpec((tm, e), lambda i: (i, 0)),
        out_shape=jax.ShapeDtypeStruct((t, e), BF16),
        scratch_shapes=[pltpu.VMEM((tm + PHALO, e), F32), pltpu.VMEM((tm, e), F32)],
        compiler_params=_params("arbitrary"),
    )(p, wg, bg, sc)


def _matmul_res(h, s, w):
    t, d = h.shape
    e = s.shape[1]
    tm = min(MATMUL_TILE, t)

    def body(h_ref, s_ref, w_ref, o_ref):
        o_ref[...] = h_ref[...] + jnp.dot(s_ref[...], w_ref[...], preferred_element_type=F32)

    return pl.pallas_call(
        body, name="matmul_res", grid=(t // tm,),
        in_specs=[pl.BlockSpec((tm, d), lambda i: (i, 0)), pl.BlockSpec((tm, e), lambda i: (i, 0)),
                  _resident((e, d))],
        out_specs=pl.BlockSpec((tm, d), lambda i: (i, 0)),
        out_shape=jax.ShapeDtypeStruct((t, d), F32),
        compiler_params=_params("parallel"),
    )(h, s, w)


def _loss_head(h, fg, tgt):
    t, d = h.shape
    tm = min(MATMUL_TILE, t)

    def body(h_ref, g_ref, t_ref, dh_ref, loss_ref, dg_ref):
        i = pl.program_id(0)

        @pl.when(i == 0)
        def _():
            loss_ref[...] = jnp.zeros_like(loss_ref)
            dg_ref[...] = jnp.zeros_like(dg_ref)

        hh = h_ref[...]
        r = lax.rsqrt(jnp.mean(hh * hh, axis=-1, keepdims=True) + RMS_EPS)
        hhat = hh * r
        err = hhat * g_ref[...] - t_ref[...]
        per_tok = jnp.mean(err * err, axis=-1, keepdims=True)
        loss_ref[...] += 0.5 * jnp.sum(per_tok, axis=0, keepdims=True)
        dy = err * (1.0 / d)
        tt = dy * g_ref[...]
        dh_ref[...] = r * (tt - hhat * jnp.mean(tt * hhat, axis=-1, keepdims=True))
        dg_ref[...] += jnp.sum(dy * hhat, axis=0, keepdims=True)

    return pl.pallas_call(
        body, name="loss_head", grid=(t // tm,),
        in_specs=[pl.BlockSpec((tm, d), lambda i: (i, 0)), pl.BlockSpec((1, d), lambda i: (0, 0)),
                  pl.BlockSpec((tm, d), lambda i: (i, 0))],
        out_specs=[pl.BlockSpec((tm, d), lambda i: (i, 0)), pl.BlockSpec((1, LANES), lambda i: (0, 0)),
                   pl.BlockSpec((1, d), lambda i: (0, 0))],
        out_shape=[jax.ShapeDtypeStruct((t, d), F32), jax.ShapeDtypeStruct((1, LANES), F32),
                   jax.ShapeDtypeStruct((1, d), F32)],
        compiler_params=_params("arbitrary"),
    )(h, fg, tgt)


def _ds_matmul(dy, w):
    t, d = dy.shape
    e = w.shape[0]
    tm = min(MATMUL_TILE, t)

    def body(dy_ref, w_ref, ds_ref):
        ds_ref[...] = lax.dot_general(dy_ref[...].astype(BF16), w_ref[...], (((1,), (1,)), ((), ())),
                                      preferred_element_type=F32).astype(BF16)

    return pl.pallas_call(
        body, name="ds_matmul", grid=(t // tm,),
        in_specs=[pl.BlockSpec((tm, d), lambda i: (i, 0)), _resident((e, d))],
        out_specs=pl.BlockSpec((tm, e), lambda i: (i, 0)),
        out_shape=jax.ShapeDtypeStruct((t, e), BF16),
        compiler_params=_params("parallel"),
    )(dy, w)


def _conv_mid_bwd(p, c, ds, dw3, lg, lb):
    t = p.shape[0]
    e = p.shape[1] // 3
    ng = e // LANES
    tm = _tile(t)
    nt = t // tm
    rc = ROW_CHUNK
    fb = FIR_BLOCK
    hb = tm // HALO

    def body(p_ref, ph_ref, c_ref, ds_ref, dw_ref, lg_ref, lb_ref,
             dp_ref, dlg_ref, dlb_ref, ddwb_ref, ddw_ref, u3, dc3, du3):
        i = pl.program_id(0)
        ti = nt - 1 - i

        @pl.when(i == 0)
        def _():
            dc3[tm * ng:(tm + HALO) * ng, :] = jnp.zeros((HALO * ng, LANES), F32)
            dlg_ref[...] = jnp.zeros_like(dlg_ref)
            dlb_ref[...] = jnp.zeros_like(dlb_ref)
            ddwb_ref[...] = jnp.zeros_like(ddwb_ref)
            ddw_ref[...] = jnp.zeros_like(ddw_ref)

        ha = ph_ref[:, 0:e].astype(F32)
        hbb = ph_ref[:, e:2 * e].astype(F32)
        _to_token_tiles(u3, 0, HALO, jnp.where(ti > 0, ha * _sig(hbb), 0.0), ng)

        def front(rci, carry):
            slg, slb, sdwb = carry
            base = pl.multiple_of(rci * rc, rc)
            rows = pl.ds(base, rc)
            a = p_ref[rows, 0:e].astype(F32)
            b = p_ref[rows, e:2 * e].astype(F32)
            _to_token_tiles(u3, HALO + base, rc, a * _sig(b), ng)
            cv = c_ref[rows, :].astype(F32)
            mu = jnp.mean(cv, axis=-1, keepdims=True)
            cc = cv - mu
            var = jnp.mean(cc * cc, axis=-1, keepdims=True)
            rs = lax.rsqrt(var + LN_EPS)
            nn = cc * rs
            ln = nn * lg_ref[...] + lb_ref[...]
            z = p_ref[rows, 2 * e:3 * e].astype(F32)
            sz = _sig(z)
            sl = _sig(ln)
            dsv = ds_ref[rows, :].astype(F32)
            dln = dsv * (z * sz) * _dsilu(ln, sl)
            dz = dsv * (ln * sl) * _dsilu(z, sz)
            dp_ref[rows, 2 * e:3 * e] = dz.astype(BF16)
            dn = dln * lg_ref[...]
            dc = rs * (dn - jnp.mean(dn, axis=-1, keepdims=True)
                       - nn * jnp.mean(dn * nn, axis=-1, keepdims=True))
            _to_token_tiles(dc3, base, rc, dc, ng)
            return (slg + jnp.sum(dln * nn, axis=0, keepdims=True),
                    slb + jnp.sum(dln, axis=0, keepdims=True),
                    sdwb + jnp.sum(dc, axis=0, keepdims=True))

        zero = jnp.zeros((1, e), F32)
        slg, slb, sdwb = lax.fori_loop(0, tm // rc, front, (zero, zero, zero), unroll=2)
        dlg_ref[...] += slg
        dlb_ref[...] += slb
        ddwb_ref[...] += sdwb

        def fir(bi, carry):
            t0 = bi * fb

            def dcs(q):
                return dc3[pl.ds(pl.multiple_of((t0 + q) * ng, ng), ng), :]

            def us(q):
                return u3[pl.ds(pl.multiple_of((t0 + HALO - (CONV_K - 1) + q) * ng, ng), ng), :]

            xs = [dcs(q) for q in range(fb - 1)]
            accs = [None] * fb
            for j in range(CONV_K):
                wk = dw_ref[(CONV_K - 1 - j) * ng:(CONV_K - j) * ng, :]
                xs.append(dcs(j + fb - 1))
                accs = [wk * xs[q + j] if accs[q] is None else accs[q] + wk * xs[q + j] for q in range(fb)]
            for q in range(fb):
                du3[pl.ds(pl.multiple_of((t0 + q) * ng, ng), ng), :] = accs[q]
            own = xs[0:fb]
            ys = [us(q) for q in range(fb - 1)]
            for k in range(CONV_K):
                ys.append(us(k + fb - 1))
                prods = [own[q] * ys[q + k] for q in range(fb)]
                while len(prods) > 1:
                    prods = [prods[2 * v] + prods[2 * v + 1] for v in range(len(prods) // 2)]
                ddw_ref[k * ng:(k + 1) * ng, :] += prods[0]
            return carry

        lax.fori_loop(0, tm // fb, fir, 0)
        dc3[tm * ng:(tm + HALO) * ng, :] = dc3[0:HALO * ng, :]

        def back(rci, carry):
            base = pl.multiple_of(rci * rc, rc)
            rows = pl.ds(base, rc)
            a = p_ref[rows, 0:e].astype(F32)
            b = p_ref[rows, e:2 * e].astype(F32)
            sb = _sig(b)
            duv = _from_token_tiles(du3, base, rc, ng)
            dp_ref[rows, 0:e] = (duv * sb).astype(BF16)
            dp_ref[rows, e:2 * e] = (duv * a * sb * (1.0 - sb)).astype(BF16)
            return carry

        lax.fori_loop(0, tm // rc, back, 0, unroll=2)

    vec = pl.BlockSpec((1, e), lambda i: (0, 0))
    taps = pl.BlockSpec((CONV_K * ng, LANES), lambda i: (0, 0))
    rev = lambda i: (nt - 1 - i, 0)
    halo = lambda i: (jnp.maximum((nt - 1 - i) * hb - 1, 0), 0)
    return pl.pallas_call(
        body, name="conv_mid_bwd", grid=(nt,),
        in_specs=[pl.BlockSpec((tm, 3 * e), rev), pl.BlockSpec((HALO, 3 * e), halo),
                  pl.BlockSpec((tm, e), rev), pl.BlockSpec((tm, e), rev), taps, vec, vec],
        out_specs=[pl.BlockSpec((tm, 3 * e), rev), vec, vec, vec, taps],
        out_shape=[jax.ShapeDtypeStruct((t, 3 * e), BF16), jax.ShapeDtypeStruct((1, e), F32),
                   jax.ShapeDtypeStruct((1, e), F32), jax.ShapeDtypeStruct((1, e), F32),
                   jax.ShapeDtypeStruct((CONV_K * ng, LANES), F32)],
        scratch_shapes=[pltpu.VMEM(((tm + HALO) * ng, LANES), F32), pltpu.VMEM(((tm + HALO) * ng, LANES), F32),
                        pltpu.VMEM((tm * ng, LANES), F32)],
        compiler_params=_params("arbitrary"),
    )(p, p, c, ds, dw3, lg, lb)


def _pool_mid_bwd(p, ds, wg, bg, sc):
    t = p.shape[0]
    e = p.shape[1] // 2
    ng = len(POOL_WINDOWS)
    gc = e // ng
    tm = _tile(t)
    nt = t // tm
    rc = ROW_CHUNK
    hb = tm // PHALO

    def body(p_ref, ph_ref, ds_ref, wg_ref, bg_ref, sc_ref,
             dp_ref, dwg_ref, dbg_ref, dsc_ref, ubuf, dbuf, ebuf, ddbuf):
        i = pl.program_id(0)
        ti = nt - 1 - i

        @pl.when(i == 0)
        def _():
            ebuf[tm:tm + PHALO, :] = jnp.zeros((PHALO, e), F32)
            dwg_ref[...] = jnp.zeros_like(dwg_ref)
            dbg_ref[...] = jnp.zeros_like(dbg_ref)
            dsc_ref[...] = jnp.zeros_like(dsc_ref)

        ubuf[0:PHALO, :] = jnp.where(ti > 0, ph_ref[:, 0:e].astype(F32), 0.0)
        ubuf[PHALO:PHALO + tm, :] = p_ref[:, 0:e].astype(F32)

        def recompute(rci, carry):
            base = pl.multiple_of(rci * rc, rc)
            _pool_window_dev(ubuf, base, rc, e, ti * tm, dbuf)
            return carry

        lax.fori_loop(0, tm // rc, recompute, 0)

        for g in range(ng):
            cols = slice(g * gc, (g + 1) * gc)
            dg = dbuf[:, cols].astype(BF16)
            q = jnp.dot(dg, wg_ref[g], preferred_element_type=F32) + bg_ref[:, cols]
            z = p_ref[:, e + g * gc:e + (g + 1) * gc].astype(F32)
            sz = _sig(z)
            dsv = ds_ref[:, cols].astype(F32)
            dz = dsv * (q * sc_ref[:, cols]) * _dsilu(z, sz)
            dp_ref[:, e + g * gc:e + (g + 1) * gc] = dz.astype(BF16)
            dy2 = dsv * (z * sz)
            dsc_ref[:, cols] += jnp.sum(dy2 * q, axis=0, keepdims=True)
            dq = dy2 * sc_ref[:, cols]
            dbg_ref[:, cols] += jnp.sum(dq, axis=0, keepdims=True)
            dqb = dq.astype(BF16)
            dwg_ref[g] += lax.dot_general(dg, dqb, (((0,), (0,)), ((), ())), preferred_element_type=F32)
            ddbuf[:, cols] = lax.dot_general(dqb, wg_ref[g], (((1,), (1,)), ((), ())),
                                             preferred_element_type=F32)

        def scale(rci, carry):
            base = pl.multiple_of(rci * rc, rc)
            for lc in range(e // LANES):
                lanes = slice(lc * LANES, (lc + 1) * LANES)
                w = POOL_WINDOWS[_pool_group(lc, e)]
                ebuf[pl.ds(base, rc), lanes] = (ddbuf[pl.ds(base, rc), lanes]
                                                * _pool_inv_count(ti * tm + base, rc, w))
            return carry

        lax.fori_loop(0, tm // rc, scale, 0)

        def chunk(rci, carry):
            base = pl.multiple_of(rci * rc, rc)
            n = rc + PHALO
            for lc in range(e // LANES):
                lanes = slice(lc * LANES, (lc + 1) * LANES)
                w = POOL_WINDOWS[_pool_group(lc, e)]
                acc = ebuf[pl.ds(base, n), lanes]
                step = 1
                while step < w:
                    acc = acc + pltpu.roll(acc, n - step, 0)
                    step *= 2
                du = acc[0:rc] - ddbuf[pl.ds(base, rc), lanes]
                dp_ref[pl.ds(base, rc), lanes] = du.astype(BF16)
            return carry

        lax.fori_loop(0, tm // rc, chunk, 0)
        ebuf[tm:tm + PHALO, :] = ebuf[0:PHALO, :]

    vec = pl.BlockSpec((1, e), lambda i: (0, 0))
    rev = lambda i: (nt - 1 - i, 0)
    halo = lambda i: (jnp.maximum((nt - 1 - i) * hb - 1, 0), 0)
    wspec = pl.BlockSpec((ng, gc, gc), lambda i: (0, 0, 0))
    return pl.pallas_call(
        body, name="pool_mid_bwd", grid=(nt,),
        in_specs=[pl.BlockSpec((tm, 2 * e), rev), pl.BlockSpec((PHALO, 2 * e), halo),
                  pl.BlockSpec((tm, e), rev), wspec, vec, vec],
        out_specs=[pl.BlockSpec((tm, 2 * e), rev), wspec, vec, vec],
        out_shape=[jax.ShapeDtypeStruct((t, 2 * e), BF16), jax.ShapeDtypeStruct((ng, gc, gc), F32),
                   jax.ShapeDtypeStruct((1, e), F32), jax.ShapeDtypeStruct((1, e), F32)],
        scratch_shapes=[pltpu.VMEM((tm + PHALO, e), F32), pltpu.VMEM((tm, e), F32),
                        pltpu.VMEM((tm + PHALO, e), F32), pltpu.VMEM((tm, e), F32)],
        compiler_params=_params("arbitrary"),
    )(p, p, ds, wg, bg, sc)


def _dhn_rms_bwd(dp, w4, h, g, dh_out, name):
    t, d = h.shape
    nk = w4.shape[-1]
    tm = min(MATMUL_TILE, t)

    def body(dp_ref, w_ref, h_ref, g_ref, dho_ref, dh_ref, dg_ref):
        i = pl.program_id(0)

        @pl.when(i == 0)
        def _():
            dg_ref[...] = jnp.zeros_like(dg_ref)

        dhn = jnp.zeros((tm, d), F32)
        for k in range(N_CHIPS):
            dhn = dhn + lax.dot_general(dp_ref[:, k * nk:(k + 1) * nk], w_ref[k], (((1,), (1,)), ((), ())),
                                        preferred_element_type=F32)
        hh = h_ref[...]
        r = lax.rsqrt(jnp.mean(hh * hh, axis=-1, keepdims=True) + RMS_EPS)
        hhat = hh * r
        tt = dhn * g_ref[...]
        dh_ref[...] = dho_ref[...] + r * (tt - hhat * jnp.mean(tt * hhat, axis=-1, keepdims=True))
        dg_ref[...] += jnp.sum(dhn * hhat, axis=0, keepdims=True)

    return pl.pallas_call(
        body, name=name, grid=(t // tm,),
        in_specs=[pl.BlockSpec((tm, N_CHIPS * nk), lambda i: (i, 0)),
                  _resident((N_CHIPS, d, nk)),
                  pl.BlockSpec((tm, d), lambda i: (i, 0)), pl.BlockSpec((1, d), lambda i: (0, 0)),
                  pl.BlockSpec((tm, d), lambda i: (i, 0))],
        out_specs=[pl.BlockSpec((tm, d), lambda i: (i, 0)), pl.BlockSpec((1, d), lambda i: (0, 0))],
        out_shape=[jax.ShapeDtypeStruct((t, d), F32), jax.ShapeDtypeStruct((1, d), F32)],
        compiler_params=_params("arbitrary"),
    )(dp, w4, h, g, dh_out)


def _wgrad(a, b, nblk, name):
    t, m = a.shape
    nn = b.shape[1] // nblk
    tk = min(WGRAD_TILE, t)

    def body(a_ref, b_ref, o_ref):
        @pl.when(pl.program_id(1) == 0)
        def _():
            o_ref[...] = jnp.zeros_like(o_ref)

        o_ref[...] += lax.dot_general(a_ref[...].astype(BF16), b_ref[...].astype(BF16),
                                      (((0,), (0,)), ((), ())), preferred_element_type=F32)

    return pl.pallas_call(
        body, name=name, grid=(nblk, t // tk),
        in_specs=[pl.BlockSpec((tk, m), lambda j, i: (i, 0)), pl.BlockSpec((tk, nn), lambda j, i: (i, j))],
        out_specs=pl.BlockSpec((None, m, nn), lambda j, i: (j, 0, 0)),
        out_shape=jax.ShapeDtypeStruct((nblk, m, nn), F32),
        compiler_params=_params("parallel", "arbitrary"),
    )(a, b)


def _rows2d(shape):
    rows = 1
    for s in shape[:-1]:
        rows *= s
    return rows, shape[-1]


def _row_tile(rows):
    for cand in (512, 256, 128, 64, 32, 16, 8):
        if rows % cand == 0:
            return cand
    return rows


def _add_pair(a0, a1, sel, r1, name):
    rows, cols = a0.shape
    tr = _row_tile(rows)

    def body(sel_ref, a0_ref, a1_ref, r_ref, o_ref):
        @pl.when(sel_ref[0] == 0)
        def _():
            o_ref[...] = (a0_ref[...] + r_ref[...]).astype(BF16)

        @pl.when(sel_ref[0] == 1)
        def _():
            o_ref[...] = (a1_ref[...] + r_ref[...]).astype(BF16)

    spec = pl.BlockSpec((tr, cols), lambda i, s: (i, 0))
    return pl.pallas_call(
        body, name=name,
        grid_spec=pltpu.PrefetchScalarGridSpec(
            num_scalar_prefetch=1, grid=(rows // tr,), in_specs=[spec, spec, spec], out_specs=spec),
        out_shape=jax.ShapeDtypeStruct((rows, cols), BF16),
        compiler_params=_params("parallel"),
    )(sel, a0, a1, r1)


def _add_four(s1, r2, chip_core, name):
    _, rows, cols = s1.shape
    tr = _row_tile(rows)

    def body(sel_ref, s_ref, r_ref, o_ref):
        o_ref[...] = ((s_ref[...].astype(F32) + r_ref[0].astype(F32)) + r_ref[1].astype(F32)) + r_ref[2].astype(F32)

    return pl.pallas_call(
        body, name=name,
        grid_spec=pltpu.PrefetchScalarGridSpec(
            num_scalar_prefetch=1, grid=(rows // tr,),
            in_specs=[pl.BlockSpec((None, tr, cols), lambda i, s: (s[0], i, 0)),
                      pl.BlockSpec((3, tr, cols), lambda i, s: (0, i, 0))],
            out_specs=pl.BlockSpec((None, tr, cols), lambda i, s: (s[1], i, 0))),
        out_shape=jax.ShapeDtypeStruct((2, rows, cols), F32),
        compiler_params=_params("parallel"),
    )(chip_core, s1, r2)


def _adamw(w, g, m, v, name):
    rows, cols = w.shape
    tr = _row_tile(rows)

    def body(w_ref, g_ref, m_ref, v_ref, d_ref, m2_ref, v2_ref):
        gg = g_ref[...]
        m2 = ADAM_B1 * m_ref[...] + (1.0 - ADAM_B1) * gg
        v2 = ADAM_B2 * v_ref[...] + (1.0 - ADAM_B2) * (gg * gg)
        m_hat = m2 / (1.0 - ADAM_B1 ** ADAM_STEP)
        v_hat = v2 / (1.0 - ADAM_B2 ** ADAM_STEP)
        d_ref[...] = -ADAM_LR * (m_hat / (jnp.sqrt(v_hat) + ADAM_EPS) + ADAM_WD * w_ref[...])
        m2_ref[...] = m2
        v2_ref[...] = v2

    spec = pl.BlockSpec((tr, cols), lambda i: (i, 0))
    shp = jax.ShapeDtypeStruct((rows, cols), F32)
    return pl.pallas_call(
        body, name=name, grid=(rows // tr,),
        in_specs=[spec, spec, spec, spec], out_specs=[spec, spec, spec], out_shape=[shp, shp, shp],
        compiler_params=_params("parallel"),
    )(w, g, m, v)


ANY = pl.BlockSpec(memory_space=pl.ANY)


def _place():
    x, y, c = lax.axis_index("x"), lax.axis_index("y"), lax.axis_index("c")
    chips = [(1 - x, y), (x, 1 - y), (1 - x, 1 - y)]
    return x, y, c, chips


def _allgather_weights(shards):
    n = len(shards)

    def body(*refs):
        ins, outs = refs[:n], refs[n:2 * n]
        send_ici, recv_ici, send_d2d, recv_d2d, loc_sem = refs[2 * n:]
        x, y, c, chips = _place()
        k0 = 2 * x + y
        sib = (x, y, 1 - c)

        locs = [pltpu.make_async_copy(ins[a], outs[a].at[k0], loc_sem.at[a]) for a in range(n)]
        for cp in locs:
            cp.start()

        def ici(a, r, src_chip, target):
            return pltpu.make_async_remote_copy(
                src_ref=ins[a].at[c], dst_ref=outs[a].at[src_chip, c],
                send_sem=send_ici.at[a * 3 + r], recv_sem=recv_ici.at[a * 3 + r],
                device_id=target, device_id_type=MESH)

        def d2d(a, r, src_chip, layer):
            return pltpu.make_async_remote_copy(
                src_ref=outs[a].at[src_chip, layer], dst_ref=outs[a].at[src_chip, layer],
                send_sem=send_d2d.at[a * 3 + r], recv_sem=recv_d2d.at[a * 3 + r],
                device_id=sib, device_id_type=MESH)

        first = [ici(a, r, k0, (cx, cy, c)) for a in range(n) for r, (cx, cy) in enumerate(chips)]
        for cp in first:
            cp.start()
        passed = []
        for a in range(n):
            for r, (cx, cy) in enumerate(chips):
                ici(a, r, 2 * cx + cy, (cx, cy, c)).wait_recv()
                cp = d2d(a, r, 2 * cx + cy, c)
                cp.start()
                passed.append(cp)
        for a in range(n):
            for r, (cx, cy) in enumerate(chips):
                d2d(a, r, 2 * cx + cy, 1 - c).wait_recv()
        for cp in first + passed:
            cp.wait_send()
        for cp in locs:
            cp.wait()

    return pl.pallas_call(
        body, name="allgather_weights",
        in_specs=[ANY] * n, out_specs=[ANY] * n,
        out_shape=[jax.ShapeDtypeStruct((N_CHIPS,) + s.shape, s.dtype) for s in shards],
        scratch_shapes=[pltpu.SemaphoreType.DMA((3 * n,)), pltpu.SemaphoreType.DMA((3 * n,)),
                        pltpu.SemaphoreType.DMA((3 * n,)), pltpu.SemaphoreType.DMA((3 * n,)),
                        pltpu.SemaphoreType.DMA((n,))],
    )(*shards)


def _swap_layers(grads0, grads1):
    n = len(grads0)

    def body(*refs):
        ins0, ins1, outs = refs[:n], refs[n:2 * n], refs[2 * n:3 * n]
        send_sem, recv_sem = refs[3 * n:]
        x, y, c, _ = _place()

        def swap(srcs):
            cps = [pltpu.make_async_remote_copy(
                src_ref=srcs[a], dst_ref=outs[a], send_sem=send_sem.at[a], recv_sem=recv_sem.at[a],
                device_id=(x, y, 1 - c), device_id_type=MESH) for a in range(n)]
            for cp in cps:
                cp.start()
            for cp in cps:
                cp.wait()

        @pl.when(c == 0)
        def _():
            swap(ins1)

        @pl.when(c == 1)
        def _():
            swap(ins0)

    return pl.pallas_call(
        body, name="swap_layers",
        in_specs=[ANY] * (2 * n), out_specs=[ANY] * n,
        out_shape=[jax.ShapeDtypeStruct(g.shape, g.dtype) for g in grads0],
        scratch_shapes=[pltpu.SemaphoreType.DMA((n,)), pltpu.SemaphoreType.DMA((n,))],
    )(*grads0, *grads1)


def _scatter_chips(sums):
    n = len(sums)

    def body(*refs):
        ins, outs = refs[:n], refs[n:2 * n]
        send_sem, recv_sem = refs[2 * n:]
        x, y, c, chips = _place()
        cps = [pltpu.make_async_remote_copy(
            src_ref=ins[a].at[2 * cx + cy], dst_ref=outs[a].at[r],
            send_sem=send_sem.at[a * 3 + r], recv_sem=recv_sem.at[a * 3 + r],
            device_id=(cx, cy, c), device_id_type=MESH)
            for a in range(n) for r, (cx, cy) in enumerate(chips)]
        for cp in cps:
            cp.start()
        for cp in cps:
            cp.wait()

    return pl.pallas_call(
        body, name="scatter_chips",
        in_specs=[ANY] * n, out_specs=[ANY] * n,
        out_shape=[jax.ShapeDtypeStruct((3,) + s.shape[1:], s.dtype) for s in sums],
        scratch_shapes=[pltpu.SemaphoreType.DMA((3 * n,)), pltpu.SemaphoreType.DMA((3 * n,))],
    )(*sums)


def _share_halves(halves):
    n = len(halves)

    def body(*refs):
        ins, outs = refs[:n], refs[n:2 * n]
        send_sem, recv_sem = refs[2 * n:]
        x, y, c, _ = _place()
        cps = [pltpu.make_async_remote_copy(
            src_ref=outs[a].at[c], dst_ref=outs[a].at[c], send_sem=send_sem.at[a], recv_sem=recv_sem.at[a],
            device_id=(x, y, 1 - c), device_id_type=MESH) for a in range(n)]
        for cp in cps:
            cp.start()
        for cp in cps:
            cp.wait()

    return pl.pallas_call(
        body, name="share_halves",
        in_specs=[ANY] * n, out_specs=[ANY] * n,
        out_shape=[jax.ShapeDtypeStruct(h.shape, h.dtype) for h in halves],
        input_output_aliases={a: a for a in range(n)},
        scratch_shapes=[pltpu.SemaphoreType.DMA((n,)), pltpu.SemaphoreType.DMA((n,))],
    )(*halves)


N_DEV = 8


def _allreduce_small(v):
    m, nc = v.shape

    def body(x_ref, out_ref, gat, send_sems, recv_sems, local_sem):
        x, y, c, chips = _place()
        me, sib = (x, y, c), (x, y, 1 - c)

        def rows(px, py, pc):
            return gat.at[pl.ds((4 * px + 2 * py + pc) * m, m), :]

        def copy(k, block, to, src=None):
            return pltpu.make_async_remote_copy(
                src_ref=rows(*block) if src is None else src, dst_ref=rows(*block),
                send_sem=send_sems.at[k], recv_sem=recv_sems.at[k], device_id=to, device_id_type=MESH)

        mine = pltpu.make_async_copy(x_ref, rows(*me), local_sem)
        mine.start()
        first = [copy(0, me, sib, src=x_ref)]
        first += [copy(1 + j, me, (*chip, c), src=x_ref) for j, chip in enumerate(chips)]
        for cp in first:
            cp.start()
        passed = [copy(4 + j, (*chip, c), sib) for j, chip in enumerate(chips)]
        for j, chip in enumerate(chips):
            copy(1 + j, (*chip, c), me).wait_recv()
            passed[j].start()
        copy(0, sib, me).wait_recv()
        for j, chip in enumerate(chips):
            copy(4 + j, (*chip, 1 - c), me).wait_recv()
        for cp in first + passed:
            cp.wait_send()
        mine.wait()
        acc = gat[0:m, :]
        for dev in range(1, N_DEV):
            acc = acc + gat[dev * m:(dev + 1) * m, :]
        out_ref[...] = acc

    return pl.pallas_call(
        body, name="allreduce_small",
        in_specs=[pl.BlockSpec(memory_space=pltpu.VMEM)],
        out_specs=pl.BlockSpec(memory_space=pltpu.VMEM),
        out_shape=jax.ShapeDtypeStruct((m, nc), F32),
        scratch_shapes=[pltpu.VMEM((N_DEV * m, nc), F32), pltpu.SemaphoreType.DMA((7,)),
                        pltpu.SemaphoreType.DMA((7,)), pltpu.SemaphoreType.DMA],
        compiler_params=pltpu.CompilerParams(vmem_limit_bytes=VMEM_LIMIT),
    )(v)


def _pad_rows(a, rows):
    return jnp.pad(a, ((0, rows - a.shape[0]), (0, 0)))


def kernel(x, norm_g, final_g, conv_w_in, conv_dw, conv_dw_b, conv_ln_g, conv_ln_b, conv_w_out, pool_w_in, pool_w_grp, pool_b_grp, pool_scale, pool_w_out, loss_target, m_norm_g, m_final_g, m_conv_w_in, m_conv_dw, m_conv_dw_b, m_conv_ln_g, m_conv_ln_b, m_conv_w_out, m_pool_w_in, m_pool_w_grp, m_pool_b_grp, m_pool_scale, m_pool_w_out, v_norm_g, v_final_g, v_conv_w_in, v_conv_dw, v_conv_dw_b, v_conv_ln_g, v_conv_ln_b, v_conv_w_out, v_pool_w_in, v_pool_w_grp, v_pool_b_grp, v_pool_scale, v_pool_w_out):
    t, d = x.shape[1], x.shape[2]
    e = conv_w_out.shape[2]
    ng = len(POOL_WINDOWS)
    gc = e // ng
    gcs = pool_w_grp.shape[2]
    ck = conv_dw.shape[1]
    es = conv_dw.shape[2]
    xi, yi, ci = lax.axis_index("x"), lax.axis_index("y"), lax.axis_index("c")
    chip = 2 * xi + yi

    small_rows = ck + 2
    small_pad = -(-small_rows // 8) * 8
    small = jnp.concatenate([conv_dw, pool_b_grp[:, None, :], pool_scale[:, None, :],
                             jnp.zeros((2, small_pad - small_rows, es), F32)], axis=1)
    g_cwi, g_cwo, g_pwi, g_pwg, g_pwo, g_small = _allgather_weights(
        [conv_w_in.astype(BF16), conv_w_out.astype(BF16), pool_w_in.astype(BF16),
         pool_w_grp.astype(BF16), pool_w_out.astype(BF16), small])
    smallf = jnp.transpose(g_small, (1, 2, 0, 3)).reshape(2, small_pad, N_CHIPS * es)
    wg_full = jnp.transpose(g_pwg, (1, 2, 0, 3, 4)).reshape(2, ng, gc, gc)

    h = x.reshape(t, d)
    tgt = loss_target.reshape(t, d)
    hs, saved = [], []
    for layer in range(4):
        j = layer // 2
        hs.append(h)
        gvec = norm_g[layer][None, :]
        if layer % 2 == 0:
            w_in4 = g_cwi[:, j]
            w_out = g_cwo[:, j].reshape(e, d)
            p, hn = _rms_matmul(h, gvec, w_in4, "rms_matmul_conv")
            dw_full = smallf[j, 0:ck]
            s, c = _conv_mid_fwd(p, dw_full.reshape(-1, LANES), conv_dw_b[j].reshape(-1, LANES),
                                 conv_ln_g[j][None, :], conv_ln_b[j][None, :])
            saved.append((p, hn, s, c, w_in4, w_out, dw_full))
        else:
            w_in4 = g_pwi[:, j]
            w_out = g_pwo[:, j].reshape(e, d)
            p, hn = _rms_matmul(h, gvec, w_in4, "rms_matmul_pool")
            bg_full = smallf[j, ck:ck + 1]
            sc_full = smallf[j, ck + 1:ck + 2]
            s = _pool_mid_fwd(p, wg_full[j], bg_full, sc_full)
            saved.append((p, hn, s, None, w_in4, w_out, (wg_full[j], bg_full, sc_full)))
        h = _matmul_res(h, s, w_out)

    dh, loss_part, dfg = _loss_head(h, final_g[None, :], tgt)
    loss = lax.psum(loss_part[0, 0], ("x", "y", "c"))

    dng = [None] * 4
    g_conv = [None, None]
    g_pool = [None, None]
    for layer in (3, 2, 1, 0):
        j = layer // 2
        p, hn, s, c, w_in4, w_out, extra = saved[layer]
        gvec = norm_g[layer][None, :]
        ds = _ds_matmul(dh, w_out)
        dw_out = _wgrad(s, dh, 1, "wgrad_out")[0].reshape(N_CHIPS, e // N_CHIPS, d)
        if layer % 2 == 0:
            dp, dlg, dlb, ddwb, ddw3 = _conv_mid_bwd(p, c, ds, extra.reshape(-1, LANES),
                                                     conv_ln_g[j][None, :], conv_ln_b[j][None, :])
            ddw = ddw3.reshape(ck, e)
            dw_in = _wgrad(hn, dp, N_CHIPS, "wgrad_in_conv")
            dh, dng[layer] = _dhn_rms_bwd(dp, w_in4, hs[layer], gvec, dh, "dhn_rms_bwd_conv")
            g_conv[j] = (dw_in, dw_out, dlg, dlb, ddwb, ddw)
        else:
            wg, bg_full, sc_full = extra
            dp, dwg, dbg, dsc = _pool_mid_bwd(p, ds, wg, bg_full, sc_full)
            dw_in = _wgrad(hn, dp, N_CHIPS, "wgrad_in_pool")
            dh, dng[layer] = _dhn_rms_bwd(dp, w_in4, hs[layer], gvec, dh, "dhn_rms_bwd_pool")
            dwg4 = jnp.transpose(dwg.reshape(ng, N_CHIPS, gcs, gc), (1, 0, 2, 3))
            g_pool[j] = (dw_in, dw_out, dwg4, dbg, dsc)
    grad_x = dh.reshape(x.shape)

    big = [[g_conv[j][0], g_conv[j][1], g_pool[j][0], g_pool[j][2], g_pool[j][1]] for j in range(2)]
    names = ["cwi", "cwo", "pwi", "pwg", "pwo"]
    recv1 = _swap_layers(big[0], big[1])
    sel_c = jnp.reshape(ci, (1,)).astype(jnp.int32)
    sel_kc = jnp.stack([chip, ci]).astype(jnp.int32)
    sums = []
    for a0, a1, r1, nm in zip(big[0], big[1], recv1, names):
        rows, cols = _rows2d(a0.shape)
        sums.append(_add_pair(a0.reshape(rows, cols), a1.reshape(rows, cols), sel_c, r1.reshape(rows, cols),
                              "add_pair_" + nm).reshape(a0.shape))
    recv2 = _scatter_chips(sums)
    halves = []
    for s1, r2, nm in zip(sums, recv2, names):
        rows, cols = _rows2d(s1.shape[1:])
        halves.append(_add_four(s1.reshape(N_CHIPS, rows, cols), r2.reshape(3, rows, cols), sel_kc,
                                "add_four_" + nm).reshape((2,) + s1.shape[1:]))
    g_cwi_f, g_cwo_f, g_pwi_f, g_pwg_f, g_pwo_f = _share_halves(halves)

    rows_list = [dng[0], dng[1], dng[2], dng[3], dfg,
                 g_conv[0][4], g_conv[1][4], g_conv[0][2], g_conv[1][2], g_conv[0][3], g_conv[1][3],
                 g_pool[0][3], g_pool[1][3], g_pool[0][4], g_pool[1][4], g_conv[0][5], g_conv[1][5]]
    slab = jnp.concatenate(rows_list, axis=0)
    nrows = slab.shape[0]
    slab = _pad_rows(slab, -(-nrows // 8) * 8)
    tot = _allreduce_small(slab)
    g_norm_g = tot[0:4]
    g_final_g = tot[4]
    g_dwb = tot[5:7]
    g_lng = tot[7:9]
    g_lnb = tot[9:11]
    g_bg = lax.dynamic_slice_in_dim(tot[11:13], chip * es, es, axis=1)
    g_sc = lax.dynamic_slice_in_dim(tot[13:15], chip * es, es, axis=1)
    g_dw = lax.dynamic_slice_in_dim(tot[15:15 + 2 * ck].reshape(2, ck, e), chip * es, es, axis=2)

    def adam_nd(w, g, m, v, nm):
        rows, cols = _rows2d(w.shape)
        outs = _adamw(w.reshape(rows, cols), g.reshape(rows, cols), m.reshape(rows, cols),
                      v.reshape(rows, cols), "adamw_" + nm)
        return [o.reshape(w.shape) for o in outs]

    res = {}
    res["conv_w_in"] = (g_cwi_f, *adam_nd(conv_w_in, g_cwi_f, m_conv_w_in, v_conv_w_in, "cwi"))
    res["conv_w_out"] = (g_cwo_f, *adam_nd(conv_w_out, g_cwo_f, m_conv_w_out, v_conv_w_out, "cwo"))
    res["pool_w_in"] = (g_pwi_f, *adam_nd(pool_w_in, g_pwi_f, m_pool_w_in, v_pool_w_in, "pwi"))
    res["pool_w_grp"] = (g_pwg_f, *adam_nd(pool_w_grp, g_pwg_f, m_pool_w_grp, v_pool_w_grp, "pwg"))
    res["pool_w_out"] = (g_pwo_f, *adam_nd(pool_w_out, g_pwo_f, m_pool_w_out, v_pool_w_out, "pwo"))

    def pack(parts, rows_to):
        return _pad_rows(jnp.concatenate([q.reshape(-1, q.shape[-1]) for q in parts], axis=0), rows_to)

    rep_w = [norm_g, final_g[None, :], conv_dw_b, conv_ln_g, conv_ln_b]
    rep_g = [g_norm_g, g_final_g[None, :], g_dwb, g_lng, g_lnb]
    rep_m = [m_norm_g, m_final_g[None, :], m_conv_dw_b, m_conv_ln_g, m_conv_ln_b]
    rep_v = [v_norm_g, v_final_g[None, :], v_conv_dw_b, v_conv_ln_g, v_conv_ln_b]
    rep = _adamw(pack(rep_w, 16), pack(rep_g, 16), pack(rep_m, 16), pack(rep_v, 16), "adamw_rep")
    rep_names = ["norm_g", "final_g", "conv_dw_b", "conv_ln_g", "conv_ln_b"]
    rep_rows = [(0, 4), (4, 5), (5, 7), (7, 9), (9, 11)]
    for nm, (lo, hi), gq, wq in zip(rep_names, rep_rows, rep_g, rep_w):
        shape = (d,) if nm == "final_g" else wq.shape
        res[nm] = (gq.reshape(shape), *[o[lo:hi].reshape(shape) for o in rep])

    sh_w = [conv_dw, pool_b_grp, pool_scale]
    sh_g = [g_dw, g_bg, g_sc]
    sh_m = [m_conv_dw, m_pool_b_grp, m_pool_scale]
    sh_v = [v_conv_dw, v_pool_b_grp, v_pool_scale]
    sh_total = 2 * ck + 4
    sh_pad = -(-sh_total // 8) * 8
    shd = _adamw(pack(sh_w, sh_pad), pack(sh_g, sh_pad), pack(sh_m, sh_pad), pack(sh_v, sh_pad), "adamw_shard")
    sh_names = ["conv_dw", "pool_b_grp", "pool_scale"]
    sh_rows = [(0, 2 * ck), (2 * ck, 2 * ck + 2), (2 * ck + 2, 2 * ck + 4)]
    for nm, (lo, hi), gq, wq in zip(sh_names, sh_rows, sh_g, sh_w):
        res[nm] = (gq.reshape(wq.shape), *[o[lo:hi].reshape(wq.shape) for o in shd])

    order = ["norm_g", "final_g", "conv_w_in", "conv_dw", "conv_dw_b", "conv_ln_g", "conv_ln_b", "conv_w_out",
             "pool_w_in", "pool_w_grp", "pool_b_grp", "pool_scale", "pool_w_out"]
    outs = [loss, grad_x]
    for part in range(4):
        outs += [res[nm][part] for nm in order]
    return tuple(outs)
```

```python
import functools

import jax
import jax.numpy as jnp
from jax import lax
from jax.experimental import pallas as pl
from jax.experimental.pallas import tpu as pltpu

F32 = jnp.float32
BF16 = jnp.bfloat16
MESH = pl.DeviceIdType.MESH

RMS_EPS = 1e-6
LN_EPS = 1e-5
CONV_K = 31
HALO = 32
PHALO = 16
POOL_WINDOWS = (2, 4, 8, 16)
N_CHIPS = 4
LANES = 128
ROW_CHUNK = 32
CONV_ROW_CHUNK = 32
FIR_BLOCK = 16
TOKEN_TILE = 512
MATMUL_TILE = 1024
WGRAD_TILE = 2048
VMEM_LIMIT = 56 * 1024 * 1024

ADAM_LR = 0.001
ADAM_B1 = 0.9
ADAM_B2 = 0.999
ADAM_EPS = 1e-08
ADAM_WD = 0.01
ADAM_STEP = 10


def _params(*sem):
    return pltpu.CompilerParams(dimension_semantics=sem, vmem_limit_bytes=VMEM_LIMIT)


def _sig(v):
    return 0.5 * jnp.tanh(0.5 * v) + 0.5


def _dsilu(v, sv):
    return sv * (1.0 + v * (1.0 - sv))


def _resident(shape):
    return pl.BlockSpec(shape, lambda *_: (0,) * len(shape), pipeline_mode=pl.Buffered(1))


def _tile(t):
    return min(TOKEN_TILE, t)


def _rms_matmul(h, g, w4, name):
    t, d = h.shape
    nk = w4.shape[-1]
    tm = min(MATMUL_TILE, t)

    def body(h_ref, g_ref, w_ref, p_ref, hn_ref):
        hh = h_ref[...]
        r = lax.rsqrt(jnp.mean(hh * hh, axis=-1, keepdims=True) + RMS_EPS)
        hn = (hh * r * g_ref[...]).astype(BF16)
        hn_ref[...] = hn
        for k in range(N_CHIPS):
            p_ref[:, k * nk:(k + 1) * nk] = jnp.dot(hn, w_ref[k], preferred_element_type=F32).astype(BF16)

    return pl.pallas_call(
        body, name=name, grid=(t // tm,),
        in_specs=[pl.BlockSpec((tm, d), lambda i: (i, 0)),
                  pl.BlockSpec((1, d), lambda i: (0, 0)),
                  _resident((N_CHIPS, d, nk))],
        out_specs=[pl.BlockSpec((tm, N_CHIPS * nk), lambda i: (i, 0)),
                   pl.BlockSpec((tm, d), lambda i: (i, 0))],
        out_shape=[jax.ShapeDtypeStruct((t, N_CHIPS * nk), BF16), jax.ShapeDtypeStruct((t, d), BF16)],
        compiler_params=_params("parallel"),
    )(h, g, w4)


def _to_token_tiles(ref, tok0, rows, val, ng):
    for j in range(ng):
        ref[pl.ds(tok0 * ng + j, rows, stride=ng), :] = val[:, j * LANES:(j + 1) * LANES]


def _from_token_tiles(ref, tok0, rows, ng):
    return jnp.concatenate([ref[pl.ds(tok0 * ng + j, rows, stride=ng), :] for j in range(ng)], axis=1)


def _conv_mid_fwd(p, dw3, dwb3, lg, lb):
    t = p.shape[0]
    e = p.shape[1] // 3
    ng = e // LANES
    tm = _tile(t)
    rc = CONV_ROW_CHUNK
    fb = FIR_BLOCK

    def body(p_ref, dw_ref, dwb_ref, lg_ref, lb_ref, s_ref, c_ref, u3, c3):
        i = pl.program_id(0)

        @pl.when(i == 0)
        def _():
            u3[0:HALO * ng, :] = jnp.zeros((HALO * ng, LANES), F32)

        def glu(rci, carry):
            base = pl.multiple_of(rci * rc, rc)
            a = p_ref[pl.ds(base, rc), 0:e].astype(F32)
            b = p_ref[pl.ds(base, rc), e:2 * e].astype(F32)
            _to_token_tiles(u3, HALO + base, rc, a * _sig(b), ng)
            return carry

        lax.fori_loop(0, tm // rc, glu, 0)

        def fir(bi, carry):
            t0 = bi * fb
            def x(q):
                return u3[pl.ds(pl.multiple_of((t0 + HALO - (CONV_K - 1) + q) * ng, ng), ng), :]

            xs = [x(q) for q in range(fb - 1)]
            accs = [dwb_ref[...]] * fb
            for k in range(CONV_K):
                wk = dw_ref[k * ng:(k + 1) * ng, :]
                xs.append(x(k + fb - 1))
                accs = [accs[q] + wk * xs[q + k] for q in range(fb)]
            for q in range(fb):
                c3[pl.ds(pl.multiple_of((t0 + q) * ng, ng), ng), :] = accs[q]
            return carry

        lax.fori_loop(0, tm // fb, fir, 0)
        u3[0:HALO * ng, :] = u3[tm * ng:(tm + HALO) * ng, :]

        def chunk(rci, carry):
            base = pl.multiple_of(rci * rc, rc)
            c = _from_token_tiles(c3, base, rc, ng)
            mu = jnp.mean(c, axis=-1, keepdims=True)
            cc = c - mu
            var = jnp.mean(cc * cc, axis=-1, keepdims=True)
            ln = cc * lax.rsqrt(var + LN_EPS) * lg_ref[...] + lb_ref[...]
            z = p_ref[pl.ds(base, rc), 2 * e:3 * e].astype(F32)
            s = (ln * _sig(ln)) * (z * _sig(z))
            s_ref[pl.ds(base, rc), :] = s.astype(BF16)
            c_ref[pl.ds(base, rc), :] = c.astype(BF16)
            return carry

        lax.fori_loop(0, tm // rc, chunk, 0, unroll=2)

    vec = pl.BlockSpec((1, e), lambda i: (0, 0))
    return pl.pallas_call(
        body, name="conv_mid_fwd", grid=(t // tm,),
        in_specs=[pl.BlockSpec((tm, 3 * e), lambda i: (i, 0)),
                  pl.BlockSpec((CONV_K * ng, LANES), lambda i: (0, 0)),
                  pl.BlockSpec((ng, LANES), lambda i: (0, 0)), vec, vec],
        out_specs=[pl.BlockSpec((tm, e), lambda i: (i, 0)), pl.BlockSpec((tm, e), lambda i: (i, 0))],
        out_shape=[jax.ShapeDtypeStruct((t, e), BF16), jax.ShapeDtypeStruct((t, e), BF16)],
        scratch_shapes=[pltpu.VMEM(((tm + HALO) * ng, LANES), F32), pltpu.VMEM((tm * ng, LANES), F32)],
        compiler_params=_params("arbitrary"),
    )(p, dw3, dwb3, lg, lb)


def _pool_group(lc, e):
    return (lc * LANES) // (e // len(POOL_WINDOWS))


def _pool_inv_count(row0, rows, w):
    tpos = row0 + lax.broadcasted_iota(jnp.int32, (rows, 1), 0)
    return 1.0 / jnp.minimum(tpos + 1, w).astype(F32)


def _pool_window_dev(ubuf, base, rc, e, row0, dbuf):
    n = rc + PHALO
    for lc in range(e // LANES):
        lanes = slice(lc * LANES, (lc + 1) * LANES)
        g = _pool_group(lc, e)
        w = POOL_WINDOWS[g]
        blk = ubuf[pl.ds(base, n), lanes]
        acc = blk
        step = 1
        while step < w:
            acc = acc + pltpu.roll(acc, step, 0)
            step *= 2
        win = acc[PHALO:n]
        tok = blk[PHALO:n]
        dbuf[pl.ds(base, rc), lanes] = win * _pool_inv_count(row0 + base, rc, w) - tok


def _pool_mid_fwd(p, wg, bg, sc):
    t = p.shape[0]
    e = p.shape[1] // 2
    gc = e // len(POOL_WINDOWS)
    tm = _tile(t)
    rc = ROW_CHUNK

    def body(p_ref, wg_ref, bg_ref, sc_ref, s_ref, ubuf, dbuf):
        i = pl.program_id(0)

        @pl.when(i == 0)
        def _():
            ubuf[0:PHALO, :] = jnp.zeros((PHALO, e), F32)

        ubuf[PHALO:PHALO + tm, :] = p_ref[:, 0:e].astype(F32)

        def chunk(rci, carry):
            base = pl.multiple_of(rci * rc, rc)
            _pool_window_dev(ubuf, base, rc, e, i * tm, dbuf)
            return carry

        lax.fori_loop(0, tm // rc, chunk, 0)
        ubuf[0:PHALO, :] = ubuf[tm:tm + PHALO, :]

        for g in range(len(POOL_WINDOWS)):
            cols = slice(g * gc, (g + 1) * gc)
            yg = jnp.dot(dbuf[:, cols].astype(BF16), wg_ref[g], preferred_element_type=F32)
            z = p_ref[:, e + g * gc:e + (g + 1) * gc].astype(F32)
            s = ((yg + bg_ref[:, cols]) * sc_ref[:, cols]) * (z * _sig(z))
            s_ref[:, cols] = s.astype(BF16)

    vec = pl.BlockSpec((1, e), lambda i: (0, 0))
    return pl.pallas_call(
        body, name="pool_mid_fwd", grid=(t // tm,),
        in_specs=[pl.BlockSpec((tm, 2 * e), lambda i: (i, 0)),
                  pl.BlockSpec((len(POOL_WINDOWS), gc, gc), lambda i: (0, 0, 0)), vec, vec],
        out_specs=pl.BlockSpec((tm, e), lambda i: (i, 0)),
        out_shape=jax.ShapeDtypeStruct((t, e), BF16),
        scratch_shapes=[pltpu.VMEM((tm + PHALO, e), F32), pltpu.VMEM((tm, e), F32)],
        compiler_params=_params("arbitrary"),
    )(p, wg, bg, sc)


def _matmul_res(h, s, w):
    t, d = h.shape
    e = s.shape[1]
    tm = min(MATMUL_TILE, t)

    def body(h_ref, s_ref, w_ref, o_ref):
        o_ref[...] = h_ref[...] + jnp.dot(s_ref[...], w_ref[...], preferred_element_type=F32)

    return pl.pallas_call(
        body, name="matmul_res", grid=(t // tm,),
        in_specs=[pl.BlockSpec((tm, d), lambda i: (i, 0)), pl.BlockSpec((tm, e), lambda i: (i, 0)),
                  _resident((e, d))],
        out_specs=pl.BlockSpec((tm, d), lambda i: (i, 0)),
        out_shape=jax.ShapeDtypeStruct((t, d), F32),
        compiler_params=_params("parallel"),
    )(h, s, w)


def _loss_head(h, fg, tgt):
    t, d = h.shape
    tm = min(MATMUL_TILE, t)

    def body(h_ref, g_ref, t_ref, dh_ref, loss_ref, dg_ref):
        i = pl.program_id(0)

        @pl.when(i == 0)
        def _():
            loss_ref[...] = jnp.zeros_like(loss_ref)
            dg_ref[...] = jnp.zeros_like(dg_ref)

        hh = h_ref[...]
        r = lax.rsqrt(jnp.mean(hh * hh, axis=-1, keepdims=True) + RMS_EPS)
        hhat = hh * r
        err = hhat * g_ref[...] - t_ref[...]
        per_tok = jnp.mean(err * err, axis=-1, keepdims=True)
        loss_ref[...] += 0.5 * jnp.sum(per_tok, axis=0, keepdims=True)
        dy = err * (1.0 / d)
        tt = dy * g_ref[...]
        dh_ref[...] = r * (tt - hhat * jnp.mean(tt * hhat, axis=-1, keepdims=True))
        dg_ref[...] += jnp.sum(dy * hhat, axis=0, keepdims=True)

    return pl.pallas_call(
        body, name="loss_head", grid=(t // tm,),
        in_specs=[pl.BlockSpec((tm, d), lambda i: (i, 0)), pl.BlockSpec((1, d), lambda i: (0, 0)),
                  pl.BlockSpec((tm, d), lambda i: (i, 0))],
        out_specs=[pl.BlockSpec((tm, d), lambda i: (i, 0)), pl.BlockSpec((1, LANES), lambda i: (0, 0)),
                   pl.BlockSpec((1, d), lambda i: (0, 0))],
        out_shape=[jax.ShapeDtypeStruct((t, d), F32), jax.ShapeDtypeStruct((1, LANES), F32),
                   jax.ShapeDtypeStruct((1, d), F32)],
        compiler_params=_params("arbitrary"),
    )(h, fg, tgt)


def _ds_matmul(dy, w):
    t, d = dy.shape
    e = w.shape[0]
    tm = min(MATMUL_TILE, t)

    def body(dy_ref, w_ref, ds_ref):
        ds_ref[...] = lax.dot_general(dy_ref[...].astype(BF16), w_ref[...], (((1,), (1,)), ((), ())),
                                      preferred_element_type=F32).astype(BF16)

    return pl.pallas_call(
        body, name="ds_matmul", grid=(t // tm,),
        in_specs=[pl.BlockSpec((tm, d), lambda i: (i, 0)), _resident((e, d))],
        out_specs=pl.BlockSpec((tm, e), lambda i: (i, 0)),
        out_shape=jax.ShapeDtypeStruct((t, e), BF16),
        compiler_params=_params("parallel"),
    )(dy, w)


def _conv_mid_bwd(p, c, ds, dw3, lg, lb):
    t = p.shape[0]
    e = p.shape[1] // 3
    ng = e // LANES
    tm = _tile(t)
    nt = t // tm
    rc = CONV_ROW_CHUNK
    fb = FIR_BLOCK
    hb = tm // HALO

    def body(p_ref, ph_ref, c_ref, ds_ref, dw_ref, lg_ref, lb_ref,
             dp_ref, dlg_ref, dlb_ref, ddwb_ref, ddw_ref, u3, dc3, du3):
        i = pl.program_id(0)
        ti = nt - 1 - i

        @pl.when(i == 0)
        def _():
            dc3[tm * ng:(tm + HALO) * ng, :] = jnp.zeros((HALO * ng, LANES), F32)
            dlg_ref[...] = jnp.zeros_like(dlg_ref)
            dlb_ref[...] = jnp.zeros_like(dlb_ref)
            ddwb_ref[...] = jnp.zeros_like(ddwb_ref)
            ddw_ref[...] = jnp.zeros_like(ddw_ref)

        ha = ph_ref[:, 0:e].astype(F32)
        hbb = ph_ref[:, e:2 * e].astype(F32)
        _to_token_tiles(u3, 0, HALO, jnp.where(ti > 0, ha * _sig(hbb), 0.0), ng)

        def front(rci, carry):
            slg, slb, sdwb = carry
            base = pl.multiple_of(rci * rc, rc)
            rows = pl.ds(base, rc)
            a = p_ref[rows, 0:e].astype(F32)
            b = p_ref[rows, e:2 * e].astype(F32)
            _to_token_tiles(u3, HALO + base, rc, a * _sig(b), ng)
            cv = c_ref[rows, :].astype(F32)
            mu = jnp.mean(cv, axis=-1, keepdims=True)
            cc = cv - mu
            var = jnp.mean(cc * cc, axis=-1, keepdims=True)
            rs = lax.rsqrt(var + LN_EPS)
            nn = cc * rs
            ln = nn * lg_ref[...] + lb_ref[...]
            z = p_ref[rows, 2 * e:3 * e].astype(F32)
            sz = _sig(z)
            sl = _sig(ln)
            dsv = ds_ref[rows, :].astype(F32)
            dln = dsv * (z * sz) * _dsilu(ln, sl)
            dz = dsv * (ln * sl) * _dsilu(z, sz)
            dp_ref[rows, 2 * e:3 * e] = dz.astype(BF16)
            dn = dln * lg_ref[...]
            dc = rs * (dn - jnp.mean(dn, axis=-1, keepdims=True)
                       - nn * jnp.mean(dn * nn, axis=-1, keepdims=True))
            _to_token_tiles(dc3, base, rc, dc, ng)
            return (slg + jnp.sum(dln * nn, axis=0, keepdims=True),
                    slb + jnp.sum(dln, axis=0, keepdims=True),
                    sdwb + jnp.sum(dc, axis=0, keepdims=True))

        zero = jnp.zeros((1, e), F32)
        slg, slb, sdwb = lax.fori_loop(0, tm // rc, front, (zero, zero, zero), unroll=2)
        dlg_ref[...] += slg
        dlb_ref[...] += slb
        ddwb_ref[...] += sdwb

        def fir(bi, carry):
            t0 = bi * fb

            def dcs(q):
                return dc3[pl.ds(pl.multiple_of((t0 + q) * ng, ng), ng), :]

            def us(q):
                return u3[pl.ds(pl.multiple_of((t0 + HALO - (CONV_K - 1) + q) * ng, ng), ng), :]

            xs = [dcs(q) for q in range(fb - 1)]
            accs = [None] * fb
            for j in range(CONV_K):
                wk = dw_ref[(CONV_K - 1 - j) * ng:(CONV_K - j) * ng, :]
                xs.append(dcs(j + fb - 1))
                accs = [wk * xs[q + j] if accs[q] is None else accs[q] + wk * xs[q + j] for q in range(fb)]
            for q in range(fb):
                du3[pl.ds(pl.multiple_of((t0 + q) * ng, ng), ng), :] = accs[q]
            own = xs[0:fb]
            ys = [us(q) for q in range(fb - 1)]
            for k in range(CONV_K):
                ys.append(us(k + fb - 1))
                prods = [own[q] * ys[q + k] for q in range(fb)]
                while len(prods) > 1:
                    prods = [prods[2 * v] + prods[2 * v + 1] for v in range(len(prods) // 2)]
                ddw_ref[k * ng:(k + 1) * ng, :] += prods[0]
            return carry

        lax.fori_loop(0, tm // fb, fir, 0)
        dc3[tm * ng:(tm + HALO) * ng, :] = dc3[0:HALO * ng, :]

        def back(rci, carry):
            base = pl.multiple_of(rci * rc, rc)
            rows = pl.ds(base, rc)
            a = p_ref[rows, 0:e].astype(F32)
            b = p_ref[rows, e:2 * e].astype(F32)
            sb = _sig(b)
            duv = _from_token_tiles(du3, base, rc, ng)
            dp_ref[rows, 0:e] = (duv * sb).astype(BF16)
            dp_ref[rows, e:2 * e] = (duv * a * sb * (1.0 - sb)).astype(BF16)
            return carry

        lax.fori_loop(0, tm // rc, back, 0, unroll=2)

    vec = pl.BlockSpec((1, e), lambda i: (0, 0))
    taps = pl.BlockSpec((CONV_K * ng, LANES), lambda i: (0, 0))
    rev = lambda i: (nt - 1 - i, 0)
    halo = lambda i: (jnp.maximum((nt - 1 - i) * hb - 1, 0), 0)
    return pl.pallas_call(
        body, name="conv_mid_bwd", grid=(nt,),
        in_specs=[pl.BlockSpec((tm, 3 * e), rev), pl.BlockSpec((HALO, 3 * e), halo),
                  pl.BlockSpec((tm, e), rev), pl.BlockSpec((tm, e), rev), taps, vec, vec],
        out_specs=[pl.BlockSpec((tm, 3 * e), rev), vec, vec, vec, taps],
        out_shape=[jax.ShapeDtypeStruct((t, 3 * e), BF16), jax.ShapeDtypeStruct((1, e), F32),
                   jax.ShapeDtypeStruct((1, e), F32), jax.ShapeDtypeStruct((1, e), F32),
                   jax.ShapeDtypeStruct((CONV_K * ng, LANES), F32)],
        scratch_shapes=[pltpu.VMEM(((tm + HALO) * ng, LANES), F32), pltpu.VMEM(((tm + HALO) * ng, LANES), F32),
                        pltpu.VMEM((tm * ng, LANES), F32)],
        compiler_params=_params("arbitrary"),
    )(p, p, c, ds, dw3, lg, lb)


def _pool_mid_bwd(p, ds, wg, bg, sc):
    t = p.shape[0]
    e = p.shape[1] // 2
    ng = len(POOL_WINDOWS)
    gc = e // ng
    tm = _tile(t)
    nt = t // tm
    rc = ROW_CHUNK
    hb = tm // PHALO

    def body(p_ref, ph_ref, ds_ref, wg_ref, bg_ref, sc_ref,
             dp_ref, dwg_ref, dbg_ref, dsc_ref, ubuf, dbuf, ebuf, ddbuf):
        i = pl.program_id(0)
        ti = nt - 1 - i

        @pl.when(i == 0)
        def _():
            ebuf[tm:tm + PHALO, :] = jnp.zeros((PHALO, e), F32)
            dwg_ref[...] = jnp.zeros_like(dwg_ref)
            dbg_ref[...] = jnp.zeros_like(dbg_ref)
            dsc_ref[...] = jnp.zeros_like(dsc_ref)

        ubuf[0:PHALO, :] = jnp.where(ti > 0, ph_ref[:, 0:e].astype(F32), 0.0)
        ubuf[PHALO:PHALO + tm, :] = p_ref[:, 0:e].astype(F32)

        def recompute(rci, carry):
            base = pl.multiple_of(rci * rc, rc)
            _pool_window_dev(ubuf, base, rc, e, ti * tm, dbuf)
            return carry

        lax.fori_loop(0, tm // rc, recompute, 0)

        for g in range(ng):
            cols = slice(g * gc, (g + 1) * gc)
            dg = dbuf[:, cols].astype(BF16)
            q = jnp.dot(dg, wg_ref[g], preferred_element_type=F32) + bg_ref[:, cols]
            z = p_ref[:, e + g * gc:e + (g + 1) * gc].astype(F32)
            sz = _sig(z)
            dsv = ds_ref[:, cols].astype(F32)
            dz = dsv * (q * sc_ref[:, cols]) * _dsilu(z, sz)
            dp_ref[:, e + g * gc:e + (g + 1) * gc] = dz.astype(BF16)
            dy2 = dsv * (z * sz)
            dsc_ref[:, cols] += jnp.sum(dy2 * q, axis=0, keepdims=True)
            dq = dy2 * sc_ref[:, cols]
            dbg_ref[:, cols] += jnp.sum(dq, axis=0, keepdims=True)
            dqb = dq.astype(BF16)
            dwg_ref[g] += lax.dot_general(dg, dqb, (((0,), (0,)), ((), ())), preferred_element_type=F32)
            ddbuf[:, cols] = lax.dot_general(dqb, wg_ref[g], (((1,), (1,)), ((), ())),
                                             preferred_element_type=F32)

        def scale(rci, carry):
            base = pl.multiple_of(rci * rc, rc)
            for lc in range(e // LANES):
                lanes = slice(lc * LANES, (lc + 1) * LANES)
                w = POOL_WINDOWS[_pool_group(lc, e)]
                ebuf[pl.ds(base, rc), lanes] = (ddbuf[pl.ds(base, rc), lanes]
                                                * _pool_inv_count(ti * tm + base, rc, w))
            return carry

        lax.fori_loop(0, tm // rc, scale, 0)

        def chunk(rci, carry):
            base = pl.multiple_of(rci * rc, rc)
            n = rc + PHALO
            for lc in range(e // LANES):
                lanes = slice(lc * LANES, (lc + 1) * LANES)
                w = POOL_WINDOWS[_pool_group(lc, e)]
                acc = ebuf[pl.ds(base, n), lanes]
                step = 1
                while step < w:
                    acc = acc + pltpu.roll(acc, n - step, 0)
                    step *= 2
                du = acc[0:rc] - ddbuf[pl.ds(base, rc), lanes]
                dp_ref[pl.ds(base, rc), lanes] = du.astype(BF16)
            return carry

        lax.fori_loop(0, tm // rc, chunk, 0)
        ebuf[tm:tm + PHALO, :] = ebuf[0:PHALO, :]

    vec = pl.BlockSpec((1, e), lambda i: (0, 0))
    rev = lambda i: (nt - 1 - i, 0)
    halo = lambda i: (jnp.maximum((nt - 1 - i) * hb - 1, 0), 0)
    wspec = pl.BlockSpec((ng, gc, gc), lambda i: (0, 0, 0))
    return pl.pallas_call(
        body, name="pool_mid_bwd", grid=(nt,),
        in_specs=[pl.BlockSpec((tm, 2 * e), rev), pl.BlockSpec((PHALO, 2 * e), halo),
                  pl.BlockSpec((tm, e), rev), wspec, vec, vec],
        out_specs=[pl.BlockSpec((tm, 2 * e), rev), wspec, vec, vec],
        out_shape=[jax.ShapeDtypeStruct((t, 2 * e), BF16), jax.ShapeDtypeStruct((ng, gc, gc), F32),
                   jax.ShapeDtypeStruct((1, e), F32), jax.ShapeDtypeStruct((1, e), F32)],
        scratch_shapes=[pltpu.VMEM((tm + PHALO, e), F32), pltpu.VMEM((tm, e), F32),
                        pltpu.VMEM((tm + PHALO, e), F32), pltpu.VMEM((tm, e), F32)],
        compiler_params=_params("arbitrary"),
    )(p, p, ds, wg, bg, sc)


def _dhn_rms_bwd(dp, w4, h, g, dh_out, name):
    t, d = h.shape
    nk = w4.shape[-1]
    tm = min(MATMUL_TILE, t)

    def body(dp_ref, w_ref, h_ref, g_ref, dho_ref, dh_ref, dg_ref):
        i = pl.program_id(0)

        @pl.when(i == 0)
        def _():
            dg_ref[...] = jnp.zeros_like(dg_ref)

        dhn = jnp.zeros((tm, d), F32)
        for k in range(N_CHIPS):
            dhn = dhn + lax.dot_general(dp_ref[:, k * nk:(k + 1) * nk], w_ref[k], (((1,), (1,)), ((), ())),
                                        preferred_element_type=F32)
        hh = h_ref[...]
        r = lax.rsqrt(jnp.mean(hh * hh, axis=-1, keepdims=True) + RMS_EPS)
        hhat = hh * r
        tt = dhn * g_ref[...]
        dh_ref[...] = dho_ref[...] + r * (tt - hhat * jnp.mean(tt * hhat, axis=-1, keepdims=True))
        dg_ref[...] += jnp.sum(dhn * hhat, axis=0, keepdims=True)

    return pl.pallas_call(
        body, name=name, grid=(t // tm,),
        in_specs=[pl.BlockSpec((tm, N_CHIPS * nk), lambda i: (i, 0)),
                  _resident((N_CHIPS, d, nk)),
                  pl.BlockSpec((tm, d), lambda i: (i, 0)), pl.BlockSpec((1, d), lambda i: (0, 0)),
                  pl.BlockSpec((tm, d), lambda i: (i, 0))],
        out_specs=[pl.BlockSpec((tm, d), lambda i: (i, 0)), pl.BlockSpec((1, d), lambda i: (0, 0))],
        out_shape=[jax.ShapeDtypeStruct((t, d), F32), jax.ShapeDtypeStruct((1, d), F32)],
        compiler_params=_params("arbitrary"),
    )(dp, w4, h, g, dh_out)


def _wgrad(a, b, nblk, name):
    t, m = a.shape
    nn = b.shape[1] // nblk
    tk = min(WGRAD_TILE, t)
    nk = t // tk

    def body(a_ref, b_ref, o_ref, ob_ref):
        @pl.when(pl.program_id(1) == 0)
        def _():
            o_ref[...] = jnp.zeros_like(o_ref)

        o_ref[...] += lax.dot_general(a_ref[...].astype(BF16), b_ref[...].astype(BF16),
                                      (((0,), (0,)), ((), ())), preferred_element_type=F32)

        @pl.when(pl.program_id(1) == nk - 1)
        def _():
            ob_ref[...] = o_ref[...].astype(BF16)

    ospec = pl.BlockSpec((None, m, nn), lambda j, i: (j, 0, 0))
    return pl.pallas_call(
        body, name=name, grid=(nblk, nk),
        in_specs=[pl.BlockSpec((tk, m), lambda j, i: (i, 0)), pl.BlockSpec((tk, nn), lambda j, i: (i, j))],
        out_specs=[ospec, ospec],
        out_shape=[jax.ShapeDtypeStruct((nblk, m, nn), F32), jax.ShapeDtypeStruct((nblk, m, nn), BF16)],
        compiler_params=_params("parallel", "arbitrary"),
    )(a, b)


def _rows2d(shape):
    rows = 1
    for s in shape[:-1]:
        rows *= s
    return rows, shape[-1]


def _row_tile(rows):
    for cand in (512, 256, 128, 64, 32, 16, 8):
        if rows % cand == 0:
            return cand
    return rows


def _add_eight(own, landed, chip_core, name):
    _, _, rows, cols = own.shape
    tr = _row_tile(rows)

    def body(sel_ref, s_ref, r_ref, o_ref):
        acc = s_ref[...]
        for peer in range(N_DEV - 1):
            acc = acc + r_ref[peer].astype(F32)
        o_ref[...] = acc

    return pl.pallas_call(
        body, name=name,
        grid_spec=pltpu.PrefetchScalarGridSpec(
            num_scalar_prefetch=1, grid=(rows // tr,),
            in_specs=[pl.BlockSpec((None, None, tr, cols), lambda i, s: (s[0], s[1], i, 0)),
                      pl.BlockSpec((N_DEV - 1, tr, cols), lambda i, s: (0, i, 0))],
            out_specs=pl.BlockSpec((None, tr, cols), lambda i, s: (s[1], i, 0))),
        out_shape=jax.ShapeDtypeStruct((2, rows, cols), F32),
        compiler_params=_params("parallel"),
    )(chip_core, own, landed)


def _adamw(w, g, m, v, name):
    rows, cols = w.shape
    tr = _row_tile(rows)

    def body(w_ref, g_ref, m_ref, v_ref, d_ref, m2_ref, v2_ref):
        gg = g_ref[...]
        m2 = ADAM_B1 * m_ref[...] + (1.0 - ADAM_B1) * gg
        v2 = ADAM_B2 * v_ref[...] + (1.0 - ADAM_B2) * (gg * gg)
        m_hat = m2 / (1.0 - ADAM_B1 ** ADAM_STEP)
        v_hat = v2 / (1.0 - ADAM_B2 ** ADAM_STEP)
        d_ref[...] = -ADAM_LR * (m_hat / (jnp.sqrt(v_hat) + ADAM_EPS) + ADAM_WD * w_ref[...])
        m2_ref[...] = m2
        v2_ref[...] = v2

    spec = pl.BlockSpec((tr, cols), lambda i: (i, 0))
    shp = jax.ShapeDtypeStruct((rows, cols), F32)
    return pl.pallas_call(
        body, name=name, grid=(rows // tr,),
        in_specs=[spec, spec, spec, spec], out_specs=[spec, spec, spec], out_shape=[shp, shp, shp],
        compiler_params=_params("parallel"),
    )(w, g, m, v)


ANY = pl.BlockSpec(memory_space=pl.ANY)


def _place():
    x, y, c = lax.axis_index("x"), lax.axis_index("y"), lax.axis_index("c")
    chips = [(1 - x, y), (x, 1 - y), (1 - x, 1 - y)]
    return x, y, c, chips


def _allgather_weights(shards):
    n = len(shards)

    def body(*refs):
        ins, outs = refs[:n], refs[n:2 * n]
        send_ici, recv_ici, send_d2d, recv_d2d, loc_sem = refs[2 * n:]
        x, y, c, chips = _place()
        k0 = 2 * x + y
        sib = (x, y, 1 - c)

        locs = [pltpu.make_async_copy(ins[a], outs[a].at[k0], loc_sem.at[a]) for a in range(n)]
        for cp in locs:
            cp.start()

        def ici(a, r, src_chip, target):
            return pltpu.make_async_remote_copy(
                src_ref=ins[a].at[c], dst_ref=outs[a].at[src_chip, c],
                send_sem=send_ici.at[a * 3 + r], recv_sem=recv_ici.at[a * 3 + r],
                device_id=target, device_id_type=MESH)

        def d2d(a, r, src_chip, layer):
            return pltpu.make_async_remote_copy(
                src_ref=outs[a].at[src_chip, layer], dst_ref=outs[a].at[src_chip, layer],
                send_sem=send_d2d.at[a * 3 + r], recv_sem=recv_d2d.at[a * 3 + r],
                device_id=sib, device_id_type=MESH)

        first = [ici(a, r, k0, (cx, cy, c)) for a in range(n) for r, (cx, cy) in enumerate(chips)]
        for cp in first:
            cp.start()
        passed = []
        for a in range(n):
            for r, (cx, cy) in enumerate(chips):
                ici(a, r, 2 * cx + cy, (cx, cy, c)).wait_recv()
                cp = d2d(a, r, 2 * cx + cy, c)
                cp.start()
                passed.append(cp)
        for a in range(n):
            for r, (cx, cy) in enumerate(chips):
                d2d(a, r, 2 * cx + cy, 1 - c).wait_recv()
        for cp in first + passed:
            cp.wait_send()
        for cp in locs:
            cp.wait()

    return pl.pallas_call(
        body, name="allgather_weights",
        in_specs=[ANY] * n, out_specs=[ANY] * n,
        out_shape=[jax.ShapeDtypeStruct((N_CHIPS,) + s.shape, s.dtype) for s in shards],
        scratch_shapes=[pltpu.SemaphoreType.DMA((3 * n,)), pltpu.SemaphoreType.DMA((3 * n,)),
                        pltpu.SemaphoreType.DMA((3 * n,)), pltpu.SemaphoreType.DMA((3 * n,)),
                        pltpu.SemaphoreType.DMA((n,))],
    )(*shards)


HBM = pl.BlockSpec(memory_space=pltpu.HBM)
SEM = pl.BlockSpec(memory_space=pltpu.SEMAPHORE)
DATAFLOW = pltpu.SideEffectType.DATAFLOW_SIDE_EFFECTING
FLIPS = [(0, 0, 1), (0, 1, 0), (0, 1, 1), (1, 0, 0), (1, 0, 1), (1, 1, 0), (1, 1, 1)]


def _gather_plan(srcs, lands, send_sem, recv_sem):
    x, y, c, chips = _place()
    return [pltpu.make_async_remote_copy(
        src_ref=srcs[a].at[c], dst_ref=lands[a].at[2 * x + y, c],
        send_sem=send_sem.at[a * 3 + r], recv_sem=recv_sem.at[a * 3 + r],
        device_id=(cx, cy, c), device_id_type=MESH)
        for a in range(len(srcs)) for r, (cx, cy) in enumerate(chips)]


def _scatter_plan(srcs, lands, send_sem, recv_sem):
    x, y, c, _ = _place()
    cps = []
    for a in range(len(srcs)):
        for r, (fx, fy, fc) in enumerate(FLIPS):
            tx, ty, tc = (1 - x if fx else x), (1 - y if fy else y), (1 - c if fc else c)
            cps.append(pltpu.make_async_remote_copy(
                src_ref=srcs[a].at[2 * tx + ty, tc], dst_ref=lands[a].at[r],
                send_sem=send_sem.at[a * len(FLIPS) + r], recv_sem=recv_sem.at[a * len(FLIPS) + r],
                device_id=(tx, ty, tc), device_id_type=MESH))
    return cps


def _split_start(name, plan, srcs, lands, n_copies, after):
    n = len(srcs)

    def body(*refs):
        src, land = refs[:n], refs[n:2 * n]
        send_sem, recv_sem = refs[2 * n + 1], refs[2 * n + 2]
        token = refs[-1]
        for cp in plan(src, land, send_sem, recv_sem):
            cp.start()
        token[...] = jnp.zeros_like(token)

    outs = pl.pallas_call(
        body, name=name,
        in_specs=[HBM] * (2 * n) + [ANY],
        out_specs=[SEM, SEM] + [HBM] * (2 * n) + [pl.BlockSpec(memory_space=pltpu.VMEM)],
        out_shape=[pltpu.SemaphoreType.DMA((n_copies,)), pltpu.SemaphoreType.DMA((n_copies,))]
        + [pltpu.HBM(s.shape, s.dtype) for s in srcs] + [pltpu.HBM(l.shape, l.dtype) for l in lands]
        + [jax.ShapeDtypeStruct((8, LANES), F32)],
        input_output_aliases={i: 2 + i for i in range(2 * n)},
        compiler_params=pltpu.CompilerParams(has_side_effects=DATAFLOW),
    )(*[pltpu.with_memory_space_constraint(s, pltpu.HBM) for s in srcs],
      *[pltpu.with_memory_space_constraint(l, pltpu.HBM) for l in lands], after)
    return outs[0], outs[1], list(outs[2:2 + n]), list(outs[2 + n:2 + 2 * n]), outs[-1]


def _split_wait(name, plan, send_sems, recv_sems, srcs, lands, after):
    n = len(srcs)

    def body(*refs):
        src, land = refs[:n], refs[n:2 * n]
        send_sem, recv_sem = refs[2 * n], refs[2 * n + 1]
        for cp in plan(src, land, send_sem, recv_sem):
            cp.wait_send()
            cp.wait_recv()

    outs = pl.pallas_call(
        body, name=name,
        in_specs=[HBM] * (2 * n) + [SEM, SEM, ANY],
        out_specs=[HBM] * (2 * n),
        out_shape=[pltpu.HBM(s.shape, s.dtype) for s in srcs] + [pltpu.HBM(l.shape, l.dtype) for l in lands],
        input_output_aliases={i: i for i in range(2 * n)},
        compiler_params=pltpu.CompilerParams(has_side_effects=DATAFLOW),
    )(*srcs, *lands, send_sems, recv_sems, after)
    return list(outs[:n]), list(outs[n:])


def _forward_halves(shards, gathered):
    n = len(shards)

    def body(*refs):
        ins, outs = refs[:n], refs[2 * n:3 * n]
        send_sem, recv_sem, loc_sem = refs[3 * n:]
        x, y, c, chips = _place()
        locs = [pltpu.make_async_copy(ins[a], outs[a].at[2 * x + y], loc_sem.at[a]) for a in range(n)]
        cps = [pltpu.make_async_remote_copy(
            src_ref=outs[a].at[2 * cx + cy, c], dst_ref=outs[a].at[2 * cx + cy, c],
            send_sem=send_sem.at[a * 3 + r], recv_sem=recv_sem.at[a * 3 + r],
            device_id=(x, y, 1 - c), device_id_type=MESH)
            for a in range(n) for r, (cx, cy) in enumerate(chips)]
        for cp in locs + cps:
            cp.start()
        for cp in cps:
            cp.wait()
        for cp in locs:
            cp.wait()

    return pl.pallas_call(
        body, name="forward_halves",
        in_specs=[ANY] * (2 * n), out_specs=[ANY] * n,
        out_shape=[jax.ShapeDtypeStruct(g.shape, g.dtype) for g in gathered],
        input_output_aliases={n + a: a for a in range(n)},
        scratch_shapes=[pltpu.SemaphoreType.DMA((3 * n,)), pltpu.SemaphoreType.DMA((3 * n,)),
                        pltpu.SemaphoreType.DMA((n,))],
    )(*shards, *gathered)


def _share_halves(halves):
    n = len(halves)

    def body(*refs):
        ins, outs = refs[:n], refs[n:2 * n]
        send_sem, recv_sem = refs[2 * n:]
        x, y, c, _ = _place()
        cps = [pltpu.make_async_remote_copy(
            src_ref=outs[a].at[c], dst_ref=outs[a].at[c], send_sem=send_sem.at[a], recv_sem=recv_sem.at[a],
            device_id=(x, y, 1 - c), device_id_type=MESH) for a in range(n)]
        for cp in cps:
            cp.start()
        for cp in cps:
            cp.wait()

    return pl.pallas_call(
        body, name="share_halves",
        in_specs=[ANY] * n, out_specs=[ANY] * n,
        out_shape=[jax.ShapeDtypeStruct(h.shape, h.dtype) for h in halves],
        input_output_aliases={a: a for a in range(n)},
        scratch_shapes=[pltpu.SemaphoreType.DMA((n,)), pltpu.SemaphoreType.DMA((n,))],
    )(*halves)


N_DEV = 8


def _allreduce_small(v):
    m, nc = v.shape

    def body(x_ref, out_ref, gat, send_sems, recv_sems, local_sem):
        x, y, c, chips = _place()
        me, sib = (x, y, c), (x, y, 1 - c)

        def rows(px, py, pc):
            return gat.at[pl.ds((4 * px + 2 * py + pc) * m, m), :]

        def copy(k, block, to, src=None):
            return pltpu.make_async_remote_copy(
                src_ref=rows(*block) if src is None else src, dst_ref=rows(*block),
                send_sem=send_sems.at[k], recv_sem=recv_sems.at[k], device_id=to, device_id_type=MESH)

        mine = pltpu.make_async_copy(x_ref, rows(*me), local_sem)
        mine.start()
        first = [copy(0, me, sib, src=x_ref)]
        first += [copy(1 + j, me, (*chip, c), src=x_ref) for j, chip in enumerate(chips)]
        for cp in first:
            cp.start()
        passed = [copy(4 + j, (*chip, c), sib) for j, chip in enumerate(chips)]
        for j, chip in enumerate(chips):
            copy(1 + j, (*chip, c), me).wait_recv()
            passed[j].start()
        copy(0, sib, me).wait_recv()
        for j, chip in enumerate(chips):
            copy(4 + j, (*chip, 1 - c), me).wait_recv()
        for cp in first + passed:
            cp.wait_send()
        mine.wait()
        acc = gat[0:m, :]
        for dev in range(1, N_DEV):
            acc = acc + gat[dev * m:(dev + 1) * m, :]
        out_ref[...] = acc

    return pl.pallas_call(
        body, name="allreduce_small",
        in_specs=[pl.BlockSpec(memory_space=pltpu.VMEM)],
        out_specs=pl.BlockSpec(memory_space=pltpu.VMEM),
        out_shape=jax.ShapeDtypeStruct((m, nc), F32),
        scratch_shapes=[pltpu.VMEM((N_DEV * m, nc), F32), pltpu.SemaphoreType.DMA((7,)),
                        pltpu.SemaphoreType.DMA((7,)), pltpu.SemaphoreType.DMA],
        compiler_params=pltpu.CompilerParams(vmem_limit_bytes=VMEM_LIMIT),
    )(v)


def _pad_rows(a, rows):
    return jnp.pad(a, ((0, rows - a.shape[0]), (0, 0)))


def kernel(x, norm_g, final_g, conv_w_in, conv_dw, conv_dw_b, conv_ln_g, conv_ln_b, conv_w_out, pool_w_in, pool_w_grp, pool_b_grp, pool_scale, pool_w_out, loss_target, m_norm_g, m_final_g, m_conv_w_in, m_conv_dw, m_conv_dw_b, m_conv_ln_g, m_conv_ln_b, m_conv_w_out, m_pool_w_in, m_pool_w_grp, m_pool_b_grp, m_pool_scale, m_pool_w_out, v_norm_g, v_final_g, v_conv_w_in, v_conv_dw, v_conv_dw_b, v_conv_ln_g, v_conv_ln_b, v_conv_w_out, v_pool_w_in, v_pool_w_grp, v_pool_b_grp, v_pool_scale, v_pool_w_out):
    t, d = x.shape[1], x.shape[2]
    e = conv_w_out.shape[2]
    ng = len(POOL_WINDOWS)
    gc = e // ng
    gcs = pool_w_grp.shape[2]
    ck = conv_dw.shape[1]
    es = conv_dw.shape[2]
    xi, yi, ci = lax.axis_index("x"), lax.axis_index("y"), lax.axis_index("c")
    chip = 2 * xi + yi

    small_rows = ck + 2
    small_pad = -(-small_rows // 8) * 8
    small = jnp.concatenate([conv_dw, pool_b_grp[:, None, :], pool_scale[:, None, :],
                             jnp.zeros((2, small_pad - small_rows, es), F32)], axis=1)
    cwi_b, cwo_b = conv_w_in.astype(BF16), conv_w_out.astype(BF16)

    def halves(a):
        return a.reshape(2, a.shape[0] // 2, a.shape[1])

    g_cwi0, g_cwo0, g_small = _allgather_weights([halves(cwi_b[0]), halves(cwo_b[0]), small])
    rest = [halves(cwi_b[1]), halves(cwo_b[1]), pool_w_in.astype(BF16), pool_w_grp.astype(BF16),
            pool_w_out.astype(BF16)]
    rest_lands = [lax.empty((N_CHIPS,) + r.shape, r.dtype) for r in rest]
    ag_send, ag_recv, rest, rest_lands, ag_token = _split_start(
        "gather_rest_start", _gather_plan, rest, rest_lands, 3 * len(rest), g_small)
    smallf = jnp.transpose(g_small, (1, 2, 0, 3)).reshape(2, small_pad, N_CHIPS * es)

    h = x.reshape(t, d)
    tgt = loss_target.reshape(t, d)
    hs, saved = [], []
    for layer in range(4):
        j = layer // 2
        hs.append(h)
        gvec = norm_g[layer][None, :]
        if layer == 0:
            gvec = gvec + ag_token[0:1, 0:1]
        if layer == 1:
            rest, rest_lands = _split_wait("gather_rest_wait", _gather_plan, ag_send, ag_recv, rest, rest_lands, h)
            g_cwi1, g_cwo1, g_pwi, g_pwg, g_pwo = _forward_halves(rest, rest_lands)
            wg_full = jnp.transpose(g_pwg, (1, 2, 0, 3, 4)).reshape(2, ng, gc, gc)
        if layer % 2 == 0:
            g_in, g_out = (g_cwi0, g_cwo0) if j == 0 else (g_cwi1, g_cwo1)
            w_in4 = g_in.reshape(N_CHIPS, d, -1)
            w_out = g_out.reshape(e, d)
            p, hn = _rms_matmul(h, gvec, w_in4, "rms_matmul_conv")
            dw_full = smallf[j, 0:ck]
            s, c = _conv_mid_fwd(p, dw_full.reshape(-1, LANES), conv_dw_b[j].reshape(-1, LANES),
                                 conv_ln_g[j][None, :], conv_ln_b[j][None, :])
            saved.append((p, hn, s, c, w_in4, w_out, dw_full))
        else:
            w_in4 = g_pwi[:, j]
            w_out = g_pwo[:, j].reshape(e, d)
            p, hn = _rms_matmul(h, gvec, w_in4, "rms_matmul_pool")
            bg_full = smallf[j, ck:ck + 1]
            sc_full = smallf[j, ck + 1:ck + 2]
            s = _pool_mid_fwd(p, wg_full[j], bg_full, sc_full)
            saved.append((p, hn, s, None, w_in4, w_out, (wg_full[j], bg_full, sc_full)))
        h = _matmul_res(h, s, w_out)

    dh, loss_part, dfg = _loss_head(h, final_g[None, :], tgt)
    loss = lax.psum(loss_part[0, 0], ("x", "y", "c"))

    def by_half(a):
        return a.reshape(N_CHIPS, 2, a.shape[1] // 2, a.shape[2])

    dng = [None] * 4
    g_conv = [None, None]
    g_pool = [None, None]
    flights = {}
    for layer in (3, 2, 1, 0):
        j = layer // 2
        p, hn, s, c, w_in4, w_out, extra = saved[layer]
        gvec = norm_g[layer][None, :]
        ds = _ds_matmul(dh, w_out)
        dw_out, dw_out_b = [q.reshape(N_CHIPS, e // N_CHIPS, d) for q in _wgrad(s, dh, 1, "wgrad_out")]
        if layer % 2 == 0:
            dp, dlg, dlb, ddwb, ddw3 = _conv_mid_bwd(p, c, ds, extra.reshape(-1, LANES),
                                                     conv_ln_g[j][None, :], conv_ln_b[j][None, :])
            ddw = ddw3.reshape(ck, e)
            dw_in, dw_in_b = _wgrad(hn, dp, N_CHIPS, "wgrad_in_conv")
            own, pay = [dw_in, dw_out], [dw_in_b, dw_out_b]
            g_conv[j] = (dlg, dlb, ddwb, ddw)
        else:
            wg, bg_full, sc_full = extra
            dp, dwg, dbg, dsc = _pool_mid_bwd(p, ds, wg, bg_full, sc_full)
            dw_in, dw_in_b = _wgrad(hn, dp, N_CHIPS, "wgrad_in_pool")
            dwg4 = jnp.transpose(dwg.reshape(ng, N_CHIPS, gcs, gc), (1, 0, 2, 3)).reshape(N_CHIPS, ng * gcs, gc)
            own, pay = [dw_in, dw_out, dwg4], [dw_in_b, dw_out_b, dwg4.astype(BF16)]
            g_pool[j] = (dbg, dsc)
        pay = [by_half(q) for q in pay]
        lands = [lax.empty((len(FLIPS),) + q.shape[2:], BF16) for q in pay]
        send, recv, pay, lands, token = _split_start(
            "scatter_start_%d" % layer, _scatter_plan, pay, lands, len(FLIPS) * len(pay), own[0])
        flights[layer] = (send, recv, pay, lands, own)
        kind = "dhn_rms_bwd_conv" if layer % 2 == 0 else "dhn_rms_bwd_pool"
        dh, dng[layer] = _dhn_rms_bwd(dp, w_in4, hs[layer], gvec + token[0:1, 0:1], dh, kind)
    grad_x = dh.reshape(x.shape)

    sel_kc = jnp.stack([chip, ci]).astype(jnp.int32)
    summed = {}
    after = dh
    for layer in (3, 2, 1, 0):
        send, recv, pay, lands, own = flights[layer]
        _, lands = _split_wait("scatter_wait_%d" % layer, _scatter_plan, send, recv, pay, lands, after)
        after = lands[0]
        summed[layer] = [_add_eight(by_half(o), l, sel_kc, "add_eight_%d_%d" % (layer % 2, a))
                         for a, (o, l) in enumerate(zip(own, lands))]
    order = [(0, 0), (0, 1), (2, 0), (2, 1), (1, 0), (1, 1), (1, 2), (3, 0), (3, 1), (3, 2)]
    shared = _share_halves([summed[l][a] for l, a in order])
    full = {la: q.reshape(q.shape[0] * q.shape[1], q.shape[2]) for la, q in zip(order, shared)}
    g_cwi_f = jnp.stack([full[(0, 0)], full[(2, 0)]])
    g_cwo_f = jnp.stack([full[(0, 1)], full[(2, 1)]])
    g_pwi_f = jnp.stack([full[(1, 0)], full[(3, 0)]])
    g_pwo_f = jnp.stack([full[(1, 1)], full[(3, 1)]])
    g_pwg_f = jnp.stack([full[(1, 2)], full[(3, 2)]]).reshape(pool_w_grp.shape)

    rows_list = [dng[0], dng[1], dng[2], dng[3], dfg,
                 g_conv[0][2], g_conv[1][2], g_conv[0][0], g_conv[1][0], g_conv[0][1], g_conv[1][1],
                 g_pool[0][0], g_pool[1][0], g_pool[0][1], g_pool[1][1], g_conv[0][3], g_conv[1][3]]
    slab = jnp.concatenate(rows_list, axis=0)
    nrows = slab.shape[0]
    slab = _pad_rows(slab, -(-nrows // 8) * 8)
    tot = _allreduce_small(slab)
    g_norm_g = tot[0:4]
    g_final_g = tot[4]
    g_dwb = tot[5:7]
    g_lng = tot[7:9]
    g_lnb = tot[9:11]
    g_bg = lax.dynamic_slice_in_dim(tot[11:13], chip * es, es, axis=1)
    g_sc = lax.dynamic_slice_in_dim(tot[13:15], chip * es, es, axis=1)
    g_dw = lax.dynamic_slice_in_dim(tot[15:15 + 2 * ck].reshape(2, ck, e), chip * es, es, axis=2)

    def adam_nd(w, g, m, v, nm):
        rows, cols = _rows2d(w.shape)
        outs = _adamw(w.reshape(rows, cols), g.reshape(rows, cols), m.reshape(rows, cols),
                      v.reshape(rows, cols), "adamw_" + nm)
        return [o.reshape(w.shape) for o in outs]

    res = {}
    res["conv_w_in"] = (g_cwi_f, *adam_nd(conv_w_in, g_cwi_f, m_conv_w_in, v_conv_w_in, "cwi"))
    res["conv_w_out"] = (g_cwo_f, *adam_nd(conv_w_out, g_cwo_f, m_conv_w_out, v_conv_w_out, "cwo"))
    res["pool_w_in"] = (g_pwi_f, *adam_nd(pool_w_in, g_pwi_f, m_pool_w_in, v_pool_w_in, "pwi"))
    res["pool_w_grp"] = (g_pwg_f, *adam_nd(pool_w_grp, g_pwg_f, m_pool_w_grp, v_pool_w_grp, "pwg"))
    res["pool_w_out"] = (g_pwo_f, *adam_nd(pool_w_out, g_pwo_f, m_pool_w_out, v_pool_w_out, "pwo"))

    def pack(parts, rows_to):
        return _pad_rows(jnp.concatenate([q.reshape(-1, q.shape[-1]) for q in parts], axis=0), rows_to)

    rep_w = [norm_g, final_g[None, :], conv_dw_b, conv_ln_g, conv_ln_b]
    rep_g = [g_norm_g, g_final_g[None, :], g_dwb, g_lng, g_lnb]
    rep_m = [m_norm_g, m_final_g[None, :], m_conv_dw_b, m_conv_ln_g, m_conv_ln_b]
    rep_v = [v_norm_g, v_final_g[None, :], v_conv_dw_b, v_conv_ln_g, v_conv_ln_b]
    rep = _adamw(pack(rep_w, 16), pack(rep_g, 16), pack(rep_m, 16), pack(rep_v, 16), "adamw_rep")
    rep_names = ["norm_g", "final_g", "conv_dw_b", "conv_ln_g", "conv_ln_b"]
    rep_rows = [(0, 4), (4, 5), (5, 7), (7, 9), (9, 11)]
    for nm, (lo, hi), gq, wq in zip(rep_names, rep_rows, rep_g, rep_w):
        shape = (d,) if nm == "final_g" else wq.shape
        res[nm] = (gq.reshape(shape), *[o[lo:hi].reshape(shape) for o in rep])

    sh_w = [conv_dw, pool_b_grp, pool_scale]
    sh_g = [g_dw, g_bg, g_sc]
    sh_m = [m_conv_dw, m_pool_b_grp, m_pool_scale]
    sh_v = [v_conv_dw, v_pool_b_grp, v_pool_scale]
    sh_total = 2 * ck + 4
    sh_pad = -(-sh_total // 8) * 8
    shd = _adamw(pack(sh_w, sh_pad), pack(sh_g, sh_pad), pack(sh_m, sh_pad), pack(sh_v, sh_pad), "adamw_shard")
    sh_names = ["conv_dw", "pool_b_grp", "pool_scale"]
    sh_rows = [(0, 2 * ck), (2 * ck, 2 * ck + 2), (2 * ck + 2, 2 * ck + 4)]
    for nm, (lo, hi), gq, wq in zip(sh_names, sh_rows, sh_g, sh_w):
        res[nm] = (gq.reshape(wq.shape), *[o[lo:hi].reshape(wq.shape) for o in shd])

    order = ["norm_g", "final_g", "conv_w_in", "conv_dw", "conv_dw_b", "conv_ln_g", "conv_ln_b", "conv_w_out",
             "pool_w_in", "pool_w_grp", "pool_b_grp", "pool_scale", "pool_w_out"]
    outs = [loss, grad_x]
    for part in range(4):
        outs += [res[nm][part] for nm in order]
    return tuple(outs)
```

```python
import functools

import jax
import jax.numpy as jnp
from jax import lax
from jax.experimental import pallas as pl
from jax.experimental.pallas import tpu as pltpu

F32 = jnp.float32
BF16 = jnp.bfloat16
MESH = pl.DeviceIdType.MESH

RMS_EPS = 1e-6
LN_EPS = 1e-5
CONV_K = 31
HALO = 32
PHALO = 16
POOL_WINDOWS = (2, 4, 8, 16)
N_CHIPS = 4
LANES = 128
ROW_CHUNK = 32
CONV_ROW_CHUNK = 32
FIR_BLOCK = 16
TOKEN_TILE = 512
MATMUL_TILE = 1024
WGRAD_TILE = 2048
VMEM_LIMIT = 56 * 1024 * 1024

ADAM_LR = 0.001
ADAM_B1 = 0.9
ADAM_B2 = 0.999
ADAM_EPS = 1e-08
ADAM_WD = 0.01
ADAM_STEP = 10


def _params(*sem):
    return pltpu.CompilerParams(dimension_semantics=sem, vmem_limit_bytes=VMEM_LIMIT)


def _sig(v):
    return 0.5 * jnp.tanh(0.5 * v) + 0.5


def _dsilu(v, sv):
    return sv * (1.0 + v * (1.0 - sv))


def _resident(shape):
    return pl.BlockSpec(shape, lambda *_: (0,) * len(shape), pipeline_mode=pl.Buffered(1))


def _tile(t):
    return min(TOKEN_TILE, t)


def _rms_matmul(h, g, w4, name):
    t, d = h.shape
    nk = w4.shape[-1]
    tm = min(MATMUL_TILE, t)

    def body(h_ref, g_ref, w_ref, p_ref, hn_ref):
        hh = h_ref[...]
        r = lax.rsqrt(jnp.mean(hh * hh, axis=-1, keepdims=True) + RMS_EPS)
        hn = (hh * r * g_ref[...]).astype(BF16)
        hn_ref[...] = hn
        for k in range(N_CHIPS):
            p_ref[:, k * nk:(k + 1) * nk] = jnp.dot(hn, w_ref[k], preferred_element_type=F32).astype(BF16)

    return pl.pallas_call(
        body, name=name, grid=(t // tm,),
        in_specs=[pl.BlockSpec((tm, d), lambda i: (i, 0)),
                  pl.BlockSpec((1, d), lambda i: (0, 0)),
                  _resident((N_CHIPS, d, nk))],
        out_specs=[pl.BlockSpec((tm, N_CHIPS * nk), lambda i: (i, 0)),
                   pl.BlockSpec((tm, d), lambda i: (i, 0))],
        out_shape=[jax.ShapeDtypeStruct((t, N_CHIPS * nk), BF16), jax.ShapeDtypeStruct((t, d), BF16)],
        compiler_params=_params("parallel"),
    )(h, g, w4)


def _to_token_tiles(ref, tok0, rows, val, ng):
    for j in range(ng):
        ref[pl.ds(tok0 * ng + j, rows, stride=ng), :] = val[:, j * LANES:(j + 1) * LANES]


def _from_token_tiles(ref, tok0, rows, ng):
    return jnp.concatenate([ref[pl.ds(tok0 * ng + j, rows, stride=ng), :] for j in range(ng)], axis=1)


def _conv_mid_fwd(p, dw3, dwb3, lg, lb):
    t = p.shape[0]
    e = p.shape[1] // 3
    ng = e // LANES
    tm = _tile(t)
    rc = CONV_ROW_CHUNK
    fb = FIR_BLOCK

    def body(p_ref, dw_ref, dwb_ref, lg_ref, lb_ref, s_ref, c_ref, u3, c3):
        i = pl.program_id(0)

        @pl.when(i == 0)
        def _():
            u3[0:HALO * ng, :] = jnp.zeros((HALO * ng, LANES), F32)

        def glu(rci, carry):
            base = pl.multiple_of(rci * rc, rc)
            a = p_ref[pl.ds(base, rc), 0:e].astype(F32)
            b = p_ref[pl.ds(base, rc), e:2 * e].astype(F32)
            _to_token_tiles(u3, HALO + base, rc, a * _sig(b), ng)
            return carry

        lax.fori_loop(0, tm // rc, glu, 0)

        def fir(bi, carry):
            t0 = bi * fb
            def x(q):
                return u3[pl.ds(pl.multiple_of((t0 + HALO - (CONV_K - 1) + q) * ng, ng), ng), :]

            xs = [x(q) for q in range(fb - 1)]
            accs = [dwb_ref[...]] * fb
            for k in range(CONV_K):
                wk = dw_ref[k * ng:(k + 1) * ng, :]
                xs.append(x(k + fb - 1))
                accs = [accs[q] + wk * xs[q + k] for q in range(fb)]
            for q in range(fb):
                c3[pl.ds(pl.multiple_of((t0 + q) * ng, ng), ng), :] = accs[q]
            return carry

        lax.fori_loop(0, tm // fb, fir, 0)
        u3[0:HALO * ng, :] = u3[tm * ng:(tm + HALO) * ng, :]

        def chunk(rci, carry):
            base = pl.multiple_of(rci * rc, rc)
            c = _from_token_tiles(c3, base, rc, ng)
            mu = jnp.mean(c, axis=-1, keepdims=True)
            cc = c - mu
            var = jnp.mean(cc * cc, axis=-1, keepdims=True)
            ln = cc * lax.rsqrt(var + LN_EPS) * lg_ref[...] + lb_ref[...]
            z = p_ref[pl.ds(base, rc), 2 * e:3 * e].astype(F32)
            s = (ln * _sig(ln)) * (z * _sig(z))
            s_ref[pl.ds(base, rc), :] = s.astype(BF16)
            c_ref[pl.ds(base, rc), :] = c.astype(BF16)
            return carry

        lax.fori_loop(0, tm // rc, chunk, 0, unroll=2)

    vec = pl.BlockSpec((1, e), lambda i: (0, 0))
    return pl.pallas_call(
        body, name="conv_mid_fwd", grid=(t // tm,),
        in_specs=[pl.BlockSpec((tm, 3 * e), lambda i: (i, 0)),
                  pl.BlockSpec((CONV_K * ng, LANES), lambda i: (0, 0)),
                  pl.BlockSpec((ng, LANES), lambda i: (0, 0)), vec, vec],
        out_specs=[pl.BlockSpec((tm, e), lambda i: (i, 0)), pl.BlockSpec((tm, e), lambda i: (i, 0))],
        out_shape=[jax.ShapeDtypeStruct((t, e), BF16), jax.ShapeDtypeStruct((t, e), BF16)],
        scratch_shapes=[pltpu.VMEM(((tm + HALO) * ng, LANES), F32), pltpu.VMEM((tm * ng, LANES), F32)],
        compiler_params=_params("arbitrary"),
    )(p, dw3, dwb3, lg, lb)


def _pool_group(lc, e):
    return (lc * LANES) // (e // len(POOL_WINDOWS))


def _pool_inv_count(row0, rows, w):
    tpos = row0 + lax.broadcasted_iota(jnp.int32, (rows, 1), 0)
    return 1.0 / jnp.minimum(tpos + 1, w).astype(F32)


def _pool_window_dev(ubuf, base, rc, e, row0, dbuf):
    n = rc + PHALO
    for lc in range(e // LANES):
        lanes = slice(lc * LANES, (lc + 1) * LANES)
        g = _pool_group(lc, e)
        w = POOL_WINDOWS[g]
        blk = ubuf[pl.ds(base, n), lanes]
        acc = blk
        step = 1
        while step < w:
            acc = acc + pltpu.roll(acc, step, 0)
            step *= 2
        win = acc[PHALO:n]
        tok = blk[PHALO:n]
        dbuf[pl.ds(base, rc), lanes] = win * _pool_inv_count(row0 + base, rc, w) - tok


def _pool_mid_fwd(p, wg, bg, sc):
    t = p.shape[0]
    e = p.shape[1] // 2
    gc = e // len(POOL_WINDOWS)
    tm = _tile(t)
    rc = ROW_CHUNK

    def body(p_ref, wg_ref, bg_ref, sc_ref, s_ref, ubuf, dbuf):
        i = pl.program_id(0)

        @pl.when(i == 0)
        def _():
            ubuf[0:PHALO, :] = jnp.zeros((PHALO, e), F32)

        ubuf[PHALO:PHALO + tm, :] = p_ref[:, 0:e].astype(F32)

        def chunk(rci, carry):
            base = pl.multiple_of(rci * rc, rc)
            _pool_window_dev(ubuf, base, rc, e, i * tm, dbuf)
            return carry

        lax.fori_loop(0, tm // rc, chunk, 0)
        ubuf[0:PHALO, :] = ubuf[tm:tm + PHALO, :]

        for g in range(len(POOL_WINDOWS)):
            cols = slice(g * gc, (g + 1) * gc)
            yg = jnp.dot(dbuf[:, cols].astype(BF16), wg_ref[g], preferred_element_type=F32)
            z = p_ref[:, e + g * gc:e + (g + 1) * gc].astype(F32)
            s = ((yg + bg_ref[:, cols]) * sc_ref[:, cols]) * (z * _sig(z))
            s_ref[:, cols] = s.astype(BF16)

    vec = pl.BlockSpec((1, e), lambda i: (0, 0))
    return pl.pallas_call(
        body, name="pool_mid_fwd", grid=(t // tm,),
        in_specs=[pl.BlockSpec((tm, 2 * e), lambda i: (i, 0)),
                  pl.BlockSpec((len(POOL_WINDOWS), gc, gc), lambda i: (0, 0, 0)), vec, vec],
        out_specs=pl.BlockSpec((tm, e), lambda i: (i, 0)),
        out_shape=jax.ShapeDtypeStruct((t, e), BF16),
        scratch_shapes=[pltpu.VMEM((tm + PHALO, e), F32), pltpu.VMEM((tm, e), F32)],
        compiler_params=_params("arbitrary"),
    )(p, wg, bg, sc)


def _matmul_res(h, s, w):
    t, d = h.shape
    e = s.shape[1]
    tm = min(MATMUL_TILE, t)

    def body(h_ref, s_ref, w_ref, o_ref):
        o_ref[...] = h_ref[...] + jnp.dot(s_ref[...], w_ref[...], preferred_element_type=F32)

    return pl.pallas_call(
        body, name="matmul_res", grid=(t // tm,),
        in_specs=[pl.BlockSpec((tm, d), lambda i: (i, 0)), pl.BlockSpec((tm, e), lambda i: (i, 0)),
                  _resident((e, d))],
        out_specs=pl.BlockSpec((tm, d), lambda i: (i, 0)),
        out_shape=jax.ShapeDtypeStruct((t, d), F32),
        compiler_params=_params("parallel"),
    )(h, s, w)


def _loss_head(h, fg, tgt):
    t, d = h.shape
    tm = min(MATMUL_TILE, t)

    def body(h_ref, g_ref, t_ref, dh_ref, loss_ref, dg_ref):
        i = pl.program_id(0)

        @pl.when(i == 0)
        def _():
            loss_ref[...] = jnp.zeros_like(loss_ref)
            dg_ref[...] = jnp.zeros_like(dg_ref)

        hh = h_ref[...]
        r = lax.rsqrt(jnp.mean(hh * hh, axis=-1, keepdims=True) + RMS_EPS)
        hhat = hh * r
        err = hhat * g_ref[...] - t_ref[...]
        per_tok = jnp.mean(err * err, axis=-1, keepdims=True)
        loss_ref[...] += 0.5 * jnp.sum(per_tok, axis=0, keepdims=True)
        dy = err * (1.0 / d)
        tt = dy * g_ref[...]
        dh_ref[...] = r * (tt - hhat * jnp.mean(tt * hhat, axis=-1, keepdims=True))
        dg_ref[...] += jnp.sum(dy * hhat, axis=0, keepdims=True)

    return pl.pallas_call(
        body, name="loss_head", grid=(t // tm,),
        in_specs=[pl.BlockSpec((tm, d), lambda i: (i, 0)), pl.BlockSpec((1, d), lambda i: (0, 0)),
                  pl.BlockSpec((tm, d), lambda i: (i, 0))],
        out_specs=[pl.BlockSpec((tm, d), lambda i: (i, 0)), pl.BlockSpec((1, LANES), lambda i: (0, 0)),
                   pl.BlockSpec((1, d), lambda i: (0, 0))],
        out_shape=[jax.ShapeDtypeStruct((t, d), F32), jax.ShapeDtypeStruct((1, LANES), F32),
                   jax.ShapeDtypeStruct((1, d), F32)],
        compiler_params=_params("arbitrary"),
    )(h, fg, tgt)


def _ds_matmul(dy, w):
    t, d = dy.shape
    e = w.shape[0]
    tm = min(MATMUL_TILE, t)

    def body(dy_ref, w_ref, ds_ref):
        ds_ref[...] = lax.dot_general(dy_ref[...].astype(BF16), w_ref[...], (((1,), (1,)), ((), ())),
                                      preferred_element_type=F32).astype(BF16)

    return pl.pallas_call(
        body, name="ds_matmul", grid=(t // tm,),
        in_specs=[pl.BlockSpec((tm, d), lambda i: (i, 0)), _resident((e, d))],
        out_specs=pl.BlockSpec((tm, e), lambda i: (i, 0)),
        out_shape=jax.ShapeDtypeStruct((t, e), BF16),
        compiler_params=_params("parallel"),
    )(dy, w)


def _conv_mid_bwd(p, c, ds, dw3, lg, lb):
    t = p.shape[0]
    e = p.shape[1] // 3
    ng = e // LANES
    tm = _tile(t)
    nt = t // tm
    rc = CONV_ROW_CHUNK
    fb = FIR_BLOCK
    hb = tm // HALO

    def body(p_ref, ph_ref, c_ref, ds_ref, dw_ref, lg_ref, lb_ref,
             dp_ref, dlg_ref, dlb_ref, ddwb_ref, ddw_ref, u3, dc3, du3):
        i = pl.program_id(0)
        ti = nt - 1 - i

        @pl.when(i == 0)
        def _():
            dc3[tm * ng:(tm + HALO) * ng, :] = jnp.zeros((HALO * ng, LANES), F32)
            dlg_ref[...] = jnp.zeros_like(dlg_ref)
            dlb_ref[...] = jnp.zeros_like(dlb_ref)
            ddwb_ref[...] = jnp.zeros_like(ddwb_ref)
            ddw_ref[...] = jnp.zeros_like(ddw_ref)

        ha = ph_ref[:, 0:e].astype(F32)
        hbb = ph_ref[:, e:2 * e].astype(F32)
        _to_token_tiles(u3, 0, HALO, jnp.where(ti > 0, ha * _sig(hbb), 0.0), ng)

        def front(rci, carry):
            slg, slb, sdwb = carry
            base = pl.multiple_of(rci * rc, rc)
            rows = pl.ds(base, rc)
            a = p_ref[rows, 0:e].astype(F32)
            b = p_ref[rows, e:2 * e].astype(F32)
            _to_token_tiles(u3, HALO + base, rc, a * _sig(b), ng)
            cv = c_ref[rows, :].astype(F32)
            mu = jnp.mean(cv, axis=-1, keepdims=True)
            cc = cv - mu
            var = jnp.mean(cc * cc, axis=-1, keepdims=True)
            rs = lax.rsqrt(var + LN_EPS)
            nn = cc * rs
            ln = nn * lg_ref[...] + lb_ref[...]
            z = p_ref[rows, 2 * e:3 * e].astype(F32)
            sz = _sig(z)
            sl = _sig(ln)
            dsv = ds_ref[rows, :].astype(F32)
            dln = dsv * (z * sz) * _dsilu(ln, sl)
            dz = dsv * (ln * sl) * _dsilu(z, sz)
            dp_ref[rows, 2 * e:3 * e] = dz.astype(BF16)
            dn = dln * lg_ref[...]
            dc = rs * (dn - jnp.mean(dn, axis=-1, keepdims=True)
                       - nn * jnp.mean(dn * nn, axis=-1, keepdims=True))
            _to_token_tiles(dc3, base, rc, dc, ng)
            return (slg + jnp.sum(dln * nn, axis=0, keepdims=True),
                    slb + jnp.sum(dln, axis=0, keepdims=True),
                    sdwb + jnp.sum(dc, axis=0, keepdims=True))

        zero = jnp.zeros((1, e), F32)
        slg, slb, sdwb = lax.fori_loop(0, tm // rc, front, (zero, zero, zero), unroll=2)
        dlg_ref[...] += slg
        dlb_ref[...] += slb
        ddwb_ref[...] += sdwb

        def fir(bi, carry):
            t0 = bi * fb

            def dcs(q):
                return dc3[pl.ds(pl.multiple_of((t0 + q) * ng, ng), ng), :]

            def us(q):
                return u3[pl.ds(pl.multiple_of((t0 + HALO - (CONV_K - 1) + q) * ng, ng), ng), :]

            xs = [dcs(q) for q in range(fb - 1)]
            accs = [None] * fb
            for j in range(CONV_K):
                wk = dw_ref[(CONV_K - 1 - j) * ng:(CONV_K - j) * ng, :]
                xs.append(dcs(j + fb - 1))
                accs = [wk * xs[q + j] if accs[q] is None else accs[q] + wk * xs[q + j] for q in range(fb)]
            for q in range(fb):
                du3[pl.ds(pl.multiple_of((t0 + q) * ng, ng), ng), :] = accs[q]
            own = xs[0:fb]
            ys = [us(q) for q in range(fb - 1)]
            for k in range(CONV_K):
                ys.append(us(k + fb - 1))
                prods = [own[q] * ys[q + k] for q in range(fb)]
                while len(prods) > 1:
                    prods = [prods[2 * v] + prods[2 * v + 1] for v in range(len(prods) // 2)]
                ddw_ref[k * ng:(k + 1) * ng, :] += prods[0]
            return carry

        lax.fori_loop(0, tm // fb, fir, 0)
        dc3[tm * ng:(tm + HALO) * ng, :] = dc3[0:HALO * ng, :]

        def back(rci, carry):
            base = pl.multiple_of(rci * rc, rc)
            rows = pl.ds(base, rc)
            a = p_ref[rows, 0:e].astype(F32)
            b = p_ref[rows, e:2 * e].astype(F32)
            sb = _sig(b)
            duv = _from_token_tiles(du3, base, rc, ng)
            dp_ref[rows, 0:e] = (duv * sb).astype(BF16)
            dp_ref[rows, e:2 * e] = (duv * a * sb * (1.0 - sb)).astype(BF16)
            return carry

        lax.fori_loop(0, tm // rc, back, 0, unroll=2)

    vec = pl.BlockSpec((1, e), lambda i: (0, 0))
    taps = pl.BlockSpec((CONV_K * ng, LANES), lambda i: (0, 0))
    rev = lambda i: (nt - 1 - i, 0)
    halo = lambda i: (jnp.maximum((nt - 1 - i) * hb - 1, 0), 0)
    return pl.pallas_call(
        body, name="conv_mid_bwd", grid=(nt,),
        in_specs=[pl.BlockSpec((tm, 3 * e), rev), pl.BlockSpec((HALO, 3 * e), halo),
                  pl.BlockSpec((tm, e), rev), pl.BlockSpec((tm, e), rev), taps, vec, vec],
        out_specs=[pl.BlockSpec((tm, 3 * e), rev), vec, vec, vec, taps],
        out_shape=[jax.ShapeDtypeStruct((t, 3 * e), BF16), jax.ShapeDtypeStruct((1, e), F32),
                   jax.ShapeDtypeStruct((1, e), F32), jax.ShapeDtypeStruct((1, e), F32),
                   jax.ShapeDtypeStruct((CONV_K * ng, LANES), F32)],
        scratch_shapes=[pltpu.VMEM(((tm + HALO) * ng, LANES), F32), pltpu.VMEM(((tm + HALO) * ng, LANES), F32),
                        pltpu.VMEM((tm * ng, LANES), F32)],
        compiler_params=_params("arbitrary"),
    )(p, p, c, ds, dw3, lg, lb)


def _pool_mid_bwd(p, ds, wg, bg, sc):
    t = p.shape[0]
    e = p.shape[1] // 2
    ng = len(POOL_WINDOWS)
    gc = e // ng
    tm = _tile(t)
    nt = t // tm
    rc = ROW_CHUNK
    hb = tm // PHALO

    def body(p_ref, ph_ref, ds_ref, wg_ref, bg_ref, sc_ref,
             dp_ref, dwg_ref, dbg_ref, dsc_ref, ubuf, dbuf, ebuf, ddbuf):
        i = pl.program_id(0)
        ti = nt - 1 - i

        @pl.when(i == 0)
        def _():
            ebuf[tm:tm + PHALO, :] = jnp.zeros((PHALO, e), F32)
            dwg_ref[...] = jnp.zeros_like(dwg_ref)
            dbg_ref[...] = jnp.zeros_like(dbg_ref)
            dsc_ref[...] = jnp.zeros_like(dsc_ref)

        ubuf[0:PHALO, :] = jnp.where(ti > 0, ph_ref[:, 0:e].astype(F32), 0.0)
        ubuf[PHALO:PHALO + tm, :] = p_ref[:, 0:e].astype(F32)

        def recompute(rci, carry):
            base = pl.multiple_of(rci * rc, rc)
            _pool_window_dev(ubuf, base, rc, e, ti * tm, dbuf)
            return carry

        lax.fori_loop(0, tm // rc, recompute, 0)

        for g in range(ng):
            cols = slice(g * gc, (g + 1) * gc)
            dg = dbuf[:, cols].astype(BF16)
            q = jnp.dot(dg, wg_ref[g], preferred_element_type=F32) + bg_ref[:, cols]
            z = p_ref[:, e + g * gc:e + (g + 1) * gc].astype(F32)
            sz = _sig(z)
            dsv = ds_ref[:, cols].astype(F32)
            dz = dsv * (q * sc_ref[:, cols]) * _dsilu(z, sz)
            dp_ref[:, e + g * gc:e + (g + 1) * gc] = dz.astype(BF16)
            dy2 = dsv * (z * sz)
            dsc_ref[:, cols] += jnp.sum(dy2 * q, axis=0, keepdims=True)
            dq = dy2 * sc_ref[:, cols]
            dbg_ref[:, cols] += jnp.sum(dq, axis=0, keepdims=True)
            dqb = dq.astype(BF16)
            dwg_ref[g] += lax.dot_general(dg, dqb, (((0,), (0,)), ((), ())), preferred_element_type=F32)
            ddbuf[:, cols] = lax.dot_general(dqb, wg_ref[g], (((1,), (1,)), ((), ())),
                                             preferred_element_type=F32)

        def scale(rci, carry):
            base = pl.multiple_of(rci * rc, rc)
            for lc in range(e // LANES):
                lanes = slice(lc * LANES, (lc + 1) * LANES)
                w = POOL_WINDOWS[_pool_group(lc, e)]
                ebuf[pl.ds(base, rc), lanes] = (ddbuf[pl.ds(base, rc), lanes]
                                                * _pool_inv_count(ti * tm + base, rc, w))
            return carry

        lax.fori_loop(0, tm // rc, scale, 0)

        def chunk(rci, carry):
            base = pl.multiple_of(rci * rc, rc)
            n = rc + PHALO
            for lc in range(e // LANES):
                lanes = slice(lc * LANES, (lc + 1) * LANES)
                w = POOL_WINDOWS[_pool_group(lc, e)]
                acc = ebuf[pl.ds(base, n), lanes]
                step = 1
                while step < w:
                    acc = acc + pltpu.roll(acc, n - step, 0)
                    step *= 2
                du = acc[0:rc] - ddbuf[pl.ds(base, rc), lanes]
                dp_ref[pl.ds(base, rc), lanes] = du.astype(BF16)
            return carry

        lax.fori_loop(0, tm // rc, chunk, 0)
        ebuf[tm:tm + PHALO, :] = ebuf[0:PHALO, :]

    vec = pl.BlockSpec((1, e), lambda i: (0, 0))
    rev = lambda i: (nt - 1 - i, 0)
    halo = lambda i: (jnp.maximum((nt - 1 - i) * hb - 1, 0), 0)
    wspec = pl.BlockSpec((ng, gc, gc), lambda i: (0, 0, 0))
    return pl.pallas_call(
        body, name="pool_mid_bwd", grid=(nt,),
        in_specs=[pl.BlockSpec((tm, 2 * e), rev), pl.BlockSpec((PHALO, 2 * e), halo),
                  pl.BlockSpec((tm, e), rev), wspec, vec, vec],
        out_specs=[pl.BlockSpec((tm, 2 * e), rev), wspec, vec, vec],
        out_shape=[jax.ShapeDtypeStruct((t, 2 * e), BF16), jax.ShapeDtypeStruct((ng, gc, gc), F32),
                   jax.ShapeDtypeStruct((1, e), F32), jax.ShapeDtypeStruct((1, e), F32)],
        scratch_shapes=[pltpu.VMEM((tm + PHALO, e), F32), pltpu.VMEM((tm, e), F32),
                        pltpu.VMEM((tm + PHALO, e), F32), pltpu.VMEM((tm, e), F32)],
        compiler_params=_params("arbitrary"),
    )(p, p, ds, wg, bg, sc)


def _dhn_rms_bwd(dp, w4, h, g, dh_out, name):
    t, d = h.shape
    nk = w4.shape[-1]
    tm = min(MATMUL_TILE, t)

    def body(dp_ref, w_ref, h_ref, g_ref, dho_ref, dh_ref, dg_ref):
        i = pl.program_id(0)

        @pl.when(i == 0)
        def _():
            dg_ref[...] = jnp.zeros_like(dg_ref)

        dhn = jnp.zeros((tm, d), F32)
        for k in range(N_CHIPS):
            dhn = dhn + lax.dot_general(dp_ref[:, k * nk:(k + 1) * nk], w_ref[k], (((1,), (1,)), ((), ())),
                                        preferred_element_type=F32)
        hh = h_ref[...]
        r = lax.rsqrt(jnp.mean(hh * hh, axis=-1, keepdims=True) + RMS_EPS)
        hhat = hh * r
        tt = dhn * g_ref[...]
        dh_ref[...] = dho_ref[...] + r * (tt - hhat * jnp.mean(tt * hhat, axis=-1, keepdims=True))
        dg_ref[...] += jnp.sum(dhn * hhat, axis=0, keepdims=True)

    return pl.pallas_call(
        body, name=name, grid=(t // tm,),
        in_specs=[pl.BlockSpec((tm, N_CHIPS * nk), lambda i: (i, 0)),
                  _resident((N_CHIPS, d, nk)),
                  pl.BlockSpec((tm, d), lambda i: (i, 0)), pl.BlockSpec((1, d), lambda i: (0, 0)),
                  pl.BlockSpec((tm, d), lambda i: (i, 0))],
        out_specs=[pl.BlockSpec((tm, d), lambda i: (i, 0)), pl.BlockSpec((1, d), lambda i: (0, 0))],
        out_shape=[jax.ShapeDtypeStruct((t, d), F32), jax.ShapeDtypeStruct((1, d), F32)],
        compiler_params=_params("arbitrary"),
    )(dp, w4, h, g, dh_out)


def _wgrad(a, b, nblk, name):
    t, m = a.shape
    nn = b.shape[1] // nblk
    tk = min(WGRAD_TILE, t)
    nk = t // tk

    def body(a_ref, b_ref, o_ref, ob_ref):
        @pl.when(pl.program_id(1) == 0)
        def _():
            o_ref[...] = jnp.zeros_like(o_ref)

        o_ref[...] += lax.dot_general(a_ref[...].astype(BF16), b_ref[...].astype(BF16),
                                      (((0,), (0,)), ((), ())), preferred_element_type=F32)

        @pl.when(pl.program_id(1) == nk - 1)
        def _():
            ob_ref[...] = o_ref[...].astype(BF16)

    ospec = pl.BlockSpec((None, m, nn), lambda j, i: (j, 0, 0))
    return pl.pallas_call(
        body, name=name, grid=(nblk, nk),
        in_specs=[pl.BlockSpec((tk, m), lambda j, i: (i, 0)), pl.BlockSpec((tk, nn), lambda j, i: (i, j))],
        out_specs=[ospec, ospec],
        out_shape=[jax.ShapeDtypeStruct((nblk, m, nn), F32), jax.ShapeDtypeStruct((nblk, m, nn), BF16)],
        compiler_params=_params("parallel", "arbitrary"),
    )(a, b)


def _rows2d(shape):
    rows = 1
    for s in shape[:-1]:
        rows *= s
    return rows, shape[-1]


def _row_tile(rows):
    for cand in (512, 256, 128, 64, 32, 16, 8):
        if rows % cand == 0:
            return cand
    return rows


def _add_eight(own, landed, chip_core, name):
    _, _, rows, cols = own.shape
    tr = _row_tile(rows)

    def body(sel_ref, s_ref, r_ref, o_ref):
        acc = s_ref[...]
        for peer in range(N_DEV - 1):
            acc = acc + r_ref[peer].astype(F32)
        o_ref[...] = acc

    return pl.pallas_call(
        body, name=name,
        grid_spec=pltpu.PrefetchScalarGridSpec(
            num_scalar_prefetch=1, grid=(rows // tr,),
            in_specs=[pl.BlockSpec((None, None, tr, cols), lambda i, s: (s[0], s[1], i, 0)),
                      pl.BlockSpec((N_DEV - 1, tr, cols), lambda i, s: (0, i, 0))],
            out_specs=pl.BlockSpec((None, tr, cols), lambda i, s: (s[1], i, 0))),
        out_shape=jax.ShapeDtypeStruct((2, rows, cols), F32),
        compiler_params=_params("parallel"),
    )(chip_core, own, landed)


def _adamw(w, g, m, v, name):
    rows, cols = w.shape
    tr = _row_tile(rows)

    def body(w_ref, g_ref, m_ref, v_ref, d_ref, m2_ref, v2_ref):
        gg = g_ref[...]
        m2 = ADAM_B1 * m_ref[...] + (1.0 - ADAM_B1) * gg
        v2 = ADAM_B2 * v_ref[...] + (1.0 - ADAM_B2) * (gg * gg)
        m_hat = m2 / (1.0 - ADAM_B1 ** ADAM_STEP)
        v_hat = v2 / (1.0 - ADAM_B2 ** ADAM_STEP)
        d_ref[...] = -ADAM_LR * (m_hat / (jnp.sqrt(v_hat) + ADAM_EPS) + ADAM_WD * w_ref[...])
        m2_ref[...] = m2
        v2_ref[...] = v2

    spec = pl.BlockSpec((tr, cols), lambda i: (i, 0))
    shp = jax.ShapeDtypeStruct((rows, cols), F32)
    return pl.pallas_call(
        body, name=name, grid=(rows // tr,),
        in_specs=[spec, spec, spec, spec], out_specs=[spec, spec, spec], out_shape=[shp, shp, shp],
        compiler_params=_params("parallel"),
    )(w, g, m, v)


ANY = pl.BlockSpec(memory_space=pl.ANY)


def _place():
    x, y, c = lax.axis_index("x"), lax.axis_index("y"), lax.axis_index("c")
    chips = [(1 - x, y), (x, 1 - y), (1 - x, 1 - y)]
    return x, y, c, chips


def _allgather_weights(shards):
    n = len(shards)

    def body(*refs):
        ins, outs = refs[:n], refs[n:2 * n]
        send_ici, recv_ici, send_d2d, recv_d2d = refs[2 * n:]
        x, y, c, chips = _place()
        k0 = 2 * x + y
        sib = (x, y, 1 - c)

        def ici(a, r, src_chip, target):
            return pltpu.make_async_remote_copy(
                src_ref=ins[a].at[c], dst_ref=outs[a].at[src_chip, c],
                send_sem=send_ici.at[a * 3 + r], recv_sem=recv_ici.at[a * 3 + r],
                device_id=target, device_id_type=MESH)

        def d2d(a, r, src_chip, layer):
            return pltpu.make_async_remote_copy(
                src_ref=outs[a].at[src_chip, layer], dst_ref=outs[a].at[src_chip, layer],
                send_sem=send_d2d.at[a * 3 + r], recv_sem=recv_d2d.at[a * 3 + r],
                device_id=sib, device_id_type=MESH)

        first = [ici(a, r, k0, (cx, cy, c)) for a in range(n) for r, (cx, cy) in enumerate(chips)]
        for cp in first:
            cp.start()
        passed = []
        for a in range(n):
            for r, (cx, cy) in enumerate(chips):
                ici(a, r, 2 * cx + cy, (cx, cy, c)).wait_recv()
                cp = d2d(a, r, 2 * cx + cy, c)
                cp.start()
                passed.append(cp)
        for a in range(n):
            for r, (cx, cy) in enumerate(chips):
                d2d(a, r, 2 * cx + cy, 1 - c).wait_recv()
        for cp in first + passed:
            cp.wait_send()

    return pl.pallas_call(
        body, name="allgather_weights",
        in_specs=[ANY] * n, out_specs=[ANY] * n,
        out_shape=[jax.ShapeDtypeStruct((N_CHIPS,) + s.shape, s.dtype) for s in shards],
        scratch_shapes=[pltpu.SemaphoreType.DMA((3 * n,)), pltpu.SemaphoreType.DMA((3 * n,)),
                        pltpu.SemaphoreType.DMA((3 * n,)), pltpu.SemaphoreType.DMA((3 * n,))],
    )(*shards)


def _put_own(gathered, shard, chip):
    return lax.dynamic_update_slice_in_dim(gathered, shard[None], chip, axis=0)


HBM = pl.BlockSpec(memory_space=pltpu.HBM)
SEM = pl.BlockSpec(memory_space=pltpu.SEMAPHORE)
DATAFLOW = pltpu.SideEffectType.DATAFLOW_SIDE_EFFECTING
FLIPS = [(0, 0, 1), (0, 1, 0), (0, 1, 1), (1, 0, 0), (1, 0, 1), (1, 1, 0), (1, 1, 1)]


def _gather_plan(srcs, lands, send_sem, recv_sem):
    x, y, c, chips = _place()
    return [pltpu.make_async_remote_copy(
        src_ref=srcs[a], dst_ref=lands[a].at[2 * x + y],
        send_sem=send_sem.at[a * 3 + r], recv_sem=recv_sem.at[a * 3 + r],
        device_id=(cx, cy, c), device_id_type=MESH)
        for a in range(len(srcs)) for r, (cx, cy) in enumerate(chips)]


def _scatter_plan(srcs, lands, send_sem, recv_sem):
    x, y, c, _ = _place()
    cps = []
    for a in range(len(srcs)):
        for r, (fx, fy, fc) in enumerate(FLIPS):
            tx, ty, tc = (1 - x if fx else x), (1 - y if fy else y), (1 - c if fc else c)
            cps.append(pltpu.make_async_remote_copy(
                src_ref=srcs[a].at[2 * tx + ty, tc], dst_ref=lands[a].at[r],
                send_sem=send_sem.at[a * len(FLIPS) + r], recv_sem=recv_sem.at[a * len(FLIPS) + r],
                device_id=(tx, ty, tc), device_id_type=MESH))
    return cps


def _split_start(name, plan, srcs, lands, n_copies, after):
    n = len(srcs)

    def body(*refs):
        src, land = refs[:n], refs[n:2 * n]
        send_sem, recv_sem = refs[2 * n + 1], refs[2 * n + 2]
        token = refs[-1]
        for cp in plan(src, land, send_sem, recv_sem):
            cp.start()
        token[...] = jnp.zeros_like(token)

    outs = pl.pallas_call(
        body, name=name,
        in_specs=[HBM] * (2 * n) + [ANY],
        out_specs=[SEM, SEM] + [HBM] * (2 * n) + [pl.BlockSpec(memory_space=pltpu.VMEM)],
        out_shape=[pltpu.SemaphoreType.DMA((n_copies,)), pltpu.SemaphoreType.DMA((n_copies,))]
        + [pltpu.HBM(s.shape, s.dtype) for s in srcs] + [pltpu.HBM(l.shape, l.dtype) for l in lands]
        + [jax.ShapeDtypeStruct((8, LANES), F32)],
        input_output_aliases={i: 2 + i for i in range(2 * n)},
        compiler_params=pltpu.CompilerParams(has_side_effects=DATAFLOW),
    )(*[pltpu.with_memory_space_constraint(s, pltpu.HBM) for s in srcs],
      *[pltpu.with_memory_space_constraint(l, pltpu.HBM) for l in lands], after)
    return outs[0], outs[1], list(outs[2:2 + n]), list(outs[2 + n:2 + 2 * n]), outs[-1]


def _split_wait(name, plan, send_sems, recv_sems, srcs, lands, after):
    n = len(srcs)

    def body(*refs):
        src, land = refs[:n], refs[n:2 * n]
        send_sem, recv_sem = refs[2 * n], refs[2 * n + 1]
        for cp in plan(src, land, send_sem, recv_sem):
            cp.wait_send()
            cp.wait_recv()

    outs = pl.pallas_call(
        body, name=name,
        in_specs=[HBM] * (2 * n) + [SEM, SEM, ANY],
        out_specs=[HBM] * (2 * n),
        out_shape=[pltpu.HBM(s.shape, s.dtype) for s in srcs] + [pltpu.HBM(l.shape, l.dtype) for l in lands],
        input_output_aliases={i: i for i in range(2 * n)},
        compiler_params=pltpu.CompilerParams(has_side_effects=DATAFLOW),
    )(*srcs, *lands, send_sems, recv_sems, after)
    return list(outs[:n]), list(outs[n:])


def _share_halves(halves):
    n = len(halves)

    def body(*refs):
        ins, outs = refs[:n], refs[n:2 * n]
        send_sem, recv_sem = refs[2 * n:]
        x, y, c, _ = _place()
        cps = [pltpu.make_async_remote_copy(
            src_ref=outs[a].at[c], dst_ref=outs[a].at[c], send_sem=send_sem.at[a], recv_sem=recv_sem.at[a],
            device_id=(x, y, 1 - c), device_id_type=MESH) for a in range(n)]
        for cp in cps:
            cp.start()
        for cp in cps:
            cp.wait()

    return pl.pallas_call(
        body, name="share_halves",
        in_specs=[ANY] * n, out_specs=[ANY] * n,
        out_shape=[jax.ShapeDtypeStruct(h.shape, h.dtype) for h in halves],
        input_output_aliases={a: a for a in range(n)},
        scratch_shapes=[pltpu.SemaphoreType.DMA((n,)), pltpu.SemaphoreType.DMA((n,))],
    )(*halves)


N_DEV = 8


def _allreduce_small(v):
    m, nc = v.shape

    def body(x_ref, out_ref, gat, send_sems, recv_sems, local_sem):
        x, y, c, chips = _place()
        me, sib = (x, y, c), (x, y, 1 - c)

        def rows(px, py, pc):
            return gat.at[pl.ds((4 * px + 2 * py + pc) * m, m), :]

        def copy(k, block, to, src=None):
            return pltpu.make_async_remote_copy(
                src_ref=rows(*block) if src is None else src, dst_ref=rows(*block),
                send_sem=send_sems.at[k], recv_sem=recv_sems.at[k], device_id=to, device_id_type=MESH)

        mine = pltpu.make_async_copy(x_ref, rows(*me), local_sem)
        mine.start()
        first = [copy(0, me, sib, src=x_ref)]
        first += [copy(1 + j, me, (*chip, c), src=x_ref) for j, chip in enumerate(chips)]
        for cp in first:
            cp.start()
        passed = [copy(4 + j, (*chip, c), sib) for j, chip in enumerate(chips)]
        for j, chip in enumerate(chips):
            copy(1 + j, (*chip, c), me).wait_recv()
            passed[j].start()
        copy(0, sib, me).wait_recv()
        for j, chip in enumerate(chips):
            copy(4 + j, (*chip, 1 - c), me).wait_recv()
        for cp in first + passed:
            cp.wait_send()
        mine.wait()
        acc = gat[0:m, :]
        for dev in range(1, N_DEV):
            acc = acc + gat[dev * m:(dev + 1) * m, :]
        out_ref[...] = acc

    return pl.pallas_call(
        body, name="allreduce_small",
        in_specs=[pl.BlockSpec(memory_space=pltpu.VMEM)],
        out_specs=pl.BlockSpec(memory_space=pltpu.VMEM),
        out_shape=jax.ShapeDtypeStruct((m, nc), F32),
        scratch_shapes=[pltpu.VMEM((N_DEV * m, nc), F32), pltpu.SemaphoreType.DMA((7,)),
                        pltpu.SemaphoreType.DMA((7,)), pltpu.SemaphoreType.DMA],
        compiler_params=pltpu.CompilerParams(vmem_limit_bytes=VMEM_LIMIT),
    )(v)


def _pad_rows(a, rows):
    return jnp.pad(a, ((0, rows - a.shape[0]), (0, 0)))


def kernel(x, norm_g, final_g, conv_w_in, conv_dw, conv_dw_b, conv_ln_g, conv_ln_b, conv_w_out, pool_w_in, pool_w_grp, pool_b_grp, pool_scale, pool_w_out, loss_target, m_norm_g, m_final_g, m_conv_w_in, m_conv_dw, m_conv_dw_b, m_conv_ln_g, m_conv_ln_b, m_conv_w_out, m_pool_w_in, m_pool_w_grp, m_pool_b_grp, m_pool_scale, m_pool_w_out, v_norm_g, v_final_g, v_conv_w_in, v_conv_dw, v_conv_dw_b, v_conv_ln_g, v_conv_ln_b, v_conv_w_out, v_pool_w_in, v_pool_w_grp, v_pool_b_grp, v_pool_scale, v_pool_w_out):
    t, d = x.shape[1], x.shape[2]
    e = conv_w_out.shape[2]
    ng = len(POOL_WINDOWS)
    gc = e // ng
    gcs = pool_w_grp.shape[2]
    ck = conv_dw.shape[1]
    es = conv_dw.shape[2]
    xi, yi, ci = lax.axis_index("x"), lax.axis_index("y"), lax.axis_index("c")
    chip = 2 * xi + yi

    small_rows = ck + 2
    small_pad = -(-small_rows // 8) * 8
    small = jnp.concatenate([conv_dw, pool_b_grp[:, None, :], pool_scale[:, None, :],
                             jnp.zeros((2, small_pad - small_rows, es), F32)], axis=1)
    cwi_b, cwo_b = conv_w_in.astype(BF16), conv_w_out.astype(BF16)

    def halves(a):
        return a.reshape(2, a.shape[0] // 2, a.shape[1])

    first = [halves(cwi_b[0]), halves(cwo_b[0]), small]
    g_cwi0, g_cwo0, g_small = [_put_own(g, q, chip) for g, q in zip(_allgather_weights(first), first)]
    rest = [cwi_b[1], cwo_b[1], pool_w_in.astype(BF16), pool_w_grp.astype(BF16), pool_w_out.astype(BF16)]
    rest_lands = [lax.empty((N_CHIPS,) + r.shape, r.dtype) for r in rest]
    ag_send, ag_recv, rest, rest_lands, ag_token = _split_start(
        "gather_rest_start", _gather_plan, rest, rest_lands, 3 * len(rest), g_small)
    smallf = jnp.transpose(g_small, (1, 2, 0, 3)).reshape(2, small_pad, N_CHIPS * es)

    h = x.reshape(t, d)
    tgt = loss_target.reshape(t, d)
    hs, saved = [], []
    for layer in range(4):
        j = layer // 2
        hs.append(h)
        gvec = norm_g[layer][None, :]
        if layer == 0:
            gvec = gvec + ag_token[0:1, 0:1]
        if layer == 1:
            rest, rest_lands = _split_wait("gather_rest_wait", _gather_plan, ag_send, ag_recv, rest, rest_lands, h)
            g_cwi1, g_cwo1, g_pwi, g_pwg, g_pwo = [_put_own(g, q, chip) for g, q in zip(rest_lands, rest)]
            wg_full = jnp.transpose(g_pwg, (1, 2, 0, 3, 4)).reshape(2, ng, gc, gc)
        if layer % 2 == 0:
            g_in, g_out = (g_cwi0, g_cwo0) if j == 0 else (g_cwi1, g_cwo1)
            w_in4 = g_in.reshape(N_CHIPS, d, -1)
            w_out = g_out.reshape(e, d)
            p, hn = _rms_matmul(h, gvec, w_in4, "rms_matmul_conv")
            dw_full = smallf[j, 0:ck]
            s, c = _conv_mid_fwd(p, dw_full.reshape(-1, LANES), conv_dw_b[j].reshape(-1, LANES),
                                 conv_ln_g[j][None, :], conv_ln_b[j][None, :])
            saved.append((p, hn, s, c, w_in4, w_out, dw_full))
        else:
            w_in4 = g_pwi[:, j]
            w_out = g_pwo[:, j].reshape(e, d)
            p, hn = _rms_matmul(h, gvec, w_in4, "rms_matmul_pool")
            bg_full = smallf[j, ck:ck + 1]
            sc_full = smallf[j, ck + 1:ck + 2]
            s = _pool_mid_fwd(p, wg_full[j], bg_full, sc_full)
            saved.append((p, hn, s, None, w_in4, w_out, (wg_full[j], bg_full, sc_full)))
        h = _matmul_res(h, s, w_out)

    dh, loss_part, dfg = _loss_head(h, final_g[None, :], tgt)
    loss = lax.psum(loss_part[0, 0], ("x", "y", "c"))

    def by_half(a):
        return a.reshape(N_CHIPS, 2, a.shape[1] // 2, a.shape[2])

    dng = [None] * 4
    g_conv = [None, None]
    g_pool = [None, None]
    flights = {}
    for layer in (3, 2, 1, 0):
        j = layer // 2
        p, hn, s, c, w_in4, w_out, extra = saved[layer]
        gvec = norm_g[layer][None, :]
        ds = _ds_matmul(dh, w_out)
        dw_out, dw_out_b = [q.reshape(N_CHIPS, e // N_CHIPS, d) for q in _wgrad(s, dh, 1, "wgrad_out")]
        if layer % 2 == 0:
            dp, dlg, dlb, ddwb, ddw3 = _conv_mid_bwd(p, c, ds, extra.reshape(-1, LANES),
                                                     conv_ln_g[j][None, :], conv_ln_b[j][None, :])
            ddw = ddw3.reshape(ck, e)
            dw_in, dw_in_b = _wgrad(hn, dp, N_CHIPS, "wgrad_in_conv")
            own, pay = [dw_in, dw_out], [dw_in_b, dw_out_b]
            g_conv[j] = (dlg, dlb, ddwb, ddw)
        else:
            wg, bg_full, sc_full = extra
            dp, dwg, dbg, dsc = _pool_mid_bwd(p, ds, wg, bg_full, sc_full)
            dw_in, dw_in_b = _wgrad(hn, dp, N_CHIPS, "wgrad_in_pool")
            dwg4 = jnp.transpose(dwg.reshape(ng, N_CHIPS, gcs, gc), (1, 0, 2, 3)).reshape(N_CHIPS, ng * gcs, gc)
            own, pay = [dw_in, dw_out, dwg4], [dw_in_b, dw_out_b, dwg4.astype(BF16)]
            g_pool[j] = (dbg, dsc)
        pay = [by_half(q) for q in pay]
        lands = [lax.empty((len(FLIPS),) + q.shape[2:], BF16) for q in pay]
        send, recv, pay, lands, token = _split_start(
            "scatter_start_%d" % layer, _scatter_plan, pay, lands, len(FLIPS) * len(pay), own[0])
        flights[layer] = (send, recv, pay, lands, own)
        kind = "dhn_rms_bwd_conv" if layer % 2 == 0 else "dhn_rms_bwd_pool"
        dh, dng[layer] = _dhn_rms_bwd(dp, w_in4, hs[layer], gvec + token[0:1, 0:1], dh, kind)
    grad_x = dh.reshape(x.shape)

    sel_kc = jnp.stack([chip, ci]).astype(jnp.int32)
    summed = {}
    after = dh
    for layer in (3, 2, 1, 0):
        send, recv, pay, lands, own = flights[layer]
        _, lands = _split_wait("scatter_wait_%d" % layer, _scatter_plan, send, recv, pay, lands, after)
        after = lands[0]
        summed[layer] = [_add_eight(by_half(o), l, sel_kc, "add_eight_%d_%d" % (layer % 2, a))
                         for a, (o, l) in enumerate(zip(own, lands))]
    order = [(0, 0), (0, 1), (2, 0), (2, 1), (1, 0), (1, 1), (1, 2), (3, 0), (3, 1), (3, 2)]
    shared = _share_halves([summed[l][a] for l, a in order])
    full = {la: q.reshape(q.shape[0] * q.shape[1], q.shape[2]) for la, q in zip(order, shared)}
    g_cwi_f = jnp.stack([full[(0, 0)], full[(2, 0)]])
    g_cwo_f = jnp.stack([full[(0, 1)], full[(2, 1)]])
    g_pwi_f = jnp.stack([full[(1, 0)], full[(3, 0)]])
    g_pwo_f = jnp.stack([full[(1, 1)], full[(3, 1)]])
    g_pwg_f = jnp.stack([full[(1, 2)], full[(3, 2)]]).reshape(pool_w_grp.shape)

    rows_list = [dng[0], dng[1], dng[2], dng[3], dfg,
                 g_conv[0][2], g_conv[1][2], g_conv[0][0], g_conv[1][0], g_conv[0][1], g_conv[1][1],
                 g_pool[0][0], g_pool[1][0], g_pool[0][1], g_pool[1][1], g_conv[0][3], g_conv[1][3]]
    slab = jnp.concatenate(rows_list, axis=0)
    nrows = slab.shape[0]
    slab = _pad_rows(slab, -(-nrows // 8) * 8)
    tot = _allreduce_small(slab)
    g_norm_g = tot[0:4]
    g_final_g = tot[4]
    g_dwb = tot[5:7]
    g_lng = tot[7:9]
    g_lnb = tot[9:11]
    g_bg = lax.dynamic_slice_in_dim(tot[11:13], chip * es, es, axis=1)
    g_sc = lax.dynamic_slice_in_dim(tot[13:15], chip * es, es, axis=1)
    g_dw = lax.dynamic_slice_in_dim(tot[15:15 + 2 * ck].reshape(2, ck, e), chip * es, es, axis=2)

    def adam_nd(w, g, m, v, nm):
        rows, cols = _rows2d(w.shape)
        outs = _adamw(w.reshape(rows, cols), g.reshape(rows, cols), m.reshape(rows, cols),
                      v.reshape(rows, cols), "adamw_" + nm)
        return [o.reshape(w.shape) for o in outs]

    res = {}
    res["conv_w_in"] = (g_cwi_f, *adam_nd(conv_w_in, g_cwi_f, m_conv_w_in, v_conv_w_in, "cwi"))
    res["conv_w_out"] = (g_cwo_f, *adam_nd(conv_w_out, g_cwo_f, m_conv_w_out, v_conv_w_out, "cwo"))
    res["pool_w_in"] = (g_pwi_f, *adam_nd(pool_w_in, g_pwi_f, m_pool_w_in, v_pool_w_in, "pwi"))
    res["pool_w_grp"] = (g_pwg_f, *adam_nd(pool_w_grp, g_pwg_f, m_pool_w_grp, v_pool_w_grp, "pwg"))
    res["pool_w_out"] = (g_pwo_f, *adam_nd(pool_w_out, g_pwo_f, m_pool_w_out, v_pool_w_out, "pwo"))

    def pack(parts, rows_to):
        return _pad_rows(jnp.concatenate([q.reshape(-1, q.shape[-1]) for q in parts], axis=0), rows_to)

    rep_w = [norm_g, final_g[None, :], conv_dw_b, conv_ln_g, conv_ln_b]
    rep_g = [g_norm_g, g_final_g[None, :], g_dwb, g_lng, g_lnb]
    rep_m = [m_norm_g, m_final_g[None, :], m_conv_dw_b, m_conv_ln_g, m_conv_ln_b]
    rep_v = [v_norm_g, v_final_g[None, :], v_conv_dw_b, v_conv_ln_g, v_conv_ln_b]
    rep = _adamw(pack(rep_w, 16), pack(rep_g, 16), pack(rep_m, 16), pack(rep_v, 16), "adamw_rep")
    rep_names = ["norm_g", "final_g", "conv_dw_b", "conv_ln_g", "conv_ln_b"]
    rep_rows = [(0, 4), (4, 5), (5, 7), (7, 9), (9, 11)]
    for nm, (lo, hi), gq, wq in zip(rep_names, rep_rows, rep_g, rep_w):
        shape = (d,) if nm == "final_g" else wq.shape
        res[nm] = (gq.reshape(shape), *[o[lo:hi].reshape(shape) for o in rep])

    sh_w = [conv_dw, pool_b_grp, pool_scale]
    sh_g = [g_dw, g_bg, g_sc]
    sh_m = [m_conv_dw, m_pool_b_grp, m_pool_scale]
    sh_v = [v_conv_dw, v_pool_b_grp, v_pool_scale]
    sh_total = 2 * ck + 4
    sh_pad = -(-sh_total // 8) * 8
    shd = _adamw(pack(sh_w, sh_pad), pack(sh_g, sh_pad), pack(sh_m, sh_pad), pack(sh_v, sh_pad), "adamw_shard")
    sh_names = ["conv_dw", "pool_b_grp", "pool_scale"]
    sh_rows = [(0, 2 * ck), (2 * ck, 2 * ck + 2), (2 * ck + 2, 2 * ck + 4)]
    for nm, (lo, hi), gq, wq in zip(sh_names, sh_rows, sh_g, sh_w):
        res[nm] = (gq.reshape(wq.shape), *[o[lo:hi].reshape(wq.shape) for o in shd])

    order = ["norm_g", "final_g", "conv_w_in", "conv_dw", "conv_dw_b", "conv_ln_g", "conv_ln_b", "conv_w_out",
             "pool_w_in", "pool_w_grp", "pool_b_grp", "pool_scale", "pool_w_out"]
    outs = [loss, grad_x]
    for part in range(4):
        outs += [res[nm][part] for nm in order]
    return tuple(outs)
```

```python
import functools

import jax
import jax.numpy as jnp
from jax import lax
from jax.experimental import pallas as pl
from jax.experimental.pallas import tpu as pltpu

F32 = jnp.float32
BF16 = jnp.bfloat16
MESH = pl.DeviceIdType.MESH

RMS_EPS = 1e-6
LN_EPS = 1e-5
CONV_K = 31
HALO = 32
PHALO = 16
POOL_WINDOWS = (2, 4, 8, 16)
N_CHIPS = 4
LANES = 128
ROW_CHUNK = 32
CONV_ROW_CHUNK = 32
FIR_BLOCK = 16
TOKEN_TILE = 512
MATMUL_TILE = 1024
WGRAD_TILE = 2048
WG_M, WG_N = 256, 512
VMEM_LIMIT = 56 * 1024 * 1024

ADAM_LR = 0.001
ADAM_B1 = 0.9
ADAM_B2 = 0.999
ADAM_EPS = 1e-08
ADAM_WD = 0.01
ADAM_STEP = 10


def _params(*sem):
    return pltpu.CompilerParams(dimension_semantics=sem, vmem_limit_bytes=VMEM_LIMIT)


def _sig(v):
    return 0.5 * jnp.tanh(0.5 * v) + 0.5


def _dsilu(v, sv):
    return sv * (1.0 + v * (1.0 - sv))


def _resident(shape):
    return pl.BlockSpec(shape, lambda *_: (0,) * len(shape), pipeline_mode=pl.Buffered(1))


def _tile(t):
    return min(TOKEN_TILE, t)


def _rms_matmul(h, g, w4, name):
    t, d = h.shape
    nk = w4.shape[-1]
    tm = min(MATMUL_TILE, t)

    def body(h_ref, g_ref, w_ref, p_ref, hn_ref):
        hh = h_ref[...]
        r = lax.rsqrt(jnp.mean(hh * hh, axis=-1, keepdims=True) + RMS_EPS)
        hn = (hh * r * g_ref[...]).astype(BF16)
        hn_ref[...] = hn
        for k in range(N_CHIPS):
            p_ref[:, k * nk:(k + 1) * nk] = jnp.dot(hn, w_ref[k], preferred_element_type=F32).astype(BF16)

    return pl.pallas_call(
        body, name=name, grid=(t // tm,),
        in_specs=[pl.BlockSpec((tm, d), lambda i: (i, 0)),
                  pl.BlockSpec((1, d), lambda i: (0, 0)),
                  _resident((N_CHIPS, d, nk))],
        out_specs=[pl.BlockSpec((tm, N_CHIPS * nk), lambda i: (i, 0)),
                   pl.BlockSpec((tm, d), lambda i: (i, 0))],
        out_shape=[jax.ShapeDtypeStruct((t, N_CHIPS * nk), BF16), jax.ShapeDtypeStruct((t, d), BF16)],
        compiler_params=_params("parallel"),
    )(h, g, w4)


def _to_token_tiles(ref, tok0, rows, val, ng):
    for j in range(ng):
        ref[pl.ds(tok0 * ng + j, rows, stride=ng), :] = val[:, j * LANES:(j + 1) * LANES]


def _from_token_tiles(ref, tok0, rows, ng):
    return jnp.concatenate([ref[pl.ds(tok0 * ng + j, rows, stride=ng), :] for j in range(ng)], axis=1)


def _conv_mid_fwd(p, dw3, dwb3, lg, lb):
    t = p.shape[0]
    e = p.shape[1] // 3
    ng = e // LANES
    tm = _tile(t)
    rc = CONV_ROW_CHUNK
    fb = FIR_BLOCK

    def body(p_ref, dw_ref, dwb_ref, lg_ref, lb_ref, s_ref, c_ref, u3, c3):
        i = pl.program_id(0)

        @pl.when(i == 0)
        def _():
            u3[0:HALO * ng, :] = jnp.zeros((HALO * ng, LANES), F32)

        def glu(rci, carry):
            base = pl.multiple_of(rci * rc, rc)
            a = p_ref[pl.ds(base, rc), 0:e].astype(F32)
            b = p_ref[pl.ds(base, rc), e:2 * e].astype(F32)
            _to_token_tiles(u3, HALO + base, rc, a * _sig(b), ng)
            return carry

        lax.fori_loop(0, tm // rc, glu, 0)

        def fir(bi, carry):
            t0 = bi * fb
            def x(q):
                return u3[pl.ds(pl.multiple_of((t0 + HALO - (CONV_K - 1) + q) * ng, ng), ng), :]

            xs = [x(q) for q in range(fb - 1)]
            accs = [dwb_ref[...]] * fb
            for k in range(CONV_K):
                wk = dw_ref[k * ng:(k + 1) * ng, :]
                xs.append(x(k + fb - 1))
                accs = [accs[q] + wk * xs[q + k] for q in range(fb)]
            for q in range(fb):
                c3[pl.ds(pl.multiple_of((t0 + q) * ng, ng), ng), :] = accs[q]
            return carry

        lax.fori_loop(0, tm // fb, fir, 0)
        u3[0:HALO * ng, :] = u3[tm * ng:(tm + HALO) * ng, :]

        def chunk(rci, carry):
            base = pl.multiple_of(rci * rc, rc)
            c = _from_token_tiles(c3, base, rc, ng)
            mu = jnp.mean(c, axis=-1, keepdims=True)
            cc = c - mu
            var = jnp.mean(cc * cc, axis=-1, keepdims=True)
            ln = cc * lax.rsqrt(var + LN_EPS) * lg_ref[...] + lb_ref[...]
            z = p_ref[pl.ds(base, rc), 2 * e:3 * e].astype(F32)
            s = (ln * _sig(ln)) * (z * _sig(z))
            s_ref[pl.ds(base, rc), :] = s.astype(BF16)
            c_ref[pl.ds(base, rc), :] = c.astype(BF16)
            return carry

        lax.fori_loop(0, tm // rc, chunk, 0, unroll=2)

    vec = pl.BlockSpec((1, e), lambda i: (0, 0))
    return pl.pallas_call(
        body, name="conv_mid_fwd", grid=(t // tm,),
        in_specs=[pl.BlockSpec((tm, 3 * e), lambda i: (i, 0)),
                  pl.BlockSpec((CONV_K * ng, LANES), lambda i: (0, 0)),
                  pl.BlockSpec((ng, LANES), lambda i: (0, 0)), vec, vec],
        out_specs=[pl.BlockSpec((tm, e), lambda i: (i, 0)), pl.BlockSpec((tm, e), lambda i: (i, 0))],
        out_shape=[jax.ShapeDtypeStruct((t, e), BF16), jax.ShapeDtypeStruct((t, e), BF16)],
        scratch_shapes=[pltpu.VMEM(((tm + HALO) * ng, LANES), F32), pltpu.VMEM((tm * ng, LANES), F32)],
        compiler_params=_params("arbitrary"),
    )(p, dw3, dwb3, lg, lb)


def _pool_group(lc, e):
    return (lc * LANES) // (e // len(POOL_WINDOWS))


def _pool_inv_count(row0, rows, w):
    tpos = row0 + lax.broadcasted_iota(jnp.int32, (rows, 1), 0)
    return 1.0 / jnp.minimum(tpos + 1, w).astype(F32)


def _pool_window_dev(ubuf, base, rc, e, row0, dbuf):
    n = rc + PHALO
    for lc in range(e // LANES):
        lanes = slice(lc * LANES, (lc + 1) * LANES)
        g = _pool_group(lc, e)
        w = POOL_WINDOWS[g]
        blk = ubuf[pl.ds(base, n), lanes]
        acc = blk
        step = 1
        while step < w:
            acc = acc + pltpu.roll(acc, step, 0)
            step *= 2
        win = acc[PHALO:n]
        tok = blk[PHALO:n]
        dbuf[pl.ds(base, rc), lanes] = win * _pool_inv_count(row0 + base, rc, w) - tok


def _pool_mid_fwd(p, wg, bg, sc):
    t = p.shape[0]
    e = p.shape[1] // 2
    gc = e // len(POOL_WINDOWS)
    tm = _tile(t)
    rc = ROW_CHUNK

    def body(p_ref, wg_ref, bg_ref, sc_ref, s_ref, ubuf, dbuf):
        i = pl.program_id(0)

        @pl.when(i == 0)
        def _():
            ubuf[0:PHALO, :] = jnp.zeros((PHALO, e), F32)

        ubuf[PHALO:PHALO + tm, :] = p_ref[:, 0:e].astype(F32)

        def chunk(rci, carry):
            base = pl.multiple_of(rci * rc, rc)
            _pool_window_dev(ubuf, base, rc, e, i * tm, dbuf)
            return carry

        lax.fori_loop(0, tm // rc, chunk, 0)
        ubuf[0:PHALO, :] = ubuf[tm:tm + PHALO, :]

        for g in range(len(POOL_WINDOWS)):
            cols = slice(g * gc, (g + 1) * gc)
            yg = jnp.dot(dbuf[:, cols].astype(BF16), wg_ref[g], preferred_element_type=F32)
            z = p_ref[:, e + g * gc:e + (g + 1) * gc].astype(F32)
            s = ((yg + bg_ref[:, cols]) * sc_ref[:, cols]) * (z * _sig(z))
            s_ref[:, cols] = s.astype(BF16)

    vec = pl.BlockSpec((1, e), lambda i: (0, 0))
    return pl.pallas_call(
        body, name="pool_mid_fwd", grid=(t // tm,),
        in_specs=[pl.BlockSpec((tm, 2 * e), lambda i: (i, 0)),
                  pl.BlockSpec((len(POOL_WINDOWS), gc, gc), lambda i: (0, 0, 0)), vec, vec],
        out_specs=pl.BlockSpec((tm, e), lambda i: (i, 0)),
        out_shape=jax.ShapeDtypeStruct((t, e), BF16),
        scratch_shapes=[pltpu.VMEM((tm + PHALO, e), F32), pltpu.VMEM((tm, e), F32)],
        compiler_params=_params("arbitrary"),
    )(p, wg, bg, sc)


def _matmul_res(h, s, w):
    t, d = h.shape
    e = s.shape[1]
    tm = min(MATMUL_TILE, t)

    def body(h_ref, s_ref, w_ref, o_ref):
        o_ref[...] = h_ref[...] + jnp.dot(s_ref[...], w_ref[...], preferred_element_type=F32)

    return pl.pallas_call(
        body, name="matmul_res", grid=(t // tm,),
        in_specs=[pl.BlockSpec((tm, d), lambda i: (i, 0)), pl.BlockSpec((tm, e), lambda i: (i, 0)),
                  _resident((e, d))],
        out_specs=pl.BlockSpec((tm, d), lambda i: (i, 0)),
        out_shape=jax.ShapeDtypeStruct((t, d), F32),
        compiler_params=_params("parallel"),
    )(h, s, w)


def _loss_head(h, fg, tgt):
    t, d = h.shape
    tm = min(MATMUL_TILE, t)

    def body(h_ref, g_ref, t_ref, dh_ref, loss_ref, dg_ref):
        i = pl.program_id(0)

        @pl.when(i == 0)
        def _():
            loss_ref[...] = jnp.zeros_like(loss_ref)
            dg_ref[...] = jnp.zeros_like(dg_ref)

        hh = h_ref[...]
        r = lax.rsqrt(jnp.mean(hh * hh, axis=-1, keepdims=True) + RMS_EPS)
        hhat = hh * r
        err = hhat * g_ref[...] - t_ref[...]
        per_tok = jnp.mean(err * err, axis=-1, keepdims=True)
        loss_ref[...] += 0.5 * jnp.sum(per_tok, axis=0, keepdims=True)
        dy = err * (1.0 / d)
        tt = dy * g_ref[...]
        dh_ref[...] = r * (tt - hhat * jnp.mean(tt * hhat, axis=-1, keepdims=True))
        dg_ref[...] += jnp.sum(dy * hhat, axis=0, keepdims=True)

    return pl.pallas_call(
        body, name="loss_head", grid=(t // tm,),
        in_specs=[pl.BlockSpec((tm, d), lambda i: (i, 0)), pl.BlockSpec((1, d), lambda i: (0, 0)),
                  pl.BlockSpec((tm, d), lambda i: (i, 0))],
        out_specs=[pl.BlockSpec((tm, d), lambda i: (i, 0)), pl.BlockSpec((1, LANES), lambda i: (0, 0)),
                   pl.BlockSpec((1, d), lambda i: (0, 0))],
        out_shape=[jax.ShapeDtypeStruct((t, d), F32), jax.ShapeDtypeStruct((1, LANES), F32),
                   jax.ShapeDtypeStruct((1, d), F32)],
        compiler_params=_params("arbitrary"),
    )(h, fg, tgt)


def _ds_matmul(dy, w):
    t, d = dy.shape
    e = w.shape[0]
    tm = min(MATMUL_TILE, t)

    def body(dy_ref, w_ref, ds_ref):
        ds_ref[...] = lax.dot_general(dy_ref[...].astype(BF16), w_ref[...], (((1,), (1,)), ((), ())),
                                      preferred_element_type=F32).astype(BF16)

    return pl.pallas_call(
        body, name="ds_matmul", grid=(t // tm,),
        in_specs=[pl.BlockSpec((tm, d), lambda i: (i, 0)), _resident((e, d))],
        out_specs=pl.BlockSpec((tm, e), lambda i: (i, 0)),
        out_shape=jax.ShapeDtypeStruct((t, e), BF16),
        compiler_params=_params("parallel"),
    )(dy, w)


def _conv_mid_bwd(p, c, ds, hn, dw3, lg, lb):
    t = p.shape[0]
    e = p.shape[1] // 3
    d = hn.shape[1]
    ng = e // LANES
    tm = _tile(t)
    nt = t // tm
    rc = CONV_ROW_CHUNK
    fb = FIR_BLOCK
    hb = tm // HALO
    wm, wn = d // WG_M, 3 * e // WG_N
    nblk = wm * wn
    assert nblk <= tm // fb and e % WG_N == 0, "at most one weight-gradient block per conv loop step"

    def body(p_ref, ph_ref, c_ref, ds_ref, hn_ref, dw_ref, lg_ref, lb_ref,
             dp_ref, dlg_ref, dlb_ref, ddwb_ref, ddw_ref, dwt_ref, u3, dc3, du3, hn_s, dp_s):
        i = pl.program_id(0)
        ti = nt - 1 - i

        def put_dp(slot, rows, col0, val):
            for q in range(val.shape[1] // WG_N):
                dp_s[slot, col0 // WG_N + q, rows, :] = val[:, q * WG_N:(q + 1) * WG_N]

        def wgrad_block(step, slot):
            dwt_ref[jnp.minimum(step, nblk)] += lax.dot_general(
                hn_s[(step // wn) % wm], dp_s[slot, step % wn], (((0,), (0,)), ((), ())),
                preferred_element_type=F32)

        @pl.when(i == 0)
        def _():
            dc3[tm * ng:(tm + HALO) * ng, :] = jnp.zeros((HALO * ng, LANES), F32)
            dlg_ref[...] = jnp.zeros_like(dlg_ref)
            dlb_ref[...] = jnp.zeros_like(dlb_ref)
            ddwb_ref[...] = jnp.zeros_like(ddwb_ref)
            ddw_ref[...] = jnp.zeros_like(ddw_ref)
            dwt_ref[...] = jnp.zeros_like(dwt_ref)
            hn_s[...] = jnp.zeros_like(hn_s)
            dp_s[1] = jnp.zeros_like(dp_s[1])

        ha = ph_ref[:, 0:e].astype(F32)
        hbb = ph_ref[:, e:2 * e].astype(F32)
        _to_token_tiles(u3, 0, HALO, jnp.where(ti > 0, ha * _sig(hbb), 0.0), ng)

        def front(rci, carry):
            slg, slb, sdwb = carry
            base = pl.multiple_of(rci * rc, rc)
            rows = pl.ds(base, rc)
            a = p_ref[rows, 0:e].astype(F32)
            b = p_ref[rows, e:2 * e].astype(F32)
            _to_token_tiles(u3, HALO + base, rc, a * _sig(b), ng)
            cv = c_ref[rows, :].astype(F32)
            mu = jnp.mean(cv, axis=-1, keepdims=True)
            cc = cv - mu
            var = jnp.mean(cc * cc, axis=-1, keepdims=True)
            rs = lax.rsqrt(var + LN_EPS)
            nn = cc * rs
            ln = nn * lg_ref[...] + lb_ref[...]
            z = p_ref[rows, 2 * e:3 * e].astype(F32)
            sz = _sig(z)
            sl = _sig(ln)
            dsv = ds_ref[rows, :].astype(F32)
            dln = dsv * (z * sz) * _dsilu(ln, sl)
            dz = dsv * (ln * sl) * _dsilu(z, sz)
            dzb = dz.astype(BF16)
            dp_ref[rows, 2 * e:3 * e] = dzb
            put_dp(i % 2, rows, 2 * e, dzb)
            dn = dln * lg_ref[...]
            dc = rs * (dn - jnp.mean(dn, axis=-1, keepdims=True)
                       - nn * jnp.mean(dn * nn, axis=-1, keepdims=True))
            _to_token_tiles(dc3, base, rc, dc, ng)
            return (slg + jnp.sum(dln * nn, axis=0, keepdims=True),
                    slb + jnp.sum(dln, axis=0, keepdims=True),
                    sdwb + jnp.sum(dc, axis=0, keepdims=True))

        zero = jnp.zeros((1, e), F32)
        slg, slb, sdwb = lax.fori_loop(0, tm // rc, front, (zero, zero, zero), unroll=2)
        dlg_ref[...] += slg
        dlb_ref[...] += slb
        ddwb_ref[...] += sdwb

        def fir(bi, carry):
            t0 = bi * fb

            def dcs(q):
                return dc3[pl.ds(pl.multiple_of((t0 + q) * ng, ng), ng), :]

            def us(q):
                return u3[pl.ds(pl.multiple_of((t0 + HALO - (CONV_K - 1) + q) * ng, ng), ng), :]

            xs = [dcs(q) for q in range(fb - 1)]
            accs = [None] * fb
            for j in range(CONV_K):
                wk = dw_ref[(CONV_K - 1 - j) * ng:(CONV_K - j) * ng, :]
                xs.append(dcs(j + fb - 1))
                accs = [wk * xs[q + j] if accs[q] is None else accs[q] + wk * xs[q + j] for q in range(fb)]
            for q in range(fb):
                du3[pl.ds(pl.multiple_of((t0 + q) * ng, ng), ng), :] = accs[q]
            own = xs[0:fb]
            ys = [us(q) for q in range(fb - 1)]
            for k in range(CONV_K):
                ys.append(us(k + fb - 1))
                prods = [own[q] * ys[q + k] for q in range(fb)]
                while len(prods) > 1:
                    prods = [prods[2 * v] + prods[2 * v + 1] for v in range(len(prods) // 2)]
                ddw_ref[k * ng:(k + 1) * ng, :] += prods[0]
            wgrad_block(bi, (i + 1) % 2)
            return carry

        lax.fori_loop(0, tm // fb, fir, 0)
        dc3[tm * ng:(tm + HALO) * ng, :] = dc3[0:HALO * ng, :]

        def back(rci, carry):
            base = pl.multiple_of(rci * rc, rc)
            rows = pl.ds(base, rc)
            a = p_ref[rows, 0:e].astype(F32)
            b = p_ref[rows, e:2 * e].astype(F32)
            sb = _sig(b)
            duv = _from_token_tiles(du3, base, rc, ng)
            dab = (duv * sb).astype(BF16)
            dbb = (duv * a * sb * (1.0 - sb)).astype(BF16)
            dp_ref[rows, 0:e] = dab
            dp_ref[rows, e:2 * e] = dbb
            put_dp(i % 2, rows, 0, dab)
            put_dp(i % 2, rows, e, dbb)
            return carry

        lax.fori_loop(0, tm // rc, back, 0, unroll=2)

        for m in range(wm):
            hn_s[m] = hn_ref[:, m * WG_M:(m + 1) * WG_M]

        @pl.when(i == nt - 1)
        def _():
            def last(blk, carry):
                wgrad_block(blk, i % 2)
                return carry

            lax.fori_loop(0, nblk, last, 0)

    vec = pl.BlockSpec((1, e), lambda i: (0, 0))
    taps = pl.BlockSpec((CONV_K * ng, LANES), lambda i: (0, 0))
    rev = lambda i: (nt - 1 - i, 0)
    halo = lambda i: (jnp.maximum((nt - 1 - i) * hb - 1, 0), 0)
    return pl.pallas_call(
        body, name="conv_mid_bwd", grid=(nt,),
        in_specs=[pl.BlockSpec((tm, 3 * e), rev), pl.BlockSpec((HALO, 3 * e), halo),
                  pl.BlockSpec((tm, e), rev), pl.BlockSpec((tm, e), rev), pl.BlockSpec((tm, d), rev),
                  taps, vec, vec],
        out_specs=[pl.BlockSpec((tm, 3 * e), rev), vec, vec, vec, taps,
                   pl.BlockSpec((nblk + 1, WG_M, WG_N), lambda i: (0, 0, 0))],
        out_shape=[jax.ShapeDtypeStruct((t, 3 * e), BF16), jax.ShapeDtypeStruct((1, e), F32),
                   jax.ShapeDtypeStruct((1, e), F32), jax.ShapeDtypeStruct((1, e), F32),
                   jax.ShapeDtypeStruct((CONV_K * ng, LANES), F32),
                   jax.ShapeDtypeStruct((nblk + 1, WG_M, WG_N), F32)],
        scratch_shapes=[pltpu.VMEM(((tm + HALO) * ng, LANES), F32), pltpu.VMEM(((tm + HALO) * ng, LANES), F32),
                        pltpu.VMEM((tm * ng, LANES), F32),
                        pltpu.VMEM((wm, tm, WG_M), BF16), pltpu.VMEM((2, wn, tm, WG_N), BF16)],
        compiler_params=_params("arbitrary"),
    )(p, p, c, ds, hn, dw3, lg, lb)


def _pool_mid_bwd(p, ds, wg, bg, sc):
    t = p.shape[0]
    e = p.shape[1] // 2
    ng = len(POOL_WINDOWS)
    gc = e // ng
    tm = _tile(t)
    nt = t // tm
    rc = ROW_CHUNK
    hb = tm // PHALO

    def body(p_ref, ph_ref, ds_ref, wg_ref, bg_ref, sc_ref,
             dp_ref, dwg_ref, dbg_ref, dsc_ref, ubuf, dbuf, ebuf, ddbuf):
        i = pl.program_id(0)
        ti = nt - 1 - i

        @pl.when(i == 0)
        def _():
            ebuf[tm:tm + PHALO, :] = jnp.zeros((PHALO, e), F32)
            dwg_ref[...] = jnp.zeros_like(dwg_ref)
            dbg_ref[...] = jnp.zeros_like(dbg_ref)
            dsc_ref[...] = jnp.zeros_like(dsc_ref)

        ubuf[0:PHALO, :] = jnp.where(ti > 0, ph_ref[:, 0:e].astype(F32), 0.0)
        ubuf[PHALO:PHALO + tm, :] = p_ref[:, 0:e].astype(F32)

        def recompute(rci, carry):
            base = pl.multiple_of(rci * rc, rc)
            _pool_window_dev(ubuf, base, rc, e, ti * tm, dbuf)
            return carry

        lax.fori_loop(0, tm // rc, recompute, 0)

        for g in range(ng):
            cols = slice(g * gc, (g + 1) * gc)
            dg = dbuf[:, cols].astype(BF16)
            q = jnp.dot(dg, wg_ref[g], preferred_element_type=F32) + bg_ref[:, cols]
            z = p_ref[:, e + g * gc:e + (g + 1) * gc].astype(F32)
            sz = _sig(z)
            dsv = ds_ref[:, cols].astype(F32)
            dz = dsv * (q * sc_ref[:, cols]) * _dsilu(z, sz)
            dp_ref[:, e + g * gc:e + (g + 1) * gc] = dz.astype(BF16)
            dy2 = dsv * (z * sz)
            dsc_ref[:, cols] += jnp.sum(dy2 * q, axis=0, keepdims=True)
            dq = dy2 * sc_ref[:, cols]
            dbg_ref[:, cols] += jnp.sum(dq, axis=0, keepdims=True)
            dqb = dq.astype(BF16)
            dwg_ref[g] += lax.dot_general(dg, dqb, (((0,), (0,)), ((), ())), preferred_element_type=F32)
            ddbuf[:, cols] = lax.dot_general(dqb, wg_ref[g], (((1,), (1,)), ((), ())),
                                             preferred_element_type=F32)

        def scale(rci, carry):
            base = pl.multiple_of(rci * rc, rc)
            for lc in range(e // LANES):
                lanes = slice(lc * LANES, (lc + 1) * LANES)
                w = POOL_WINDOWS[_pool_group(lc, e)]
                ebuf[pl.ds(base, rc), lanes] = (ddbuf[pl.ds(base, rc), lanes]
                                                * _pool_inv_count(ti * tm + base, rc, w))
            return carry

        lax.fori_loop(0, tm // rc, scale, 0)

        def chunk(rci, carry):
            base = pl.multiple_of(rci * rc, rc)
            n = rc + PHALO
            for lc in range(e // LANES):
                lanes = slice(lc * LANES, (lc + 1) * LANES)
                w = POOL_WINDOWS[_pool_group(lc, e)]
                acc = ebuf[pl.ds(base, n), lanes]
                step = 1
                while step < w:
                    acc = acc + pltpu.roll(acc, n - step, 0)
                    step *= 2
                du = acc[0:rc] - ddbuf[pl.ds(base, rc), lanes]
                dp_ref[pl.ds(base, rc), lanes] = du.astype(BF16)
            return carry

        lax.fori_loop(0, tm // rc, chunk, 0)
        ebuf[tm:tm + PHALO, :] = ebuf[0:PHALO, :]

    vec = pl.BlockSpec((1, e), lambda i: (0, 0))
    rev = lambda i: (nt - 1 - i, 0)
    halo = lambda i: (jnp.maximum((nt - 1 - i) * hb - 1, 0), 0)
    wspec = pl.BlockSpec((ng, gc, gc), lambda i: (0, 0, 0))
    return pl.pallas_call(
        body, name="pool_mid_bwd", grid=(nt,),
        in_specs=[pl.BlockSpec((tm, 2 * e), rev), pl.BlockSpec((PHALO, 2 * e), halo),
                  pl.BlockSpec((tm, e), rev), wspec, vec, vec],
        out_specs=[pl.BlockSpec((tm, 2 * e), rev), wspec, vec, vec],
        out_shape=[jax.ShapeDtypeStruct((t, 2 * e), BF16), jax.ShapeDtypeStruct((ng, gc, gc), F32),
                   jax.ShapeDtypeStruct((1, e), F32), jax.ShapeDtypeStruct((1, e), F32)],
        scratch_shapes=[pltpu.VMEM((tm + PHALO, e), F32), pltpu.VMEM((tm, e), F32),
                        pltpu.VMEM((tm + PHALO, e), F32), pltpu.VMEM((tm, e), F32)],
        compiler_params=_params("arbitrary"),
    )(p, p, ds, wg, bg, sc)


def _dhn_rms_bwd(dp, w4, h, g, dh_out, name):
    t, d = h.shape
    nk = w4.shape[-1]
    tm = min(MATMUL_TILE, t)

    def body(dp_ref, w_ref, h_ref, g_ref, dho_ref, dh_ref, dg_ref):
        i = pl.program_id(0)

        @pl.when(i == 0)
        def _():
            dg_ref[...] = jnp.zeros_like(dg_ref)

        dhn = jnp.zeros((tm, d), F32)
        for k in range(N_CHIPS):
            dhn = dhn + lax.dot_general(dp_ref[:, k * nk:(k + 1) * nk], w_ref[k], (((1,), (1,)), ((), ())),
                                        preferred_element_type=F32)
        hh = h_ref[...]
        r = lax.rsqrt(jnp.mean(hh * hh, axis=-1, keepdims=True) + RMS_EPS)
        hhat = hh * r
        tt = dhn * g_ref[...]
        dh_ref[...] = dho_ref[...] + r * (tt - hhat * jnp.mean(tt * hhat, axis=-1, keepdims=True))
        dg_ref[...] += jnp.sum(dhn * hhat, axis=0, keepdims=True)

    return pl.pallas_call(
        body, name=name, grid=(t // tm,),
        in_specs=[pl.BlockSpec((tm, N_CHIPS * nk), lambda i: (i, 0)),
                  _resident((N_CHIPS, d, nk)),
                  pl.BlockSpec((tm, d), lambda i: (i, 0)), pl.BlockSpec((1, d), lambda i: (0, 0)),
                  pl.BlockSpec((tm, d), lambda i: (i, 0))],
        out_specs=[pl.BlockSpec((tm, d), lambda i: (i, 0)), pl.BlockSpec((1, d), lambda i: (0, 0))],
        out_shape=[jax.ShapeDtypeStruct((t, d), F32), jax.ShapeDtypeStruct((1, d), F32)],
        compiler_params=_params("arbitrary"),
    )(dp, w4, h, g, dh_out)


def _wgrad(a, b, nblk, name):
    t, m = a.shape
    nn = b.shape[1] // nblk
    tk = min(WGRAD_TILE, t)
    nk = t // tk

    def body(a_ref, b_ref, o_ref, ob_ref):
        @pl.when(pl.program_id(1) == 0)
        def _():
            o_ref[...] = jnp.zeros_like(o_ref)

        o_ref[...] += lax.dot_general(a_ref[...].astype(BF16), b_ref[...].astype(BF16),
                                      (((0,), (0,)), ((), ())), preferred_element_type=F32)

        @pl.when(pl.program_id(1) == nk - 1)
        def _():
            ob_ref[...] = o_ref[...].astype(BF16)

    ospec = pl.BlockSpec((None, m, nn), lambda j, i: (j, 0, 0))
    return pl.pallas_call(
        body, name=name, grid=(nblk, nk),
        in_specs=[pl.BlockSpec((tk, m), lambda j, i: (i, 0)), pl.BlockSpec((tk, nn), lambda j, i: (i, j))],
        out_specs=[ospec, ospec],
        out_shape=[jax.ShapeDtypeStruct((nblk, m, nn), F32), jax.ShapeDtypeStruct((nblk, m, nn), BF16)],
        compiler_params=_params("parallel", "arbitrary"),
    )(a, b)


def _rows2d(shape):
    rows = 1
    for s in shape[:-1]:
        rows *= s
    return rows, shape[-1]


def _row_tile(rows):
    for cand in (512, 256, 128, 64, 32, 16, 8):
        if rows % cand == 0:
            return cand
    return rows


def _add_eight(own, landed, chip_core, name):
    _, _, rows, cols = own.shape
    tr = _row_tile(rows)

    def body(sel_ref, s_ref, r_ref, o_ref):
        acc = s_ref[...]
        for peer in range(N_DEV - 1):
            acc = acc + r_ref[peer].astype(F32)
        o_ref[...] = acc

    return pl.pallas_call(
        body, name=name,
        grid_spec=pltpu.PrefetchScalarGridSpec(
            num_scalar_prefetch=1, grid=(rows // tr,),
            in_specs=[pl.BlockSpec((None, None, tr, cols), lambda i, s: (s[0], s[1], i, 0)),
                      pl.BlockSpec((N_DEV - 1, tr, cols), lambda i, s: (0, i, 0))],
            out_specs=pl.BlockSpec((None, tr, cols), lambda i, s: (s[1], i, 0))),
        out_shape=jax.ShapeDtypeStruct((2, rows, cols), F32),
        compiler_params=_params("parallel"),
    )(chip_core, own, landed)


def _adamw(w, g, m, v, name):
    rows, cols = w.shape
    tr = _row_tile(rows)

    def body(w_ref, g_ref, m_ref, v_ref, d_ref, m2_ref, v2_ref):
        gg = g_ref[...]
        m2 = ADAM_B1 * m_ref[...] + (1.0 - ADAM_B1) * gg
        v2 = ADAM_B2 * v_ref[...] + (1.0 - ADAM_B2) * (gg * gg)
        m_hat = m2 / (1.0 - ADAM_B1 ** ADAM_STEP)
        v_hat = v2 / (1.0 - ADAM_B2 ** ADAM_STEP)
        d_ref[...] = -ADAM_LR * (m_hat / (jnp.sqrt(v_hat) + ADAM_EPS) + ADAM_WD * w_ref[...])
        m2_ref[...] = m2
        v2_ref[...] = v2

    spec = pl.BlockSpec((tr, cols), lambda i: (i, 0))
    shp = jax.ShapeDtypeStruct((rows, cols), F32)
    return pl.pallas_call(
        body, name=name, grid=(rows // tr,),
        in_specs=[spec, spec, spec, spec], out_specs=[spec, spec, spec], out_shape=[shp, shp, shp],
        compiler_params=_params("parallel"),
    )(w, g, m, v)


ANY = pl.BlockSpec(memory_space=pl.ANY)


def _place():
    x, y, c = lax.axis_index("x"), lax.axis_index("y"), lax.axis_index("c")
    chips = [(1 - x, y), (x, 1 - y), (1 - x, 1 - y)]
    return x, y, c, chips


def _allgather_weights(shards):
    n = len(shards)

    def body(*refs):
        ins, outs = refs[:n], refs[n:2 * n]
        send_ici, recv_ici, send_d2d, recv_d2d = refs[2 * n:]
        x, y, c, chips = _place()
        k0 = 2 * x + y
        sib = (x, y, 1 - c)

        def ici(a, r, src_chip, target):
            return pltpu.make_async_remote_copy(
                src_ref=ins[a].at[c], dst_ref=outs[a].at[src_chip, c],
                send_sem=send_ici.at[a * 3 + r], recv_sem=recv_ici.at[a * 3 + r],
                device_id=target, device_id_type=MESH)

        def d2d(a, r, src_chip, layer):
            return pltpu.make_async_remote_copy(
                src_ref=outs[a].at[src_chip, layer], dst_ref=outs[a].at[src_chip, layer],
                send_sem=send_d2d.at[a * 3 + r], recv_sem=recv_d2d.at[a * 3 + r],
                device_id=sib, device_id_type=MESH)

        first = [ici(a, r, k0, (cx, cy, c)) for a in range(n) for r, (cx, cy) in enumerate(chips)]
        for cp in first:
            cp.start()
        passed = []
        for a in range(n):
            for r, (cx, cy) in enumerate(chips):
                ici(a, r, 2 * cx + cy, (cx, cy, c)).wait_recv()
                cp = d2d(a, r, 2 * cx + cy, c)
                cp.start()
                passed.append(cp)
        for a in range(n):
            for r, (cx, cy) in enumerate(chips):
                d2d(a, r, 2 * cx + cy, 1 - c).wait_recv()
        for cp in first + passed:
            cp.wait_send()

    return pl.pallas_call(
        body, name="allgather_weights",
        in_specs=[ANY] * n, out_specs=[ANY] * n,
        out_shape=[jax.ShapeDtypeStruct((N_CHIPS,) + s.shape, s.dtype) for s in shards],
        scratch_shapes=[pltpu.SemaphoreType.DMA((3 * n,)), pltpu.SemaphoreType.DMA((3 * n,)),
                        pltpu.SemaphoreType.DMA((3 * n,)), pltpu.SemaphoreType.DMA((3 * n,))],
    )(*shards)


def _put_own(gathered, shard, chip):
    return lax.dynamic_update_slice_in_dim(gathered, shard[None], chip, axis=0)


HBM = pl.BlockSpec(memory_space=pltpu.HBM)
SEM = pl.BlockSpec(memory_space=pltpu.SEMAPHORE)
DATAFLOW = pltpu.SideEffectType.DATAFLOW_SIDE_EFFECTING
FLIPS = [(0, 0, 1), (0, 1, 0), (0, 1, 1), (1, 0, 0), (1, 0, 1), (1, 1, 0), (1, 1, 1)]


def _gather_plan(srcs, lands, send_sem, recv_sem):
    x, y, c, chips = _place()
    return [pltpu.make_async_remote_copy(
        src_ref=srcs[a], dst_ref=lands[a].at[2 * x + y],
        send_sem=send_sem.at[a * 3 + r], recv_sem=recv_sem.at[a * 3 + r],
        device_id=(cx, cy, c), device_id_type=MESH)
        for a in range(len(srcs)) for r, (cx, cy) in enumerate(chips)]


def _scatter_plan(srcs, lands, send_sem, recv_sem):
    x, y, c, _ = _place()
    cps = []
    for a in range(len(srcs)):
        for r, (fx, fy, fc) in enumerate(FLIPS):
            tx, ty, tc = (1 - x if fx else x), (1 - y if fy else y), (1 - c if fc else c)
            cps.append(pltpu.make_async_remote_copy(
                src_ref=srcs[a].at[2 * tx + ty, tc], dst_ref=lands[a].at[r],
                send_sem=send_sem.at[a * len(FLIPS) + r], recv_sem=recv_sem.at[a * len(FLIPS) + r],
                device_id=(tx, ty, tc), device_id_type=MESH))
    return cps


def _split_start(name, plan, srcs, lands, n_copies, after):
    n = len(srcs)

    def body(*refs):
        src, land = refs[:n], refs[n:2 * n]
        send_sem, recv_sem = refs[2 * n + 1], refs[2 * n + 2]
        token = refs[-1]
        for cp in plan(src, land, send_sem, recv_sem):
            cp.start()
        token[...] = jnp.zeros_like(token)

    outs = pl.pallas_call(
        body, name=name,
        in_specs=[HBM] * (2 * n) + [ANY],
        out_specs=[SEM, SEM] + [HBM] * (2 * n) + [pl.BlockSpec(memory_space=pltpu.VMEM)],
        out_shape=[pltpu.SemaphoreType.DMA((n_copies,)), pltpu.SemaphoreType.DMA((n_copies,))]
        + [pltpu.HBM(s.shape, s.dtype) for s in srcs] + [pltpu.HBM(l.shape, l.dtype) for l in lands]
        + [jax.ShapeDtypeStruct((8, LANES), F32)],
        input_output_aliases={i: 2 + i for i in range(2 * n)},
        compiler_params=pltpu.CompilerParams(has_side_effects=DATAFLOW),
    )(*[pltpu.with_memory_space_constraint(s, pltpu.HBM) for s in srcs],
      *[pltpu.with_memory_space_constraint(l, pltpu.HBM) for l in lands], after)
    return outs[0], outs[1], list(outs[2:2 + n]), list(outs[2 + n:2 + 2 * n]), outs[-1]


def _split_wait(name, plan, send_sems, recv_sems, srcs, lands, after):
    n = len(srcs)

    def body(*refs):
        src, land = refs[:n], refs[n:2 * n]
        send_sem, recv_sem = refs[2 * n], refs[2 * n + 1]
        for cp in plan(src, land, send_sem, recv_sem):
            cp.wait_send()
            cp.wait_recv()

    outs = pl.pallas_call(
        body, name=name,
        in_specs=[HBM] * (2 * n) + [SEM, SEM, ANY],
        out_specs=[HBM] * (2 * n),
        out_shape=[pltpu.HBM(s.shape, s.dtype) for s in srcs] + [pltpu.HBM(l.shape, l.dtype) for l in lands],
        input_output_aliases={i: i for i in range(2 * n)},
        compiler_params=pltpu.CompilerParams(has_side_effects=DATAFLOW),
    )(*srcs, *lands, send_sems, recv_sems, after)
    return list(outs[:n]), list(outs[n:])


def _share_halves(halves):
    n = len(halves)

    def body(*refs):
        ins, outs = refs[:n], refs[n:2 * n]
        send_sem, recv_sem = refs[2 * n:]
        x, y, c, _ = _place()
        cps = [pltpu.make_async_remote_copy(
            src_ref=outs[a].at[c], dst_ref=outs[a].at[c], send_sem=send_sem.at[a], recv_sem=recv_sem.at[a],
            device_id=(x, y, 1 - c), device_id_type=MESH) for a in range(n)]
        for cp in cps:
            cp.start()
        for cp in cps:
            cp.wait()

    return pl.pallas_call(
        body, name="share_halves",
        in_specs=[ANY] * n, out_specs=[ANY] * n,
        out_shape=[jax.ShapeDtypeStruct(h.shape, h.dtype) for h in halves],
        input_output_aliases={a: a for a in range(n)},
        scratch_shapes=[pltpu.SemaphoreType.DMA((n,)), pltpu.SemaphoreType.DMA((n,))],
    )(*halves)


N_DEV = 8


def _allreduce_small(v):
    m, nc = v.shape

    def body(x_ref, out_ref, gat, send_sems, recv_sems, local_sem):
        x, y, c, chips = _place()
        me, sib = (x, y, c), (x, y, 1 - c)

        def rows(px, py, pc):
            return gat.at[pl.ds((4 * px + 2 * py + pc) * m, m), :]

        def copy(k, block, to, src=None):
            return pltpu.make_async_remote_copy(
                src_ref=rows(*block) if src is None else src, dst_ref=rows(*block),
                send_sem=send_sems.at[k], recv_sem=recv_sems.at[k], device_id=to, device_id_type=MESH)

        mine = pltpu.make_async_copy(x_ref, rows(*me), local_sem)
        mine.start()
        first = [copy(0, me, sib, src=x_ref)]
        first += [copy(1 + j, me, (*chip, c), src=x_ref) for j, chip in enumerate(chips)]
        for cp in first:
            cp.start()
        passed = [copy(4 + j, (*chip, c), sib) for j, chip in enumerate(chips)]
        for j, chip in enumerate(chips):
            copy(1 + j, (*chip, c), me).wait_recv()
            passed[j].start()
        copy(0, sib, me).wait_recv()
        for j, chip in enumerate(chips):
            copy(4 + j, (*chip, 1 - c), me).wait_recv()
        for cp in first + passed:
            cp.wait_send()
        mine.wait()
        acc = gat[0:m, :]
        for dev in range(1, N_DEV):
            acc = acc + gat[dev * m:(dev + 1) * m, :]
        out_ref[...] = acc

    return pl.pallas_call(
        body, name="allreduce_small",
        in_specs=[pl.BlockSpec(memory_space=pltpu.VMEM)],
        out_specs=pl.BlockSpec(memory_space=pltpu.VMEM),
        out_shape=jax.ShapeDtypeStruct((m, nc), F32),
        scratch_shapes=[pltpu.VMEM((N_DEV * m, nc), F32), pltpu.SemaphoreType.DMA((7,)),
                        pltpu.SemaphoreType.DMA((7,)), pltpu.SemaphoreType.DMA],
        compiler_params=pltpu.CompilerParams(vmem_limit_bytes=VMEM_LIMIT),
    )(v)


def _pad_rows(a, rows):
    return jnp.pad(a, ((0, rows - a.shape[0]), (0, 0)))


def kernel(x, norm_g, final_g, conv_w_in, conv_dw, conv_dw_b, conv_ln_g, conv_ln_b, conv_w_out, pool_w_in, pool_w_grp, pool_b_grp, pool_scale, pool_w_out, loss_target, m_norm_g, m_final_g, m_conv_w_in, m_conv_dw, m_conv_dw_b, m_conv_ln_g, m_conv_ln_b, m_conv_w_out, m_pool_w_in, m_pool_w_grp, m_pool_b_grp, m_pool_scale, m_pool_w_out, v_norm_g, v_final_g, v_conv_w_in, v_conv_dw, v_conv_dw_b, v_conv_ln_g, v_conv_ln_b, v_conv_w_out, v_pool_w_in, v_pool_w_grp, v_pool_b_grp, v_pool_scale, v_pool_w_out):
    t, d = x.shape[1], x.shape[2]
    e = conv_w_out.shape[2]
    ng = len(POOL_WINDOWS)
    gc = e // ng
    gcs = pool_w_grp.shape[2]
    ck = conv_dw.shape[1]
    es = conv_dw.shape[2]
    xi, yi, ci = lax.axis_index("x"), lax.axis_index("y"), lax.axis_index("c")
    chip = 2 * xi + yi

    small_rows = ck + 2
    small_pad = -(-small_rows // 8) * 8
    small = jnp.concatenate([conv_dw, pool_b_grp[:, None, :], pool_scale[:, None, :],
                             jnp.zeros((2, small_pad - small_rows, es), F32)], axis=1)
    cwi_b, cwo_b = conv_w_in.astype(BF16), conv_w_out.astype(BF16)

    def halves(a):
        return a.reshape(2, a.shape[0] // 2, a.shape[1])

    first = [halves(cwi_b[0]), halves(cwo_b[0]), small]
    g_cwi0, g_cwo0, g_small = [_put_own(g, q, chip) for g, q in zip(_allgather_weights(first), first)]
    rest = [cwi_b[1], cwo_b[1], pool_w_in.astype(BF16), pool_w_grp.astype(BF16), pool_w_out.astype(BF16)]
    rest_lands = [lax.empty((N_CHIPS,) + r.shape, r.dtype) for r in rest]
    ag_send, ag_recv, rest, rest_lands, ag_token = _split_start(
        "gather_rest_start", _gather_plan, rest, rest_lands, 3 * len(rest), g_small)
    smallf = jnp.transpose(g_small, (1, 2, 0, 3)).reshape(2, small_pad, N_CHIPS * es)

    h = x.reshape(t, d)
    tgt = loss_target.reshape(t, d)
    hs, saved = [], []
    for layer in range(4):
        j = layer // 2
        hs.append(h)
        gvec = norm_g[layer][None, :]
        if layer == 0:
            gvec = gvec + ag_token[0:1, 0:1]
        if layer == 1:
            rest, rest_lands = _split_wait("gather_rest_wait", _gather_plan, ag_send, ag_recv, rest, rest_lands, h)
            g_cwi1, g_cwo1, g_pwi, g_pwg, g_pwo = [_put_own(g, q, chip) for g, q in zip(rest_lands, rest)]
            wg_full = jnp.transpose(g_pwg, (1, 2, 0, 3, 4)).reshape(2, ng, gc, gc)
        if layer % 2 == 0:
            g_in, g_out = (g_cwi0, g_cwo0) if j == 0 else (g_cwi1, g_cwo1)
            w_in4 = g_in.reshape(N_CHIPS, d, -1)
            w_out = g_out.reshape(e, d)
            p, hn = _rms_matmul(h, gvec, w_in4, "rms_matmul_conv")
            dw_full = smallf[j, 0:ck]
            s, c = _conv_mid_fwd(p, dw_full.reshape(-1, LANES), conv_dw_b[j].reshape(-1, LANES),
                                 conv_ln_g[j][None, :], conv_ln_b[j][None, :])
            saved.append((p, hn, s, c, w_in4, w_out, dw_full))
        else:
            w_in4 = g_pwi[:, j]
            w_out = g_pwo[:, j].reshape(e, d)
            p, hn = _rms_matmul(h, gvec, w_in4, "rms_matmul_pool")
            bg_full = smallf[j, ck:ck + 1]
            sc_full = smallf[j, ck + 1:ck + 2]
            s = _pool_mid_fwd(p, wg_full[j], bg_full, sc_full)
            saved.append((p, hn, s, None, w_in4, w_out, (wg_full[j], bg_full, sc_full)))
        h = _matmul_res(h, s, w_out)

    dh, loss_part, dfg = _loss_head(h, final_g[None, :], tgt)
    loss = lax.psum(loss_part[0, 0], ("x", "y", "c"))

    def by_half(a):
        return a.reshape(N_CHIPS, 2, a.shape[1] // 2, a.shape[2])

    dng = [None] * 4
    g_conv = [None, None]
    g_pool = [None, None]
    flights = {}
    for layer in (3, 2, 1, 0):
        j = layer // 2
        p, hn, s, c, w_in4, w_out, extra = saved[layer]
        gvec = norm_g[layer][None, :]
        ds = _ds_matmul(dh, w_out)
        dw_out, dw_out_b = [q.reshape(N_CHIPS, e // N_CHIPS, d) for q in _wgrad(s, dh, 1, "wgrad_out")]
        if layer % 2 == 0:
            dp, dlg, dlb, ddwb, ddw3, dwt = _conv_mid_bwd(p, c, ds, hn, extra.reshape(-1, LANES),
                                                          conv_ln_g[j][None, :], conv_ln_b[j][None, :])
            ddw = ddw3.reshape(ck, e)
            wm, wn = d // WG_M, 3 * e // WG_N
            dw_full = jnp.transpose(dwt[:wm * wn].reshape(wm, wn, WG_M, WG_N), (0, 2, 1, 3)).reshape(d, 3 * e)
            dw_in = jnp.transpose(dw_full.reshape(d, N_CHIPS, 3 * e // N_CHIPS), (1, 0, 2))
            dw_in_b = dw_in.astype(BF16)
            own, pay = [dw_in, dw_out], [dw_in_b, dw_out_b]
            g_conv[j] = (dlg, dlb, ddwb, ddw)
        else:
            wg, bg_full, sc_full = extra
            dp, dwg, dbg, dsc = _pool_mid_bwd(p, ds, wg, bg_full, sc_full)
            dw_in, dw_in_b = _wgrad(hn, dp, N_CHIPS, "wgrad_in_pool")
            dwg4 = jnp.transpose(dwg.reshape(ng, N_CHIPS, gcs, gc), (1, 0, 2, 3)).reshape(N_CHIPS, ng * gcs, gc)
            own, pay = [dw_in, dw_out, dwg4], [dw_in_b, dw_out_b, dwg4.astype(BF16)]
            g_pool[j] = (dbg, dsc)
        pay = [by_half(q) for q in pay]
        lands = [lax.empty((len(FLIPS),) + q.shape[2:], BF16) for q in pay]
        send, recv, pay, lands, token = _split_start(
            "scatter_start_%d" % layer, _scatter_plan, pay, lands, len(FLIPS) * len(pay), own[0])
        flights[layer] = (send, recv, pay, lands, own)
        kind = "dhn_rms_bwd_conv" if layer % 2 == 0 else "dhn_rms_bwd_pool"
        dh, dng[layer] = _dhn_rms_bwd(dp, w_in4, hs[layer], gvec + token[0:1, 0:1], dh, kind)
    grad_x = dh.reshape(x.shape)

    sel_kc = jnp.stack([chip, ci]).astype(jnp.int32)
    summed = {}
    after = dh
    for layer in (3, 2, 1, 0):
        send, recv, pay, lands, own = flights[layer]
        _, lands = _split_wait("scatter_wait_%d" % layer, _scatter_plan, send, recv, pay, lands, after)
        after = lands[0]
        summed[layer] = [_add_eight(by_half(o), l, sel_kc, "add_eight_%d_%d" % (layer % 2, a))
                         for a, (o, l) in enumerate(zip(own, lands))]
    order = [(0, 0), (0, 1), (2, 0), (2, 1), (1, 0), (1, 1), (1, 2), (3, 0), (3, 1), (3, 2)]
    shared = _share_halves([summed[l][a] for l, a in order])
    full = {la: q.reshape(q.shape[0] * q.shape[1], q.shape[2]) for la, q in zip(order, shared)}
    g_cwi_f = jnp.stack([full[(0, 0)], full[(2, 0)]])
    g_cwo_f = jnp.stack([full[(0, 1)], full[(2, 1)]])
    g_pwi_f = jnp.stack([full[(1, 0)], full[(3, 0)]])
    g_pwo_f = jnp.stack([full[(1, 1)], full[(3, 1)]])
    g_pwg_f = jnp.stack([full[(1, 2)], full[(3, 2)]]).reshape(pool_w_grp.shape)

    rows_list = [dng[0], dng[1], dng[2], dng[3], dfg,
                 g_conv[0][2], g_conv[1][2], g_conv[0][0], g_conv[1][0], g_conv[0][1], g_conv[1][1],
                 g_pool[0][0], g_pool[1][0], g_pool[0][1], g_pool[1][1], g_conv[0][3], g_conv[1][3]]
    slab = jnp.concatenate(rows_list, axis=0)
    nrows = slab.shape[0]
    slab = _pad_rows(slab, -(-nrows // 8) * 8)
    tot = _allreduce_small(slab)
    g_norm_g = tot[0:4]
    g_final_g = tot[4]
    g_dwb = tot[5:7]
    g_lng = tot[7:9]
    g_lnb = tot[9:11]
    g_bg = lax.dynamic_slice_in_dim(tot[11:13], chip * es, es, axis=1)
    g_sc = lax.dynamic_slice_in_dim(tot[13:15], chip * es, es, axis=1)
    g_dw = lax.dynamic_slice_in_dim(tot[15:15 + 2 * ck].reshape(2, ck, e), chip * es, es, axis=2)

    def adam_nd(w, g, m, v, nm):
        rows, cols = _rows2d(w.shape)
        outs = _adamw(w.reshape(rows, cols), g.reshape(rows, cols), m.reshape(rows, cols),
                      v.reshape(rows, cols), "adamw_" + nm)
        return [o.reshape(w.shape) for o in outs]

    res = {}
    res["conv_w_in"] = (g_cwi_f, *adam_nd(conv_w_in, g_cwi_f, m_conv_w_in, v_conv_w_in, "cwi"))
    res["conv_w_out"] = (g_cwo_f, *adam_nd(conv_w_out, g_cwo_f, m_conv_w_out, v_conv_w_out, "cwo"))
    res["pool_w_in"] = (g_pwi_f, *adam_nd(pool_w_in, g_pwi_f, m_pool_w_in, v_pool_w_in, "pwi"))
    res["pool_w_grp"] = (g_pwg_f, *adam_nd(pool_w_grp, g_pwg_f, m_pool_w_grp, v_pool_w_grp, "pwg"))
    res["pool_w_out"] = (g_pwo_f, *adam_nd(pool_w_out, g_pwo_f, m_pool_w_out, v_pool_w_out, "pwo"))

    def pack(parts, rows_to):
        return _pad_rows(jnp.concatenate([q.reshape(-1, q.shape[-1]) for q in parts], axis=0), rows_to)

    rep_w = [norm_g, final_g[None, :], conv_dw_b, conv_ln_g, conv_ln_b]
    rep_g = [g_norm_g, g_final_g[None, :], g_dwb, g_lng, g_lnb]
    rep_m = [m_norm_g, m_final_g[None, :], m_conv_dw_b, m_conv_ln_g, m_conv_ln_b]
    rep_v = [v_norm_g, v_final_g[None, :], v_conv_dw_b, v_conv_ln_g, v_conv_ln_b]
    rep = _adamw(pack(rep_w, 16), pack(rep_g, 16), pack(rep_m, 16), pack(rep_v, 16), "adamw_rep")
    rep_names = ["norm_g", "final_g", "conv_dw_b", "conv_ln_g", "conv_ln_b"]
    rep_rows = [(0, 4), (4, 5), (5, 7), (7, 9), (9, 11)]
    for nm, (lo, hi), gq, wq in zip(rep_names, rep_rows, rep_g, rep_w):
        shape = (d,) if nm == "final_g" else wq.shape
        res[nm] = (gq.reshape(shape), *[o[lo:hi].reshape(shape) for o in rep])

    sh_w = [conv_dw, pool_b_grp, pool_scale]
    sh_g = [g_dw, g_bg, g_sc]
    sh_m = [m_conv_dw, m_pool_b_grp, m_pool_scale]
    sh_v = [v_conv_dw, v_pool_b_grp, v_pool_scale]
    sh_total = 2 * ck + 4
    sh_pad = -(-sh_total // 8) * 8
    shd = _adamw(pack(sh_w, sh_pad), pack(sh_g, sh_pad), pack(sh_m, sh_pad), pack(sh_v, sh_pad), "adamw_shard")
    sh_names = ["conv_dw", "pool_b_grp", "pool_scale"]
    sh_rows = [(0, 2 * ck), (2 * ck, 2 * ck + 2), (2 * ck + 2, 2 * ck + 4)]
    for nm, (lo, hi), gq, wq in zip(sh_names, sh_rows, sh_g, sh_w):
        res[nm] = (gq.reshape(wq.shape), *[o[lo:hi].reshape(wq.shape) for o in shd])

    order = ["norm_g", "final_g", "conv_w_in", "conv_dw", "conv_dw_b", "conv_ln_g", "conv_ln_b", "conv_w_out",
             "pool_w_in", "pool_w_grp", "pool_b_grp", "pool_scale", "pool_w_out"]
    outs = [loss, grad_x]
    for part in range(4):
        outs += [res[nm][part] for nm in order]
    return tuple(outs)
```

```python
import functools

import jax
import jax.numpy as jnp
from jax import lax
from jax.experimental import pallas as pl
from jax.experimental.pallas import tpu as pltpu

F32 = jnp.float32
BF16 = jnp.bfloat16
MESH = pl.DeviceIdType.MESH

RMS_EPS = 1e-6
LN_EPS = 1e-5
CONV_K = 31
HALO = 32
PHALO = 16
POOL_WINDOWS = (2, 4, 8, 16)
N_CHIPS = 4
LANES = 128
ROW_CHUNK = 32
CONV_ROW_CHUNK = 32
FIR_BLOCK = 16
TOKEN_TILE = 512
MATMUL_TILE = 1024
WGRAD_TILE = 2048
VMEM_LIMIT = 56 * 1024 * 1024

ADAM_LR = 0.001
ADAM_B1 = 0.9
ADAM_B2 = 0.999
ADAM_EPS = 1e-08
ADAM_WD = 0.01
ADAM_STEP = 10


def _params(*sem):
    return pltpu.CompilerParams(dimension_semantics=sem, vmem_limit_bytes=VMEM_LIMIT)


def _sig(v):
    return 0.5 * jnp.tanh(0.5 * v) + 0.5


def _dsilu(v, sv):
    return sv * (1.0 + v * (1.0 - sv))


def _resident(shape):
    return pl.BlockSpec(shape, lambda *_: (0,) * len(shape), pipeline_mode=pl.Buffered(1))


def _tile(t):
    return min(TOKEN_TILE, t)


def _rms_matmul(h, g, w4, name):
    t, d = h.shape
    nk = w4.shape[-1]
    tm = min(MATMUL_TILE, t)

    def body(h_ref, g_ref, w_ref, p_ref, hn_ref):
        hh = h_ref[...]
        r = lax.rsqrt(jnp.mean(hh * hh, axis=-1, keepdims=True) + RMS_EPS)
        hn = (hh * r * g_ref[...]).astype(BF16)
        hn_ref[...] = hn
        for k in range(N_CHIPS):
            p_ref[:, k * nk:(k + 1) * nk] = jnp.dot(hn, w_ref[k], preferred_element_type=F32).astype(BF16)

    return pl.pallas_call(
        body, name=name, grid=(t // tm,),
        in_specs=[pl.BlockSpec((tm, d), lambda i: (i, 0)),
                  pl.BlockSpec((1, d), lambda i: (0, 0)),
                  _resident((N_CHIPS, d, nk))],
        out_specs=[pl.BlockSpec((tm, N_CHIPS * nk), lambda i: (i, 0)),
                   pl.BlockSpec((tm, d), lambda i: (i, 0))],
        out_shape=[jax.ShapeDtypeStruct((t, N_CHIPS * nk), BF16), jax.ShapeDtypeStruct((t, d), BF16)],
        compiler_params=_params("parallel"),
    )(h, g, w4)


def _to_token_tiles(ref, tok0, rows, val, ng):
    for j in range(ng):
        ref[pl.ds(tok0 * ng + j, rows, stride=ng), :] = val[:, j * LANES:(j + 1) * LANES]


def _from_token_tiles(ref, tok0, rows, ng):
    return jnp.concatenate([ref[pl.ds(tok0 * ng + j, rows, stride=ng), :] for j in range(ng)], axis=1)


def _conv_mid_fwd(p, dw3, dwb3, lg, lb):
    t = p.shape[0]
    e = p.shape[1] // 3
    ng = e // LANES
    tm = _tile(t)
    rc = CONV_ROW_CHUNK
    fb = FIR_BLOCK

    def body(p_ref, dw_ref, dwb_ref, lg_ref, lb_ref, s_ref, c_ref, u3, c3):
        i = pl.program_id(0)

        @pl.when(i == 0)
        def _():
            u3[0:HALO * ng, :] = jnp.zeros((HALO * ng, LANES), F32)

        def glu(rci, carry):
            base = pl.multiple_of(rci * rc, rc)
            a = p_ref[pl.ds(base, rc), 0:e].astype(F32)
            b = p_ref[pl.ds(base, rc), e:2 * e].astype(F32)
            _to_token_tiles(u3, HALO + base, rc, a * _sig(b), ng)
            return carry

        lax.fori_loop(0, tm // rc, glu, 0)

        def fir(bi, carry):
            t0 = bi * fb
            def x(q):
                return u3[pl.ds(pl.multiple_of((t0 + HALO - (CONV_K - 1) + q) * ng, ng), ng), :]

            xs = [x(q) for q in range(fb - 1)]
            accs = [dwb_ref[...]] * fb
            for k in range(CONV_K):
                wk = dw_ref[k * ng:(k + 1) * ng, :]
                xs.append(x(k + fb - 1))
                accs = [accs[q] + wk * xs[q + k] for q in range(fb)]
            for q in range(fb):
                c3[pl.ds(pl.multiple_of((t0 + q) * ng, ng), ng), :] = accs[q]
            return carry

        lax.fori_loop(0, tm // fb, fir, 0)
        u3[0:HALO * ng, :] = u3[tm * ng:(tm + HALO) * ng, :]

        def chunk(rci, carry):
            base = pl.multiple_of(rci * rc, rc)
            c = _from_token_tiles(c3, base, rc, ng)
            mu = jnp.mean(c, axis=-1, keepdims=True)
            cc = c - mu
            var = jnp.mean(cc * cc, axis=-1, keepdims=True)
            ln = cc * lax.rsqrt(var + LN_EPS) * lg_ref[...] + lb_ref[...]
            z = p_ref[pl.ds(base, rc), 2 * e:3 * e].astype(F32)
            s = (ln * _sig(ln)) * (z * _sig(z))
            s_ref[pl.ds(base, rc), :] = s.astype(BF16)
            c_ref[pl.ds(base, rc), :] = c.astype(BF16)
            return carry

        lax.fori_loop(0, tm // rc, chunk, 0, unroll=2)

    vec = pl.BlockSpec((1, e), lambda i: (0, 0))
    return pl.pallas_call(
        body, name="conv_mid_fwd", grid=(t // tm,),
        in_specs=[pl.BlockSpec((tm, 3 * e), lambda i: (i, 0)),
                  pl.BlockSpec((CONV_K * ng, LANES), lambda i: (0, 0)),
                  pl.BlockSpec((ng, LANES), lambda i: (0, 0)), vec, vec],
        out_specs=[pl.BlockSpec((tm, e), lambda i: (i, 0)), pl.BlockSpec((tm, e), lambda i: (i, 0))],
        out_shape=[jax.ShapeDtypeStruct((t, e), BF16), jax.ShapeDtypeStruct((t, e), BF16)],
        scratch_shapes=[pltpu.VMEM(((tm + HALO) * ng, LANES), F32), pltpu.VMEM((tm * ng, LANES), F32)],
        compiler_params=_params("arbitrary"),
    )(p, dw3, dwb3, lg, lb)


def _conv_mid_out_fwd(p, h, w_out, dw3, dwb3, lg, lb):
    t = p.shape[0]
    e = p.shape[1] // 3
    d = h.shape[1]
    ng = e // LANES
    tm = _tile(t)
    nt = t // tm
    rc = CONV_ROW_CHUNK
    fb = FIR_BLOCK

    def body(p_ref, h_ref, w_ref, dw_ref, dwb_ref, lg_ref, lb_ref, s_ref, c_ref, ho_ref, u3, c3, s_prev):
        i = pl.program_id(0)

        def project():
            ho_ref[...] = h_ref[...] + jnp.dot(s_prev[...], w_ref[...], preferred_element_type=F32)

        @pl.when(i == 0)
        def _():
            u3[0:HALO * ng, :] = jnp.zeros((HALO * ng, LANES), F32)
            s_prev[...] = jnp.zeros_like(s_prev)

        @pl.when(i < nt)
        def _():
            def glu(rci, carry):
                base = pl.multiple_of(rci * rc, rc)
                a = p_ref[pl.ds(base, rc), 0:e].astype(F32)
                b = p_ref[pl.ds(base, rc), e:2 * e].astype(F32)
                _to_token_tiles(u3, HALO + base, rc, a * _sig(b), ng)
                return carry

            lax.fori_loop(0, tm // rc, glu, 0)

            project()
            for bi in range(tm // fb):
                t0 = bi * fb
                xs = [u3[(t0 + HALO - (CONV_K - 1) + q) * ng:(t0 + HALO - (CONV_K - 1) + q + 1) * ng, :]
                      for q in range(fb - 1)]
                accs = [dwb_ref[...]] * fb
                for k in range(CONV_K):
                    wk = dw_ref[k * ng:(k + 1) * ng, :]
                    q1 = t0 + HALO - (CONV_K - 1) + k + fb - 1
                    xs.append(u3[q1 * ng:(q1 + 1) * ng, :])
                    accs = [accs[q] + wk * xs[q + k] for q in range(fb)]
                for q in range(fb):
                    c3[(t0 + q) * ng:(t0 + q + 1) * ng, :] = accs[q]
            u3[0:HALO * ng, :] = u3[tm * ng:(tm + HALO) * ng, :]

            def chunk(rci, carry):
                base = pl.multiple_of(rci * rc, rc)
                c = _from_token_tiles(c3, base, rc, ng)
                mu = jnp.mean(c, axis=-1, keepdims=True)
                cc = c - mu
                var = jnp.mean(cc * cc, axis=-1, keepdims=True)
                ln = cc * lax.rsqrt(var + LN_EPS) * lg_ref[...] + lb_ref[...]
                z = p_ref[pl.ds(base, rc), 2 * e:3 * e].astype(F32)
                s = ((ln * _sig(ln)) * (z * _sig(z))).astype(BF16)
                s_ref[pl.ds(base, rc), :] = s
                s_prev[pl.ds(base, rc), :] = s
                c_ref[pl.ds(base, rc), :] = c.astype(BF16)
                return carry

            lax.fori_loop(0, tm // rc, chunk, 0, unroll=2)

        @pl.when(i == nt)
        def _():
            project()

    vec = pl.BlockSpec((1, e), lambda i: (0, 0))
    cur = lambda i: (jnp.minimum(i, nt - 1), 0)
    lag = lambda i: (jnp.maximum(i - 1, 0), 0)
    return pl.pallas_call(
        body, name="conv_mid_out_fwd", grid=(nt + 1,),
        in_specs=[pl.BlockSpec((tm, 3 * e), cur), pl.BlockSpec((tm, d), lag), _resident((e, d)),
                  pl.BlockSpec((CONV_K * ng, LANES), lambda i: (0, 0)),
                  pl.BlockSpec((ng, LANES), lambda i: (0, 0)), vec, vec],
        out_specs=[pl.BlockSpec((tm, e), cur), pl.BlockSpec((tm, e), cur), pl.BlockSpec((tm, d), lag)],
        out_shape=[jax.ShapeDtypeStruct((t, e), BF16), jax.ShapeDtypeStruct((t, e), BF16),
                   jax.ShapeDtypeStruct((t, d), F32)],
        scratch_shapes=[pltpu.VMEM(((tm + HALO) * ng, LANES), F32), pltpu.VMEM((tm * ng, LANES), F32),
                        pltpu.VMEM((tm, e), BF16)],
        compiler_params=_params("arbitrary"),
    )(p, h, w_out, dw3, dwb3, lg, lb)


def _pool_group(lc, e):
    return (lc * LANES) // (e // len(POOL_WINDOWS))


def _pool_inv_count(row0, rows, w):
    tpos = row0 + lax.broadcasted_iota(jnp.int32, (rows, 1), 0)
    return 1.0 / jnp.minimum(tpos + 1, w).astype(F32)


def _pool_window_dev(ubuf, base, rc, e, row0, dbuf):
    n = rc + PHALO
    for lc in range(e // LANES):
        lanes = slice(lc * LANES, (lc + 1) * LANES)
        g = _pool_group(lc, e)
        w = POOL_WINDOWS[g]
        blk = ubuf[pl.ds(base, n), lanes]
        acc = blk
        step = 1
        while step < w:
            acc = acc + pltpu.roll(acc, step, 0)
            step *= 2
        win = acc[PHALO:n]
        tok = blk[PHALO:n]
        dbuf[pl.ds(base, rc), lanes] = win * _pool_inv_count(row0 + base, rc, w) - tok


def _pool_mid_fwd(p, wg, bg, sc):
    t = p.shape[0]
    e = p.shape[1] // 2
    gc = e // len(POOL_WINDOWS)
    tm = _tile(t)
    rc = ROW_CHUNK

    def body(p_ref, wg_ref, bg_ref, sc_ref, s_ref, ubuf, dbuf):
        i = pl.program_id(0)

        @pl.when(i == 0)
        def _():
            ubuf[0:PHALO, :] = jnp.zeros((PHALO, e), F32)

        ubuf[PHALO:PHALO + tm, :] = p_ref[:, 0:e].astype(F32)

        def chunk(rci, carry):
            base = pl.multiple_of(rci * rc, rc)
            _pool_window_dev(ubuf, base, rc, e, i * tm, dbuf)
            return carry

        lax.fori_loop(0, tm // rc, chunk, 0)
        ubuf[0:PHALO, :] = ubuf[tm:tm + PHALO, :]

        for g in range(len(POOL_WINDOWS)):
            cols = slice(g * gc, (g + 1) * gc)
            yg = jnp.dot(dbuf[:, cols].astype(BF16), wg_ref[g], preferred_element_type=F32)
            z = p_ref[:, e + g * gc:e + (g + 1) * gc].astype(F32)
            s = ((yg + bg_ref[:, cols]) * sc_ref[:, cols]) * (z * _sig(z))
            s_ref[:, cols] = s.astype(BF16)

    vec = pl.BlockSpec((1, e), lambda i: (0, 0))
    return pl.pallas_call(
        body, name="pool_mid_fwd", grid=(t // tm,),
        in_specs=[pl.BlockSpec((tm, 2 * e), lambda i: (i, 0)),
                  pl.BlockSpec((len(POOL_WINDOWS), gc, gc), lambda i: (0, 0, 0)), vec, vec],
        out_specs=pl.BlockSpec((tm, e), lambda i: (i, 0)),
        out_shape=jax.ShapeDtypeStruct((t, e), BF16),
        scratch_shapes=[pltpu.VMEM((tm + PHALO, e), F32), pltpu.VMEM((tm, e), F32)],
        compiler_params=_params("arbitrary"),
    )(p, wg, bg, sc)


def _matmul_res(h, s, w):
    t, d = h.shape
    e = s.shape[1]
    tm = min(MATMUL_TILE, t)

    def body(h_ref, s_ref, w_ref, o_ref):
        o_ref[...] = h_ref[...] + jnp.dot(s_ref[...], w_ref[...], preferred_element_type=F32)

    return pl.pallas_call(
        body, name="matmul_res", grid=(t // tm,),
        in_specs=[pl.BlockSpec((tm, d), lambda i: (i, 0)), pl.BlockSpec((tm, e), lambda i: (i, 0)),
                  _resident((e, d))],
        out_specs=pl.BlockSpec((tm, d), lambda i: (i, 0)),
        out_shape=jax.ShapeDtypeStruct((t, d), F32),
        compiler_params=_params("parallel"),
    )(h, s, w)


def _loss_head(h, fg, tgt):
    t, d = h.shape
    tm = min(MATMUL_TILE, t)

    def body(h_ref, g_ref, t_ref, dh_ref, loss_ref, dg_ref):
        i = pl.program_id(0)

        @pl.when(i == 0)
        def _():
            loss_ref[...] = jnp.zeros_like(loss_ref)
            dg_ref[...] = jnp.zeros_like(dg_ref)

        hh = h_ref[...]
        r = lax.rsqrt(jnp.mean(hh * hh, axis=-1, keepdims=True) + RMS_EPS)
        hhat = hh * r
        err = hhat * g_ref[...] - t_ref[...]
        per_tok = jnp.mean(err * err, axis=-1, keepdims=True)
        loss_ref[...] += 0.5 * jnp.sum(per_tok, axis=0, keepdims=True)
        dy = err * (1.0 / d)
        tt = dy * g_ref[...]
        dh_ref[...] = r * (tt - hhat * jnp.mean(tt * hhat, axis=-1, keepdims=True))
        dg_ref[...] += jnp.sum(dy * hhat, axis=0, keepdims=True)

    return pl.pallas_call(
        body, name="loss_head", grid=(t // tm,),
        in_specs=[pl.BlockSpec((tm, d), lambda i: (i, 0)), pl.BlockSpec((1, d), lambda i: (0, 0)),
                  pl.BlockSpec((tm, d), lambda i: (i, 0))],
        out_specs=[pl.BlockSpec((tm, d), lambda i: (i, 0)), pl.BlockSpec((1, LANES), lambda i: (0, 0)),
                   pl.BlockSpec((1, d), lambda i: (0, 0))],
        out_shape=[jax.ShapeDtypeStruct((t, d), F32), jax.ShapeDtypeStruct((1, LANES), F32),
                   jax.ShapeDtypeStruct((1, d), F32)],
        compiler_params=_params("arbitrary"),
    )(h, fg, tgt)


def _ds_matmul(dy, w):
    t, d = dy.shape
    e = w.shape[0]
    tm = min(MATMUL_TILE, t)

    def body(dy_ref, w_ref, ds_ref):
        ds_ref[...] = lax.dot_general(dy_ref[...].astype(BF16), w_ref[...], (((1,), (1,)), ((), ())),
                                      preferred_element_type=F32).astype(BF16)

    return pl.pallas_call(
        body, name="ds_matmul", grid=(t // tm,),
        in_specs=[pl.BlockSpec((tm, d), lambda i: (i, 0)), _resident((e, d))],
        out_specs=pl.BlockSpec((tm, e), lambda i: (i, 0)),
        out_shape=jax.ShapeDtypeStruct((t, e), BF16),
        compiler_params=_params("parallel"),
    )(dy, w)


def _conv_mid_bwd(p, c, ds, dw3, lg, lb):
    t = p.shape[0]
    e = p.shape[1] // 3
    ng = e // LANES
    tm = _tile(t)
    nt = t // tm
    rc = CONV_ROW_CHUNK
    fb = FIR_BLOCK
    hb = tm // HALO

    def body(p_ref, ph_ref, c_ref, ds_ref, dw_ref, lg_ref, lb_ref,
             dp_ref, dlg_ref, dlb_ref, ddwb_ref, ddw_ref, u3, dc3, du3):
        i = pl.program_id(0)
        ti = nt - 1 - i

        @pl.when(i == 0)
        def _():
            dc3[tm * ng:(tm + HALO) * ng, :] = jnp.zeros((HALO * ng, LANES), F32)
            dlg_ref[...] = jnp.zeros_like(dlg_ref)
            dlb_ref[...] = jnp.zeros_like(dlb_ref)
            ddwb_ref[...] = jnp.zeros_like(ddwb_ref)
            ddw_ref[...] = jnp.zeros_like(ddw_ref)

        ha = ph_ref[:, 0:e].astype(F32)
        hbb = ph_ref[:, e:2 * e].astype(F32)
        _to_token_tiles(u3, 0, HALO, jnp.where(ti > 0, ha * _sig(hbb), 0.0), ng)

        def front(rci, carry):
            slg, slb, sdwb = carry
            base = pl.multiple_of(rci * rc, rc)
            rows = pl.ds(base, rc)
            a = p_ref[rows, 0:e].astype(F32)
            b = p_ref[rows, e:2 * e].astype(F32)
            _to_token_tiles(u3, HALO + base, rc, a * _sig(b), ng)
            cv = c_ref[rows, :].astype(F32)
            mu = jnp.mean(cv, axis=-1, keepdims=True)
            cc = cv - mu
            var = jnp.mean(cc * cc, axis=-1, keepdims=True)
            rs = lax.rsqrt(var + LN_EPS)
            nn = cc * rs
            ln = nn * lg_ref[...] + lb_ref[...]
            z = p_ref[rows, 2 * e:3 * e].astype(F32)
            sz = _sig(z)
            sl = _sig(ln)
            dsv = ds_ref[rows, :].astype(F32)
            dln = dsv * (z * sz) * _dsilu(ln, sl)
            dz = dsv * (ln * sl) * _dsilu(z, sz)
            dp_ref[rows, 2 * e:3 * e] = dz.astype(BF16)
            dn = dln * lg_ref[...]
            dc = rs * (dn - jnp.mean(dn, axis=-1, keepdims=True)
                       - nn * jnp.mean(dn * nn, axis=-1, keepdims=True))
            _to_token_tiles(dc3, base, rc, dc, ng)
            return (slg + jnp.sum(dln * nn, axis=0, keepdims=True),
                    slb + jnp.sum(dln, axis=0, keepdims=True),
                    sdwb + jnp.sum(dc, axis=0, keepdims=True))

        zero = jnp.zeros((1, e), F32)
        slg, slb, sdwb = lax.fori_loop(0, tm // rc, front, (zero, zero, zero), unroll=2)
        dlg_ref[...] += slg
        dlb_ref[...] += slb
        ddwb_ref[...] += sdwb

        def fir(bi, carry):
            t0 = bi * fb

            def dcs(q):
                return dc3[pl.ds(pl.multiple_of((t0 + q) * ng, ng), ng), :]

            def us(q):
                return u3[pl.ds(pl.multiple_of((t0 + HALO - (CONV_K - 1) + q) * ng, ng), ng), :]

            xs = [dcs(q) for q in range(fb - 1)]
            accs = [None] * fb
            for j in range(CONV_K):
                wk = dw_ref[(CONV_K - 1 - j) * ng:(CONV_K - j) * ng, :]
                xs.append(dcs(j + fb - 1))
                accs = [wk * xs[q + j] if accs[q] is None else accs[q] + wk * xs[q + j] for q in range(fb)]
            for q in range(fb):
                du3[pl.ds(pl.multiple_of((t0 + q) * ng, ng), ng), :] = accs[q]
            own = xs[0:fb]
            ys = [us(q) for q in range(fb - 1)]
            for k in range(CONV_K):
                ys.append(us(k + fb - 1))
                prods = [own[q] * ys[q + k] for q in range(fb)]
                while len(prods) > 1:
                    prods = [prods[2 * v] + prods[2 * v + 1] for v in range(len(prods) // 2)]
                ddw_ref[k * ng:(k + 1) * ng, :] += prods[0]
            return carry

        lax.fori_loop(0, tm // fb, fir, 0)
        dc3[tm * ng:(tm + HALO) * ng, :] = dc3[0:HALO * ng, :]

        def back(rci, carry):
            base = pl.multiple_of(rci * rc, rc)
            rows = pl.ds(base, rc)
            a = p_ref[rows, 0:e].astype(F32)
            b = p_ref[rows, e:2 * e].astype(F32)
            sb = _sig(b)
            duv = _from_token_tiles(du3, base, rc, ng)
            dp_ref[rows, 0:e] = (duv * sb).astype(BF16)
            dp_ref[rows, e:2 * e] = (duv * a * sb * (1.0 - sb)).astype(BF16)
            return carry

        lax.fori_loop(0, tm // rc, back, 0, unroll=2)

    vec = pl.BlockSpec((1, e), lambda i: (0, 0))
    taps = pl.BlockSpec((CONV_K * ng, LANES), lambda i: (0, 0))
    rev = lambda i: (nt - 1 - i, 0)
    halo = lambda i: (jnp.maximum((nt - 1 - i) * hb - 1, 0), 0)
    return pl.pallas_call(
        body, name="conv_mid_bwd", grid=(nt,),
        in_specs=[pl.BlockSpec((tm, 3 * e), rev), pl.BlockSpec((HALO, 3 * e), halo),
                  pl.BlockSpec((tm, e), rev), pl.BlockSpec((tm, e), rev), taps, vec, vec],
        out_specs=[pl.BlockSpec((tm, 3 * e), rev), vec, vec, vec, taps],
        out_shape=[jax.ShapeDtypeStruct((t, 3 * e), BF16), jax.ShapeDtypeStruct((1, e), F32),
                   jax.ShapeDtypeStruct((1, e), F32), jax.ShapeDtypeStruct((1, e), F32),
                   jax.ShapeDtypeStruct((CONV_K * ng, LANES), F32)],
        scratch_shapes=[pltpu.VMEM(((tm + HALO) * ng, LANES), F32), pltpu.VMEM(((tm + HALO) * ng, LANES), F32),
                        pltpu.VMEM((tm * ng, LANES), F32)],
        compiler_params=_params("arbitrary"),
    )(p, p, c, ds, dw3, lg, lb)


def _pool_mid_bwd(p, ds, wg, bg, sc):
    t = p.shape[0]
    e = p.shape[1] // 2
    ng = len(POOL_WINDOWS)
    gc = e // ng
    tm = _tile(t)
    nt = t // tm
    rc = ROW_CHUNK
    hb = tm // PHALO

    def body(p_ref, ph_ref, ds_ref, wg_ref, bg_ref, sc_ref,
             dp_ref, dwg_ref, dbg_ref, dsc_ref, ubuf, dbuf, ebuf, ddbuf):
        i = pl.program_id(0)
        ti = nt - 1 - i

        @pl.when(i == 0)
        def _():
            ebuf[tm:tm + PHALO, :] = jnp.zeros((PHALO, e), F32)
            dwg_ref[...] = jnp.zeros_like(dwg_ref)
            dbg_ref[...] = jnp.zeros_like(dbg_ref)
            dsc_ref[...] = jnp.zeros_like(dsc_ref)

        ubuf[0:PHALO, :] = jnp.where(ti > 0, ph_ref[:, 0:e].astype(F32), 0.0)
        ubuf[PHALO:PHALO + tm, :] = p_ref[:, 0:e].astype(F32)

        def recompute(rci, carry):
            base = pl.multiple_of(rci * rc, rc)
            _pool_window_dev(ubuf, base, rc, e, ti * tm, dbuf)
            return carry

        lax.fori_loop(0, tm // rc, recompute, 0)

        for g in range(ng):
            cols = slice(g * gc, (g + 1) * gc)
            dg = dbuf[:, cols].astype(BF16)
            q = jnp.dot(dg, wg_ref[g], preferred_element_type=F32) + bg_ref[:, cols]
            z = p_ref[:, e + g * gc:e + (g + 1) * gc].astype(F32)
            sz = _sig(z)
            dsv = ds_ref[:, cols].astype(F32)
            dz = dsv * (q * sc_ref[:, cols]) * _dsilu(z, sz)
            dp_ref[:, e + g * gc:e + (g + 1) * gc] = dz.astype(BF16)
            dy2 = dsv * (z * sz)
            dsc_ref[:, cols] += jnp.sum(dy2 * q, axis=0, keepdims=True)
            dq = dy2 * sc_ref[:, cols]
            dbg_ref[:, cols] += jnp.sum(dq, axis=0, keepdims=True)
            dqb = dq.astype(BF16)
            dwg_ref[g] += lax.dot_general(dg, dqb, (((0,), (0,)), ((), ())), preferred_element_type=F32)
            ddbuf[:, cols] = lax.dot_general(dqb, wg_ref[g], (((1,), (1,)), ((), ())),
                                             preferred_element_type=F32)

        def scale(rci, carry):
            base = pl.multiple_of(rci * rc, rc)
            for lc in range(e // LANES):
                lanes = slice(lc * LANES, (lc + 1) * LANES)
                w = POOL_WINDOWS[_pool_group(lc, e)]
                ebuf[pl.ds(base, rc), lanes] = (ddbuf[pl.ds(base, rc), lanes]
                                                * _pool_inv_count(ti * tm + base, rc, w))
            return carry

        lax.fori_loop(0, tm // rc, scale, 0)

        def chunk(rci, carry):
            base = pl.multiple_of(rci * rc, rc)
            n = rc + PHALO
            for lc in range(e // LANES):
                lanes = slice(lc * LANES, (lc + 1) * LANES)
                w = POOL_WINDOWS[_pool_group(lc, e)]
                acc = ebuf[pl.ds(base, n), lanes]
                step = 1
                while step < w:
                    acc = acc + pltpu.roll(acc, n - step, 0)
                    step *= 2
                du = acc[0:rc] - ddbuf[pl.ds(base, rc), lanes]
                dp_ref[pl.ds(base, rc), lanes] = du.astype(BF16)
            return carry

        lax.fori_loop(0, tm // rc, chunk, 0)
        ebuf[tm:tm + PHALO, :] = ebuf[0:PHALO, :]

    vec = pl.BlockSpec((1, e), lambda i: (0, 0))
    rev = lambda i: (nt - 1 - i, 0)
    halo = lambda i: (jnp.maximum((nt - 1 - i) * hb - 1, 0), 0)
    wspec = pl.BlockSpec((ng, gc, gc), lambda i: (0, 0, 0))
    return pl.pallas_call(
        body, name="pool_mid_bwd", grid=(nt,),
        in_specs=[pl.BlockSpec((tm, 2 * e), rev), pl.BlockSpec((PHALO, 2 * e), halo),
                  pl.BlockSpec((tm, e), rev), wspec, vec, vec],
        out_specs=[pl.BlockSpec((tm, 2 * e), rev), wspec, vec, vec],
        out_shape=[jax.ShapeDtypeStruct((t, 2 * e), BF16), jax.ShapeDtypeStruct((ng, gc, gc), F32),
                   jax.ShapeDtypeStruct((1, e), F32), jax.ShapeDtypeStruct((1, e), F32)],
        scratch_shapes=[pltpu.VMEM((tm + PHALO, e), F32), pltpu.VMEM((tm, e), F32),
                        pltpu.VMEM((tm + PHALO, e), F32), pltpu.VMEM((tm, e), F32)],
        compiler_params=_params("arbitrary"),
    )(p, p, ds, wg, bg, sc)


def _dhn_rms_bwd(dp, w4, h, g, dh_out, name):
    t, d = h.shape
    nk = w4.shape[-1]
    tm = min(MATMUL_TILE, t)

    def body(dp_ref, w_ref, h_ref, g_ref, dho_ref, dh_ref, dg_ref):
        i = pl.program_id(0)

        @pl.when(i == 0)
        def _():
            dg_ref[...] = jnp.zeros_like(dg_ref)

        dhn = jnp.zeros((tm, d), F32)
        for k in range(N_CHIPS):
            dhn = dhn + lax.dot_general(dp_ref[:, k * nk:(k + 1) * nk], w_ref[k], (((1,), (1,)), ((), ())),
                                        preferred_element_type=F32)
        hh = h_ref[...]
        r = lax.rsqrt(jnp.mean(hh * hh, axis=-1, keepdims=True) + RMS_EPS)
        hhat = hh * r
        tt = dhn * g_ref[...]
        dh_ref[...] = dho_ref[...] + r * (tt - hhat * jnp.mean(tt * hhat, axis=-1, keepdims=True))
        dg_ref[...] += jnp.sum(dhn * hhat, axis=0, keepdims=True)

    return pl.pallas_call(
        body, name=name, grid=(t // tm,),
        in_specs=[pl.BlockSpec((tm, N_CHIPS * nk), lambda i: (i, 0)),
                  _resident((N_CHIPS, d, nk)),
                  pl.BlockSpec((tm, d), lambda i: (i, 0)), pl.BlockSpec((1, d), lambda i: (0, 0)),
                  pl.BlockSpec((tm, d), lambda i: (i, 0))],
        out_specs=[pl.BlockSpec((tm, d), lambda i: (i, 0)), pl.BlockSpec((1, d), lambda i: (0, 0))],
        out_shape=[jax.ShapeDtypeStruct((t, d), F32), jax.ShapeDtypeStruct((1, d), F32)],
        compiler_params=_params("arbitrary"),
    )(dp, w4, h, g, dh_out)


def _wgrad(a, b, nblk, name):
    t, m = a.shape
    nn = b.shape[1] // nblk
    tk = min(WGRAD_TILE, t)
    nk = t // tk

    def body(a_ref, b_ref, o_ref, ob_ref):
        @pl.when(pl.program_id(1) == 0)
        def _():
            o_ref[...] = jnp.zeros_like(o_ref)

        o_ref[...] += lax.dot_general(a_ref[...].astype(BF16), b_ref[...].astype(BF16),
                                      (((0,), (0,)), ((), ())), preferred_element_type=F32)

        @pl.when(pl.program_id(1) == nk - 1)
        def _():
            ob_ref[...] = o_ref[...].astype(BF16)

    ospec = pl.BlockSpec((None, m, nn), lambda j, i: (j, 0, 0))
    return pl.pallas_call(
        body, name=name, grid=(nblk, nk),
        in_specs=[pl.BlockSpec((tk, m), lambda j, i: (i, 0)), pl.BlockSpec((tk, nn), lambda j, i: (i, j))],
        out_specs=[ospec, ospec],
        out_shape=[jax.ShapeDtypeStruct((nblk, m, nn), F32), jax.ShapeDtypeStruct((nblk, m, nn), BF16)],
        compiler_params=_params("parallel", "arbitrary"),
    )(a, b)


def _rows2d(shape):
    rows = 1
    for s in shape[:-1]:
        rows *= s
    return rows, shape[-1]


def _row_tile(rows):
    for cand in (512, 256, 128, 64, 32, 16, 8):
        if rows % cand == 0:
            return cand
    return rows


def _add_eight(own, landed, chip_core, name):
    _, _, rows, cols = own.shape
    tr = _row_tile(rows)

    def body(sel_ref, s_ref, r_ref, o_ref):
        acc = s_ref[...]
        for peer in range(N_DEV - 1):
            acc = acc + r_ref[peer].astype(F32)
        o_ref[...] = acc

    return pl.pallas_call(
        body, name=name,
        grid_spec=pltpu.PrefetchScalarGridSpec(
            num_scalar_prefetch=1, grid=(rows // tr,),
            in_specs=[pl.BlockSpec((None, None, tr, cols), lambda i, s: (s[0], s[1], i, 0)),
                      pl.BlockSpec((N_DEV - 1, tr, cols), lambda i, s: (0, i, 0))],
            out_specs=pl.BlockSpec((None, tr, cols), lambda i, s: (s[1], i, 0))),
        out_shape=jax.ShapeDtypeStruct((2, rows, cols), F32),
        compiler_params=_params("parallel"),
    )(chip_core, own, landed)


def _adamw(w, g, m, v, name):
    rows, cols = w.shape
    tr = _row_tile(rows)

    def body(w_ref, g_ref, m_ref, v_ref, d_ref, m2_ref, v2_ref):
        gg = g_ref[...]
        m2 = ADAM_B1 * m_ref[...] + (1.0 - ADAM_B1) * gg
        v2 = ADAM_B2 * v_ref[...] + (1.0 - ADAM_B2) * (gg * gg)
        m_hat = m2 / (1.0 - ADAM_B1 ** ADAM_STEP)
        v_hat = v2 / (1.0 - ADAM_B2 ** ADAM_STEP)
        d_ref[...] = -ADAM_LR * (m_hat / (jnp.sqrt(v_hat) + ADAM_EPS) + ADAM_WD * w_ref[...])
        m2_ref[...] = m2
        v2_ref[...] = v2

    spec = pl.BlockSpec((tr, cols), lambda i: (i, 0))
    shp = jax.ShapeDtypeStruct((rows, cols), F32)
    return pl.pallas_call(
        body, name=name, grid=(rows // tr,),
        in_specs=[spec, spec, spec, spec], out_specs=[spec, spec, spec], out_shape=[shp, shp, shp],
        compiler_params=_params("parallel"),
    )(w, g, m, v)


ANY = pl.BlockSpec(memory_space=pl.ANY)


def _place():
    x, y, c = lax.axis_index("x"), lax.axis_index("y"), lax.axis_index("c")
    chips = [(1 - x, y), (x, 1 - y), (1 - x, 1 - y)]
    return x, y, c, chips


def _allgather_weights(shards):
    n = len(shards)

    def body(*refs):
        ins, outs = refs[:n], refs[n:2 * n]
        send_ici, recv_ici, send_d2d, recv_d2d = refs[2 * n:]
        x, y, c, chips = _place()
        k0 = 2 * x + y
        sib = (x, y, 1 - c)

        def ici(a, r, src_chip, target):
            return pltpu.make_async_remote_copy(
                src_ref=ins[a].at[c], dst_ref=outs[a].at[src_chip, c],
                send_sem=send_ici.at[a * 3 + r], recv_sem=recv_ici.at[a * 3 + r],
                device_id=target, device_id_type=MESH)

        def d2d(a, r, src_chip, layer):
            return pltpu.make_async_remote_copy(
                src_ref=outs[a].at[src_chip, layer], dst_ref=outs[a].at[src_chip, layer],
                send_sem=send_d2d.at[a * 3 + r], recv_sem=recv_d2d.at[a * 3 + r],
                device_id=sib, device_id_type=MESH)

        first = [ici(a, r, k0, (cx, cy, c)) for a in range(n) for r, (cx, cy) in enumerate(chips)]
        for cp in first:
            cp.start()
        passed = []
        for a in range(n):
            for r, (cx, cy) in enumerate(chips):
                ici(a, r, 2 * cx + cy, (cx, cy, c)).wait_recv()
                cp = d2d(a, r, 2 * cx + cy, c)
                cp.start()
                passed.append(cp)
        for a in range(n):
            for r, (cx, cy) in enumerate(chips):
                d2d(a, r, 2 * cx + cy, 1 - c).wait_recv()
        for cp in first + passed:
            cp.wait_send()

    return pl.pallas_call(
        body, name="allgather_weights",
        in_specs=[ANY] * n, out_specs=[ANY] * n,
        out_shape=[jax.ShapeDtypeStruct((N_CHIPS,) + s.shape, s.dtype) for s in shards],
        scratch_shapes=[pltpu.SemaphoreType.DMA((3 * n,)), pltpu.SemaphoreType.DMA((3 * n,)),
                        pltpu.SemaphoreType.DMA((3 * n,)), pltpu.SemaphoreType.DMA((3 * n,))],
    )(*shards)


def _put_own(gathered, shard, chip):
    return lax.dynamic_update_slice_in_dim(gathered, shard[None], chip, axis=0)


HBM = pl.BlockSpec(memory_space=pltpu.HBM)
SEM = pl.BlockSpec(memory_space=pltpu.SEMAPHORE)
DATAFLOW = pltpu.SideEffectType.DATAFLOW_SIDE_EFFECTING
FLIPS = [(0, 0, 1), (0, 1, 0), (0, 1, 1), (1, 0, 0), (1, 0, 1), (1, 1, 0), (1, 1, 1)]


def _gather_plan(srcs, lands, send_sem, recv_sem):
    x, y, c, chips = _place()
    return [pltpu.make_async_remote_copy(
        src_ref=srcs[a], dst_ref=lands[a].at[2 * x + y],
        send_sem=send_sem.at[a * 3 + r], recv_sem=recv_sem.at[a * 3 + r],
        device_id=(cx, cy, c), device_id_type=MESH)
        for a in range(len(srcs)) for r, (cx, cy) in enumerate(chips)]


def _scatter_plan(srcs, lands, send_sem, recv_sem):
    x, y, c, _ = _place()
    cps = []
    for a in range(len(srcs)):
        for r, (fx, fy, fc) in enumerate(FLIPS):
            tx, ty, tc = (1 - x if fx else x), (1 - y if fy else y), (1 - c if fc else c)
            cps.append(pltpu.make_async_remote_copy(
                src_ref=srcs[a].at[2 * tx + ty, tc], dst_ref=lands[a].at[r],
                send_sem=send_sem.at[a * len(FLIPS) + r], recv_sem=recv_sem.at[a * len(FLIPS) + r],
                device_id=(tx, ty, tc), device_id_type=MESH))
    return cps


def _split_start(name, plan, srcs, lands, n_copies, after):
    n = len(srcs)

    def body(*refs):
        src, land = refs[:n], refs[n:2 * n]
        send_sem, recv_sem = refs[2 * n + 1], refs[2 * n + 2]
        token = refs[-1]
        for cp in plan(src, land, send_sem, recv_sem):
            cp.start()
        token[...] = jnp.zeros_like(token)

    outs = pl.pallas_call(
        body, name=name,
        in_specs=[HBM] * (2 * n) + [ANY],
        out_specs=[SEM, SEM] + [HBM] * (2 * n) + [pl.BlockSpec(memory_space=pltpu.VMEM)],
        out_shape=[pltpu.SemaphoreType.DMA((n_copies,)), pltpu.SemaphoreType.DMA((n_copies,))]
        + [pltpu.HBM(s.shape, s.dtype) for s in srcs] + [pltpu.HBM(l.shape, l.dtype) for l in lands]
        + [jax.ShapeDtypeStruct((8, LANES), F32)],
        input_output_aliases={i: 2 + i for i in range(2 * n)},
        compiler_params=pltpu.CompilerParams(has_side_effects=DATAFLOW),
    )(*[pltpu.with_memory_space_constraint(s, pltpu.HBM) for s in srcs],
      *[pltpu.with_memory_space_constraint(l, pltpu.HBM) for l in lands], after)
    return outs[0], outs[1], list(outs[2:2 + n]), list(outs[2 + n:2 + 2 * n]), outs[-1]


def _split_wait(name, plan, send_sems, recv_sems, srcs, lands, after):
    n = len(srcs)

    def body(*refs):
        src, land = refs[:n], refs[n:2 * n]
        send_sem, recv_sem = refs[2 * n], refs[2 * n + 1]
        for cp in plan(src, land, send_sem, recv_sem):
            cp.wait_send()
            cp.wait_recv()

    outs = pl.pallas_call(
        body, name=name,
        in_specs=[HBM] * (2 * n) + [SEM, SEM, ANY],
        out_specs=[HBM] * (2 * n),
        out_shape=[pltpu.HBM(s.shape, s.dtype) for s in srcs] + [pltpu.HBM(l.shape, l.dtype) for l in lands],
        input_output_aliases={i: i for i in range(2 * n)},
        compiler_params=pltpu.CompilerParams(has_side_effects=DATAFLOW),
    )(*srcs, *lands, send_sems, recv_sems, after)
    return list(outs[:n]), list(outs[n:])


def _share_halves(halves):
    n = len(halves)

    def body(*refs):
        ins, outs = refs[:n], refs[n:2 * n]
        send_sem, recv_sem = refs[2 * n:]
        x, y, c, _ = _place()
        cps = [pltpu.make_async_remote_copy(
            src_ref=outs[a].at[c], dst_ref=outs[a].at[c], send_sem=send_sem.at[a], recv_sem=recv_sem.at[a],
            device_id=(x, y, 1 - c), device_id_type=MESH) for a in range(n)]
        for cp in cps:
            cp.start()
        for cp in cps:
            cp.wait()

    return pl.pallas_call(
        body, name="share_halves",
        in_specs=[ANY] * n, out_specs=[ANY] * n,
        out_shape=[jax.ShapeDtypeStruct(h.shape, h.dtype) for h in halves],
        input_output_aliases={a: a for a in range(n)},
        scratch_shapes=[pltpu.SemaphoreType.DMA((n,)), pltpu.SemaphoreType.DMA((n,))],
    )(*halves)


N_DEV = 8


def _allreduce_small(v):
    m, nc = v.shape

    def body(x_ref, out_ref, gat, send_sems, recv_sems, local_sem):
        x, y, c, chips = _place()
        me, sib = (x, y, c), (x, y, 1 - c)

        def rows(px, py, pc):
            return gat.at[pl.ds((4 * px + 2 * py + pc) * m, m), :]

        def copy(k, block, to, src=None):
            return pltpu.make_async_remote_copy(
                src_ref=rows(*block) if src is None else src, dst_ref=rows(*block),
                send_sem=send_sems.at[k], recv_sem=recv_sems.at[k], device_id=to, device_id_type=MESH)

        mine = pltpu.make_async_copy(x_ref, rows(*me), local_sem)
        mine.start()
        first = [copy(0, me, sib, src=x_ref)]
        first += [copy(1 + j, me, (*chip, c), src=x_ref) for j, chip in enumerate(chips)]
        for cp in first:
            cp.start()
        passed = [copy(4 + j, (*chip, c), sib) for j, chip in enumerate(chips)]
        for j, chip in enumerate(chips):
            copy(1 + j, (*chip, c), me).wait_recv()
            passed[j].start()
        copy(0, sib, me).wait_recv()
        for j, chip in enumerate(chips):
            copy(4 + j, (*chip, 1 - c), me).wait_recv()
        for cp in first + passed:
            cp.wait_send()
        mine.wait()
        acc = gat[0:m, :]
        for dev in range(1, N_DEV):
            acc = acc + gat[dev * m:(dev + 1) * m, :]
        out_ref[...] = acc

    return pl.pallas_call(
        body, name="allreduce_small",
        in_specs=[pl.BlockSpec(memory_space=pltpu.VMEM)],
        out_specs=pl.BlockSpec(memory_space=pltpu.VMEM),
        out_shape=jax.ShapeDtypeStruct((m, nc), F32),
        scratch_shapes=[pltpu.VMEM((N_DEV * m, nc), F32), pltpu.SemaphoreType.DMA((7,)),
                        pltpu.SemaphoreType.DMA((7,)), pltpu.SemaphoreType.DMA],
        compiler_params=pltpu.CompilerParams(vmem_limit_bytes=VMEM_LIMIT),
    )(v)


def _pad_rows(a, rows):
    return jnp.pad(a, ((0, rows - a.shape[0]), (0, 0)))


def kernel(x, norm_g, final_g, conv_w_in, conv_dw, conv_dw_b, conv_ln_g, conv_ln_b, conv_w_out, pool_w_in, pool_w_grp, pool_b_grp, pool_scale, pool_w_out, loss_target, m_norm_g, m_final_g, m_conv_w_in, m_conv_dw, m_conv_dw_b, m_conv_ln_g, m_conv_ln_b, m_conv_w_out, m_pool_w_in, m_pool_w_grp, m_pool_b_grp, m_pool_scale, m_pool_w_out, v_norm_g, v_final_g, v_conv_w_in, v_conv_dw, v_conv_dw_b, v_conv_ln_g, v_conv_ln_b, v_conv_w_out, v_pool_w_in, v_pool_w_grp, v_pool_b_grp, v_pool_scale, v_pool_w_out):
    t, d = x.shape[1], x.shape[2]
    e = conv_w_out.shape[2]
    ng = len(POOL_WINDOWS)
    gc = e // ng
    gcs = pool_w_grp.shape[2]
    ck = conv_dw.shape[1]
    es = conv_dw.shape[2]
    xi, yi, ci = lax.axis_index("x"), lax.axis_index("y"), lax.axis_index("c")
    chip = 2 * xi + yi

    small_rows = ck + 2
    small_pad = -(-small_rows // 8) * 8
    small = jnp.concatenate([conv_dw, pool_b_grp[:, None, :], pool_scale[:, None, :],
                             jnp.zeros((2, small_pad - small_rows, es), F32)], axis=1)
    cwi_b, cwo_b = conv_w_in.astype(BF16), conv_w_out.astype(BF16)

    def halves(a):
        return a.reshape(2, a.shape[0] // 2, a.shape[1])

    first = [halves(cwi_b[0]), halves(cwo_b[0]), small]
    g_cwi0, g_cwo0, g_small = [_put_own(g, q, chip) for g, q in zip(_allgather_weights(first), first)]
    rest = [cwi_b[1], cwo_b[1], pool_w_in.astype(BF16), pool_w_grp.astype(BF16), pool_w_out.astype(BF16)]
    rest_lands = [lax.empty((N_CHIPS,) + r.shape, r.dtype) for r in rest]
    ag_send, ag_recv, rest, rest_lands, ag_token = _split_start(
        "gather_rest_start", _gather_plan, rest, rest_lands, 3 * len(rest), g_small)
    smallf = jnp.transpose(g_small, (1, 2, 0, 3)).reshape(2, small_pad, N_CHIPS * es)

    h = x.reshape(t, d)
    tgt = loss_target.reshape(t, d)
    hs, saved = [], []
    for layer in range(4):
        j = layer // 2
        hs.append(h)
        gvec = norm_g[layer][None, :]
        if layer == 0:
            gvec = gvec + ag_token[0:1, 0:1]
        if layer == 1:
            rest, rest_lands = _split_wait("gather_rest_wait", _gather_plan, ag_send, ag_recv, rest, rest_lands, h)
            g_cwi1, g_cwo1, g_pwi, g_pwg, g_pwo = [_put_own(g, q, chip) for g, q in zip(rest_lands, rest)]
            wg_full = jnp.transpose(g_pwg, (1, 2, 0, 3, 4)).reshape(2, ng, gc, gc)
        if layer % 2 == 0:
            g_in, g_out = (g_cwi0, g_cwo0) if j == 0 else (g_cwi1, g_cwo1)
            w_in4 = g_in.reshape(N_CHIPS, d, -1)
            w_out = g_out.reshape(e, d)
            p, hn = _rms_matmul(h, gvec, w_in4, "rms_matmul_conv")
            dw_full = smallf[j, 0:ck]
            s, c, h = _conv_mid_out_fwd(p, h, w_out, dw_full.reshape(-1, LANES), conv_dw_b[j].reshape(-1, LANES),
                                        conv_ln_g[j][None, :], conv_ln_b[j][None, :])
            saved.append((p, hn, s, c, w_in4, w_out, dw_full))
        else:
            w_in4 = g_pwi[:, j]
            w_out = g_pwo[:, j].reshape(e, d)
            p, hn = _rms_matmul(h, gvec, w_in4, "rms_matmul_pool")
            bg_full = smallf[j, ck:ck + 1]
            sc_full = smallf[j, ck + 1:ck + 2]
            s = _pool_mid_fwd(p, wg_full[j], bg_full, sc_full)
            saved.append((p, hn, s, None, w_in4, w_out, (wg_full[j], bg_full, sc_full)))
            h = _matmul_res(h, s, w_out)

    dh, loss_part, dfg = _loss_head(h, final_g[None, :], tgt)
    loss = lax.psum(loss_part[0, 0], ("x", "y", "c"))

    def by_half(a):
        return a.reshape(N_CHIPS, 2, a.shape[1] // 2, a.shape[2])

    dng = [None] * 4
    g_conv = [None, None]
    g_pool = [None, None]
    flights = {}
    for layer in (3, 2, 1, 0):
        j = layer // 2
        p, hn, s, c, w_in4, w_out, extra = saved[layer]
        gvec = norm_g[layer][None, :]
        ds = _ds_matmul(dh, w_out)
        dw_out, dw_out_b = [q.reshape(N_CHIPS, e // N_CHIPS, d) for q in _wgrad(s, dh, 1, "wgrad_out")]
        if layer % 2 == 0:
            dp, dlg, dlb, ddwb, ddw3 = _conv_mid_bwd(p, c, ds, extra.reshape(-1, LANES),
                                                     conv_ln_g[j][None, :], conv_ln_b[j][None, :])
            ddw = ddw3.reshape(ck, e)
            dw_in, dw_in_b = _wgrad(hn, dp, N_CHIPS, "wgrad_in_conv")
            own, pay = [dw_in, dw_out], [dw_in_b, dw_out_b]
            g_conv[j] = (dlg, dlb, ddwb, ddw)
        else:
            wg, bg_full, sc_full = extra
            dp, dwg, dbg, dsc = _pool_mid_bwd(p, ds, wg, bg_full, sc_full)
            dw_in, dw_in_b = _wgrad(hn, dp, N_CHIPS, "wgrad_in_pool")
            dwg4 = jnp.transpose(dwg.reshape(ng, N_CHIPS, gcs, gc), (1, 0, 2, 3)).reshape(N_CHIPS, ng * gcs, gc)
            own, pay = [dw_in, dw_out, dwg4], [dw_in_b, dw_out_b, dwg4.astype(BF16)]
            g_pool[j] = (dbg, dsc)
        pay = [by_half(q) for q in pay]
        lands = [lax.empty((len(FLIPS),) + q.shape[2:], BF16) for q in pay]
        send, recv, pay, lands, token = _split_start(
            "scatter_start_%d" % layer, _scatter_plan, pay, lands, len(FLIPS) * len(pay), own[0])
        flights[layer] = (send, recv, pay, lands, own)
        kind = "dhn_rms_bwd_conv" if layer % 2 == 0 else "dhn_rms_bwd_pool"
        dh, dng[layer] = _dhn_rms_bwd(dp, w_in4, hs[layer], gvec + token[0:1, 0:1], dh, kind)
    grad_x = dh.reshape(x.shape)

    sel_kc = jnp.stack([chip, ci]).astype(jnp.int32)
    summed = {}
    after = dh
    for layer in (3, 2, 1, 0):
        send, recv, pay, lands, own = flights[layer]
        _, lands = _split_wait("scatter_wait_%d" % layer, _scatter_plan, send, recv, pay, lands, after)
        after = lands[0]
        summed[layer] = [_add_eight(by_half(o), l, sel_kc, "add_eight_%d_%d" % (layer % 2, a))
                         for a, (o, l) in enumerate(zip(own, lands))]
    order = [(0, 0), (0, 1), (2, 0), (2, 1), (1, 0), (1, 1), (1, 2), (3, 0), (3, 1), (3, 2)]
    shared = _share_halves([summed[l][a] for l, a in order])
    full = {la: q.reshape(q.shape[0] * q.shape[1], q.shape[2]) for la, q in zip(order, shared)}
    g_cwi_f = jnp.stack([full[(0, 0)], full[(2, 0)]])
    g_cwo_f = jnp.stack([full[(0, 1)], full[(2, 1)]])
    g_pwi_f = jnp.stack([full[(1, 0)], full[(3, 0)]])
    g_pwo_f = jnp.stack([full[(1, 1)], full[(3, 1)]])
    g_pwg_f = jnp.stack([full[(1, 2)], full[(3, 2)]]).reshape(pool_w_grp.shape)

    rows_list = [dng[0], dng[1], dng[2], dng[3], dfg,
                 g_conv[0][2], g_conv[1][2], g_conv[0][0], g_conv[1][0], g_conv[0][1], g_conv[1][1],
                 g_pool[0][0], g_pool[1][0], g_pool[0][1], g_pool[1][1], g_conv[0][3], g_conv[1][3]]
    slab = jnp.concatenate(rows_list, axis=0)
    nrows = slab.shape[0]
    slab = _pad_rows(slab, -(-nrows // 8) * 8)
    tot = _allreduce_small(slab)
    g_norm_g = tot[0:4]
    g_final_g = tot[4]
    g_dwb = tot[5:7]
    g_lng = tot[7:9]
    g_lnb = tot[9:11]
    g_bg = lax.dynamic_slice_in_dim(tot[11:13], chip * es, es, axis=1)
    g_sc = lax.dynamic_slice_in_dim(tot[13:15], chip * es, es, axis=1)
    g_dw = lax.dynamic_slice_in_dim(tot[15:15 + 2 * ck].reshape(2, ck, e), chip * es, es, axis=2)

    def adam_nd(w, g, m, v, nm):
        rows, cols = _rows2d(w.shape)
        outs = _adamw(w.reshape(rows, cols), g.reshape(rows, cols), m.reshape(rows, cols),
                      v.reshape(rows, cols), "adamw_" + nm)
        return [o.reshape(w.shape) for o in outs]

    res = {}
    res["conv_w_in"] = (g_cwi_f, *adam_nd(conv_w_in, g_cwi_f, m_conv_w_in, v_conv_w_in, "cwi"))
    res["conv_w_out"] = (g_cwo_f, *adam_nd(conv_w_out, g_cwo_f, m_conv_w_out, v_conv_w_out, "cwo"))
    res["pool_w_in"] = (g_pwi_f, *adam_nd(pool_w_in, g_pwi_f, m_pool_w_in, v_pool_w_in, "pwi"))
    res["pool_w_grp"] = (g_pwg_f, *adam_nd(pool_w_grp, g_pwg_f, m_pool_w_grp, v_pool_w_grp, "pwg"))
    res["pool_w_out"] = (g_pwo_f, *adam_nd(pool_w_out, g_pwo_f, m_pool_w_out, v_pool_w_out, "pwo"))

    def pack(parts, rows_to):
        return _pad_rows(jnp.concatenate([q.reshape(-1, q.shape[-1]) for q in parts], axis=0), rows_to)

    rep_w = [norm_g, final_g[None, :], conv_dw_b, conv_ln_g, conv_ln_b]
    rep_g = [g_norm_g, g_final_g[None, :], g_dwb, g_lng, g_lnb]
    rep_m = [m_norm_g, m_final_g[None, :], m_conv_dw_b, m_conv_ln_g, m_conv_ln_b]
    rep_v = [v_norm_g, v_final_g[None, :], v_conv_dw_b, v_conv_ln_g, v_conv_ln_b]
    rep = _adamw(pack(rep_w, 16), pack(rep_g, 16), pack(rep_m, 16), pack(rep_v, 16), "adamw_rep")
    rep_names = ["norm_g", "final_g", "conv_dw_b", "conv_ln_g", "conv_ln_b"]
    rep_rows = [(0, 4), (4, 5), (5, 7), (7, 9), (9, 11)]
    for nm, (lo, hi), gq, wq in zip(rep_names, rep_rows, rep_g, rep_w):
        shape = (d,) if nm == "final_g" else wq.shape
        res[nm] = (gq.reshape(shape), *[o[lo:hi].reshape(shape) for o in rep])

    sh_w = [conv_dw, pool_b_grp, pool_scale]
    sh_g = [g_dw, g_bg, g_sc]
    sh_m = [m_conv_dw, m_pool_b_grp, m_pool_scale]
    sh_v = [v_conv_dw, v_pool_b_grp, v_pool_scale]
    sh_total = 2 * ck + 4
    sh_pad = -(-sh_total // 8) * 8
    shd = _adamw(pack(sh_w, sh_pad), pack(sh_g, sh_pad), pack(sh_m, sh_pad), pack(sh_v, sh_pad), "adamw_shard")
    sh_names = ["conv_dw", "pool_b_grp", "pool_scale"]
    sh_rows = [(0, 2 * ck), (2 * ck, 2 * ck + 2), (2 * ck + 2, 2 * ck + 4)]
    for nm, (lo, hi), gq, wq in zip(sh_names, sh_rows, sh_g, sh_w):
        res[nm] = (gq.reshape(wq.shape), *[o[lo:hi].reshape(wq.shape) for o in shd])

    order = ["norm_g", "final_g", "conv_w_in", "conv_dw", "conv_dw_b", "conv_ln_g", "conv_ln_b", "conv_w_out",
             "pool_w_in", "pool_w_grp", "pool_b_grp", "pool_scale", "pool_w_out"]
    outs = [loss, grad_x]
    for part in range(4):
        outs += [res[nm][part] for nm in order]
    return tuple(outs)
```

```python
import functools

import jax
import jax.numpy as jnp
from jax import lax
from jax.experimental import pallas as pl
from jax.experimental.pallas import tpu as pltpu

F32 = jnp.float32
BF16 = jnp.bfloat16
MESH = pl.DeviceIdType.MESH

RMS_EPS = 1e-6
LN_EPS = 1e-5
CONV_K = 31
HALO = 32
PHALO = 16
POOL_WINDOWS = (2, 4, 8, 16)
N_CHIPS = 4
LANES = 128
ROW_CHUNK = 32
CONV_ROW_CHUNK = 32
FIR_BLOCK = 16
TOKEN_TILE = 512
MATMUL_TILE = 1024
WGRAD_TILE = 2048
VMEM_LIMIT = 56 * 1024 * 1024

ADAM_LR = 0.001
ADAM_B1 = 0.9
ADAM_B2 = 0.999
ADAM_EPS = 1e-08
ADAM_WD = 0.01
ADAM_STEP = 10


def _params(*sem):
    return pltpu.CompilerParams(dimension_semantics=sem, vmem_limit_bytes=VMEM_LIMIT)


def _sig(v):
    return 0.5 * jnp.tanh(0.5 * v) + 0.5


def _dsilu(v, sv):
    return sv * (1.0 + v * (1.0 - sv))


def _resident(shape):
    return pl.BlockSpec(shape, lambda *_: (0,) * len(shape), pipeline_mode=pl.Buffered(1))


def _tile(t):
    return min(TOKEN_TILE, t)


def _rms_matmul(h, g, w4, name):
    t, d = h.shape
    nk = w4.shape[-1]
    tm = min(MATMUL_TILE, t)

    def body(h_ref, g_ref, w_ref, p_ref, hn_ref):
        hh = h_ref[...]
        r = lax.rsqrt(jnp.mean(hh * hh, axis=-1, keepdims=True) + RMS_EPS)
        hn = (hh * r * g_ref[...]).astype(BF16)
        hn_ref[...] = hn
        for k in range(N_CHIPS):
            p_ref[:, k * nk:(k + 1) * nk] = jnp.dot(hn, w_ref[k], preferred_element_type=F32).astype(BF16)

    return pl.pallas_call(
        body, name=name, grid=(t // tm,),
        in_specs=[pl.BlockSpec((tm, d), lambda i: (i, 0)),
                  pl.BlockSpec((1, d), lambda i: (0, 0)),
                  _resident((N_CHIPS, d, nk))],
        out_specs=[pl.BlockSpec((tm, N_CHIPS * nk), lambda i: (i, 0)),
                   pl.BlockSpec((tm, d), lambda i: (i, 0))],
        out_shape=[jax.ShapeDtypeStruct((t, N_CHIPS * nk), BF16), jax.ShapeDtypeStruct((t, d), BF16)],
        compiler_params=_params("parallel"),
    )(h, g, w4)


def _to_token_tiles(ref, tok0, rows, val, ng):
    for j in range(ng):
        ref[pl.ds(tok0 * ng + j, rows, stride=ng), :] = val[:, j * LANES:(j + 1) * LANES]


def _from_token_tiles(ref, tok0, rows, ng):
    return jnp.concatenate([ref[pl.ds(tok0 * ng + j, rows, stride=ng), :] for j in range(ng)], axis=1)


def _conv_mid_fwd(p, dw3, dwb3, lg, lb):
    t = p.shape[0]
    e = p.shape[1] // 3
    ng = e // LANES
    tm = _tile(t)
    rc = CONV_ROW_CHUNK
    fb = FIR_BLOCK

    def body(p_ref, dw_ref, dwb_ref, lg_ref, lb_ref, s_ref, c_ref, u3, c3):
        i = pl.program_id(0)

        @pl.when(i == 0)
        def _():
            u3[0:HALO * ng, :] = jnp.zeros((HALO * ng, LANES), F32)

        def glu(rci, carry):
            base = pl.multiple_of(rci * rc, rc)
            a = p_ref[pl.ds(base, rc), 0:e].astype(F32)
            b = p_ref[pl.ds(base, rc), e:2 * e].astype(F32)
            _to_token_tiles(u3, HALO + base, rc, a * _sig(b), ng)
            return carry

        lax.fori_loop(0, tm // rc, glu, 0)

        def fir(bi, carry):
            t0 = bi * fb
            def x(q):
                return u3[pl.ds(pl.multiple_of((t0 + HALO - (CONV_K - 1) + q) * ng, ng), ng), :]

            xs = [x(q) for q in range(fb - 1)]
            accs = [dwb_ref[...]] * fb
            for k in range(CONV_K):
                wk = dw_ref[k * ng:(k + 1) * ng, :]
                xs.append(x(k + fb - 1))
                accs = [accs[q] + wk * xs[q + k] for q in range(fb)]
            for q in range(fb):
                c3[pl.ds(pl.multiple_of((t0 + q) * ng, ng), ng), :] = accs[q]
            return carry

        lax.fori_loop(0, tm // fb, fir, 0)
        u3[0:HALO * ng, :] = u3[tm * ng:(tm + HALO) * ng, :]

        def chunk(rci, carry):
            base = pl.multiple_of(rci * rc, rc)
            c = _from_token_tiles(c3, base, rc, ng)
            mu = jnp.mean(c, axis=-1, keepdims=True)
            cc = c - mu
            var = jnp.mean(cc * cc, axis=-1, keepdims=True)
            ln = cc * lax.rsqrt(var + LN_EPS) * lg_ref[...] + lb_ref[...]
            z = p_ref[pl.ds(base, rc), 2 * e:3 * e].astype(F32)
            s = (ln * _sig(ln)) * (z * _sig(z))
            s_ref[pl.ds(base, rc), :] = s.astype(BF16)
            c_ref[pl.ds(base, rc), :] = c.astype(BF16)
            return carry

        lax.fori_loop(0, tm // rc, chunk, 0, unroll=2)

    vec = pl.BlockSpec((1, e), lambda i: (0, 0))
    return pl.pallas_call(
        body, name="conv_mid_fwd", grid=(t // tm,),
        in_specs=[pl.BlockSpec((tm, 3 * e), lambda i: (i, 0)),
                  pl.BlockSpec((CONV_K * ng, LANES), lambda i: (0, 0)),
                  pl.BlockSpec((ng, LANES), lambda i: (0, 0)), vec, vec],
        out_specs=[pl.BlockSpec((tm, e), lambda i: (i, 0)), pl.BlockSpec((tm, e), lambda i: (i, 0))],
        out_shape=[jax.ShapeDtypeStruct((t, e), BF16), jax.ShapeDtypeStruct((t, e), BF16)],
        scratch_shapes=[pltpu.VMEM(((tm + HALO) * ng, LANES), F32), pltpu.VMEM((tm * ng, LANES), F32)],
        compiler_params=_params("arbitrary"),
    )(p, dw3, dwb3, lg, lb)


def _conv_mid_out_fwd(p, h, w_out, dw3, dwb3, lg, lb):
    t = p.shape[0]
    e = p.shape[1] // 3
    d = h.shape[1]
    ng = e // LANES
    tm = _tile(t)
    nt = t // tm
    rc = CONV_ROW_CHUNK
    fb = FIR_BLOCK

    def body(p_ref, h_ref, w_ref, dw_ref, dwb_ref, lg_ref, lb_ref, s_ref, c_ref, ho_ref, u3, c3, s_prev):
        i = pl.program_id(0)

        def project():
            ho_ref[...] = h_ref[...] + jnp.dot(s_prev[...], w_ref[...], preferred_element_type=F32)

        @pl.when(i == 0)
        def _():
            u3[0:HALO * ng, :] = jnp.zeros((HALO * ng, LANES), F32)
            s_prev[...] = jnp.zeros_like(s_prev)

        @pl.when(i < nt)
        def _():
            def glu(rci, carry):
                base = pl.multiple_of(rci * rc, rc)
                a = p_ref[pl.ds(base, rc), 0:e].astype(F32)
                b = p_ref[pl.ds(base, rc), e:2 * e].astype(F32)
                _to_token_tiles(u3, HALO + base, rc, a * _sig(b), ng)
                return carry

            lax.fori_loop(0, tm // rc, glu, 0)

            project()
            for bi in range(tm // fb):
                t0 = bi * fb
                xs = [u3[(t0 + HALO - (CONV_K - 1) + q) * ng:(t0 + HALO - (CONV_K - 1) + q + 1) * ng, :]
                      for q in range(fb - 1)]
                accs = [dwb_ref[...]] * fb
                for k in range(CONV_K):
                    wk = dw_ref[k * ng:(k + 1) * ng, :]
                    q1 = t0 + HALO - (CONV_K - 1) + k + fb - 1
                    xs.append(u3[q1 * ng:(q1 + 1) * ng, :])
                    accs = [accs[q] + wk * xs[q + k] for q in range(fb)]
                for q in range(fb):
                    c3[(t0 + q) * ng:(t0 + q + 1) * ng, :] = accs[q]
            u3[0:HALO * ng, :] = u3[tm * ng:(tm + HALO) * ng, :]

            def chunk(rci, carry):
                base = pl.multiple_of(rci * rc, rc)
                c = _from_token_tiles(c3, base, rc, ng)
                mu = jnp.mean(c, axis=-1, keepdims=True)
                cc = c - mu
                var = jnp.mean(cc * cc, axis=-1, keepdims=True)
                ln = cc * lax.rsqrt(var + LN_EPS) * lg_ref[...] + lb_ref[...]
                z = p_ref[pl.ds(base, rc), 2 * e:3 * e].astype(F32)
                s = ((ln * _sig(ln)) * (z * _sig(z))).astype(BF16)
                s_ref[pl.ds(base, rc), :] = s
                s_prev[pl.ds(base, rc), :] = s
                c_ref[pl.ds(base, rc), :] = c.astype(BF16)
                return carry

            lax.fori_loop(0, tm // rc, chunk, 0, unroll=2)

        @pl.when(i == nt)
        def _():
            project()

    vec = pl.BlockSpec((1, e), lambda i: (0, 0))
    cur = lambda i: (jnp.minimum(i, nt - 1), 0)
    lag = lambda i: (jnp.maximum(i - 1, 0), 0)
    return pl.pallas_call(
        body, name="conv_mid_out_fwd", grid=(nt + 1,),
        in_specs=[pl.BlockSpec((tm, 3 * e), cur), pl.BlockSpec((tm, d), lag), _resident((e, d)),
                  pl.BlockSpec((CONV_K * ng, LANES), lambda i: (0, 0)),
                  pl.BlockSpec((ng, LANES), lambda i: (0, 0)), vec, vec],
        out_specs=[pl.BlockSpec((tm, e), cur), pl.BlockSpec((tm, e), cur), pl.BlockSpec((tm, d), lag)],
        out_shape=[jax.ShapeDtypeStruct((t, e), BF16), jax.ShapeDtypeStruct((t, e), BF16),
                   jax.ShapeDtypeStruct((t, d), F32)],
        scratch_shapes=[pltpu.VMEM(((tm + HALO) * ng, LANES), F32), pltpu.VMEM((tm * ng, LANES), F32),
                        pltpu.VMEM((tm, e), BF16)],
        compiler_params=_params("arbitrary"),
    )(p, h, w_out, dw3, dwb3, lg, lb)


def _pool_group(lc, e):
    return (lc * LANES) // (e // len(POOL_WINDOWS))


def _pool_inv_count(row0, rows, w):
    tpos = row0 + lax.broadcasted_iota(jnp.int32, (rows, 1), 0)
    return 1.0 / jnp.minimum(tpos + 1, w).astype(F32)


def _pool_window_dev(ubuf, base, rc, e, row0, dbuf):
    n = rc + PHALO
    for lc in range(e // LANES):
        lanes = slice(lc * LANES, (lc + 1) * LANES)
        g = _pool_group(lc, e)
        w = POOL_WINDOWS[g]
        blk = ubuf[pl.ds(base, n), lanes]
        acc = blk
        step = 1
        while step < w:
            acc = acc + pltpu.roll(acc, step, 0)
            step *= 2
        win = acc[PHALO:n]
        tok = blk[PHALO:n]
        dbuf[pl.ds(base, rc), lanes] = win * _pool_inv_count(row0 + base, rc, w) - tok


def _pool_mid_fwd(p, wg, bg, sc):
    t = p.shape[0]
    e = p.shape[1] // 2
    gc = e // len(POOL_WINDOWS)
    tm = _tile(t)
    rc = ROW_CHUNK

    def body(p_ref, wg_ref, bg_ref, sc_ref, s_ref, ubuf, dbuf):
        i = pl.program_id(0)

        @pl.when(i == 0)
        def _():
            ubuf[0:PHALO, :] = jnp.zeros((PHALO, e), F32)

        ubuf[PHALO:PHALO + tm, :] = p_ref[:, 0:e].astype(F32)

        def chunk(rci, carry):
            base = pl.multiple_of(rci * rc, rc)
            _pool_window_dev(ubuf, base, rc, e, i * tm, dbuf)
            return carry

        lax.fori_loop(0, tm // rc, chunk, 0)
        ubuf[0:PHALO, :] = ubuf[tm:tm + PHALO, :]

        for g in range(len(POOL_WINDOWS)):
            cols = slice(g * gc, (g + 1) * gc)
            yg = jnp.dot(dbuf[:, cols].astype(BF16), wg_ref[g], preferred_element_type=F32)
            z = p_ref[:, e + g * gc:e + (g + 1) * gc].astype(F32)
            s = ((yg + bg_ref[:, cols]) * sc_ref[:, cols]) * (z * _sig(z))
            s_ref[:, cols] = s.astype(BF16)

    vec = pl.BlockSpec((1, e), lambda i: (0, 0))
    return pl.pallas_call(
        body, name="pool_mid_fwd", grid=(t // tm,),
        in_specs=[pl.BlockSpec((tm, 2 * e), lambda i: (i, 0)),
                  pl.BlockSpec((len(POOL_WINDOWS), gc, gc), lambda i: (0, 0, 0)), vec, vec],
        out_specs=pl.BlockSpec((tm, e), lambda i: (i, 0)),
        out_shape=jax.ShapeDtypeStruct((t, e), BF16),
        scratch_shapes=[pltpu.VMEM((tm + PHALO, e), F32), pltpu.VMEM((tm, e), F32)],
        compiler_params=_params("arbitrary"),
    )(p, wg, bg, sc)


def _matmul_res(h, s, w):
    t, d = h.shape
    e = s.shape[1]
    tm = min(MATMUL_TILE, t)

    def body(h_ref, s_ref, w_ref, o_ref):
        o_ref[...] = h_ref[...] + jnp.dot(s_ref[...], w_ref[...], preferred_element_type=F32)

    return pl.pallas_call(
        body, name="matmul_res", grid=(t // tm,),
        in_specs=[pl.BlockSpec((tm, d), lambda i: (i, 0)), pl.BlockSpec((tm, e), lambda i: (i, 0)),
                  _resident((e, d))],
        out_specs=pl.BlockSpec((tm, d), lambda i: (i, 0)),
        out_shape=jax.ShapeDtypeStruct((t, d), F32),
        compiler_params=_params("parallel"),
    )(h, s, w)


def _loss_head(h, fg, tgt):
    t, d = h.shape
    tm = min(MATMUL_TILE, t)

    def body(h_ref, g_ref, t_ref, dh_ref, loss_ref, dg_ref):
        i = pl.program_id(0)

        @pl.when(i == 0)
        def _():
            loss_ref[...] = jnp.zeros_like(loss_ref)
            dg_ref[...] = jnp.zeros_like(dg_ref)

        hh = h_ref[...]
        r = lax.rsqrt(jnp.mean(hh * hh, axis=-1, keepdims=True) + RMS_EPS)
        hhat = hh * r
        err = hhat * g_ref[...] - t_ref[...]
        per_tok = jnp.mean(err * err, axis=-1, keepdims=True)
        loss_ref[...] += 0.5 * jnp.sum(per_tok, axis=0, keepdims=True)
        dy = err * (1.0 / d)
        tt = dy * g_ref[...]
        dh_ref[...] = r * (tt - hhat * jnp.mean(tt * hhat, axis=-1, keepdims=True))
        dg_ref[...] += jnp.sum(dy * hhat, axis=0, keepdims=True)

    return pl.pallas_call(
        body, name="loss_head", grid=(t // tm,),
        in_specs=[pl.BlockSpec((tm, d), lambda i: (i, 0)), pl.BlockSpec((1, d), lambda i: (0, 0)),
                  pl.BlockSpec((tm, d), lambda i: (i, 0))],
        out_specs=[pl.BlockSpec((tm, d), lambda i: (i, 0)), pl.BlockSpec((1, LANES), lambda i: (0, 0)),
                   pl.BlockSpec((1, d), lambda i: (0, 0))],
        out_shape=[jax.ShapeDtypeStruct((t, d), F32), jax.ShapeDtypeStruct((1, LANES), F32),
                   jax.ShapeDtypeStruct((1, d), F32)],
        compiler_params=_params("arbitrary"),
    )(h, fg, tgt)


def _ds_matmul(dy, w):
    t, d = dy.shape
    e = w.shape[0]
    tm = min(MATMUL_TILE, t)

    def body(dy_ref, w_ref, ds_ref):
        ds_ref[...] = lax.dot_general(dy_ref[...].astype(BF16), w_ref[...], (((1,), (1,)), ((), ())),
                                      preferred_element_type=F32).astype(BF16)

    return pl.pallas_call(
        body, name="ds_matmul", grid=(t // tm,),
        in_specs=[pl.BlockSpec((tm, d), lambda i: (i, 0)), _resident((e, d))],
        out_specs=pl.BlockSpec((tm, e), lambda i: (i, 0)),
        out_shape=jax.ShapeDtypeStruct((t, e), BF16),
        compiler_params=_params("parallel"),
    )(dy, w)


def _conv_mid_bwd(p, c, ds, dw3, lg, lb):
    t = p.shape[0]
    e = p.shape[1] // 3
    ng = e // LANES
    tm = _tile(t)
    nt = t // tm
    rc = CONV_ROW_CHUNK
    fb = FIR_BLOCK
    hb = tm // HALO

    def body(p_ref, ph_ref, c_ref, ds_ref, dw_ref, lg_ref, lb_ref,
             dp_ref, dlg_ref, dlb_ref, ddwb_ref, ddw_ref, u3, dc3, du3):
        i = pl.program_id(0)
        ti = nt - 1 - i

        @pl.when(i == 0)
        def _():
            dc3[tm * ng:(tm + HALO) * ng, :] = jnp.zeros((HALO * ng, LANES), F32)
            dlg_ref[...] = jnp.zeros_like(dlg_ref)
            dlb_ref[...] = jnp.zeros_like(dlb_ref)
            ddwb_ref[...] = jnp.zeros_like(ddwb_ref)
            ddw_ref[...] = jnp.zeros_like(ddw_ref)

        ha = ph_ref[:, 0:e].astype(F32)
        hbb = ph_ref[:, e:2 * e].astype(F32)
        _to_token_tiles(u3, 0, HALO, jnp.where(ti > 0, ha * _sig(hbb), 0.0), ng)

        def front(rci, carry):
            slg, slb, sdwb = carry
            base = pl.multiple_of(rci * rc, rc)
            rows = pl.ds(base, rc)
            a = p_ref[rows, 0:e].astype(F32)
            b = p_ref[rows, e:2 * e].astype(F32)
            _to_token_tiles(u3, HALO + base, rc, a * _sig(b), ng)
            cv = c_ref[rows, :].astype(F32)
            mu = jnp.mean(cv, axis=-1, keepdims=True)
            cc = cv - mu
            var = jnp.mean(cc * cc, axis=-1, keepdims=True)
            rs = lax.rsqrt(var + LN_EPS)
            nn = cc * rs
            ln = nn * lg_ref[...] + lb_ref[...]
            z = p_ref[rows, 2 * e:3 * e].astype(F32)
            sz = _sig(z)
            sl = _sig(ln)
            dsv = ds_ref[rows, :].astype(F32)
            dln = dsv * (z * sz) * _dsilu(ln, sl)
            dz = dsv * (ln * sl) * _dsilu(z, sz)
            dp_ref[rows, 2 * e:3 * e] = dz.astype(BF16)
            dn = dln * lg_ref[...]
            dc = rs * (dn - jnp.mean(dn, axis=-1, keepdims=True)
                       - nn * jnp.mean(dn * nn, axis=-1, keepdims=True))
            _to_token_tiles(dc3, base, rc, dc, ng)
            return (slg + jnp.sum(dln * nn, axis=0, keepdims=True),
                    slb + jnp.sum(dln, axis=0, keepdims=True),
                    sdwb + jnp.sum(dc, axis=0, keepdims=True))

        zero = jnp.zeros((1, e), F32)
        slg, slb, sdwb = lax.fori_loop(0, tm // rc, front, (zero, zero, zero), unroll=2)
        dlg_ref[...] += slg
        dlb_ref[...] += slb
        ddwb_ref[...] += sdwb

        def fir(bi, carry):
            t0 = bi * fb

            def dcs(q):
                return dc3[pl.ds(pl.multiple_of((t0 + q) * ng, ng), ng), :]

            def us(q):
                return u3[pl.ds(pl.multiple_of((t0 + HALO - (CONV_K - 1) + q) * ng, ng), ng), :]

            xs = [dcs(q) for q in range(fb - 1)]
            accs = [None] * fb
            for j in range(CONV_K):
                wk = dw_ref[(CONV_K - 1 - j) * ng:(CONV_K - j) * ng, :]
                xs.append(dcs(j + fb - 1))
                accs = [wk * xs[q + j] if accs[q] is None else accs[q] + wk * xs[q + j] for q in range(fb)]
            for q in range(fb):
                du3[pl.ds(pl.multiple_of((t0 + q) * ng, ng), ng), :] = accs[q]
            own = xs[0:fb]
            ys = [us(q) for q in range(fb - 1)]
            for k in range(CONV_K):
                ys.append(us(k + fb - 1))
                prods = [own[q] * ys[q + k] for q in range(fb)]
                while len(prods) > 1:
                    prods = [prods[2 * v] + prods[2 * v + 1] for v in range(len(prods) // 2)]
                ddw_ref[k * ng:(k + 1) * ng, :] += prods[0]
            return carry

        lax.fori_loop(0, tm // fb, fir, 0)
        dc3[tm * ng:(tm + HALO) * ng, :] = dc3[0:HALO * ng, :]

        def back(rci, carry):
            base = pl.multiple_of(rci * rc, rc)
            rows = pl.ds(base, rc)
            a = p_ref[rows, 0:e].astype(F32)
            b = p_ref[rows, e:2 * e].astype(F32)
            sb = _sig(b)
            duv = _from_token_tiles(du3, base, rc, ng)
            dp_ref[rows, 0:e] = (duv * sb).astype(BF16)
            dp_ref[rows, e:2 * e] = (duv * a * sb * (1.0 - sb)).astype(BF16)
            return carry

        lax.fori_loop(0, tm // rc, back, 0, unroll=2)

    vec = pl.BlockSpec((1, e), lambda i: (0, 0))
    taps = pl.BlockSpec((CONV_K * ng, LANES), lambda i: (0, 0))
    rev = lambda i: (nt - 1 - i, 0)
    halo = lambda i: (jnp.maximum((nt - 1 - i) * hb - 1, 0), 0)
    return pl.pallas_call(
        body, name="conv_mid_bwd", grid=(nt,),
        in_specs=[pl.BlockSpec((tm, 3 * e), rev), pl.BlockSpec((HALO, 3 * e), halo),
                  pl.BlockSpec((tm, e), rev), pl.BlockSpec((tm, e), rev), taps, vec, vec],
        out_specs=[pl.BlockSpec((tm, 3 * e), rev), vec, vec, vec, taps],
        out_shape=[jax.ShapeDtypeStruct((t, 3 * e), BF16), jax.ShapeDtypeStruct((1, e), F32),
                   jax.ShapeDtypeStruct((1, e), F32), jax.ShapeDtypeStruct((1, e), F32),
                   jax.ShapeDtypeStruct((CONV_K * ng, LANES), F32)],
        scratch_shapes=[pltpu.VMEM(((tm + HALO) * ng, LANES), F32), pltpu.VMEM(((tm + HALO) * ng, LANES), F32),
                        pltpu.VMEM((tm * ng, LANES), F32)],
        compiler_params=_params("arbitrary"),
    )(p, p, c, ds, dw3, lg, lb)


def _conv_bwd(p, c, dy, hn, w_out, dw3, lg, lb):
    t = p.shape[0]
    e = p.shape[1] // 3
    d = hn.shape[1]
    nk = 3 * e // N_CHIPS
    ng = e // LANES
    tm = _tile(t)
    nt = t // tm
    rc = CONV_ROW_CHUNK
    fb = FIR_BLOCK
    hb = tm // HALO

    def body(p_ref, ph_ref, c_ref, dyn_ref, hnp_ref, w_ref, dw_ref, lg_ref, lb_ref,
             dp_ref, dlg_ref, dlb_ref, ddwb_ref, ddw_ref, dwi_ref, u3, dc3, du3, ds_s, dp_s):
        g = pl.program_id(0)
        ti = nt - g

        def next_ds():
            ds_s[...] = lax.dot_general(dyn_ref[...].astype(BF16), w_ref[...], (((1,), (1,)), ((), ())),
                                        preferred_element_type=F32).astype(BF16)

        def prev_wgrad(slot):
            for k in range(N_CHIPS):
                dwi_ref[k] += lax.dot_general(hnp_ref[...], dp_s[slot, :, k * nk:(k + 1) * nk],
                                              (((0,), (0,)), ((), ())), preferred_element_type=F32)

        @pl.when(g == 0)
        def _():
            dc3[tm * ng:(tm + HALO) * ng, :] = jnp.zeros((HALO * ng, LANES), F32)
            dlg_ref[...] = jnp.zeros_like(dlg_ref)
            dlb_ref[...] = jnp.zeros_like(dlb_ref)
            ddwb_ref[...] = jnp.zeros_like(ddwb_ref)
            ddw_ref[...] = jnp.zeros_like(ddw_ref)
            dwi_ref[...] = jnp.zeros_like(dwi_ref)
            dp_s[0] = jnp.zeros_like(dp_s[0])
            next_ds()

        @pl.when(jnp.logical_and(g >= 1, g <= nt))
        def _():
            ha = ph_ref[:, 0:e].astype(F32)
            hbb = ph_ref[:, e:2 * e].astype(F32)
            _to_token_tiles(u3, 0, HALO, jnp.where(ti > 0, ha * _sig(hbb), 0.0), ng)
            cur = g % 2

            def front(rci, carry):
                slg, slb, sdwb = carry
                base = pl.multiple_of(rci * rc, rc)
                rows = pl.ds(base, rc)
                a = p_ref[rows, 0:e].astype(F32)
                b = p_ref[rows, e:2 * e].astype(F32)
                _to_token_tiles(u3, HALO + base, rc, a * _sig(b), ng)
                cv = c_ref[rows, :].astype(F32)
                mu = jnp.mean(cv, axis=-1, keepdims=True)
                cc = cv - mu
                var = jnp.mean(cc * cc, axis=-1, keepdims=True)
                rs = lax.rsqrt(var + LN_EPS)
                nn = cc * rs
                ln = nn * lg_ref[...] + lb_ref[...]
                z = p_ref[rows, 2 * e:3 * e].astype(F32)
                sz = _sig(z)
                sl = _sig(ln)
                dsv = ds_s[rows, :].astype(F32)
                dln = dsv * (z * sz) * _dsilu(ln, sl)
                dzb = (dsv * (ln * sl) * _dsilu(z, sz)).astype(BF16)
                dp_ref[rows, 2 * e:3 * e] = dzb
                dp_s[cur, rows, 2 * e:3 * e] = dzb
                dn = dln * lg_ref[...]
                dc = rs * (dn - jnp.mean(dn, axis=-1, keepdims=True)
                           - nn * jnp.mean(dn * nn, axis=-1, keepdims=True))
                _to_token_tiles(dc3, base, rc, dc, ng)
                return (slg + jnp.sum(dln * nn, axis=0, keepdims=True),
                        slb + jnp.sum(dln, axis=0, keepdims=True),
                        sdwb + jnp.sum(dc, axis=0, keepdims=True))

            zero = jnp.zeros((1, e), F32)
            slg, slb, sdwb = lax.fori_loop(0, tm // rc, front, (zero, zero, zero), unroll=2)
            dlg_ref[...] += slg
            dlb_ref[...] += slb
            ddwb_ref[...] += sdwb

            next_ds()
            prev_wgrad(1 - cur)
            for bi in range(tm // fb):
                t0 = bi * fb

                def dcs(q):
                    return dc3[(t0 + q) * ng:(t0 + q + 1) * ng, :]

                def us(q):
                    r0 = t0 + HALO - (CONV_K - 1) + q
                    return u3[r0 * ng:(r0 + 1) * ng, :]

                xs = [dcs(q) for q in range(fb - 1)]
                accs = [None] * fb
                for j in range(CONV_K):
                    wk = dw_ref[(CONV_K - 1 - j) * ng:(CONV_K - j) * ng, :]
                    xs.append(dcs(j + fb - 1))
                    accs = [wk * xs[q + j] if accs[q] is None else accs[q] + wk * xs[q + j] for q in range(fb)]
                for q in range(fb):
                    du3[(t0 + q) * ng:(t0 + q + 1) * ng, :] = accs[q]
                own = xs[0:fb]
                ys = [us(q) for q in range(fb - 1)]
                for k in range(CONV_K):
                    ys.append(us(k + fb - 1))
                    prods = [own[q] * ys[q + k] for q in range(fb)]
                    while len(prods) > 1:
                        prods = [prods[2 * v] + prods[2 * v + 1] for v in range(len(prods) // 2)]
                    ddw_ref[k * ng:(k + 1) * ng, :] += prods[0]
            dc3[tm * ng:(tm + HALO) * ng, :] = dc3[0:HALO * ng, :]

            def back(rci, carry):
                base = pl.multiple_of(rci * rc, rc)
                rows = pl.ds(base, rc)
                a = p_ref[rows, 0:e].astype(F32)
                b = p_ref[rows, e:2 * e].astype(F32)
                sb = _sig(b)
                duv = _from_token_tiles(du3, base, rc, ng)
                dab = (duv * sb).astype(BF16)
                dbb = (duv * a * sb * (1.0 - sb)).astype(BF16)
                dp_ref[rows, 0:e] = dab
                dp_ref[rows, e:2 * e] = dbb
                dp_s[cur, rows, 0:e] = dab
                dp_s[cur, rows, e:2 * e] = dbb
                return carry

            lax.fori_loop(0, tm // rc, back, 0, unroll=2)

        @pl.when(g == nt + 1)
        def _():
            prev_wgrad(nt % 2)

    vec = pl.BlockSpec((1, e), lambda g: (0, 0))
    taps = pl.BlockSpec((CONV_K * ng, LANES), lambda g: (0, 0))
    tile_of = lambda g: jnp.clip(nt - g, 0, nt - 1)
    cur_map = lambda g: (tile_of(g), 0)
    halo = lambda g: (jnp.maximum(tile_of(g) * hb - 1, 0), 0)
    nxt_map = lambda g: (jnp.clip(nt - 1 - g, 0, nt - 1), 0)
    prv_map = lambda g: (jnp.clip(nt + 1 - g, 0, nt - 1), 0)
    return pl.pallas_call(
        body, name="conv_bwd", grid=(nt + 2,),
        in_specs=[pl.BlockSpec((tm, 3 * e), cur_map), pl.BlockSpec((HALO, 3 * e), halo),
                  pl.BlockSpec((tm, e), cur_map), pl.BlockSpec((tm, d), nxt_map),
                  pl.BlockSpec((tm, d), prv_map), _resident((e, d)), taps, vec, vec],
        out_specs=[pl.BlockSpec((tm, 3 * e), cur_map), vec, vec, vec, taps,
                   pl.BlockSpec((N_CHIPS, d, nk), lambda g: (0, 0, 0))],
        out_shape=[jax.ShapeDtypeStruct((t, 3 * e), BF16), jax.ShapeDtypeStruct((1, e), F32),
                   jax.ShapeDtypeStruct((1, e), F32), jax.ShapeDtypeStruct((1, e), F32),
                   jax.ShapeDtypeStruct((CONV_K * ng, LANES), F32),
                   jax.ShapeDtypeStruct((N_CHIPS, d, nk), F32)],
        scratch_shapes=[pltpu.VMEM(((tm + HALO) * ng, LANES), F32), pltpu.VMEM(((tm + HALO) * ng, LANES), F32),
                        pltpu.VMEM((tm * ng, LANES), F32), pltpu.VMEM((tm, e), BF16),
                        pltpu.VMEM((2, tm, 3 * e), BF16)],
        compiler_params=_params("arbitrary"),
    )(p, p, c, dy, hn, w_out, dw3, lg, lb)


def _pool_mid_bwd(p, ds, wg, bg, sc):
    t = p.shape[0]
    e = p.shape[1] // 2
    ng = len(POOL_WINDOWS)
    gc = e // ng
    tm = _tile(t)
    nt = t // tm
    rc = ROW_CHUNK
    hb = tm // PHALO

    def body(p_ref, ph_ref, ds_ref, wg_ref, bg_ref, sc_ref,
             dp_ref, dwg_ref, dbg_ref, dsc_ref, ubuf, dbuf, ebuf, ddbuf):
        i = pl.program_id(0)
        ti = nt - 1 - i

        @pl.when(i == 0)
        def _():
            ebuf[tm:tm + PHALO, :] = jnp.zeros((PHALO, e), F32)
            dwg_ref[...] = jnp.zeros_like(dwg_ref)
            dbg_ref[...] = jnp.zeros_like(dbg_ref)
            dsc_ref[...] = jnp.zeros_like(dsc_ref)

        ubuf[0:PHALO, :] = jnp.where(ti > 0, ph_ref[:, 0:e].astype(F32), 0.0)
        ubuf[PHALO:PHALO + tm, :] = p_ref[:, 0:e].astype(F32)

        def recompute(rci, carry):
            base = pl.multiple_of(rci * rc, rc)
            _pool_window_dev(ubuf, base, rc, e, ti * tm, dbuf)
            return carry

        lax.fori_loop(0, tm // rc, recompute, 0)

        for g in range(ng):
            cols = slice(g * gc, (g + 1) * gc)
            dg = dbuf[:, cols].astype(BF16)
            q = jnp.dot(dg, wg_ref[g], preferred_element_type=F32) + bg_ref[:, cols]
            z = p_ref[:, e + g * gc:e + (g + 1) * gc].astype(F32)
            sz = _sig(z)
            dsv = ds_ref[:, cols].astype(F32)
            dz = dsv * (q * sc_ref[:, cols]) * _dsilu(z, sz)
            dp_ref[:, e + g * gc:e + (g + 1) * gc] = dz.astype(BF16)
            dy2 = dsv * (z * sz)
            dsc_ref[:, cols] += jnp.sum(dy2 * q, axis=0, keepdims=True)
            dq = dy2 * sc_ref[:, cols]
            dbg_ref[:, cols] += jnp.sum(dq, axis=0, keepdims=True)
            dqb = dq.astype(BF16)
            dwg_ref[g] += lax.dot_general(dg, dqb, (((0,), (0,)), ((), ())), preferred_element_type=F32)
            ddbuf[:, cols] = lax.dot_general(dqb, wg_ref[g], (((1,), (1,)), ((), ())),
                                             preferred_element_type=F32)

        def scale(rci, carry):
            base = pl.multiple_of(rci * rc, rc)
            for lc in range(e // LANES):
                lanes = slice(lc * LANES, (lc + 1) * LANES)
                w = POOL_WINDOWS[_pool_group(lc, e)]
                ebuf[pl.ds(base, rc), lanes] = (ddbuf[pl.ds(base, rc), lanes]
                                                * _pool_inv_count(ti * tm + base, rc, w))
            return carry

        lax.fori_loop(0, tm // rc, scale, 0)

        def chunk(rci, carry):
            base = pl.multiple_of(rci * rc, rc)
            n = rc + PHALO
            for lc in range(e // LANES):
                lanes = slice(lc * LANES, (lc + 1) * LANES)
                w = POOL_WINDOWS[_pool_group(lc, e)]
                acc = ebuf[pl.ds(base, n), lanes]
                step = 1
                while step < w:
                    acc = acc + pltpu.roll(acc, n - step, 0)
                    step *= 2
                du = acc[0:rc] - ddbuf[pl.ds(base, rc), lanes]
                dp_ref[pl.ds(base, rc), lanes] = du.astype(BF16)
            return carry

        lax.fori_loop(0, tm // rc, chunk, 0)
        ebuf[tm:tm + PHALO, :] = ebuf[0:PHALO, :]

    vec = pl.BlockSpec((1, e), lambda i: (0, 0))
    rev = lambda i: (nt - 1 - i, 0)
    halo = lambda i: (jnp.maximum((nt - 1 - i) * hb - 1, 0), 0)
    wspec = pl.BlockSpec((ng, gc, gc), lambda i: (0, 0, 0))
    return pl.pallas_call(
        body, name="pool_mid_bwd", grid=(nt,),
        in_specs=[pl.BlockSpec((tm, 2 * e), rev), pl.BlockSpec((PHALO, 2 * e), halo),
                  pl.BlockSpec((tm, e), rev), wspec, vec, vec],
        out_specs=[pl.BlockSpec((tm, 2 * e), rev), wspec, vec, vec],
        out_shape=[jax.ShapeDtypeStruct((t, 2 * e), BF16), jax.ShapeDtypeStruct((ng, gc, gc), F32),
                   jax.ShapeDtypeStruct((1, e), F32), jax.ShapeDtypeStruct((1, e), F32)],
        scratch_shapes=[pltpu.VMEM((tm + PHALO, e), F32), pltpu.VMEM((tm, e), F32),
                        pltpu.VMEM((tm + PHALO, e), F32), pltpu.VMEM((tm, e), F32)],
        compiler_params=_params("arbitrary"),
    )(p, p, ds, wg, bg, sc)


def _dhn_rms_bwd(dp, w4, h, g, dh_out, name):
    t, d = h.shape
    nk = w4.shape[-1]
    tm = min(MATMUL_TILE, t)

    def body(dp_ref, w_ref, h_ref, g_ref, dho_ref, dh_ref, dg_ref):
        i = pl.program_id(0)

        @pl.when(i == 0)
        def _():
            dg_ref[...] = jnp.zeros_like(dg_ref)

        dhn = jnp.zeros((tm, d), F32)
        for k in range(N_CHIPS):
            dhn = dhn + lax.dot_general(dp_ref[:, k * nk:(k + 1) * nk], w_ref[k], (((1,), (1,)), ((), ())),
                                        preferred_element_type=F32)
        hh = h_ref[...]
        r = lax.rsqrt(jnp.mean(hh * hh, axis=-1, keepdims=True) + RMS_EPS)
        hhat = hh * r
        tt = dhn * g_ref[...]
        dh_ref[...] = dho_ref[...] + r * (tt - hhat * jnp.mean(tt * hhat, axis=-1, keepdims=True))
        dg_ref[...] += jnp.sum(dhn * hhat, axis=0, keepdims=True)

    return pl.pallas_call(
        body, name=name, grid=(t // tm,),
        in_specs=[pl.BlockSpec((tm, N_CHIPS * nk), lambda i: (i, 0)),
                  _resident((N_CHIPS, d, nk)),
                  pl.BlockSpec((tm, d), lambda i: (i, 0)), pl.BlockSpec((1, d), lambda i: (0, 0)),
                  pl.BlockSpec((tm, d), lambda i: (i, 0))],
        out_specs=[pl.BlockSpec((tm, d), lambda i: (i, 0)), pl.BlockSpec((1, d), lambda i: (0, 0))],
        out_shape=[jax.ShapeDtypeStruct((t, d), F32), jax.ShapeDtypeStruct((1, d), F32)],
        compiler_params=_params("arbitrary"),
    )(dp, w4, h, g, dh_out)


def _wgrad(a, b, nblk, name):
    t, m = a.shape
    nn = b.shape[1] // nblk
    tk = min(WGRAD_TILE, t)
    nk = t // tk

    def body(a_ref, b_ref, o_ref, ob_ref):
        @pl.when(pl.program_id(1) == 0)
        def _():
            o_ref[...] = jnp.zeros_like(o_ref)

        o_ref[...] += lax.dot_general(a_ref[...].astype(BF16), b_ref[...].astype(BF16),
                                      (((0,), (0,)), ((), ())), preferred_element_type=F32)

        @pl.when(pl.program_id(1) == nk - 1)
        def _():
            ob_ref[...] = o_ref[...].astype(BF16)

    ospec = pl.BlockSpec((None, m, nn), lambda j, i: (j, 0, 0))
    return pl.pallas_call(
        body, name=name, grid=(nblk, nk),
        in_specs=[pl.BlockSpec((tk, m), lambda j, i: (i, 0)), pl.BlockSpec((tk, nn), lambda j, i: (i, j))],
        out_specs=[ospec, ospec],
        out_shape=[jax.ShapeDtypeStruct((nblk, m, nn), F32), jax.ShapeDtypeStruct((nblk, m, nn), BF16)],
        compiler_params=_params("parallel", "arbitrary"),
    )(a, b)


def _rows2d(shape):
    rows = 1
    for s in shape[:-1]:
        rows *= s
    return rows, shape[-1]


def _row_tile(rows):
    for cand in (512, 256, 128, 64, 32, 16, 8):
        if rows % cand == 0:
            return cand
    return rows


def _add_eight(own, landed, chip_core, name):
    _, _, rows, cols = own.shape
    tr = _row_tile(rows)

    def body(sel_ref, s_ref, r_ref, o_ref):
        acc = s_ref[...]
        for peer in range(N_DEV - 1):
            acc = acc + r_ref[peer].astype(F32)
        o_ref[...] = acc

    return pl.pallas_call(
        body, name=name,
        grid_spec=pltpu.PrefetchScalarGridSpec(
            num_scalar_prefetch=1, grid=(rows // tr,),
            in_specs=[pl.BlockSpec((None, None, tr, cols), lambda i, s: (s[0], s[1], i, 0)),
                      pl.BlockSpec((N_DEV - 1, tr, cols), lambda i, s: (0, i, 0))],
            out_specs=pl.BlockSpec((None, tr, cols), lambda i, s: (s[1], i, 0))),
        out_shape=jax.ShapeDtypeStruct((2, rows, cols), F32),
        compiler_params=_params("parallel"),
    )(chip_core, own, landed)


def _adamw(w, g, m, v, name):
    rows, cols = w.shape
    tr = _row_tile(rows)

    def body(w_ref, g_ref, m_ref, v_ref, d_ref, m2_ref, v2_ref):
        gg = g_ref[...]
        m2 = ADAM_B1 * m_ref[...] + (1.0 - ADAM_B1) * gg
        v2 = ADAM_B2 * v_ref[...] + (1.0 - ADAM_B2) * (gg * gg)
        m_hat = m2 / (1.0 - ADAM_B1 ** ADAM_STEP)
        v_hat = v2 / (1.0 - ADAM_B2 ** ADAM_STEP)
        d_ref[...] = -ADAM_LR * (m_hat / (jnp.sqrt(v_hat) + ADAM_EPS) + ADAM_WD * w_ref[...])
        m2_ref[...] = m2
        v2_ref[...] = v2

    spec = pl.BlockSpec((tr, cols), lambda i: (i, 0))
    shp = jax.ShapeDtypeStruct((rows, cols), F32)
    return pl.pallas_call(
        body, name=name, grid=(rows // tr,),
        in_specs=[spec, spec, spec, spec], out_specs=[spec, spec, spec], out_shape=[shp, shp, shp],
        compiler_params=_params("parallel"),
    )(w, g, m, v)


ANY = pl.BlockSpec(memory_space=pl.ANY)


def _place():
    x, y, c = lax.axis_index("x"), lax.axis_index("y"), lax.axis_index("c")
    chips = [(1 - x, y), (x, 1 - y), (1 - x, 1 - y)]
    return x, y, c, chips


def _allgather_weights(shards):
    n = len(shards)

    def body(*refs):
        ins, outs = refs[:n], refs[n:2 * n]
        send_ici, recv_ici, send_d2d, recv_d2d = refs[2 * n:]
        x, y, c, chips = _place()
        k0 = 2 * x + y
        sib = (x, y, 1 - c)

        def ici(a, r, src_chip, target):
            return pltpu.make_async_remote_copy(
                src_ref=ins[a].at[c], dst_ref=outs[a].at[src_chip, c],
                send_sem=send_ici.at[a * 3 + r], recv_sem=recv_ici.at[a * 3 + r],
                device_id=target, device_id_type=MESH)

        def d2d(a, r, src_chip, layer):
            return pltpu.make_async_remote_copy(
                src_ref=outs[a].at[src_chip, layer], dst_ref=outs[a].at[src_chip, layer],
                send_sem=send_d2d.at[a * 3 + r], recv_sem=recv_d2d.at[a * 3 + r],
                device_id=sib, device_id_type=MESH)

        first = [ici(a, r, k0, (cx, cy, c)) for a in range(n) for r, (cx, cy) in enumerate(chips)]
        for cp in first:
            cp.start()
        passed = []
        for a in range(n):
            for r, (cx, cy) in enumerate(chips):
                ici(a, r, 2 * cx + cy, (cx, cy, c)).wait_recv()
                cp = d2d(a, r, 2 * cx + cy, c)
                cp.start()
                passed.append(cp)
        for a in range(n):
            for r, (cx, cy) in enumerate(chips):
                d2d(a, r, 2 * cx + cy, 1 - c).wait_recv()
        for cp in first + passed:
            cp.wait_send()

    return pl.pallas_call(
        body, name="allgather_weights",
        in_specs=[ANY] * n, out_specs=[ANY] * n,
        out_shape=[jax.ShapeDtypeStruct((N_CHIPS,) + s.shape, s.dtype) for s in shards],
        scratch_shapes=[pltpu.SemaphoreType.DMA((3 * n,)), pltpu.SemaphoreType.DMA((3 * n,)),
                        pltpu.SemaphoreType.DMA((3 * n,)), pltpu.SemaphoreType.DMA((3 * n,))],
    )(*shards)


def _put_own(gathered, shard, chip):
    return lax.dynamic_update_slice_in_dim(gathered, shard[None], chip, axis=0)


HBM = pl.BlockSpec(memory_space=pltpu.HBM)
SEM = pl.BlockSpec(memory_space=pltpu.SEMAPHORE)
DATAFLOW = pltpu.SideEffectType.DATAFLOW_SIDE_EFFECTING
FLIPS = [(0, 0, 1), (0, 1, 0), (0, 1, 1), (1, 0, 0), (1, 0, 1), (1, 1, 0), (1, 1, 1)]


def _gather_plan(srcs, lands, send_sem, recv_sem):
    x, y, c, chips = _place()
    return [pltpu.make_async_remote_copy(
        src_ref=srcs[a], dst_ref=lands[a].at[2 * x + y],
        send_sem=send_sem.at[a * 3 + r], recv_sem=recv_sem.at[a * 3 + r],
        device_id=(cx, cy, c), device_id_type=MESH)
        for a in range(len(srcs)) for r, (cx, cy) in enumerate(chips)]


def _scatter_plan(srcs, lands, send_sem, recv_sem):
    x, y, c, _ = _place()
    cps = []
    for a in range(len(srcs)):
        for r, (fx, fy, fc) in enumerate(FLIPS):
            tx, ty, tc = (1 - x if fx else x), (1 - y if fy else y), (1 - c if fc else c)
            cps.append(pltpu.make_async_remote_copy(
                src_ref=srcs[a].at[2 * tx + ty, tc], dst_ref=lands[a].at[r],
                send_sem=send_sem.at[a * len(FLIPS) + r], recv_sem=recv_sem.at[a * len(FLIPS) + r],
                device_id=(tx, ty, tc), device_id_type=MESH))
    return cps


def _split_start(name, plan, srcs, lands, n_copies, after):
    n = len(srcs)

    def body(*refs):
        src, land = refs[:n], refs[n:2 * n]
        send_sem, recv_sem = refs[2 * n + 1], refs[2 * n + 2]
        token = refs[-1]
        for cp in plan(src, land, send_sem, recv_sem):
            cp.start()
        token[...] = jnp.zeros_like(token)

    outs = pl.pallas_call(
        body, name=name,
        in_specs=[HBM] * (2 * n) + [ANY],
        out_specs=[SEM, SEM] + [HBM] * (2 * n) + [pl.BlockSpec(memory_space=pltpu.VMEM)],
        out_shape=[pltpu.SemaphoreType.DMA((n_copies,)), pltpu.SemaphoreType.DMA((n_copies,))]
        + [pltpu.HBM(s.shape, s.dtype) for s in srcs] + [pltpu.HBM(l.shape, l.dtype) for l in lands]
        + [jax.ShapeDtypeStruct((8, LANES), F32)],
        input_output_aliases={i: 2 + i for i in range(2 * n)},
        compiler_params=pltpu.CompilerParams(has_side_effects=DATAFLOW),
    )(*[pltpu.with_memory_space_constraint(s, pltpu.HBM) for s in srcs],
      *[pltpu.with_memory_space_constraint(l, pltpu.HBM) for l in lands], after)
    return outs[0], outs[1], list(outs[2:2 + n]), list(outs[2 + n:2 + 2 * n]), outs[-1]


def _split_wait(name, plan, send_sems, recv_sems, srcs, lands, after):
    n = len(srcs)

    def body(*refs):
        src, land = refs[:n], refs[n:2 * n]
        send_sem, recv_sem = refs[2 * n], refs[2 * n + 1]
        for cp in plan(src, land, send_sem, recv_sem):
            cp.wait_send()
            cp.wait_recv()

    outs = pl.pallas_call(
        body, name=name,
        in_specs=[HBM] * (2 * n) + [SEM, SEM, ANY],
        out_specs=[HBM] * (2 * n),
        out_shape=[pltpu.HBM(s.shape, s.dtype) for s in srcs] + [pltpu.HBM(l.shape, l.dtype) for l in lands],
        input_output_aliases={i: i for i in range(2 * n)},
        compiler_params=pltpu.CompilerParams(has_side_effects=DATAFLOW),
    )(*srcs, *lands, send_sems, recv_sems, after)
    return list(outs[:n]), list(outs[n:])


def _share_halves(halves):
    n = len(halves)

    def body(*refs):
        ins, outs = refs[:n], refs[n:2 * n]
        send_sem, recv_sem = refs[2 * n:]
        x, y, c, _ = _place()
        cps = [pltpu.make_async_remote_copy(
            src_ref=outs[a].at[c], dst_ref=outs[a].at[c], send_sem=send_sem.at[a], recv_sem=recv_sem.at[a],
            device_id=(x, y, 1 - c), device_id_type=MESH) for a in range(n)]
        for cp in cps:
            cp.start()
        for cp in cps:
            cp.wait()

    return pl.pallas_call(
        body, name="share_halves",
        in_specs=[ANY] * n, out_specs=[ANY] * n,
        out_shape=[jax.ShapeDtypeStruct(h.shape, h.dtype) for h in halves],
        input_output_aliases={a: a for a in range(n)},
        scratch_shapes=[pltpu.SemaphoreType.DMA((n,)), pltpu.SemaphoreType.DMA((n,))],
    )(*halves)


N_DEV = 8


def _allreduce_small(v):
    m, nc = v.shape

    def body(x_ref, out_ref, gat, send_sems, recv_sems, local_sem):
        x, y, c, chips = _place()
        me, sib = (x, y, c), (x, y, 1 - c)

        def rows(px, py, pc):
            return gat.at[pl.ds((4 * px + 2 * py + pc) * m, m), :]

        def copy(k, block, to, src=None):
            return pltpu.make_async_remote_copy(
                src_ref=rows(*block) if src is None else src, dst_ref=rows(*block),
                send_sem=send_sems.at[k], recv_sem=recv_sems.at[k], device_id=to, device_id_type=MESH)

        mine = pltpu.make_async_copy(x_ref, rows(*me), local_sem)
        mine.start()
        first = [copy(0, me, sib, src=x_ref)]
        first += [copy(1 + j, me, (*chip, c), src=x_ref) for j, chip in enumerate(chips)]
        for cp in first:
            cp.start()
        passed = [copy(4 + j, (*chip, c), sib) for j, chip in enumerate(chips)]
        for j, chip in enumerate(chips):
            copy(1 + j, (*chip, c), me).wait_recv()
            passed[j].start()
        copy(0, sib, me).wait_recv()
        for j, chip in enumerate(chips):
            copy(4 + j, (*chip, 1 - c), me).wait_recv()
        for cp in first + passed:
            cp.wait_send()
        mine.wait()
        acc = gat[0:m, :]
        for dev in range(1, N_DEV):
            acc = acc + gat[dev * m:(dev + 1) * m, :]
        out_ref[...] = acc

    return pl.pallas_call(
        body, name="allreduce_small",
        in_specs=[pl.BlockSpec(memory_space=pltpu.VMEM)],
        out_specs=pl.BlockSpec(memory_space=pltpu.VMEM),
        out_shape=jax.ShapeDtypeStruct((m, nc), F32),
        scratch_shapes=[pltpu.VMEM((N_DEV * m, nc), F32), pltpu.SemaphoreType.DMA((7,)),
                        pltpu.SemaphoreType.DMA((7,)), pltpu.SemaphoreType.DMA],
        compiler_params=pltpu.CompilerParams(vmem_limit_bytes=VMEM_LIMIT),
    )(v)


def _pad_rows(a, rows):
    return jnp.pad(a, ((0, rows - a.shape[0]), (0, 0)))


def kernel(x, norm_g, final_g, conv_w_in, conv_dw, conv_dw_b, conv_ln_g, conv_ln_b, conv_w_out, pool_w_in, pool_w_grp, pool_b_grp, pool_scale, pool_w_out, loss_target, m_norm_g, m_final_g, m_conv_w_in, m_conv_dw, m_conv_dw_b, m_conv_ln_g, m_conv_ln_b, m_conv_w_out, m_pool_w_in, m_pool_w_grp, m_pool_b_grp, m_pool_scale, m_pool_w_out, v_norm_g, v_final_g, v_conv_w_in, v_conv_dw, v_conv_dw_b, v_conv_ln_g, v_conv_ln_b, v_conv_w_out, v_pool_w_in, v_pool_w_grp, v_pool_b_grp, v_pool_scale, v_pool_w_out):
    t, d = x.shape[1], x.shape[2]
    e = conv_w_out.shape[2]
    ng = len(POOL_WINDOWS)
    gc = e // ng
    gcs = pool_w_grp.shape[2]
    ck = conv_dw.shape[1]
    es = conv_dw.shape[2]
    xi, yi, ci = lax.axis_index("x"), lax.axis_index("y"), lax.axis_index("c")
    chip = 2 * xi + yi

    small_rows = ck + 2
    small_pad = -(-small_rows // 8) * 8
    small = jnp.concatenate([conv_dw, pool_b_grp[:, None, :], pool_scale[:, None, :],
                             jnp.zeros((2, small_pad - small_rows, es), F32)], axis=1)
    cwi_b, cwo_b = conv_w_in.astype(BF16), conv_w_out.astype(BF16)

    def halves(a):
        return a.reshape(2, a.shape[0] // 2, a.shape[1])

    first = [halves(cwi_b[0]), halves(cwo_b[0]), small]
    g_cwi0, g_cwo0, g_small = [_put_own(g, q, chip) for g, q in zip(_allgather_weights(first), first)]
    rest = [cwi_b[1], cwo_b[1], pool_w_in.astype(BF16), pool_w_grp.astype(BF16), pool_w_out.astype(BF16)]
    rest_lands = [lax.empty((N_CHIPS,) + r.shape, r.dtype) for r in rest]
    ag_send, ag_recv, rest, rest_lands, ag_token = _split_start(
        "gather_rest_start", _gather_plan, rest, rest_lands, 3 * len(rest), g_small)
    smallf = jnp.transpose(g_small, (1, 2, 0, 3)).reshape(2, small_pad, N_CHIPS * es)

    h = x.reshape(t, d)
    tgt = loss_target.reshape(t, d)
    hs, saved = [], []
    for layer in range(4):
        j = layer // 2
        hs.append(h)
        gvec = norm_g[layer][None, :]
        if layer == 0:
            gvec = gvec + ag_token[0:1, 0:1]
        if layer == 1:
            rest, rest_lands = _split_wait("gather_rest_wait", _gather_plan, ag_send, ag_recv, rest, rest_lands, h)
            g_cwi1, g_cwo1, g_pwi, g_pwg, g_pwo = [_put_own(g, q, chip) for g, q in zip(rest_lands, rest)]
            wg_full = jnp.transpose(g_pwg, (1, 2, 0, 3, 4)).reshape(2, ng, gc, gc)
        if layer % 2 == 0:
            g_in, g_out = (g_cwi0, g_cwo0) if j == 0 else (g_cwi1, g_cwo1)
            w_in4 = g_in.reshape(N_CHIPS, d, -1)
            w_out = g_out.reshape(e, d)
            p, hn = _rms_matmul(h, gvec, w_in4, "rms_matmul_conv")
            dw_full = smallf[j, 0:ck]
            s, c, h = _conv_mid_out_fwd(p, h, w_out, dw_full.reshape(-1, LANES), conv_dw_b[j].reshape(-1, LANES),
                                        conv_ln_g[j][None, :], conv_ln_b[j][None, :])
            saved.append((p, hn, s, c, w_in4, w_out, dw_full))
        else:
            w_in4 = g_pwi[:, j]
            w_out = g_pwo[:, j].reshape(e, d)
            p, hn = _rms_matmul(h, gvec, w_in4, "rms_matmul_pool")
            bg_full = smallf[j, ck:ck + 1]
            sc_full = smallf[j, ck + 1:ck + 2]
            s = _pool_mid_fwd(p, wg_full[j], bg_full, sc_full)
            saved.append((p, hn, s, None, w_in4, w_out, (wg_full[j], bg_full, sc_full)))
            h = _matmul_res(h, s, w_out)

    dh, loss_part, dfg = _loss_head(h, final_g[None, :], tgt)
    loss = lax.psum(loss_part[0, 0], ("x", "y", "c"))

    def by_half(a):
        return a.reshape(N_CHIPS, 2, a.shape[1] // 2, a.shape[2])

    dng = [None] * 4
    g_conv = [None, None]
    g_pool = [None, None]
    flights = {}
    for layer in (3, 2, 1, 0):
        j = layer // 2
        p, hn, s, c, w_in4, w_out, extra = saved[layer]
        gvec = norm_g[layer][None, :]
        dw_out, dw_out_b = [q.reshape(N_CHIPS, e // N_CHIPS, d) for q in _wgrad(s, dh, 1, "wgrad_out")]
        if layer % 2 == 0:
            dp, dlg, dlb, ddwb, ddw3, dw_in = _conv_bwd(p, c, dh, hn, w_out, extra.reshape(-1, LANES),
                                                        conv_ln_g[j][None, :], conv_ln_b[j][None, :])
            ddw = ddw3.reshape(ck, e)
            dw_in_b = dw_in.astype(BF16)
            own, pay = [dw_in, dw_out], [dw_in_b, dw_out_b]
            g_conv[j] = (dlg, dlb, ddwb, ddw)
        else:
            wg, bg_full, sc_full = extra
            ds = _ds_matmul(dh, w_out)
            dp, dwg, dbg, dsc = _pool_mid_bwd(p, ds, wg, bg_full, sc_full)
            dw_in, dw_in_b = _wgrad(hn, dp, N_CHIPS, "wgrad_in_pool")
            dwg4 = jnp.transpose(dwg.reshape(ng, N_CHIPS, gcs, gc), (1, 0, 2, 3)).reshape(N_CHIPS, ng * gcs, gc)
            own, pay = [dw_in, dw_out, dwg4], [dw_in_b, dw_out_b, dwg4.astype(BF16)]
            g_pool[j] = (dbg, dsc)
        pay = [by_half(q) for q in pay]
        lands = [lax.empty((len(FLIPS),) + q.shape[2:], BF16) for q in pay]
        send, recv, pay, lands, token = _split_start(
            "scatter_start_%d" % layer, _scatter_plan, pay, lands, len(FLIPS) * len(pay), own[0])
        flights[layer] = (send, recv, pay, lands, own)
        kind = "dhn_rms_bwd_conv" if layer % 2 == 0 else "dhn_rms_bwd_pool"
        dh, dng[layer] = _dhn_rms_bwd(dp, w_in4, hs[layer], gvec + token[0:1, 0:1], dh, kind)
    grad_x = dh.reshape(x.shape)

    sel_kc = jnp.stack([chip, ci]).astype(jnp.int32)
    summed = {}
    after = dh
    for layer in (3, 2, 1, 0):
        send, recv, pay, lands, own = flights[layer]
        _, lands = _split_wait("scatter_wait_%d" % layer, _scatter_plan, send, recv, pay, lands, after)
        after = lands[0]
        summed[layer] = [_add_eight(by_half(o), l, sel_kc, "add_eight_%d_%d" % (layer % 2, a))
                         for a, (o, l) in enumerate(zip(own, lands))]
    order = [(0, 0), (0, 1), (2, 0), (2, 1), (1, 0), (1, 1), (1, 2), (3, 0), (3, 1), (3, 2)]
    shared = _share_halves([summed[l][a] for l, a in order])
    full = {la: q.reshape(q.shape[0] * q.shape[1], q.shape[2]) for la, q in zip(order, shared)}
    g_cwi_f = jnp.stack([full[(0, 0)], full[(2, 0)]])
    g_cwo_f = jnp.stack([full[(0, 1)], full[(2, 1)]])
    g_pwi_f = jnp.stack([full[(1, 0)], full[(3, 0)]])
    g_pwo_f = jnp.stack([full[(1, 1)], full[(3, 1)]])
    g_pwg_f = jnp.stack([full[(1, 2)], full[(3, 2)]]).reshape(pool_w_grp.shape)

    rows_list = [dng[0], dng[1], dng[2], dng[3], dfg,
                 g_conv[0][2], g_conv[1][2], g_conv[0][0], g_conv[1][0], g_conv[0][1], g_conv[1][1],
                 g_pool[0][0], g_pool[1][0], g_pool[0][1], g_pool[1][1], g_conv[0][3], g_conv[1][3]]
    slab = jnp.concatenate(rows_list, axis=0)
    nrows = slab.shape[0]
    slab = _pad_rows(slab, -(-nrows // 8) * 8)
    tot = _allreduce_small(slab)
    g_norm_g = tot[0:4]
    g_final_g = tot[4]
    g_dwb = tot[5:7]
    g_lng = tot[7:9]
    g_lnb = tot[9:11]
    g_bg = lax.dynamic_slice_in_dim(tot[11:13], chip * es, es, axis=1)
    g_sc = lax.dynamic_slice_in_dim(tot[13:15], chip * es, es, axis=1)
    g_dw = lax.dynamic_slice_in_dim(tot[15:15 + 2 * ck].reshape(2, ck, e), chip * es, es, axis=2)

    def adam_nd(w, g, m, v, nm):
        rows, cols = _rows2d(w.shape)
        outs = _adamw(w.reshape(rows, cols), g.reshape(rows, cols), m.reshape(rows, cols),
                      v.reshape(rows, cols), "adamw_" + nm)
        return [o.reshape(w.shape) for o in outs]

    res = {}
    res["conv_w_in"] = (g_cwi_f, *adam_nd(conv_w_in, g_cwi_f, m_conv_w_in, v_conv_w_in, "cwi"))
    res["conv_w_out"] = (g_cwo_f, *adam_nd(conv_w_out, g_cwo_f, m_conv_w_out, v_conv_w_out, "cwo"))
    res["pool_w_in"] = (g_pwi_f, *adam_nd(pool_w_in, g_pwi_f, m_pool_w_in, v_pool_w_in, "pwi"))
    res["pool_w_grp"] = (g_pwg_f, *adam_nd(pool_w_grp, g_pwg_f, m_pool_w_grp, v_pool_w_grp, "pwg"))
    res["pool_w_out"] = (g_pwo_f, *adam_nd(pool_w_out, g_pwo_f, m_pool_w_out, v_pool_w_out, "pwo"))

    def pack(parts, rows_to):
        return _pad_rows(jnp.concatenate([q.reshape(-1, q.shape[-1]) for q in parts], axis=0), rows_to)

    rep_w = [norm_g, final_g[None, :], conv_dw_b, conv_ln_g, conv_ln_b]
    rep_g = [g_norm_g, g_final_g[None, :], g_dwb, g_lng, g_lnb]
    rep_m = [m_norm_g, m_final_g[None, :], m_conv_dw_b, m_conv_ln_g, m_conv_ln_b]
    rep_v = [v_norm_g, v_final_g[None, :], v_conv_dw_b, v_conv_ln_g, v_conv_ln_b]
    rep = _adamw(pack(rep_w, 16), pack(rep_g, 16), pack(rep_m, 16), pack(rep_v, 16), "adamw_rep")
    rep_names = ["norm_g", "final_g", "conv_dw_b", "conv_ln_g", "conv_ln_b"]
    rep_rows = [(0, 4), (4, 5), (5, 7), (7, 9), (9, 11)]
    for nm, (lo, hi), gq, wq in zip(rep_names, rep_rows, rep_g, rep_w):
        shape = (d,) if nm == "final_g" else wq.shape
        res[nm] = (gq.reshape(shape), *[o[lo:hi].reshape(shape) for o in rep])

    sh_w = [conv_dw, pool_b_grp, pool_scale]
    sh_g = [g_dw, g_bg, g_sc]
    sh_m = [m_conv_dw, m_pool_b_grp, m_pool_scale]
    sh_v = [v_conv_dw, v_pool_b_grp, v_pool_scale]
    sh_total = 2 * ck + 4
    sh_pad = -(-sh_total // 8) * 8
    shd = _adamw(pack(sh_w, sh_pad), pack(sh_g, sh_pad), pack(sh_m, sh_pad), pack(sh_v, sh_pad), "adamw_shard")
    sh_names = ["conv_dw", "pool_b_grp", "pool_scale"]
    sh_rows = [(0, 2 * ck), (2 * ck, 2 * ck + 2), (2 * ck + 2, 2 * ck + 4)]
    for nm, (lo, hi), gq, wq in zip(sh_names, sh_rows, sh_g, sh_w):
        res[nm] = (gq.reshape(wq.shape), *[o[lo:hi].reshape(wq.shape) for o in shd])

    order = ["norm_g", "final_g", "conv_w_in", "conv_dw", "conv_dw_b", "conv_ln_g", "conv_ln_b", "conv_w_out",
             "pool_w_in", "pool_w_grp", "pool_b_grp", "pool_scale", "pool_w_out"]
    outs = [loss, grad_x]
    for part in range(4):
        outs += [res[nm][part] for nm in order]
    return tuple(outs)
```

```python
import functools

import jax
import jax.numpy as jnp
from jax import lax
from jax.experimental import pallas as pl
from jax.experimental.pallas import tpu as pltpu

F32 = jnp.float32
BF16 = jnp.bfloat16
MESH = pl.DeviceIdType.MESH

RMS_EPS = 1e-6
LN_EPS = 1e-5
CONV_K = 31
HALO = 32
PHALO = 16
POOL_WINDOWS = (2, 4, 8, 16)
N_CHIPS = 4
LANES = 128
ROW_CHUNK = 32
CONV_ROW_CHUNK = 32
FIR_BLOCK = 16
TOKEN_TILE = 512
MATMUL_TILE = 1024
WGRAD_TILE = 2048
VMEM_LIMIT = 56 * 1024 * 1024

ADAM_LR = 0.001
ADAM_B1 = 0.9
ADAM_B2 = 0.999
ADAM_EPS = 1e-08
ADAM_WD = 0.01
ADAM_STEP = 10


def _params(*sem):
    return pltpu.CompilerParams(dimension_semantics=sem, vmem_limit_bytes=VMEM_LIMIT)


def _sig(v):
    return 0.5 * jnp.tanh(0.5 * v) + 0.5


def _dsilu(v, sv):
    return sv * (1.0 + v * (1.0 - sv))


def _resident(shape):
    return pl.BlockSpec(shape, lambda *_: (0,) * len(shape), pipeline_mode=pl.Buffered(1))


def _tile(t):
    return min(TOKEN_TILE, t)


def _rms_matmul(h, g, w4, name):
    t, d = h.shape
    nk = w4.shape[-1]
    tm = min(MATMUL_TILE, t)

    def body(h_ref, g_ref, w_ref, p_ref, hn_ref):
        hh = h_ref[...]
        r = lax.rsqrt(jnp.mean(hh * hh, axis=-1, keepdims=True) + RMS_EPS)
        hn = (hh * r * g_ref[...]).astype(BF16)
        hn_ref[...] = hn
        for k in range(N_CHIPS):
            p_ref[:, k * nk:(k + 1) * nk] = jnp.dot(hn, w_ref[k], preferred_element_type=F32).astype(BF16)

    return pl.pallas_call(
        body, name=name, grid=(t // tm,),
        in_specs=[pl.BlockSpec((tm, d), lambda i: (i, 0)),
                  pl.BlockSpec((1, d), lambda i: (0, 0)),
                  _resident((N_CHIPS, d, nk))],
        out_specs=[pl.BlockSpec((tm, N_CHIPS * nk), lambda i: (i, 0)),
                   pl.BlockSpec((tm, d), lambda i: (i, 0))],
        out_shape=[jax.ShapeDtypeStruct((t, N_CHIPS * nk), BF16), jax.ShapeDtypeStruct((t, d), BF16)],
        compiler_params=_params("parallel"),
    )(h, g, w4)


def _to_token_tiles(ref, tok0, rows, val, ng):
    for j in range(ng):
        ref[pl.ds(tok0 * ng + j, rows, stride=ng), :] = val[:, j * LANES:(j + 1) * LANES]


def _from_token_tiles(ref, tok0, rows, ng):
    return jnp.concatenate([ref[pl.ds(tok0 * ng + j, rows, stride=ng), :] for j in range(ng)], axis=1)


def _conv_mid_fwd(p, dw3, dwb3, lg, lb):
    t = p.shape[0]
    e = p.shape[1] // 3
    ng = e // LANES
    tm = _tile(t)
    rc = CONV_ROW_CHUNK
    fb = FIR_BLOCK

    def body(p_ref, dw_ref, dwb_ref, lg_ref, lb_ref, s_ref, c_ref, u3, c3):
        i = pl.program_id(0)

        @pl.when(i == 0)
        def _():
            u3[0:HALO * ng, :] = jnp.zeros((HALO * ng, LANES), F32)

        def glu(rci, carry):
            base = pl.multiple_of(rci * rc, rc)
            a = p_ref[pl.ds(base, rc), 0:e].astype(F32)
            b = p_ref[pl.ds(base, rc), e:2 * e].astype(F32)
            _to_token_tiles(u3, HALO + base, rc, a * _sig(b), ng)
            return carry

        lax.fori_loop(0, tm // rc, glu, 0)

        def fir(bi, carry):
            t0 = bi * fb
            def x(q):
                return u3[pl.ds(pl.multiple_of((t0 + HALO - (CONV_K - 1) + q) * ng, ng), ng), :]

            xs = [x(q) for q in range(fb - 1)]
            accs = [dwb_ref[...]] * fb
            for k in range(CONV_K):
                wk = dw_ref[k * ng:(k + 1) * ng, :]
                xs.append(x(k + fb - 1))
                accs = [accs[q] + wk * xs[q + k] for q in range(fb)]
            for q in range(fb):
                c3[pl.ds(pl.multiple_of((t0 + q) * ng, ng), ng), :] = accs[q]
            return carry

        lax.fori_loop(0, tm // fb, fir, 0)
        u3[0:HALO * ng, :] = u3[tm * ng:(tm + HALO) * ng, :]

        def chunk(rci, carry):
            base = pl.multiple_of(rci * rc, rc)
            c = _from_token_tiles(c3, base, rc, ng)
            mu = jnp.mean(c, axis=-1, keepdims=True)
            cc = c - mu
            var = jnp.mean(cc * cc, axis=-1, keepdims=True)
            ln = cc * lax.rsqrt(var + LN_EPS) * lg_ref[...] + lb_ref[...]
            z = p_ref[pl.ds(base, rc), 2 * e:3 * e].astype(F32)
            s = (ln * _sig(ln)) * (z * _sig(z))
            s_ref[pl.ds(base, rc), :] = s.astype(BF16)
            c_ref[pl.ds(base, rc), :] = c.astype(BF16)
            return carry

        lax.fori_loop(0, tm // rc, chunk, 0, unroll=2)

    vec = pl.BlockSpec((1, e), lambda i: (0, 0))
    return pl.pallas_call(
        body, name="conv_mid_fwd", grid=(t // tm,),
        in_specs=[pl.BlockSpec((tm, 3 * e), lambda i: (i, 0)),
                  pl.BlockSpec((CONV_K * ng, LANES), lambda i: (0, 0)),
                  pl.BlockSpec((ng, LANES), lambda i: (0, 0)), vec, vec],
        out_specs=[pl.BlockSpec((tm, e), lambda i: (i, 0)), pl.BlockSpec((tm, e), lambda i: (i, 0))],
        out_shape=[jax.ShapeDtypeStruct((t, e), BF16), jax.ShapeDtypeStruct((t, e), BF16)],
        scratch_shapes=[pltpu.VMEM(((tm + HALO) * ng, LANES), F32), pltpu.VMEM((tm * ng, LANES), F32)],
        compiler_params=_params("arbitrary"),
    )(p, dw3, dwb3, lg, lb)


def _conv_mid_out_fwd(p, h, w_out, dw3, dwb3, lg, lb):
    t = p.shape[0]
    e = p.shape[1] // 3
    d = h.shape[1]
    ng = e // LANES
    tm = _tile(t)
    nt = t // tm
    rc = CONV_ROW_CHUNK
    fb = FIR_BLOCK

    def body(p_ref, h_ref, w_ref, dw_ref, dwb_ref, lg_ref, lb_ref, s_ref, c_ref, ho_ref, u3, c3, s_prev):
        i = pl.program_id(0)

        def project():
            ho_ref[...] = h_ref[...] + jnp.dot(s_prev[...], w_ref[...], preferred_element_type=F32)

        @pl.when(i == 0)
        def _():
            u3[0:HALO * ng, :] = jnp.zeros((HALO * ng, LANES), F32)
            s_prev[...] = jnp.zeros_like(s_prev)

        @pl.when(i < nt)
        def _():
            def glu(rci, carry):
                base = pl.multiple_of(rci * rc, rc)
                a = p_ref[pl.ds(base, rc), 0:e].astype(F32)
                b = p_ref[pl.ds(base, rc), e:2 * e].astype(F32)
                _to_token_tiles(u3, HALO + base, rc, a * _sig(b), ng)
                return carry

            lax.fori_loop(0, tm // rc, glu, 0)

            project()
            for bi in range(tm // fb):
                t0 = bi * fb
                xs = [u3[(t0 + HALO - (CONV_K - 1) + q) * ng:(t0 + HALO - (CONV_K - 1) + q + 1) * ng, :]
                      for q in range(fb - 1)]
                accs = [dwb_ref[...]] * fb
                for k in range(CONV_K):
                    wk = dw_ref[k * ng:(k + 1) * ng, :]
                    q1 = t0 + HALO - (CONV_K - 1) + k + fb - 1
                    xs.append(u3[q1 * ng:(q1 + 1) * ng, :])
                    accs = [accs[q] + wk * xs[q + k] for q in range(fb)]
                for q in range(fb):
                    c3[(t0 + q) * ng:(t0 + q + 1) * ng, :] = accs[q]
            u3[0:HALO * ng, :] = u3[tm * ng:(tm + HALO) * ng, :]

            def chunk(rci, carry):
                base = pl.multiple_of(rci * rc, rc)
                c = _from_token_tiles(c3, base, rc, ng)
                mu = jnp.mean(c, axis=-1, keepdims=True)
                cc = c - mu
                var = jnp.mean(cc * cc, axis=-1, keepdims=True)
                ln = cc * lax.rsqrt(var + LN_EPS) * lg_ref[...] + lb_ref[...]
                z = p_ref[pl.ds(base, rc), 2 * e:3 * e].astype(F32)
                s = ((ln * _sig(ln)) * (z * _sig(z))).astype(BF16)
                s_ref[pl.ds(base, rc), :] = s
                s_prev[pl.ds(base, rc), :] = s
                c_ref[pl.ds(base, rc), :] = c.astype(BF16)
                return carry

            lax.fori_loop(0, tm // rc, chunk, 0, unroll=2)

        @pl.when(i == nt)
        def _():
            project()

    vec = pl.BlockSpec((1, e), lambda i: (0, 0))
    cur = lambda i: (jnp.minimum(i, nt - 1), 0)
    lag = lambda i: (jnp.maximum(i - 1, 0), 0)
    return pl.pallas_call(
        body, name="conv_mid_out_fwd", grid=(nt + 1,),
        in_specs=[pl.BlockSpec((tm, 3 * e), cur), pl.BlockSpec((tm, d), lag), _resident((e, d)),
                  pl.BlockSpec((CONV_K * ng, LANES), lambda i: (0, 0)),
                  pl.BlockSpec((ng, LANES), lambda i: (0, 0)), vec, vec],
        out_specs=[pl.BlockSpec((tm, e), cur), pl.BlockSpec((tm, e), cur), pl.BlockSpec((tm, d), lag)],
        out_shape=[jax.ShapeDtypeStruct((t, e), BF16), jax.ShapeDtypeStruct((t, e), BF16),
                   jax.ShapeDtypeStruct((t, d), F32)],
        scratch_shapes=[pltpu.VMEM(((tm + HALO) * ng, LANES), F32), pltpu.VMEM((tm * ng, LANES), F32),
                        pltpu.VMEM((tm, e), BF16)],
        compiler_params=_params("arbitrary"),
    )(p, h, w_out, dw3, dwb3, lg, lb)


def _conv_fwd(h, g, w_in4, w_out, dw3, dwb3, lg, lb):
    t, d = h.shape
    nk = w_in4.shape[-1]
    e = N_CHIPS * nk // 3
    ng = e // LANES
    tm = _tile(t)
    nt = t // tm
    rc = CONV_ROW_CHUNK
    fb = FIR_BLOCK

    def body(hl_ref, hg_ref, g_ref, wi_ref, wo_ref, dw_ref, dwb_ref, lg_ref, lb_ref,
             p_ref, hn_ref, s_ref, c_ref, ho_ref, u3, c3, s_prev, p_s, hn_s):
        q = pl.program_id(0)
        nxt = q % 2
        cur = 1 - nxt

        def normed():
            hh = hl_ref[...]
            r = lax.rsqrt(jnp.mean(hh * hh, axis=-1, keepdims=True) + RMS_EPS)
            hn = (hh * r * g_ref[...]).astype(BF16)
            hn_s[nxt] = hn
            return hn

        def in_project(hn, k):
            p_s[nxt, :, k * nk:(k + 1) * nk] = jnp.dot(hn, wi_ref[k], preferred_element_type=F32).astype(BF16)

        def out_project():
            ho_ref[...] = hg_ref[...] + jnp.dot(s_prev[...], wo_ref[...], preferred_element_type=F32)

        @pl.when(q == 0)
        def _():
            u3[0:HALO * ng, :] = jnp.zeros((HALO * ng, LANES), F32)
            s_prev[...] = jnp.zeros_like(s_prev)
            hn0 = normed()
            for k in range(N_CHIPS):
                in_project(hn0, k)

        @pl.when(jnp.logical_and(q >= 1, q <= nt))
        def _():
            def glu(rci, carry):
                base = pl.multiple_of(rci * rc, rc)
                rows = pl.ds(base, rc)
                pa = p_s[cur, rows, 0:e]
                pb = p_s[cur, rows, e:2 * e]
                p_ref[rows, 0:e] = pa
                p_ref[rows, e:2 * e] = pb
                hn_ref[rows, :] = hn_s[cur, rows, :]
                _to_token_tiles(u3, HALO + base, rc, pa.astype(F32) * _sig(pb.astype(F32)), ng)
                return carry

            lax.fori_loop(0, tm // rc, glu, 0)

            nfir = tm // fb
            spots = {(m + 1) * nfir // (N_CHIPS + 2): m for m in range(N_CHIPS + 1)}
            hn = normed()
            for bi in range(nfir):
                if bi in spots:
                    if spots[bi] < N_CHIPS:
                        in_project(hn, spots[bi])
                    else:
                        out_project()
                t0 = bi * fb
                xs = [u3[(t0 + HALO - (CONV_K - 1) + j) * ng:(t0 + HALO - (CONV_K - 1) + j + 1) * ng, :]
                      for j in range(fb - 1)]
                accs = [dwb_ref[...]] * fb
                for k in range(CONV_K):
                    wk = dw_ref[k * ng:(k + 1) * ng, :]
                    j1 = t0 + HALO - (CONV_K - 1) + k + fb - 1
                    xs.append(u3[j1 * ng:(j1 + 1) * ng, :])
                    accs = [accs[j] + wk * xs[j + k] for j in range(fb)]
                for j in range(fb):
                    c3[(t0 + j) * ng:(t0 + j + 1) * ng, :] = accs[j]
            u3[0:HALO * ng, :] = u3[tm * ng:(tm + HALO) * ng, :]

            def chunk(rci, carry):
                base = pl.multiple_of(rci * rc, rc)
                rows = pl.ds(base, rc)
                c = _from_token_tiles(c3, base, rc, ng)
                mu = jnp.mean(c, axis=-1, keepdims=True)
                cc = c - mu
                var = jnp.mean(cc * cc, axis=-1, keepdims=True)
                ln = cc * lax.rsqrt(var + LN_EPS) * lg_ref[...] + lb_ref[...]
                pz = p_s[cur, rows, 2 * e:3 * e]
                p_ref[rows, 2 * e:3 * e] = pz
                z = pz.astype(F32)
                s = ((ln * _sig(ln)) * (z * _sig(z))).astype(BF16)
                s_ref[rows, :] = s
                s_prev[rows, :] = s
                c_ref[rows, :] = c.astype(BF16)
                return carry

            lax.fori_loop(0, tm // rc, chunk, 0, unroll=2)

        @pl.when(q == nt + 1)
        def _():
            out_project()

    vec = pl.BlockSpec((1, e), lambda q: (0, 0))
    lead = lambda q: (jnp.clip(q, 0, nt - 1), 0)
    cur_map = lambda q: (jnp.clip(q - 1, 0, nt - 1), 0)
    lag = lambda q: (jnp.clip(q - 2, 0, nt - 1), 0)
    return pl.pallas_call(
        body, name="conv_fwd", grid=(nt + 2,),
        in_specs=[pl.BlockSpec((tm, d), lead), pl.BlockSpec((tm, d), lag), pl.BlockSpec((1, d), lambda q: (0, 0)),
                  _resident((N_CHIPS, d, nk)), _resident((e, d)),
                  pl.BlockSpec((CONV_K * ng, LANES), lambda q: (0, 0)),
                  pl.BlockSpec((ng, LANES), lambda q: (0, 0)), vec, vec],
        out_specs=[pl.BlockSpec((tm, 3 * e), cur_map), pl.BlockSpec((tm, d), cur_map),
                   pl.BlockSpec((tm, e), cur_map), pl.BlockSpec((tm, e), cur_map), pl.BlockSpec((tm, d), lag)],
        out_shape=[jax.ShapeDtypeStruct((t, 3 * e), BF16), jax.ShapeDtypeStruct((t, d), BF16),
                   jax.ShapeDtypeStruct((t, e), BF16), jax.ShapeDtypeStruct((t, e), BF16),
                   jax.ShapeDtypeStruct((t, d), F32)],
        scratch_shapes=[pltpu.VMEM(((tm + HALO) * ng, LANES), F32), pltpu.VMEM((tm * ng, LANES), F32),
                        pltpu.VMEM((tm, e), BF16), pltpu.VMEM((2, tm, 3 * e), BF16),
                        pltpu.VMEM((2, tm, d), BF16)],
        compiler_params=_params("arbitrary"),
    )(h, h, g, w_in4, w_out, dw3, dwb3, lg, lb)


def _pool_group(lc, e):
    return (lc * LANES) // (e // len(POOL_WINDOWS))


def _pool_inv_count(row0, rows, w):
    tpos = row0 + lax.broadcasted_iota(jnp.int32, (rows, 1), 0)
    return 1.0 / jnp.minimum(tpos + 1, w).astype(F32)


def _pool_window_dev(ubuf, base, rc, e, row0, dbuf):
    n = rc + PHALO
    for lc in range(e // LANES):
        lanes = slice(lc * LANES, (lc + 1) * LANES)
        g = _pool_group(lc, e)
        w = POOL_WINDOWS[g]
        blk = ubuf[pl.ds(base, n), lanes]
        acc = blk
        step = 1
        while step < w:
            acc = acc + pltpu.roll(acc, step, 0)
            step *= 2
        win = acc[PHALO:n]
        tok = blk[PHALO:n]
        dbuf[pl.ds(base, rc), lanes] = win * _pool_inv_count(row0 + base, rc, w) - tok


def _pool_mid_fwd(p, wg, bg, sc):
    t = p.shape[0]
    e = p.shape[1] // 2
    gc = e // len(POOL_WINDOWS)
    tm = _tile(t)
    rc = ROW_CHUNK

    def body(p_ref, wg_ref, bg_ref, sc_ref, s_ref, ubuf, dbuf):
        i = pl.program_id(0)

        @pl.when(i == 0)
        def _():
            ubuf[0:PHALO, :] = jnp.zeros((PHALO, e), F32)

        ubuf[PHALO:PHALO + tm, :] = p_ref[:, 0:e].astype(F32)

        def chunk(rci, carry):
            base = pl.multiple_of(rci * rc, rc)
            _pool_window_dev(ubuf, base, rc, e, i * tm, dbuf)
            return carry

        lax.fori_loop(0, tm // rc, chunk, 0)
        ubuf[0:PHALO, :] = ubuf[tm:tm + PHALO, :]

        for g in range(len(POOL_WINDOWS)):
            cols = slice(g * gc, (g + 1) * gc)
            yg = jnp.dot(dbuf[:, cols].astype(BF16), wg_ref[g], preferred_element_type=F32)
            z = p_ref[:, e + g * gc:e + (g + 1) * gc].astype(F32)
            s = ((yg + bg_ref[:, cols]) * sc_ref[:, cols]) * (z * _sig(z))
            s_ref[:, cols] = s.astype(BF16)

    vec = pl.BlockSpec((1, e), lambda i: (0, 0))
    return pl.pallas_call(
        body, name="pool_mid_fwd", grid=(t // tm,),
        in_specs=[pl.BlockSpec((tm, 2 * e), lambda i: (i, 0)),
                  pl.BlockSpec((len(POOL_WINDOWS), gc, gc), lambda i: (0, 0, 0)), vec, vec],
        out_specs=pl.BlockSpec((tm, e), lambda i: (i, 0)),
        out_shape=jax.ShapeDtypeStruct((t, e), BF16),
        scratch_shapes=[pltpu.VMEM((tm + PHALO, e), F32), pltpu.VMEM((tm, e), F32)],
        compiler_params=_params("arbitrary"),
    )(p, wg, bg, sc)


def _matmul_res(h, s, w):
    t, d = h.shape
    e = s.shape[1]
    tm = min(MATMUL_TILE, t)

    def body(h_ref, s_ref, w_ref, o_ref):
        o_ref[...] = h_ref[...] + jnp.dot(s_ref[...], w_ref[...], preferred_element_type=F32)

    return pl.pallas_call(
        body, name="matmul_res", grid=(t // tm,),
        in_specs=[pl.BlockSpec((tm, d), lambda i: (i, 0)), pl.BlockSpec((tm, e), lambda i: (i, 0)),
                  _resident((e, d))],
        out_specs=pl.BlockSpec((tm, d), lambda i: (i, 0)),
        out_shape=jax.ShapeDtypeStruct((t, d), F32),
        compiler_params=_params("parallel"),
    )(h, s, w)


def _loss_head(h, fg, tgt):
    t, d = h.shape
    tm = min(MATMUL_TILE, t)

    def body(h_ref, g_ref, t_ref, dh_ref, loss_ref, dg_ref):
        i = pl.program_id(0)

        @pl.when(i == 0)
        def _():
            loss_ref[...] = jnp.zeros_like(loss_ref)
            dg_ref[...] = jnp.zeros_like(dg_ref)

        hh = h_ref[...]
        r = lax.rsqrt(jnp.mean(hh * hh, axis=-1, keepdims=True) + RMS_EPS)
        hhat = hh * r
        err = hhat * g_ref[...] - t_ref[...]
        per_tok = jnp.mean(err * err, axis=-1, keepdims=True)
        loss_ref[...] += 0.5 * jnp.sum(per_tok, axis=0, keepdims=True)
        dy = err * (1.0 / d)
        tt = dy * g_ref[...]
        dh_ref[...] = r * (tt - hhat * jnp.mean(tt * hhat, axis=-1, keepdims=True))
        dg_ref[...] += jnp.sum(dy * hhat, axis=0, keepdims=True)

    return pl.pallas_call(
        body, name="loss_head", grid=(t // tm,),
        in_specs=[pl.BlockSpec((tm, d), lambda i: (i, 0)), pl.BlockSpec((1, d), lambda i: (0, 0)),
                  pl.BlockSpec((tm, d), lambda i: (i, 0))],
        out_specs=[pl.BlockSpec((tm, d), lambda i: (i, 0)), pl.BlockSpec((1, LANES), lambda i: (0, 0)),
                   pl.BlockSpec((1, d), lambda i: (0, 0))],
        out_shape=[jax.ShapeDtypeStruct((t, d), F32), jax.ShapeDtypeStruct((1, LANES), F32),
                   jax.ShapeDtypeStruct((1, d), F32)],
        compiler_params=_params("arbitrary"),
    )(h, fg, tgt)


def _ds_matmul(dy, w):
    t, d = dy.shape
    e = w.shape[0]
    tm = min(MATMUL_TILE, t)

    def body(dy_ref, w_ref, ds_ref):
        ds_ref[...] = lax.dot_general(dy_ref[...].astype(BF16), w_ref[...], (((1,), (1,)), ((), ())),
                                      preferred_element_type=F32).astype(BF16)

    return pl.pallas_call(
        body, name="ds_matmul", grid=(t // tm,),
        in_specs=[pl.BlockSpec((tm, d), lambda i: (i, 0)), _resident((e, d))],
        out_specs=pl.BlockSpec((tm, e), lambda i: (i, 0)),
        out_shape=jax.ShapeDtypeStruct((t, e), BF16),
        compiler_params=_params("parallel"),
    )(dy, w)


def _conv_mid_bwd(p, c, ds, dw3, lg, lb):
    t = p.shape[0]
    e = p.shape[1] // 3
    ng = e // LANES
    tm = _tile(t)
    nt = t // tm
    rc = CONV_ROW_CHUNK
    fb = FIR_BLOCK
    hb = tm // HALO

    def body(p_ref, ph_ref, c_ref, ds_ref, dw_ref, lg_ref, lb_ref,
             dp_ref, dlg_ref, dlb_ref, ddwb_ref, ddw_ref, u3, dc3, du3):
        i = pl.program_id(0)
        ti = nt - 1 - i

        @pl.when(i == 0)
        def _():
            dc3[tm * ng:(tm + HALO) * ng, :] = jnp.zeros((HALO * ng, LANES), F32)
            dlg_ref[...] = jnp.zeros_like(dlg_ref)
            dlb_ref[...] = jnp.zeros_like(dlb_ref)
            ddwb_ref[...] = jnp.zeros_like(ddwb_ref)
            ddw_ref[...] = jnp.zeros_like(ddw_ref)

        ha = ph_ref[:, 0:e].astype(F32)
        hbb = ph_ref[:, e:2 * e].astype(F32)
        _to_token_tiles(u3, 0, HALO, jnp.where(ti > 0, ha * _sig(hbb), 0.0), ng)

        def front(rci, carry):
            slg, slb, sdwb = carry
            base = pl.multiple_of(rci * rc, rc)
            rows = pl.ds(base, rc)
            a = p_ref[rows, 0:e].astype(F32)
            b = p_ref[rows, e:2 * e].astype(F32)
            _to_token_tiles(u3, HALO + base, rc, a * _sig(b), ng)
            cv = c_ref[rows, :].astype(F32)
            mu = jnp.mean(cv, axis=-1, keepdims=True)
            cc = cv - mu
            var = jnp.mean(cc * cc, axis=-1, keepdims=True)
            rs = lax.rsqrt(var + LN_EPS)
            nn = cc * rs
            ln = nn * lg_ref[...] + lb_ref[...]
            z = p_ref[rows, 2 * e:3 * e].astype(F32)
            sz = _sig(z)
            sl = _sig(ln)
            dsv = ds_ref[rows, :].astype(F32)
            dln = dsv * (z * sz) * _dsilu(ln, sl)
            dz = dsv * (ln * sl) * _dsilu(z, sz)
            dp_ref[rows, 2 * e:3 * e] = dz.astype(BF16)
            dn = dln * lg_ref[...]
            dc = rs * (dn - jnp.mean(dn, axis=-1, keepdims=True)
                       - nn * jnp.mean(dn * nn, axis=-1, keepdims=True))
            _to_token_tiles(dc3, base, rc, dc, ng)
            return (slg + jnp.sum(dln * nn, axis=0, keepdims=True),
                    slb + jnp.sum(dln, axis=0, keepdims=True),
                    sdwb + jnp.sum(dc, axis=0, keepdims=True))

        zero = jnp.zeros((1, e), F32)
        slg, slb, sdwb = lax.fori_loop(0, tm // rc, front, (zero, zero, zero), unroll=2)
        dlg_ref[...] += slg
        dlb_ref[...] += slb
        ddwb_ref[...] += sdwb

        def fir(bi, carry):
            t0 = bi * fb

            def dcs(q):
                return dc3[pl.ds(pl.multiple_of((t0 + q) * ng, ng), ng), :]

            def us(q):
                return u3[pl.ds(pl.multiple_of((t0 + HALO - (CONV_K - 1) + q) * ng, ng), ng), :]

            xs = [dcs(q) for q in range(fb - 1)]
            accs = [None] * fb
            for j in range(CONV_K):
                wk = dw_ref[(CONV_K - 1 - j) * ng:(CONV_K - j) * ng, :]
                xs.append(dcs(j + fb - 1))
                accs = [wk * xs[q + j] if accs[q] is None else accs[q] + wk * xs[q + j] for q in range(fb)]
            for q in range(fb):
                du3[pl.ds(pl.multiple_of((t0 + q) * ng, ng), ng), :] = accs[q]
            own = xs[0:fb]
            ys = [us(q) for q in range(fb - 1)]
            for k in range(CONV_K):
                ys.append(us(k + fb - 1))
                prods = [own[q] * ys[q + k] for q in range(fb)]
                while len(prods) > 1:
                    prods = [prods[2 * v] + prods[2 * v + 1] for v in range(len(prods) // 2)]
                ddw_ref[k * ng:(k + 1) * ng, :] += prods[0]
            return carry

        lax.fori_loop(0, tm // fb, fir, 0)
        dc3[tm * ng:(tm + HALO) * ng, :] = dc3[0:HALO * ng, :]

        def back(rci, carry):
            base = pl.multiple_of(rci * rc, rc)
            rows = pl.ds(base, rc)
            a = p_ref[rows, 0:e].astype(F32)
            b = p_ref[rows, e:2 * e].astype(F32)
            sb = _sig(b)
            duv = _from_token_tiles(du3, base, rc, ng)
            dp_ref[rows, 0:e] = (duv * sb).astype(BF16)
            dp_ref[rows, e:2 * e] = (duv * a * sb * (1.0 - sb)).astype(BF16)
            return carry

        lax.fori_loop(0, tm // rc, back, 0, unroll=2)

    vec = pl.BlockSpec((1, e), lambda i: (0, 0))
    taps = pl.BlockSpec((CONV_K * ng, LANES), lambda i: (0, 0))
    rev = lambda i: (nt - 1 - i, 0)
    halo = lambda i: (jnp.maximum((nt - 1 - i) * hb - 1, 0), 0)
    return pl.pallas_call(
        body, name="conv_mid_bwd", grid=(nt,),
        in_specs=[pl.BlockSpec((tm, 3 * e), rev), pl.BlockSpec((HALO, 3 * e), halo),
                  pl.BlockSpec((tm, e), rev), pl.BlockSpec((tm, e), rev), taps, vec, vec],
        out_specs=[pl.BlockSpec((tm, 3 * e), rev), vec, vec, vec, taps],
        out_shape=[jax.ShapeDtypeStruct((t, 3 * e), BF16), jax.ShapeDtypeStruct((1, e), F32),
                   jax.ShapeDtypeStruct((1, e), F32), jax.ShapeDtypeStruct((1, e), F32),
                   jax.ShapeDtypeStruct((CONV_K * ng, LANES), F32)],
        scratch_shapes=[pltpu.VMEM(((tm + HALO) * ng, LANES), F32), pltpu.VMEM(((tm + HALO) * ng, LANES), F32),
                        pltpu.VMEM((tm * ng, LANES), F32)],
        compiler_params=_params("arbitrary"),
    )(p, p, c, ds, dw3, lg, lb)


def _conv_bwd(p, c, dy, hn, w_out, dw3, lg, lb):
    t = p.shape[0]
    e = p.shape[1] // 3
    d = hn.shape[1]
    nk = 3 * e // N_CHIPS
    ng = e // LANES
    tm = _tile(t)
    nt = t // tm
    rc = CONV_ROW_CHUNK
    fb = FIR_BLOCK
    hb = tm // HALO

    def body(p_ref, ph_ref, c_ref, dyn_ref, hnp_ref, w_ref, dw_ref, lg_ref, lb_ref,
             dp_ref, dlg_ref, dlb_ref, ddwb_ref, ddw_ref, dwi_ref, u3, dc3, du3, ds_s, dp_s):
        g = pl.program_id(0)
        ti = nt - g

        def next_ds():
            ds_s[...] = lax.dot_general(dyn_ref[...].astype(BF16), w_ref[...], (((1,), (1,)), ((), ())),
                                        preferred_element_type=F32).astype(BF16)

        def prev_wgrad(slot):
            for k in range(N_CHIPS):
                dwi_ref[k] += lax.dot_general(hnp_ref[...], dp_s[slot, :, k * nk:(k + 1) * nk],
                                              (((0,), (0,)), ((), ())), preferred_element_type=F32)

        @pl.when(g == 0)
        def _():
            dc3[tm * ng:(tm + HALO) * ng, :] = jnp.zeros((HALO * ng, LANES), F32)
            dlg_ref[...] = jnp.zeros_like(dlg_ref)
            dlb_ref[...] = jnp.zeros_like(dlb_ref)
            ddwb_ref[...] = jnp.zeros_like(ddwb_ref)
            ddw_ref[...] = jnp.zeros_like(ddw_ref)
            dwi_ref[...] = jnp.zeros_like(dwi_ref)
            dp_s[0] = jnp.zeros_like(dp_s[0])
            next_ds()

        @pl.when(jnp.logical_and(g >= 1, g <= nt))
        def _():
            ha = ph_ref[:, 0:e].astype(F32)
            hbb = ph_ref[:, e:2 * e].astype(F32)
            _to_token_tiles(u3, 0, HALO, jnp.where(ti > 0, ha * _sig(hbb), 0.0), ng)
            cur = g % 2

            def front(rci, carry):
                slg, slb, sdwb = carry
                base = pl.multiple_of(rci * rc, rc)
                rows = pl.ds(base, rc)
                a = p_ref[rows, 0:e].astype(F32)
                b = p_ref[rows, e:2 * e].astype(F32)
                _to_token_tiles(u3, HALO + base, rc, a * _sig(b), ng)
                cv = c_ref[rows, :].astype(F32)
                mu = jnp.mean(cv, axis=-1, keepdims=True)
                cc = cv - mu
                var = jnp.mean(cc * cc, axis=-1, keepdims=True)
                rs = lax.rsqrt(var + LN_EPS)
                nn = cc * rs
                ln = nn * lg_ref[...] + lb_ref[...]
                z = p_ref[rows, 2 * e:3 * e].astype(F32)
                sz = _sig(z)
                sl = _sig(ln)
                dsv = ds_s[rows, :].astype(F32)
                dln = dsv * (z * sz) * _dsilu(ln, sl)
                dzb = (dsv * (ln * sl) * _dsilu(z, sz)).astype(BF16)
                dp_ref[rows, 2 * e:3 * e] = dzb
                dp_s[cur, rows, 2 * e:3 * e] = dzb
                dn = dln * lg_ref[...]
                dc = rs * (dn - jnp.mean(dn, axis=-1, keepdims=True)
                           - nn * jnp.mean(dn * nn, axis=-1, keepdims=True))
                _to_token_tiles(dc3, base, rc, dc, ng)
                return (slg + jnp.sum(dln * nn, axis=0, keepdims=True),
                        slb + jnp.sum(dln, axis=0, keepdims=True),
                        sdwb + jnp.sum(dc, axis=0, keepdims=True))

            zero = jnp.zeros((1, e), F32)
            slg, slb, sdwb = lax.fori_loop(0, tm // rc, front, (zero, zero, zero), unroll=2)
            dlg_ref[...] += slg
            dlb_ref[...] += slb
            ddwb_ref[...] += sdwb

            next_ds()
            prev_wgrad(1 - cur)
            for bi in range(tm // fb):
                t0 = bi * fb

                def dcs(q):
                    return dc3[(t0 + q) * ng:(t0 + q + 1) * ng, :]

                def us(q):
                    r0 = t0 + HALO - (CONV_K - 1) + q
                    return u3[r0 * ng:(r0 + 1) * ng, :]

                xs = [dcs(q) for q in range(fb - 1)]
                accs = [None] * fb
                for j in range(CONV_K):
                    wk = dw_ref[(CONV_K - 1 - j) * ng:(CONV_K - j) * ng, :]
                    xs.append(dcs(j + fb - 1))
                    accs = [wk * xs[q + j] if accs[q] is None else accs[q] + wk * xs[q + j] for q in range(fb)]
                for q in range(fb):
                    du3[(t0 + q) * ng:(t0 + q + 1) * ng, :] = accs[q]
                own = xs[0:fb]
                ys = [us(q) for q in range(fb - 1)]
                for k in range(CONV_K):
                    ys.append(us(k + fb - 1))
                    prods = [own[q] * ys[q + k] for q in range(fb)]
                    while len(prods) > 1:
                        prods = [prods[2 * v] + prods[2 * v + 1] for v in range(len(prods) // 2)]
                    ddw_ref[k * ng:(k + 1) * ng, :] += prods[0]
            dc3[tm * ng:(tm + HALO) * ng, :] = dc3[0:HALO * ng, :]

            def back(rci, carry):
                base = pl.multiple_of(rci * rc, rc)
                rows = pl.ds(base, rc)
                a = p_ref[rows, 0:e].astype(F32)
                b = p_ref[rows, e:2 * e].astype(F32)
                sb = _sig(b)
                duv = _from_token_tiles(du3, base, rc, ng)
                dab = (duv * sb).astype(BF16)
                dbb = (duv * a * sb * (1.0 - sb)).astype(BF16)
                dp_ref[rows, 0:e] = dab
                dp_ref[rows, e:2 * e] = dbb
                dp_s[cur, rows, 0:e] = dab
                dp_s[cur, rows, e:2 * e] = dbb
                return carry

            lax.fori_loop(0, tm // rc, back, 0, unroll=2)

        @pl.when(g == nt + 1)
        def _():
            prev_wgrad(nt % 2)

    vec = pl.BlockSpec((1, e), lambda g: (0, 0))
    taps = pl.BlockSpec((CONV_K * ng, LANES), lambda g: (0, 0))
    tile_of = lambda g: jnp.clip(nt - g, 0, nt - 1)
    cur_map = lambda g: (tile_of(g), 0)
    halo = lambda g: (jnp.maximum(tile_of(g) * hb - 1, 0), 0)
    nxt_map = lambda g: (jnp.clip(nt - 1 - g, 0, nt - 1), 0)
    prv_map = lambda g: (jnp.clip(nt + 1 - g, 0, nt - 1), 0)
    return pl.pallas_call(
        body, name="conv_bwd", grid=(nt + 2,),
        in_specs=[pl.BlockSpec((tm, 3 * e), cur_map), pl.BlockSpec((HALO, 3 * e), halo),
                  pl.BlockSpec((tm, e), cur_map), pl.BlockSpec((tm, d), nxt_map),
                  pl.BlockSpec((tm, d), prv_map), _resident((e, d)), taps, vec, vec],
        out_specs=[pl.BlockSpec((tm, 3 * e), cur_map), vec, vec, vec, taps,
                   pl.BlockSpec((N_CHIPS, d, nk), lambda g: (0, 0, 0))],
        out_shape=[jax.ShapeDtypeStruct((t, 3 * e), BF16), jax.ShapeDtypeStruct((1, e), F32),
                   jax.ShapeDtypeStruct((1, e), F32), jax.ShapeDtypeStruct((1, e), F32),
                   jax.ShapeDtypeStruct((CONV_K * ng, LANES), F32),
                   jax.ShapeDtypeStruct((N_CHIPS, d, nk), F32)],
        scratch_shapes=[pltpu.VMEM(((tm + HALO) * ng, LANES), F32), pltpu.VMEM(((tm + HALO) * ng, LANES), F32),
                        pltpu.VMEM((tm * ng, LANES), F32), pltpu.VMEM((tm, e), BF16),
                        pltpu.VMEM((2, tm, 3 * e), BF16)],
        compiler_params=_params("arbitrary"),
    )(p, p, c, dy, hn, w_out, dw3, lg, lb)


def _pool_mid_bwd(p, ds, wg, bg, sc):
    t = p.shape[0]
    e = p.shape[1] // 2
    ng = len(POOL_WINDOWS)
    gc = e // ng
    tm = _tile(t)
    nt = t // tm
    rc = ROW_CHUNK
    hb = tm // PHALO

    def body(p_ref, ph_ref, ds_ref, wg_ref, bg_ref, sc_ref,
             dp_ref, dwg_ref, dbg_ref, dsc_ref, ubuf, dbuf, ebuf, ddbuf):
        i = pl.program_id(0)
        ti = nt - 1 - i

        @pl.when(i == 0)
        def _():
            ebuf[tm:tm + PHALO, :] = jnp.zeros((PHALO, e), F32)
            dwg_ref[...] = jnp.zeros_like(dwg_ref)
            dbg_ref[...] = jnp.zeros_like(dbg_ref)
            dsc_ref[...] = jnp.zeros_like(dsc_ref)

        ubuf[0:PHALO, :] = jnp.where(ti > 0, ph_ref[:, 0:e].astype(F32), 0.0)
        ubuf[PHALO:PHALO + tm, :] = p_ref[:, 0:e].astype(F32)

        def recompute(rci, carry):
            base = pl.multiple_of(rci * rc, rc)
            _pool_window_dev(ubuf, base, rc, e, ti * tm, dbuf)
            return carry

        lax.fori_loop(0, tm // rc, recompute, 0)

        for g in range(ng):
            cols = slice(g * gc, (g + 1) * gc)
            dg = dbuf[:, cols].astype(BF16)
            q = jnp.dot(dg, wg_ref[g], preferred_element_type=F32) + bg_ref[:, cols]
            z = p_ref[:, e + g * gc:e + (g + 1) * gc].astype(F32)
            sz = _sig(z)
            dsv = ds_ref[:, cols].astype(F32)
            dz = dsv * (q * sc_ref[:, cols]) * _dsilu(z, sz)
            dp_ref[:, e + g * gc:e + (g + 1) * gc] = dz.astype(BF16)
            dy2 = dsv * (z * sz)
            dsc_ref[:, cols] += jnp.sum(dy2 * q, axis=0, keepdims=True)
            dq = dy2 * sc_ref[:, cols]
            dbg_ref[:, cols] += jnp.sum(dq, axis=0, keepdims=True)
            dqb = dq.astype(BF16)
            dwg_ref[g] += lax.dot_general(dg, dqb, (((0,), (0,)), ((), ())), preferred_element_type=F32)
            ddbuf[:, cols] = lax.dot_general(dqb, wg_ref[g], (((1,), (1,)), ((), ())),
                                             preferred_element_type=F32)

        def scale(rci, carry):
            base = pl.multiple_of(rci * rc, rc)
            for lc in range(e // LANES):
                lanes = slice(lc * LANES, (lc + 1) * LANES)
                w = POOL_WINDOWS[_pool_group(lc, e)]
                ebuf[pl.ds(base, rc), lanes] = (ddbuf[pl.ds(base, rc), lanes]
                                                * _pool_inv_count(ti * tm + base, rc, w))
            return carry

        lax.fori_loop(0, tm // rc, scale, 0)

        def chunk(rci, carry):
            base = pl.multiple_of(rci * rc, rc)
            n = rc + PHALO
            for lc in range(e // LANES):
                lanes = slice(lc * LANES, (lc + 1) * LANES)
                w = POOL_WINDOWS[_pool_group(lc, e)]
                acc = ebuf[pl.ds(base, n), lanes]
                step = 1
                while step < w:
                    acc = acc + pltpu.roll(acc, n - step, 0)
                    step *= 2
                du = acc[0:rc] - ddbuf[pl.ds(base, rc), lanes]
                dp_ref[pl.ds(base, rc), lanes] = du.astype(BF16)
            return carry

        lax.fori_loop(0, tm // rc, chunk, 0)
        ebuf[tm:tm + PHALO, :] = ebuf[0:PHALO, :]

    vec = pl.BlockSpec((1, e), lambda i: (0, 0))
    rev = lambda i: (nt - 1 - i, 0)
    halo = lambda i: (jnp.maximum((nt - 1 - i) * hb - 1, 0), 0)
    wspec = pl.BlockSpec((ng, gc, gc), lambda i: (0, 0, 0))
    return pl.pallas_call(
        body, name="pool_mid_bwd", grid=(nt,),
        in_specs=[pl.BlockSpec((tm, 2 * e), rev), pl.BlockSpec((PHALO, 2 * e), halo),
                  pl.BlockSpec((tm, e), rev), wspec, vec, vec],
        out_specs=[pl.BlockSpec((tm, 2 * e), rev), wspec, vec, vec],
        out_shape=[jax.ShapeDtypeStruct((t, 2 * e), BF16), jax.ShapeDtypeStruct((ng, gc, gc), F32),
                   jax.ShapeDtypeStruct((1, e), F32), jax.ShapeDtypeStruct((1, e), F32)],
        scratch_shapes=[pltpu.VMEM((tm + PHALO, e), F32), pltpu.VMEM((tm, e), F32),
                        pltpu.VMEM((tm + PHALO, e), F32), pltpu.VMEM((tm, e), F32)],
        compiler_params=_params("arbitrary"),
    )(p, p, ds, wg, bg, sc)


def _dhn_rms_bwd(dp, w4, h, g, dh_out, name):
    t, d = h.shape
    nk = w4.shape[-1]
    tm = min(MATMUL_TILE, t)

    def body(dp_ref, w_ref, h_ref, g_ref, dho_ref, dh_ref, dg_ref):
        i = pl.program_id(0)

        @pl.when(i == 0)
        def _():
            dg_ref[...] = jnp.zeros_like(dg_ref)

        dhn = jnp.zeros((tm, d), F32)
        for k in range(N_CHIPS):
            dhn = dhn + lax.dot_general(dp_ref[:, k * nk:(k + 1) * nk], w_ref[k], (((1,), (1,)), ((), ())),
                                        preferred_element_type=F32)
        hh = h_ref[...]
        r = lax.rsqrt(jnp.mean(hh * hh, axis=-1, keepdims=True) + RMS_EPS)
        hhat = hh * r
        tt = dhn * g_ref[...]
        dh_ref[...] = dho_ref[...] + r * (tt - hhat * jnp.mean(tt * hhat, axis=-1, keepdims=True))
        dg_ref[...] += jnp.sum(dhn * hhat, axis=0, keepdims=True)

    return pl.pallas_call(
        body, name=name, grid=(t // tm,),
        in_specs=[pl.BlockSpec((tm, N_CHIPS * nk), lambda i: (i, 0)),
                  _resident((N_CHIPS, d, nk)),
                  pl.BlockSpec((tm, d), lambda i: (i, 0)), pl.BlockSpec((1, d), lambda i: (0, 0)),
                  pl.BlockSpec((tm, d), lambda i: (i, 0))],
        out_specs=[pl.BlockSpec((tm, d), lambda i: (i, 0)), pl.BlockSpec((1, d), lambda i: (0, 0))],
        out_shape=[jax.ShapeDtypeStruct((t, d), F32), jax.ShapeDtypeStruct((1, d), F32)],
        compiler_params=_params("arbitrary"),
    )(dp, w4, h, g, dh_out)


def _wgrad(a, b, nblk, name):
    t, m = a.shape
    nn = b.shape[1] // nblk
    tk = min(WGRAD_TILE, t)
    nk = t // tk

    def body(a_ref, b_ref, o_ref, ob_ref):
        @pl.when(pl.program_id(1) == 0)
        def _():
            o_ref[...] = jnp.zeros_like(o_ref)

        o_ref[...] += lax.dot_general(a_ref[...].astype(BF16), b_ref[...].astype(BF16),
                                      (((0,), (0,)), ((), ())), preferred_element_type=F32)

        @pl.when(pl.program_id(1) == nk - 1)
        def _():
            ob_ref[...] = o_ref[...].astype(BF16)

    ospec = pl.BlockSpec((None, m, nn), lambda j, i: (j, 0, 0))
    return pl.pallas_call(
        body, name=name, grid=(nblk, nk),
        in_specs=[pl.BlockSpec((tk, m), lambda j, i: (i, 0)), pl.BlockSpec((tk, nn), lambda j, i: (i, j))],
        out_specs=[ospec, ospec],
        out_shape=[jax.ShapeDtypeStruct((nblk, m, nn), F32), jax.ShapeDtypeStruct((nblk, m, nn), BF16)],
        compiler_params=_params("parallel", "arbitrary"),
    )(a, b)


def _rows2d(shape):
    rows = 1
    for s in shape[:-1]:
        rows *= s
    return rows, shape[-1]


def _row_tile(rows):
    for cand in (512, 256, 128, 64, 32, 16, 8):
        if rows % cand == 0:
            return cand
    return rows


def _add_eight(own, landed, chip_core, name):
    _, _, rows, cols = own.shape
    tr = _row_tile(rows)

    def body(sel_ref, s_ref, r_ref, o_ref):
        acc = s_ref[...]
        for peer in range(N_DEV - 1):
            acc = acc + r_ref[peer].astype(F32)
        o_ref[...] = acc

    return pl.pallas_call(
        body, name=name,
        grid_spec=pltpu.PrefetchScalarGridSpec(
            num_scalar_prefetch=1, grid=(rows // tr,),
            in_specs=[pl.BlockSpec((None, None, tr, cols), lambda i, s: (s[0], s[1], i, 0)),
                      pl.BlockSpec((N_DEV - 1, tr, cols), lambda i, s: (0, i, 0))],
            out_specs=pl.BlockSpec((None, tr, cols), lambda i, s: (s[1], i, 0))),
        out_shape=jax.ShapeDtypeStruct((2, rows, cols), F32),
        compiler_params=_params("parallel"),
    )(chip_core, own, landed)


def _adamw(w, g, m, v, name):
    rows, cols = w.shape
    tr = _row_tile(rows)

    def body(w_ref, g_ref, m_ref, v_ref, d_ref, m2_ref, v2_ref):
        gg = g_ref[...]
        m2 = ADAM_B1 * m_ref[...] + (1.0 - ADAM_B1) * gg
        v2 = ADAM_B2 * v_ref[...] + (1.0 - ADAM_B2) * (gg * gg)
        m_hat = m2 / (1.0 - ADAM_B1 ** ADAM_STEP)
        v_hat = v2 / (1.0 - ADAM_B2 ** ADAM_STEP)
        d_ref[...] = -ADAM_LR * (m_hat / (jnp.sqrt(v_hat) + ADAM_EPS) + ADAM_WD * w_ref[...])
        m2_ref[...] = m2
        v2_ref[...] = v2

    spec = pl.BlockSpec((tr, cols), lambda i: (i, 0))
    shp = jax.ShapeDtypeStruct((rows, cols), F32)
    return pl.pallas_call(
        body, name=name, grid=(rows // tr,),
        in_specs=[spec, spec, spec, spec], out_specs=[spec, spec, spec], out_shape=[shp, shp, shp],
        compiler_params=_params("parallel"),
    )(w, g, m, v)


ANY = pl.BlockSpec(memory_space=pl.ANY)


def _place():
    x, y, c = lax.axis_index("x"), lax.axis_index("y"), lax.axis_index("c")
    chips = [(1 - x, y), (x, 1 - y), (1 - x, 1 - y)]
    return x, y, c, chips


def _allgather_weights(shards):
    n = len(shards)

    def body(*refs):
        ins, outs = refs[:n], refs[n:2 * n]
        send_ici, recv_ici, send_d2d, recv_d2d = refs[2 * n:]
        x, y, c, chips = _place()
        k0 = 2 * x + y
        sib = (x, y, 1 - c)

        def ici(a, r, src_chip, target):
            return pltpu.make_async_remote_copy(
                src_ref=ins[a].at[c], dst_ref=outs[a].at[src_chip, c],
                send_sem=send_ici.at[a * 3 + r], recv_sem=recv_ici.at[a * 3 + r],
                device_id=target, device_id_type=MESH)

        def d2d(a, r, src_chip, layer):
            return pltpu.make_async_remote_copy(
                src_ref=outs[a].at[src_chip, layer], dst_ref=outs[a].at[src_chip, layer],
                send_sem=send_d2d.at[a * 3 + r], recv_sem=recv_d2d.at[a * 3 + r],
                device_id=sib, device_id_type=MESH)

        first = [ici(a, r, k0, (cx, cy, c)) for a in range(n) for r, (cx, cy) in enumerate(chips)]
        for cp in first:
            cp.start()
        passed = []
        for a in range(n):
            for r, (cx, cy) in enumerate(chips):
                ici(a, r, 2 * cx + cy, (cx, cy, c)).wait_recv()
                cp = d2d(a, r, 2 * cx + cy, c)
                cp.start()
                passed.append(cp)
        for a in range(n):
            for r, (cx, cy) in enumerate(chips):
                d2d(a, r, 2 * cx + cy, 1 - c).wait_recv()
        for cp in first + passed:
            cp.wait_send()

    return pl.pallas_call(
        body, name="allgather_weights",
        in_specs=[ANY] * n, out_specs=[ANY] * n,
        out_shape=[jax.ShapeDtypeStruct((N_CHIPS,) + s.shape, s.dtype) for s in shards],
        scratch_shapes=[pltpu.SemaphoreType.DMA((3 * n,)), pltpu.SemaphoreType.DMA((3 * n,)),
                        pltpu.SemaphoreType.DMA((3 * n,)), pltpu.SemaphoreType.DMA((3 * n,))],
    )(*shards)


def _put_own(gathered, shard, chip):
    return lax.dynamic_update_slice_in_dim(gathered, shard[None], chip, axis=0)


HBM = pl.BlockSpec(memory_space=pltpu.HBM)
SEM = pl.BlockSpec(memory_space=pltpu.SEMAPHORE)
DATAFLOW = pltpu.SideEffectType.DATAFLOW_SIDE_EFFECTING
FLIPS = [(0, 0, 1), (0, 1, 0), (0, 1, 1), (1, 0, 0), (1, 0, 1), (1, 1, 0), (1, 1, 1)]


def _gather_plan(srcs, lands, send_sem, recv_sem):
    x, y, c, chips = _place()
    return [pltpu.make_async_remote_copy(
        src_ref=srcs[a], dst_ref=lands[a].at[2 * x + y],
        send_sem=send_sem.at[a * 3 + r], recv_sem=recv_sem.at[a * 3 + r],
        device_id=(cx, cy, c), device_id_type=MESH)
        for a in range(len(srcs)) for r, (cx, cy) in enumerate(chips)]


def _scatter_plan(srcs, lands, send_sem, recv_sem):
    x, y, c, _ = _place()
    cps = []
    for a in range(len(srcs)):
        for r, (fx, fy, fc) in enumerate(FLIPS):
            tx, ty, tc = (1 - x if fx else x), (1 - y if fy else y), (1 - c if fc else c)
            cps.append(pltpu.make_async_remote_copy(
                src_ref=srcs[a].at[2 * tx + ty, tc], dst_ref=lands[a].at[r],
                send_sem=send_sem.at[a * len(FLIPS) + r], recv_sem=recv_sem.at[a * len(FLIPS) + r],
                device_id=(tx, ty, tc), device_id_type=MESH))
    return cps


def _split_start(name, plan, srcs, lands, n_copies, after):
    n = len(srcs)

    def body(*refs):
        src, land = refs[:n], refs[n:2 * n]
        send_sem, recv_sem = refs[2 * n + 1], refs[2 * n + 2]
        token = refs[-1]
        for cp in plan(src, land, send_sem, recv_sem):
            cp.start()
        token[...] = jnp.zeros_like(token)

    outs = pl.pallas_call(
        body, name=name,
        in_specs=[HBM] * (2 * n) + [ANY],
        out_specs=[SEM, SEM] + [HBM] * (2 * n) + [pl.BlockSpec(memory_space=pltpu.VMEM)],
        out_shape=[pltpu.SemaphoreType.DMA((n_copies,)), pltpu.SemaphoreType.DMA((n_copies,))]
        + [pltpu.HBM(s.shape, s.dtype) for s in srcs] + [pltpu.HBM(l.shape, l.dtype) for l in lands]
        + [jax.ShapeDtypeStruct((8, LANES), F32)],
        input_output_aliases={i: 2 + i for i in range(2 * n)},
        compiler_params=pltpu.CompilerParams(has_side_effects=DATAFLOW),
    )(*[pltpu.with_memory_space_constraint(s, pltpu.HBM) for s in srcs],
      *[pltpu.with_memory_space_constraint(l, pltpu.HBM) for l in lands], after)
    return outs[0], outs[1], list(outs[2:2 + n]), list(outs[2 + n:2 + 2 * n]), outs[-1]


def _split_wait(name, plan, send_sems, recv_sems, srcs, lands, after):
    n = len(srcs)

    def body(*refs):
        src, land = refs[:n], refs[n:2 * n]
        send_sem, recv_sem = refs[2 * n], refs[2 * n + 1]
        for cp in plan(src, land, send_sem, recv_sem):
            cp.wait_send()
            cp.wait_recv()

    outs = pl.pallas_call(
        body, name=name,
        in_specs=[HBM] * (2 * n) + [SEM, SEM, ANY],
        out_specs=[HBM] * (2 * n),
        out_shape=[pltpu.HBM(s.shape, s.dtype) for s in srcs] + [pltpu.HBM(l.shape, l.dtype) for l in lands],
        input_output_aliases={i: i for i in range(2 * n)},
        compiler_params=pltpu.CompilerParams(has_side_effects=DATAFLOW),
    )(*srcs, *lands, send_sems, recv_sems, after)
    return list(outs[:n]), list(outs[n:])


def _share_halves(halves):
    n = len(halves)

    def body(*refs):
        ins, outs = refs[:n], refs[n:2 * n]
        send_sem, recv_sem = refs[2 * n:]
        x, y, c, _ = _place()
        cps = [pltpu.make_async_remote_copy(
            src_ref=outs[a].at[c], dst_ref=outs[a].at[c], send_sem=send_sem.at[a], recv_sem=recv_sem.at[a],
            device_id=(x, y, 1 - c), device_id_type=MESH) for a in range(n)]
        for cp in cps:
            cp.start()
        for cp in cps:
            cp.wait()

    return pl.pallas_call(
        body, name="share_halves",
        in_specs=[ANY] * n, out_specs=[ANY] * n,
        out_shape=[jax.ShapeDtypeStruct(h.shape, h.dtype) for h in halves],
        input_output_aliases={a: a for a in range(n)},
        scratch_shapes=[pltpu.SemaphoreType.DMA((n,)), pltpu.SemaphoreType.DMA((n,))],
    )(*halves)


N_DEV = 8


def _allreduce_small(v):
    m, nc = v.shape

    def body(x_ref, out_ref, gat, send_sems, recv_sems, local_sem):
        x, y, c, chips = _place()
        me, sib = (x, y, c), (x, y, 1 - c)

        def rows(px, py, pc):
            return gat.at[pl.ds((4 * px + 2 * py + pc) * m, m), :]

        def copy(k, block, to, src=None):
            return pltpu.make_async_remote_copy(
                src_ref=rows(*block) if src is None else src, dst_ref=rows(*block),
                send_sem=send_sems.at[k], recv_sem=recv_sems.at[k], device_id=to, device_id_type=MESH)

        mine = pltpu.make_async_copy(x_ref, rows(*me), local_sem)
        mine.start()
        first = [copy(0, me, sib, src=x_ref)]
        first += [copy(1 + j, me, (*chip, c), src=x_ref) for j, chip in enumerate(chips)]
        for cp in first:
            cp.start()
        passed = [copy(4 + j, (*chip, c), sib) for j, chip in enumerate(chips)]
        for j, chip in enumerate(chips):
            copy(1 + j, (*chip, c), me).wait_recv()
            passed[j].start()
        copy(0, sib, me).wait_recv()
        for j, chip in enumerate(chips):
            copy(4 + j, (*chip, 1 - c), me).wait_recv()
        for cp in first + passed:
            cp.wait_send()
        mine.wait()
        acc = gat[0:m, :]
        for dev in range(1, N_DEV):
            acc = acc + gat[dev * m:(dev + 1) * m, :]
        out_ref[...] = acc

    return pl.pallas_call(
        body, name="allreduce_small",
        in_specs=[pl.BlockSpec(memory_space=pltpu.VMEM)],
        out_specs=pl.BlockSpec(memory_space=pltpu.VMEM),
        out_shape=jax.ShapeDtypeStruct((m, nc), F32),
        scratch_shapes=[pltpu.VMEM((N_DEV * m, nc), F32), pltpu.SemaphoreType.DMA((7,)),
                        pltpu.SemaphoreType.DMA((7,)), pltpu.SemaphoreType.DMA],
        compiler_params=pltpu.CompilerParams(vmem_limit_bytes=VMEM_LIMIT),
    )(v)


def _pad_rows(a, rows):
    return jnp.pad(a, ((0, rows - a.shape[0]), (0, 0)))


def kernel(x, norm_g, final_g, conv_w_in, conv_dw, conv_dw_b, conv_ln_g, conv_ln_b, conv_w_out, pool_w_in, pool_w_grp, pool_b_grp, pool_scale, pool_w_out, loss_target, m_norm_g, m_final_g, m_conv_w_in, m_conv_dw, m_conv_dw_b, m_conv_ln_g, m_conv_ln_b, m_conv_w_out, m_pool_w_in, m_pool_w_grp, m_pool_b_grp, m_pool_scale, m_pool_w_out, v_norm_g, v_final_g, v_conv_w_in, v_conv_dw, v_conv_dw_b, v_conv_ln_g, v_conv_ln_b, v_conv_w_out, v_pool_w_in, v_pool_w_grp, v_pool_b_grp, v_pool_scale, v_pool_w_out):
    t, d = x.shape[1], x.shape[2]
    e = conv_w_out.shape[2]
    ng = len(POOL_WINDOWS)
    gc = e // ng
    gcs = pool_w_grp.shape[2]
    ck = conv_dw.shape[1]
    es = conv_dw.shape[2]
    xi, yi, ci = lax.axis_index("x"), lax.axis_index("y"), lax.axis_index("c")
    chip = 2 * xi + yi

    small_rows = ck + 2
    small_pad = -(-small_rows // 8) * 8
    small = jnp.concatenate([conv_dw, pool_b_grp[:, None, :], pool_scale[:, None, :],
                             jnp.zeros((2, small_pad - small_rows, es), F32)], axis=1)
    cwi_b, cwo_b = conv_w_in.astype(BF16), conv_w_out.astype(BF16)

    def halves(a):
        return a.reshape(2, a.shape[0] // 2, a.shape[1])

    first = [halves(cwi_b[0]), halves(cwo_b[0]), small]
    g_cwi0, g_cwo0, g_small = [_put_own(g, q, chip) for g, q in zip(_allgather_weights(first), first)]
    rest = [cwi_b[1], cwo_b[1], pool_w_in.astype(BF16), pool_w_grp.astype(BF16), pool_w_out.astype(BF16)]
    rest_lands = [lax.empty((N_CHIPS,) + r.shape, r.dtype) for r in rest]
    ag_send, ag_recv, rest, rest_lands, ag_token = _split_start(
        "gather_rest_start", _gather_plan, rest, rest_lands, 3 * len(rest), g_small)
    smallf = jnp.transpose(g_small, (1, 2, 0, 3)).reshape(2, small_pad, N_CHIPS * es)

    h = x.reshape(t, d)
    tgt = loss_target.reshape(t, d)
    hs, saved = [], []
    for layer in range(4):
        j = layer // 2
        hs.append(h)
        gvec = norm_g[layer][None, :]
        if layer == 0:
            gvec = gvec + ag_token[0:1, 0:1]
        if layer == 1:
            rest, rest_lands = _split_wait("gather_rest_wait", _gather_plan, ag_send, ag_recv, rest, rest_lands, h)
            g_cwi1, g_cwo1, g_pwi, g_pwg, g_pwo = [_put_own(g, q, chip) for g, q in zip(rest_lands, rest)]
            wg_full = jnp.transpose(g_pwg, (1, 2, 0, 3, 4)).reshape(2, ng, gc, gc)
        if layer % 2 == 0:
            g_in, g_out = (g_cwi0, g_cwo0) if j == 0 else (g_cwi1, g_cwo1)
            w_in4 = g_in.reshape(N_CHIPS, d, -1)
            w_out = g_out.reshape(e, d)
            dw_full = smallf[j, 0:ck]
            p, hn, s, c, h = _conv_fwd(h, gvec, w_in4, w_out, dw_full.reshape(-1, LANES),
                                       conv_dw_b[j].reshape(-1, LANES), conv_ln_g[j][None, :], conv_ln_b[j][None, :])
            saved.append((p, hn, s, c, w_in4, w_out, dw_full))
        else:
            w_in4 = g_pwi[:, j]
            w_out = g_pwo[:, j].reshape(e, d)
            p, hn = _rms_matmul(h, gvec, w_in4, "rms_matmul_pool")
            bg_full = smallf[j, ck:ck + 1]
            sc_full = smallf[j, ck + 1:ck + 2]
            s = _pool_mid_fwd(p, wg_full[j], bg_full, sc_full)
            saved.append((p, hn, s, None, w_in4, w_out, (wg_full[j], bg_full, sc_full)))
            h = _matmul_res(h, s, w_out)

    dh, loss_part, dfg = _loss_head(h, final_g[None, :], tgt)
    loss = lax.psum(loss_part[0, 0], ("x", "y", "c"))

    def by_half(a):
        return a.reshape(N_CHIPS, 2, a.shape[1] // 2, a.shape[2])

    dng = [None] * 4
    g_conv = [None, None]
    g_pool = [None, None]
    flights = {}
    for layer in (3, 2, 1, 0):
        j = layer // 2
        p, hn, s, c, w_in4, w_out, extra = saved[layer]
        gvec = norm_g[layer][None, :]
        dw_out, dw_out_b = [q.reshape(N_CHIPS, e // N_CHIPS, d) for q in _wgrad(s, dh, 1, "wgrad_out")]
        if layer % 2 == 0:
            dp, dlg, dlb, ddwb, ddw3, dw_in = _conv_bwd(p, c, dh, hn, w_out, extra.reshape(-1, LANES),
                                                        conv_ln_g[j][None, :], conv_ln_b[j][None, :])
            ddw = ddw3.reshape(ck, e)
            dw_in_b = dw_in.astype(BF16)
            own, pay = [dw_in, dw_out], [dw_in_b, dw_out_b]
            g_conv[j] = (dlg, dlb, ddwb, ddw)
        else:
            wg, bg_full, sc_full = extra
            ds = _ds_matmul(dh, w_out)
            dp, dwg, dbg, dsc = _pool_mid_bwd(p, ds, wg, bg_full, sc_full)
            dw_in, dw_in_b = _wgrad(hn, dp, N_CHIPS, "wgrad_in_pool")
            dwg4 = jnp.transpose(dwg.reshape(ng, N_CHIPS, gcs, gc), (1, 0, 2, 3)).reshape(N_CHIPS, ng * gcs, gc)
            own, pay = [dw_in, dw_out, dwg4], [dw_in_b, dw_out_b, dwg4.astype(BF16)]
            g_pool[j] = (dbg, dsc)
        pay = [by_half(q) for q in pay]
        lands = [lax.empty((len(FLIPS),) + q.shape[2:], BF16) for q in pay]
        send, recv, pay, lands, token = _split_start(
            "scatter_start_%d" % layer, _scatter_plan, pay, lands, len(FLIPS) * len(pay), own[0])
        flights[layer] = (send, recv, pay, lands, own)
        kind = "dhn_rms_bwd_conv" if layer % 2 == 0 else "dhn_rms_bwd_pool"
        dh, dng[layer] = _dhn_rms_bwd(dp, w_in4, hs[layer], gvec + token[0:1, 0:1], dh, kind)
    grad_x = dh.reshape(x.shape)

    sel_kc = jnp.stack([chip, ci]).astype(jnp.int32)
    summed = {}
    after = dh
    for layer in (3, 2, 1, 0):
        send, recv, pay, lands, own = flights[layer]
        _, lands = _split_wait("scatter_wait_%d" % layer, _scatter_plan, send, recv, pay, lands, after)
        after = lands[0]
        summed[layer] = [_add_eight(by_half(o), l, sel_kc, "add_eight_%d_%d" % (layer % 2, a))
                         for a, (o, l) in enumerate(zip(own, lands))]
    order = [(0, 0), (0, 1), (2, 0), (2, 1), (1, 0), (1, 1), (1, 2), (3, 0), (3, 1), (3, 2)]
    shared = _share_halves([summed[l][a] for l, a in order])
    full = {la: q.reshape(q.shape[0] * q.shape[1], q.shape[2]) for la, q in zip(order, shared)}
    g_cwi_f = jnp.stack([full[(0, 0)], full[(2, 0)]])
    g_cwo_f = jnp.stack([full[(0, 1)], full[(2, 1)]])
    g_pwi_f = jnp.stack([full[(1, 0)], full[(3, 0)]])
    g_pwo_f = jnp.stack([full[(1, 1)], full[(3, 1)]])
    g_pwg_f = jnp.stack([full[(1, 2)], full[(3, 2)]]).reshape(pool_w_grp.shape)

    rows_list = [dng[0], dng[1], dng[2], dng[3], dfg,
                 g_conv[0][2], g_conv[1][2], g_conv[0][0], g_conv[1][0], g_conv[0][1], g_conv[1][1],
                 g_pool[0][0], g_pool[1][0], g_pool[0][1], g_pool[1][1], g_conv[0][3], g_conv[1][3]]
    slab = jnp.concatenate(rows_list, axis=0)
    nrows = slab.shape[0]
    slab = _pad_rows(slab, -(-nrows // 8) * 8)
    tot = _allreduce_small(slab)
    g_norm_g = tot[0:4]
    g_final_g = tot[4]
    g_dwb = tot[5:7]
    g_lng = tot[7:9]
    g_lnb = tot[9:11]
    g_bg = lax.dynamic_slice_in_dim(tot[11:13], chip * es, es, axis=1)
    g_sc = lax.dynamic_slice_in_dim(tot[13:15], chip * es, es, axis=1)
    g_dw = lax.dynamic_slice_in_dim(tot[15:15 + 2 * ck].reshape(2, ck, e), chip * es, es, axis=2)

    def adam_nd(w, g, m, v, nm):
        rows, cols = _rows2d(w.shape)
        outs = _adamw(w.reshape(rows, cols), g.reshape(rows, cols), m.reshape(rows, cols),
                      v.reshape(rows, cols), "adamw_" + nm)
        return [o.reshape(w.shape) for o in outs]

    res = {}
    res["conv_w_in"] = (g_cwi_f, *adam_nd(conv_w_in, g_cwi_f, m_conv_w_in, v_conv_w_in, "cwi"))
    res["conv_w_out"] = (g_cwo_f, *adam_nd(conv_w_out, g_cwo_f, m_conv_w_out, v_conv_w_out, "cwo"))
    res["pool_w_in"] = (g_pwi_f, *adam_nd(pool_w_in, g_pwi_f, m_pool_w_in, v_pool_w_in, "pwi"))
    res["pool_w_grp"] = (g_pwg_f, *adam_nd(pool_w_grp, g_pwg_f, m_pool_w_grp, v_pool_w_grp, "pwg"))
    res["pool_w_out"] = (g_pwo_f, *adam_nd(pool_w_out, g_pwo_f, m_pool_w_out, v_pool_w_out, "pwo"))

    def pack(parts, rows_to):
        return _pad_rows(jnp.concatenate([q.reshape(-1, q.shape[-1]) for q in parts], axis=0), rows_to)

    rep_w = [norm_g, final_g[None, :], conv_dw_b, conv_ln_g, conv_ln_b]
    rep_g = [g_norm_g, g_final_g[None, :], g_dwb, g_lng, g_lnb]
    rep_m = [m_norm_g, m_final_g[None, :], m_conv_dw_b, m_conv_ln_g, m_conv_ln_b]
    rep_v = [v_norm_g, v_final_g[None, :], v_conv_dw_b, v_conv_ln_g, v_conv_ln_b]
    rep = _adamw(pack(rep_w, 16), pack(rep_g, 16), pack(rep_m, 16), pack(rep_v, 16), "adamw_rep")
    rep_names = ["norm_g", "final_g", "conv_dw_b", "conv_ln_g", "conv_ln_b"]
    rep_rows = [(0, 4), (4, 5), (5, 7), (7, 9), (9, 11)]
    for nm, (lo, hi), gq, wq in zip(rep_names, rep_rows, rep_g, rep_w):
        shape = (d,) if nm == "final_g" else wq.shape
        res[nm] = (gq.reshape(shape), *[o[lo:hi].reshape(shape) for o in rep])

    sh_w = [conv_dw, pool_b_grp, pool_scale]
    sh_g = [g_dw, g_bg, g_sc]
    sh_m = [m_conv_dw, m_pool_b_grp, m_pool_scale]
    sh_v = [v_conv_dw, v_pool_b_grp, v_pool_scale]
    sh_total = 2 * ck + 4
    sh_pad = -(-sh_total // 8) * 8
    shd = _adamw(pack(sh_w, sh_pad), pack(sh_g, sh_pad), pack(sh_m, sh_pad), pack(sh_v, sh_pad), "adamw_shard")
    sh_names = ["conv_dw", "pool_b_grp", "pool_scale"]
    sh_rows = [(0, 2 * ck), (2 * ck, 2 * ck + 2), (2 * ck + 2, 2 * ck + 4)]
    for nm, (lo, hi), gq, wq in zip(sh_names, sh_rows, sh_g, sh_w):
        res[nm] = (gq.reshape(wq.shape), *[o[lo:hi].reshape(wq.shape) for o in shd])

    order = ["norm_g", "final_g", "conv_w_in", "conv_dw", "conv_dw_b", "conv_ln_g", "conv_ln_b", "conv_w_out",
             "pool_w_in", "pool_w_grp", "pool_b_grp", "pool_scale", "pool_w_out"]
    outs = [loss, grad_x]
    for part in range(4):
        outs += [res[nm][part] for nm in order]
    return tuple(outs)
```

```python
import functools

import jax
import jax.numpy as jnp
from jax import lax
from jax.experimental import pallas as pl
from jax.experimental.pallas import tpu as pltpu

F32 = jnp.float32
BF16 = jnp.bfloat16
MESH = pl.DeviceIdType.MESH

RMS_EPS = 1e-6
LN_EPS = 1e-5
CONV_K = 31
HALO = 32
PHALO = 16
POOL_WINDOWS = (2, 4, 8, 16)
N_CHIPS = 4
LANES = 128
ROW_CHUNK = 32
CONV_ROW_CHUNK = 32
FIR_BLOCK = 16
TOKEN_TILE = 512
MATMUL_TILE = 1024
WGRAD_TILE = 2048
VMEM_LIMIT = 56 * 1024 * 1024

ADAM_LR = 0.001
ADAM_B1 = 0.9
ADAM_B2 = 0.999
ADAM_EPS = 1e-08
ADAM_WD = 0.01
ADAM_STEP = 10


def _params(*sem):
    return pltpu.CompilerParams(dimension_semantics=sem, vmem_limit_bytes=VMEM_LIMIT)


def _sig(v):
    return 0.5 * jnp.tanh(0.5 * v) + 0.5


def _dsilu(v, sv):
    return sv * (1.0 + v * (1.0 - sv))


def _resident(shape):
    return pl.BlockSpec(shape, lambda *_: (0,) * len(shape), pipeline_mode=pl.Buffered(1))


def _tile(t):
    return min(TOKEN_TILE, t)


def _rms_matmul(h, g, w4, name):
    t, d = h.shape
    nk = w4.shape[-1]
    tm = min(MATMUL_TILE, t)

    def body(h_ref, g_ref, w_ref, p_ref, hn_ref):
        hh = h_ref[...]
        r = lax.rsqrt(jnp.mean(hh * hh, axis=-1, keepdims=True) + RMS_EPS)
        hn = (hh * r * g_ref[...]).astype(BF16)
        hn_ref[...] = hn
        for k in range(N_CHIPS):
            p_ref[:, k * nk:(k + 1) * nk] = jnp.dot(hn, w_ref[k], preferred_element_type=F32).astype(BF16)

    return pl.pallas_call(
        body, name=name, grid=(t // tm,),
        in_specs=[pl.BlockSpec((tm, d), lambda i: (i, 0)),
                  pl.BlockSpec((1, d), lambda i: (0, 0)),
                  _resident((N_CHIPS, d, nk))],
        out_specs=[pl.BlockSpec((tm, N_CHIPS * nk), lambda i: (i, 0)),
                   pl.BlockSpec((tm, d), lambda i: (i, 0))],
        out_shape=[jax.ShapeDtypeStruct((t, N_CHIPS * nk), BF16), jax.ShapeDtypeStruct((t, d), BF16)],
        compiler_params=_params("parallel"),
    )(h, g, w4)


def _to_token_tiles(ref, tok0, rows, val, ng):
    for j in range(ng):
        ref[pl.ds(tok0 * ng + j, rows, stride=ng), :] = val[:, j * LANES:(j + 1) * LANES]


def _from_token_tiles(ref, tok0, rows, ng):
    return jnp.concatenate([ref[pl.ds(tok0 * ng + j, rows, stride=ng), :] for j in range(ng)], axis=1)


def _conv_mid_fwd(p, dw3, dwb3, lg, lb):
    t = p.shape[0]
    e = p.shape[1] // 3
    ng = e // LANES
    tm = _tile(t)
    rc = CONV_ROW_CHUNK
    fb = FIR_BLOCK

    def body(p_ref, dw_ref, dwb_ref, lg_ref, lb_ref, s_ref, c_ref, u3, c3):
        i = pl.program_id(0)

        @pl.when(i == 0)
        def _():
            u3[0:HALO * ng, :] = jnp.zeros((HALO * ng, LANES), F32)

        def glu(rci, carry):
            base = pl.multiple_of(rci * rc, rc)
            a = p_ref[pl.ds(base, rc), 0:e].astype(F32)
            b = p_ref[pl.ds(base, rc), e:2 * e].astype(F32)
            _to_token_tiles(u3, HALO + base, rc, a * _sig(b), ng)
            return carry

        lax.fori_loop(0, tm // rc, glu, 0)

        def fir(bi, carry):
            t0 = bi * fb
            def x(q):
                return u3[pl.ds(pl.multiple_of((t0 + HALO - (CONV_K - 1) + q) * ng, ng), ng), :]

            xs = [x(q) for q in range(fb - 1)]
            accs = [dwb_ref[...]] * fb
            for k in range(CONV_K):
                wk = dw_ref[k * ng:(k + 1) * ng, :]
                xs.append(x(k + fb - 1))
                accs = [accs[q] + wk * xs[q + k] for q in range(fb)]
            for q in range(fb):
                c3[pl.ds(pl.multiple_of((t0 + q) * ng, ng), ng), :] = accs[q]
            return carry

        lax.fori_loop(0, tm // fb, fir, 0)
        u3[0:HALO * ng, :] = u3[tm * ng:(tm + HALO) * ng, :]

        def chunk(rci, carry):
            base = pl.multiple_of(rci * rc, rc)
            c = _from_token_tiles(c3, base, rc, ng)
            mu = jnp.mean(c, axis=-1, keepdims=True)
            cc = c - mu
            var = jnp.mean(cc * cc, axis=-1, keepdims=True)
            ln = cc * lax.rsqrt(var + LN_EPS) * lg_ref[...] + lb_ref[...]
            z = p_ref[pl.ds(base, rc), 2 * e:3 * e].astype(F32)
            s = (ln * _sig(ln)) * (z * _sig(z))
            s_ref[pl.ds(base, rc), :] = s.astype(BF16)
            c_ref[pl.ds(base, rc), :] = c.astype(BF16)
            return carry

        lax.fori_loop(0, tm // rc, chunk, 0, unroll=2)

    vec = pl.BlockSpec((1, e), lambda i: (0, 0))
    return pl.pallas_call(
        body, name="conv_mid_fwd", grid=(t // tm,),
        in_specs=[pl.BlockSpec((tm, 3 * e), lambda i: (i, 0)),
                  pl.BlockSpec((CONV_K * ng, LANES), lambda i: (0, 0)),
                  pl.BlockSpec((ng, LANES), lambda i: (0, 0)), vec, vec],
        out_specs=[pl.BlockSpec((tm, e), lambda i: (i, 0)), pl.BlockSpec((tm, e), lambda i: (i, 0))],
        out_shape=[jax.ShapeDtypeStruct((t, e), BF16), jax.ShapeDtypeStruct((t, e), BF16)],
        scratch_shapes=[pltpu.VMEM(((tm + HALO) * ng, LANES), F32), pltpu.VMEM((tm * ng, LANES), F32)],
        compiler_params=_params("arbitrary"),
    )(p, dw3, dwb3, lg, lb)


def _conv_mid_out_fwd(p, h, w_out, dw3, dwb3, lg, lb):
    t = p.shape[0]
    e = p.shape[1] // 3
    d = h.shape[1]
    ng = e // LANES
    tm = _tile(t)
    nt = t // tm
    rc = CONV_ROW_CHUNK
    fb = FIR_BLOCK

    def body(p_ref, h_ref, w_ref, dw_ref, dwb_ref, lg_ref, lb_ref, s_ref, c_ref, ho_ref, u3, c3, s_prev):
        i = pl.program_id(0)

        def project():
            ho_ref[...] = h_ref[...] + jnp.dot(s_prev[...], w_ref[...], preferred_element_type=F32)

        @pl.when(i == 0)
        def _():
            u3[0:HALO * ng, :] = jnp.zeros((HALO * ng, LANES), F32)
            s_prev[...] = jnp.zeros_like(s_prev)

        @pl.when(i < nt)
        def _():
            def glu(rci, carry):
                base = pl.multiple_of(rci * rc, rc)
                a = p_ref[pl.ds(base, rc), 0:e].astype(F32)
                b = p_ref[pl.ds(base, rc), e:2 * e].astype(F32)
                _to_token_tiles(u3, HALO + base, rc, a * _sig(b), ng)
                return carry

            lax.fori_loop(0, tm // rc, glu, 0)

            project()
            for bi in range(tm // fb):
                t0 = bi * fb
                xs = [u3[(t0 + HALO - (CONV_K - 1) + q) * ng:(t0 + HALO - (CONV_K - 1) + q + 1) * ng, :]
                      for q in range(fb - 1)]
                accs = [dwb_ref[...]] * fb
                for k in range(CONV_K):
                    wk = dw_ref[k * ng:(k + 1) * ng, :]
                    q1 = t0 + HALO - (CONV_K - 1) + k + fb - 1
                    xs.append(u3[q1 * ng:(q1 + 1) * ng, :])
                    accs = [accs[q] + wk * xs[q + k] for q in range(fb)]
                for q in range(fb):
                    c3[(t0 + q) * ng:(t0 + q + 1) * ng, :] = accs[q]
            u3[0:HALO * ng, :] = u3[tm * ng:(tm + HALO) * ng, :]

            def chunk(rci, carry):
                base = pl.multiple_of(rci * rc, rc)
                c = _from_token_tiles(c3, base, rc, ng)
                mu = jnp.mean(c, axis=-1, keepdims=True)
                cc = c - mu
                var = jnp.mean(cc * cc, axis=-1, keepdims=True)
                ln = cc * lax.rsqrt(var + LN_EPS) * lg_ref[...] + lb_ref[...]
                z = p_ref[pl.ds(base, rc), 2 * e:3 * e].astype(F32)
                s = ((ln * _sig(ln)) * (z * _sig(z))).astype(BF16)
                s_ref[pl.ds(base, rc), :] = s
                s_prev[pl.ds(base, rc), :] = s
                c_ref[pl.ds(base, rc), :] = c.astype(BF16)
                return carry

            lax.fori_loop(0, tm // rc, chunk, 0, unroll=2)

        @pl.when(i == nt)
        def _():
            project()

    vec = pl.BlockSpec((1, e), lambda i: (0, 0))
    cur = lambda i: (jnp.minimum(i, nt - 1), 0)
    lag = lambda i: (jnp.maximum(i - 1, 0), 0)
    return pl.pallas_call(
        body, name="conv_mid_out_fwd", grid=(nt + 1,),
        in_specs=[pl.BlockSpec((tm, 3 * e), cur), pl.BlockSpec((tm, d), lag), _resident((e, d)),
                  pl.BlockSpec((CONV_K * ng, LANES), lambda i: (0, 0)),
                  pl.BlockSpec((ng, LANES), lambda i: (0, 0)), vec, vec],
        out_specs=[pl.BlockSpec((tm, e), cur), pl.BlockSpec((tm, e), cur), pl.BlockSpec((tm, d), lag)],
        out_shape=[jax.ShapeDtypeStruct((t, e), BF16), jax.ShapeDtypeStruct((t, e), BF16),
                   jax.ShapeDtypeStruct((t, d), F32)],
        scratch_shapes=[pltpu.VMEM(((tm + HALO) * ng, LANES), F32), pltpu.VMEM((tm * ng, LANES), F32),
                        pltpu.VMEM((tm, e), BF16)],
        compiler_params=_params("arbitrary"),
    )(p, h, w_out, dw3, dwb3, lg, lb)


def _conv_fwd(h, g, w_in4, w_out, dw3, dwb3, lg, lb):
    t, d = h.shape
    nk = w_in4.shape[-1]
    e = N_CHIPS * nk // 3
    ng = e // LANES
    tm = _tile(t)
    nt = t // tm
    rc = CONV_ROW_CHUNK
    fb = FIR_BLOCK

    def body(hl_ref, hg_ref, g_ref, wi_ref, wo_ref, dw_ref, dwb_ref, lg_ref, lb_ref,
             p_ref, hn_ref, s_ref, c_ref, ho_ref, u3, c3, s_prev, p_s, hn_s):
        q = pl.program_id(0)
        nxt = q % 2
        cur = 1 - nxt

        def normed():
            hh = hl_ref[...]
            r = lax.rsqrt(jnp.mean(hh * hh, axis=-1, keepdims=True) + RMS_EPS)
            hn = (hh * r * g_ref[...]).astype(BF16)
            hn_s[nxt] = hn
            return hn

        def in_project(hn, k):
            p_s[nxt, :, k * nk:(k + 1) * nk] = jnp.dot(hn, wi_ref[k], preferred_element_type=F32).astype(BF16)

        def out_project():
            ho_ref[...] = hg_ref[...] + jnp.dot(s_prev[...], wo_ref[...], preferred_element_type=F32)

        @pl.when(q == 0)
        def _():
            u3[0:HALO * ng, :] = jnp.zeros((HALO * ng, LANES), F32)
            s_prev[...] = jnp.zeros_like(s_prev)
            hn0 = normed()
            for k in range(N_CHIPS):
                in_project(hn0, k)

        @pl.when(jnp.logical_and(q >= 1, q <= nt))
        def _():
            def glu(rci, carry):
                base = pl.multiple_of(rci * rc, rc)
                rows = pl.ds(base, rc)
                pa = p_s[cur, rows, 0:e]
                pb = p_s[cur, rows, e:2 * e]
                p_ref[rows, 0:e] = pa
                p_ref[rows, e:2 * e] = pb
                hn_ref[rows, :] = hn_s[cur, rows, :]
                _to_token_tiles(u3, HALO + base, rc, pa.astype(F32) * _sig(pb.astype(F32)), ng)
                return carry

            lax.fori_loop(0, tm // rc, glu, 0)

            nfir = tm // fb
            spots = {(m + 1) * nfir // (N_CHIPS + 2): m for m in range(N_CHIPS + 1)}
            hn = normed()
            for bi in range(nfir):
                if bi in spots:
                    if spots[bi] < N_CHIPS:
                        in_project(hn, spots[bi])
                    else:
                        out_project()
                t0 = bi * fb
                xs = [u3[(t0 + HALO - (CONV_K - 1) + j) * ng:(t0 + HALO - (CONV_K - 1) + j + 1) * ng, :]
                      for j in range(fb - 1)]
                accs = [dwb_ref[...]] * fb
                for k in range(CONV_K):
                    wk = dw_ref[k * ng:(k + 1) * ng, :]
                    j1 = t0 + HALO - (CONV_K - 1) + k + fb - 1
                    xs.append(u3[j1 * ng:(j1 + 1) * ng, :])
                    accs = [accs[j] + wk * xs[j + k] for j in range(fb)]
                for j in range(fb):
                    c3[(t0 + j) * ng:(t0 + j + 1) * ng, :] = accs[j]
            u3[0:HALO * ng, :] = u3[tm * ng:(tm + HALO) * ng, :]

            def chunk(rci, carry):
                base = pl.multiple_of(rci * rc, rc)
                rows = pl.ds(base, rc)
                c = _from_token_tiles(c3, base, rc, ng)
                mu = jnp.mean(c, axis=-1, keepdims=True)
                cc = c - mu
                var = jnp.mean(cc * cc, axis=-1, keepdims=True)
                ln = cc * lax.rsqrt(var + LN_EPS) * lg_ref[...] + lb_ref[...]
                pz = p_s[cur, rows, 2 * e:3 * e]
                p_ref[rows, 2 * e:3 * e] = pz
                z = pz.astype(F32)
                s = ((ln * _sig(ln)) * (z * _sig(z))).astype(BF16)
                s_ref[rows, :] = s
                s_prev[rows, :] = s
                c_ref[rows, :] = c.astype(BF16)
                return carry

            lax.fori_loop(0, tm // rc, chunk, 0, unroll=2)

        @pl.when(q == nt + 1)
        def _():
            out_project()

    vec = pl.BlockSpec((1, e), lambda q: (0, 0))
    lead = lambda q: (jnp.clip(q, 0, nt - 1), 0)
    cur_map = lambda q: (jnp.clip(q - 1, 0, nt - 1), 0)
    lag = lambda q: (jnp.clip(q - 2, 0, nt - 1), 0)
    return pl.pallas_call(
        body, name="conv_fwd", grid=(nt + 2,),
        in_specs=[pl.BlockSpec((tm, d), lead), pl.BlockSpec((tm, d), lag), pl.BlockSpec((1, d), lambda q: (0, 0)),
                  _resident((N_CHIPS, d, nk)), _resident((e, d)),
                  pl.BlockSpec((CONV_K * ng, LANES), lambda q: (0, 0)),
                  pl.BlockSpec((ng, LANES), lambda q: (0, 0)), vec, vec],
        out_specs=[pl.BlockSpec((tm, 3 * e), cur_map), pl.BlockSpec((tm, d), cur_map),
                   pl.BlockSpec((tm, e), cur_map), pl.BlockSpec((tm, e), cur_map), pl.BlockSpec((tm, d), lag)],
        out_shape=[jax.ShapeDtypeStruct((t, 3 * e), BF16), jax.ShapeDtypeStruct((t, d), BF16),
                   jax.ShapeDtypeStruct((t, e), BF16), jax.ShapeDtypeStruct((t, e), BF16),
                   jax.ShapeDtypeStruct((t, d), F32)],
        scratch_shapes=[pltpu.VMEM(((tm + HALO) * ng, LANES), F32), pltpu.VMEM((tm * ng, LANES), F32),
                        pltpu.VMEM((tm, e), BF16), pltpu.VMEM((2, tm, 3 * e), BF16),
                        pltpu.VMEM((2, tm, d), BF16)],
        compiler_params=_params("arbitrary"),
    )(h, h, g, w_in4, w_out, dw3, dwb3, lg, lb)


def _pool_group(lc, e):
    return (lc * LANES) // (e // len(POOL_WINDOWS))


def _pool_inv_count(row0, rows, w):
    tpos = row0 + lax.broadcasted_iota(jnp.int32, (rows, 1), 0)
    return 1.0 / jnp.minimum(tpos + 1, w).astype(F32)


def _pool_window_dev(ubuf, base, rc, e, row0, dbuf):
    n = rc + PHALO
    for lc in range(e // LANES):
        lanes = slice(lc * LANES, (lc + 1) * LANES)
        g = _pool_group(lc, e)
        w = POOL_WINDOWS[g]
        blk = ubuf[pl.ds(base, n), lanes]
        acc = blk
        step = 1
        while step < w:
            acc = acc + pltpu.roll(acc, step, 0)
            step *= 2
        win = acc[PHALO:n]
        tok = blk[PHALO:n]
        dbuf[pl.ds(base, rc), lanes] = win * _pool_inv_count(row0 + base, rc, w) - tok


def _pool_mid_out_fwd(p, h, w_out, wg, bg, sc):
    t = p.shape[0]
    e = p.shape[1] // 2
    d = h.shape[1]
    gc = e // len(POOL_WINDOWS)
    tm = _tile(t)
    nt = t // tm
    rc = ROW_CHUNK

    def body(p_ref, h_ref, w_ref, wg_ref, bg_ref, sc_ref, s_ref, ho_ref, ubuf, dbuf, s_prev):
        i = pl.program_id(0)

        def project():
            ho_ref[...] = h_ref[...] + jnp.dot(s_prev[...], w_ref[...], preferred_element_type=F32)

        @pl.when(i == 0)
        def _():
            ubuf[0:PHALO, :] = jnp.zeros((PHALO, e), F32)
            s_prev[...] = jnp.zeros_like(s_prev)

        @pl.when(i < nt)
        def _():
            ubuf[PHALO:PHALO + tm, :] = p_ref[:, 0:e].astype(F32)

            def chunk(rci, carry):
                base = pl.multiple_of(rci * rc, rc)
                _pool_window_dev(ubuf, base, rc, e, i * tm, dbuf)
                return carry

            lax.fori_loop(0, tm // rc, chunk, 0)
            ubuf[0:PHALO, :] = ubuf[tm:tm + PHALO, :]

            project()
            for g in range(len(POOL_WINDOWS)):
                cols = slice(g * gc, (g + 1) * gc)
                yg = jnp.dot(dbuf[:, cols].astype(BF16), wg_ref[g], preferred_element_type=F32)
                z = p_ref[:, e + g * gc:e + (g + 1) * gc].astype(F32)
                s = (((yg + bg_ref[:, cols]) * sc_ref[:, cols]) * (z * _sig(z))).astype(BF16)
                s_ref[:, cols] = s
                s_prev[:, cols] = s

        @pl.when(i == nt)
        def _():
            project()

    vec = pl.BlockSpec((1, e), lambda i: (0, 0))
    cur = lambda i: (jnp.minimum(i, nt - 1), 0)
    lag = lambda i: (jnp.maximum(i - 1, 0), 0)
    return pl.pallas_call(
        body, name="pool_mid_out_fwd", grid=(nt + 1,),
        in_specs=[pl.BlockSpec((tm, 2 * e), cur), pl.BlockSpec((tm, d), lag), _resident((e, d)),
                  pl.BlockSpec((len(POOL_WINDOWS), gc, gc), lambda i: (0, 0, 0)), vec, vec],
        out_specs=[pl.BlockSpec((tm, e), cur), pl.BlockSpec((tm, d), lag)],
        out_shape=[jax.ShapeDtypeStruct((t, e), BF16), jax.ShapeDtypeStruct((t, d), F32)],
        scratch_shapes=[pltpu.VMEM((tm + PHALO, e), F32), pltpu.VMEM((tm, e), F32), pltpu.VMEM((tm, e), BF16)],
        compiler_params=_params("arbitrary"),
    )(p, h, w_out, wg, bg, sc)


def _matmul_res(h, s, w):
    t, d = h.shape
    e = s.shape[1]
    tm = min(MATMUL_TILE, t)

    def body(h_ref, s_ref, w_ref, o_ref):
        o_ref[...] = h_ref[...] + jnp.dot(s_ref[...], w_ref[...], preferred_element_type=F32)

    return pl.pallas_call(
        body, name="matmul_res", grid=(t // tm,),
        in_specs=[pl.BlockSpec((tm, d), lambda i: (i, 0)), pl.BlockSpec((tm, e), lambda i: (i, 0)),
                  _resident((e, d))],
        out_specs=pl.BlockSpec((tm, d), lambda i: (i, 0)),
        out_shape=jax.ShapeDtypeStruct((t, d), F32),
        compiler_params=_params("parallel"),
    )(h, s, w)


def _loss_head(h, fg, tgt):
    t, d = h.shape
    tm = min(MATMUL_TILE, t)

    def body(h_ref, g_ref, t_ref, dh_ref, loss_ref, dg_ref):
        i = pl.program_id(0)

        @pl.when(i == 0)
        def _():
            loss_ref[...] = jnp.zeros_like(loss_ref)
            dg_ref[...] = jnp.zeros_like(dg_ref)

        hh = h_ref[...]
        r = lax.rsqrt(jnp.mean(hh * hh, axis=-1, keepdims=True) + RMS_EPS)
        hhat = hh * r
        err = hhat * g_ref[...] - t_ref[...]
        per_tok = jnp.mean(err * err, axis=-1, keepdims=True)
        loss_ref[...] += 0.5 * jnp.sum(per_tok, axis=0, keepdims=True)
        dy = err * (1.0 / d)
        tt = dy * g_ref[...]
        dh_ref[...] = r * (tt - hhat * jnp.mean(tt * hhat, axis=-1, keepdims=True))
        dg_ref[...] += jnp.sum(dy * hhat, axis=0, keepdims=True)

    return pl.pallas_call(
        body, name="loss_head", grid=(t // tm,),
        in_specs=[pl.BlockSpec((tm, d), lambda i: (i, 0)), pl.BlockSpec((1, d), lambda i: (0, 0)),
                  pl.BlockSpec((tm, d), lambda i: (i, 0))],
        out_specs=[pl.BlockSpec((tm, d), lambda i: (i, 0)), pl.BlockSpec((1, LANES), lambda i: (0, 0)),
                   pl.BlockSpec((1, d), lambda i: (0, 0))],
        out_shape=[jax.ShapeDtypeStruct((t, d), F32), jax.ShapeDtypeStruct((1, LANES), F32),
                   jax.ShapeDtypeStruct((1, d), F32)],
        compiler_params=_params("arbitrary"),
    )(h, fg, tgt)


def _ds_matmul(dy, w):
    t, d = dy.shape
    e = w.shape[0]
    tm = min(MATMUL_TILE, t)

    def body(dy_ref, w_ref, ds_ref):
        ds_ref[...] = lax.dot_general(dy_ref[...].astype(BF16), w_ref[...], (((1,), (1,)), ((), ())),
                                      preferred_element_type=F32).astype(BF16)

    return pl.pallas_call(
        body, name="ds_matmul", grid=(t // tm,),
        in_specs=[pl.BlockSpec((tm, d), lambda i: (i, 0)), _resident((e, d))],
        out_specs=pl.BlockSpec((tm, e), lambda i: (i, 0)),
        out_shape=jax.ShapeDtypeStruct((t, e), BF16),
        compiler_params=_params("parallel"),
    )(dy, w)


def _conv_mid_bwd(p, c, ds, dw3, lg, lb):
    t = p.shape[0]
    e = p.shape[1] // 3
    ng = e // LANES
    tm = _tile(t)
    nt = t // tm
    rc = CONV_ROW_CHUNK
    fb = FIR_BLOCK
    hb = tm // HALO

    def body(p_ref, ph_ref, c_ref, ds_ref, dw_ref, lg_ref, lb_ref,
             dp_ref, dlg_ref, dlb_ref, ddwb_ref, ddw_ref, u3, dc3, du3):
        i = pl.program_id(0)
        ti = nt - 1 - i

        @pl.when(i == 0)
        def _():
            dc3[tm * ng:(tm + HALO) * ng, :] = jnp.zeros((HALO * ng, LANES), F32)
            dlg_ref[...] = jnp.zeros_like(dlg_ref)
            dlb_ref[...] = jnp.zeros_like(dlb_ref)
            ddwb_ref[...] = jnp.zeros_like(ddwb_ref)
            ddw_ref[...] = jnp.zeros_like(ddw_ref)

        ha = ph_ref[:, 0:e].astype(F32)
        hbb = ph_ref[:, e:2 * e].astype(F32)
        _to_token_tiles(u3, 0, HALO, jnp.where(ti > 0, ha * _sig(hbb), 0.0), ng)

        def front(rci, carry):
            slg, slb, sdwb = carry
            base = pl.multiple_of(rci * rc, rc)
            rows = pl.ds(base, rc)
            a = p_ref[rows, 0:e].astype(F32)
            b = p_ref[rows, e:2 * e].astype(F32)
            _to_token_tiles(u3, HALO + base, rc, a * _sig(b), ng)
            cv = c_ref[rows, :].astype(F32)
            mu = jnp.mean(cv, axis=-1, keepdims=True)
            cc = cv - mu
            var = jnp.mean(cc * cc, axis=-1, keepdims=True)
            rs = lax.rsqrt(var + LN_EPS)
            nn = cc * rs
            ln = nn * lg_ref[...] + lb_ref[...]
            z = p_ref[rows, 2 * e:3 * e].astype(F32)
            sz = _sig(z)
            sl = _sig(ln)
            dsv = ds_ref[rows, :].astype(F32)
            dln = dsv * (z * sz) * _dsilu(ln, sl)
            dz = dsv * (ln * sl) * _dsilu(z, sz)
            dp_ref[rows, 2 * e:3 * e] = dz.astype(BF16)
            dn = dln * lg_ref[...]
            dc = rs * (dn - jnp.mean(dn, axis=-1, keepdims=True)
                       - nn * jnp.mean(dn * nn, axis=-1, keepdims=True))
            _to_token_tiles(dc3, base, rc, dc, ng)
            return (slg + jnp.sum(dln * nn, axis=0, keepdims=True),
                    slb + jnp.sum(dln, axis=0, keepdims=True),
                    sdwb + jnp.sum(dc, axis=0, keepdims=True))

        zero = jnp.zeros((1, e), F32)
        slg, slb, sdwb = lax.fori_loop(0, tm // rc, front, (zero, zero, zero), unroll=2)
        dlg_ref[...] += slg
        dlb_ref[...] += slb
        ddwb_ref[...] += sdwb

        def fir(bi, carry):
            t0 = bi * fb

            def dcs(q):
                return dc3[pl.ds(pl.multiple_of((t0 + q) * ng, ng), ng), :]

            def us(q):
                return u3[pl.ds(pl.multiple_of((t0 + HALO - (CONV_K - 1) + q) * ng, ng), ng), :]

            xs = [dcs(q) for q in range(fb - 1)]
            accs = [None] * fb
            for j in range(CONV_K):
                wk = dw_ref[(CONV_K - 1 - j) * ng:(CONV_K - j) * ng, :]
                xs.append(dcs(j + fb - 1))
                accs = [wk * xs[q + j] if accs[q] is None else accs[q] + wk * xs[q + j] for q in range(fb)]
            for q in range(fb):
                du3[pl.ds(pl.multiple_of((t0 + q) * ng, ng), ng), :] = accs[q]
            own = xs[0:fb]
            ys = [us(q) for q in range(fb - 1)]
            for k in range(CONV_K):
                ys.append(us(k + fb - 1))
                prods = [own[q] * ys[q + k] for q in range(fb)]
                while len(prods) > 1:
                    prods = [prods[2 * v] + prods[2 * v + 1] for v in range(len(prods) // 2)]
                ddw_ref[k * ng:(k + 1) * ng, :] += prods[0]
            return carry

        lax.fori_loop(0, tm // fb, fir, 0)
        dc3[tm * ng:(tm + HALO) * ng, :] = dc3[0:HALO * ng, :]

        def back(rci, carry):
            base = pl.multiple_of(rci * rc, rc)
            rows = pl.ds(base, rc)
            a = p_ref[rows, 0:e].astype(F32)
            b = p_ref[rows, e:2 * e].astype(F32)
            sb = _sig(b)
            duv = _from_token_tiles(du3, base, rc, ng)
            dp_ref[rows, 0:e] = (duv * sb).astype(BF16)
            dp_ref[rows, e:2 * e] = (duv * a * sb * (1.0 - sb)).astype(BF16)
            return carry

        lax.fori_loop(0, tm // rc, back, 0, unroll=2)

    vec = pl.BlockSpec((1, e), lambda i: (0, 0))
    taps = pl.BlockSpec((CONV_K * ng, LANES), lambda i: (0, 0))
    rev = lambda i: (nt - 1 - i, 0)
    halo = lambda i: (jnp.maximum((nt - 1 - i) * hb - 1, 0), 0)
    return pl.pallas_call(
        body, name="conv_mid_bwd", grid=(nt,),
        in_specs=[pl.BlockSpec((tm, 3 * e), rev), pl.BlockSpec((HALO, 3 * e), halo),
                  pl.BlockSpec((tm, e), rev), pl.BlockSpec((tm, e), rev), taps, vec, vec],
        out_specs=[pl.BlockSpec((tm, 3 * e), rev), vec, vec, vec, taps],
        out_shape=[jax.ShapeDtypeStruct((t, 3 * e), BF16), jax.ShapeDtypeStruct((1, e), F32),
                   jax.ShapeDtypeStruct((1, e), F32), jax.ShapeDtypeStruct((1, e), F32),
                   jax.ShapeDtypeStruct((CONV_K * ng, LANES), F32)],
        scratch_shapes=[pltpu.VMEM(((tm + HALO) * ng, LANES), F32), pltpu.VMEM(((tm + HALO) * ng, LANES), F32),
                        pltpu.VMEM((tm * ng, LANES), F32)],
        compiler_params=_params("arbitrary"),
    )(p, p, c, ds, dw3, lg, lb)


def _conv_bwd(p, c, dy, hn, w_out, dw3, lg, lb):
    t = p.shape[0]
    e = p.shape[1] // 3
    d = hn.shape[1]
    nk = 3 * e // N_CHIPS
    ng = e // LANES
    tm = _tile(t)
    nt = t // tm
    rc = CONV_ROW_CHUNK
    fb = FIR_BLOCK
    hb = tm // HALO

    def body(p_ref, ph_ref, c_ref, dyn_ref, hnp_ref, w_ref, dw_ref, lg_ref, lb_ref,
             dp_ref, dlg_ref, dlb_ref, ddwb_ref, ddw_ref, dwi_ref, u3, dc3, du3, ds_s, dp_s):
        g = pl.program_id(0)
        ti = nt - g

        def next_ds():
            ds_s[...] = lax.dot_general(dyn_ref[...].astype(BF16), w_ref[...], (((1,), (1,)), ((), ())),
                                        preferred_element_type=F32).astype(BF16)

        def prev_wgrad(slot):
            for k in range(N_CHIPS):
                dwi_ref[k] += lax.dot_general(hnp_ref[...], dp_s[slot, :, k * nk:(k + 1) * nk],
                                              (((0,), (0,)), ((), ())), preferred_element_type=F32)

        @pl.when(g == 0)
        def _():
            dc3[tm * ng:(tm + HALO) * ng, :] = jnp.zeros((HALO * ng, LANES), F32)
            dlg_ref[...] = jnp.zeros_like(dlg_ref)
            dlb_ref[...] = jnp.zeros_like(dlb_ref)
            ddwb_ref[...] = jnp.zeros_like(ddwb_ref)
            ddw_ref[...] = jnp.zeros_like(ddw_ref)
            dwi_ref[...] = jnp.zeros_like(dwi_ref)
            dp_s[0] = jnp.zeros_like(dp_s[0])
            next_ds()

        @pl.when(jnp.logical_and(g >= 1, g <= nt))
        def _():
            ha = ph_ref[:, 0:e].astype(F32)
            hbb = ph_ref[:, e:2 * e].astype(F32)
            _to_token_tiles(u3, 0, HALO, jnp.where(ti > 0, ha * _sig(hbb), 0.0), ng)
            cur = g % 2

            def front(rci, carry):
                slg, slb, sdwb = carry
                base = pl.multiple_of(rci * rc, rc)
                rows = pl.ds(base, rc)
                a = p_ref[rows, 0:e].astype(F32)
                b = p_ref[rows, e:2 * e].astype(F32)
                _to_token_tiles(u3, HALO + base, rc, a * _sig(b), ng)
                cv = c_ref[rows, :].astype(F32)
                mu = jnp.mean(cv, axis=-1, keepdims=True)
                cc = cv - mu
                var = jnp.mean(cc * cc, axis=-1, keepdims=True)
                rs = lax.rsqrt(var + LN_EPS)
                nn = cc * rs
                ln = nn * lg_ref[...] + lb_ref[...]
                z = p_ref[rows, 2 * e:3 * e].astype(F32)
                sz = _sig(z)
                sl = _sig(ln)
                dsv = ds_s[rows, :].astype(F32)
                dln = dsv * (z * sz) * _dsilu(ln, sl)
                dzb = (dsv * (ln * sl) * _dsilu(z, sz)).astype(BF16)
                dp_ref[rows, 2 * e:3 * e] = dzb
                dp_s[cur, rows, 2 * e:3 * e] = dzb
                dn = dln * lg_ref[...]
                dc = rs * (dn - jnp.mean(dn, axis=-1, keepdims=True)
                           - nn * jnp.mean(dn * nn, axis=-1, keepdims=True))
                _to_token_tiles(dc3, base, rc, dc, ng)
                return (slg + jnp.sum(dln * nn, axis=0, keepdims=True),
                        slb + jnp.sum(dln, axis=0, keepdims=True),
                        sdwb + jnp.sum(dc, axis=0, keepdims=True))

            zero = jnp.zeros((1, e), F32)
            slg, slb, sdwb = lax.fori_loop(0, tm // rc, front, (zero, zero, zero), unroll=2)
            dlg_ref[...] += slg
            dlb_ref[...] += slb
            ddwb_ref[...] += sdwb

            next_ds()
            prev_wgrad(1 - cur)
            for bi in range(tm // fb):
                t0 = bi * fb

                def dcs(q):
                    return dc3[(t0 + q) * ng:(t0 + q + 1) * ng, :]

                def us(q):
                    r0 = t0 + HALO - (CONV_K - 1) + q
                    return u3[r0 * ng:(r0 + 1) * ng, :]

                xs = [dcs(q) for q in range(fb - 1)]
                accs = [None] * fb
                for j in range(CONV_K):
                    wk = dw_ref[(CONV_K - 1 - j) * ng:(CONV_K - j) * ng, :]
                    xs.append(dcs(j + fb - 1))
                    accs = [wk * xs[q + j] if accs[q] is None else accs[q] + wk * xs[q + j] for q in range(fb)]
                for q in range(fb):
                    du3[(t0 + q) * ng:(t0 + q + 1) * ng, :] = accs[q]
                own = xs[0:fb]
                ys = [us(q) for q in range(fb - 1)]
                for k in range(CONV_K):
                    ys.append(us(k + fb - 1))
                    prods = [own[q] * ys[q + k] for q in range(fb)]
                    while len(prods) > 1:
                        prods = [prods[2 * v] + prods[2 * v + 1] for v in range(len(prods) // 2)]
                    ddw_ref[k * ng:(k + 1) * ng, :] += prods[0]
            dc3[tm * ng:(tm + HALO) * ng, :] = dc3[0:HALO * ng, :]

            def back(rci, carry):
                base = pl.multiple_of(rci * rc, rc)
                rows = pl.ds(base, rc)
                a = p_ref[rows, 0:e].astype(F32)
                b = p_ref[rows, e:2 * e].astype(F32)
                sb = _sig(b)
                duv = _from_token_tiles(du3, base, rc, ng)
                dab = (duv * sb).astype(BF16)
                dbb = (duv * a * sb * (1.0 - sb)).astype(BF16)
                dp_ref[rows, 0:e] = dab
                dp_ref[rows, e:2 * e] = dbb
                dp_s[cur, rows, 0:e] = dab
                dp_s[cur, rows, e:2 * e] = dbb
                return carry

            lax.fori_loop(0, tm // rc, back, 0, unroll=2)

        @pl.when(g == nt + 1)
        def _():
            prev_wgrad(nt % 2)

    vec = pl.BlockSpec((1, e), lambda g: (0, 0))
    taps = pl.BlockSpec((CONV_K * ng, LANES), lambda g: (0, 0))
    tile_of = lambda g: jnp.clip(nt - g, 0, nt - 1)
    cur_map = lambda g: (tile_of(g), 0)
    halo = lambda g: (jnp.maximum(tile_of(g) * hb - 1, 0), 0)
    nxt_map = lambda g: (jnp.clip(nt - 1 - g, 0, nt - 1), 0)
    prv_map = lambda g: (jnp.clip(nt + 1 - g, 0, nt - 1), 0)
    return pl.pallas_call(
        body, name="conv_bwd", grid=(nt + 2,),
        in_specs=[pl.BlockSpec((tm, 3 * e), cur_map), pl.BlockSpec((HALO, 3 * e), halo),
                  pl.BlockSpec((tm, e), cur_map), pl.BlockSpec((tm, d), nxt_map),
                  pl.BlockSpec((tm, d), prv_map), _resident((e, d)), taps, vec, vec],
        out_specs=[pl.BlockSpec((tm, 3 * e), cur_map), vec, vec, vec, taps,
                   pl.BlockSpec((N_CHIPS, d, nk), lambda g: (0, 0, 0))],
        out_shape=[jax.ShapeDtypeStruct((t, 3 * e), BF16), jax.ShapeDtypeStruct((1, e), F32),
                   jax.ShapeDtypeStruct((1, e), F32), jax.ShapeDtypeStruct((1, e), F32),
                   jax.ShapeDtypeStruct((CONV_K * ng, LANES), F32),
                   jax.ShapeDtypeStruct((N_CHIPS, d, nk), F32)],
        scratch_shapes=[pltpu.VMEM(((tm + HALO) * ng, LANES), F32), pltpu.VMEM(((tm + HALO) * ng, LANES), F32),
                        pltpu.VMEM((tm * ng, LANES), F32), pltpu.VMEM((tm, e), BF16),
                        pltpu.VMEM((2, tm, 3 * e), BF16)],
        compiler_params=_params("arbitrary"),
    )(p, p, c, dy, hn, w_out, dw3, lg, lb)


def _pool_mid_bwd(p, ds, wg, bg, sc):
    t = p.shape[0]
    e = p.shape[1] // 2
    ng = len(POOL_WINDOWS)
    gc = e // ng
    tm = _tile(t)
    nt = t // tm
    rc = ROW_CHUNK
    hb = tm // PHALO

    def body(p_ref, ph_ref, ds_ref, wg_ref, bg_ref, sc_ref,
             dp_ref, dwg_ref, dbg_ref, dsc_ref, ubuf, dbuf, ebuf, ddbuf):
        i = pl.program_id(0)
        ti = nt - 1 - i

        @pl.when(i == 0)
        def _():
            ebuf[tm:tm + PHALO, :] = jnp.zeros((PHALO, e), F32)
            dwg_ref[...] = jnp.zeros_like(dwg_ref)
            dbg_ref[...] = jnp.zeros_like(dbg_ref)
            dsc_ref[...] = jnp.zeros_like(dsc_ref)

        ubuf[0:PHALO, :] = jnp.where(ti > 0, ph_ref[:, 0:e].astype(F32), 0.0)
        ubuf[PHALO:PHALO + tm, :] = p_ref[:, 0:e].astype(F32)

        def recompute(rci, carry):
            base = pl.multiple_of(rci * rc, rc)
            _pool_window_dev(ubuf, base, rc, e, ti * tm, dbuf)
            return carry

        lax.fori_loop(0, tm // rc, recompute, 0)

        for g in range(ng):
            cols = slice(g * gc, (g + 1) * gc)
            dg = dbuf[:, cols].astype(BF16)
            q = jnp.dot(dg, wg_ref[g], preferred_element_type=F32) + bg_ref[:, cols]
            z = p_ref[:, e + g * gc:e + (g + 1) * gc].astype(F32)
            sz = _sig(z)
            dsv = ds_ref[:, cols].astype(F32)
            dz = dsv * (q * sc_ref[:, cols]) * _dsilu(z, sz)
            dp_ref[:, e + g * gc:e + (g + 1) * gc] = dz.astype(BF16)
            dy2 = dsv * (z * sz)
            dsc_ref[:, cols] += jnp.sum(dy2 * q, axis=0, keepdims=True)
            dq = dy2 * sc_ref[:, cols]
            dbg_ref[:, cols] += jnp.sum(dq, axis=0, keepdims=True)
            dqb = dq.astype(BF16)
            dwg_ref[g] += lax.dot_general(dg, dqb, (((0,), (0,)), ((), ())), preferred_element_type=F32)
            ddbuf[:, cols] = lax.dot_general(dqb, wg_ref[g], (((1,), (1,)), ((), ())),
                                             preferred_element_type=F32)

        def scale(rci, carry):
            base = pl.multiple_of(rci * rc, rc)
            for lc in range(e // LANES):
                lanes = slice(lc * LANES, (lc + 1) * LANES)
                w = POOL_WINDOWS[_pool_group(lc, e)]
                ebuf[pl.ds(base, rc), lanes] = (ddbuf[pl.ds(base, rc), lanes]
                                                * _pool_inv_count(ti * tm + base, rc, w))
            return carry

        lax.fori_loop(0, tm // rc, scale, 0)

        def chunk(rci, carry):
            base = pl.multiple_of(rci * rc, rc)
            n = rc + PHALO
            for lc in range(e // LANES):
                lanes = slice(lc * LANES, (lc + 1) * LANES)
                w = POOL_WINDOWS[_pool_group(lc, e)]
                acc = ebuf[pl.ds(base, n), lanes]
                step = 1
                while step < w:
                    acc = acc + pltpu.roll(acc, n - step, 0)
                    step *= 2
                du = acc[0:rc] - ddbuf[pl.ds(base, rc), lanes]
                dp_ref[pl.ds(base, rc), lanes] = du.astype(BF16)
            return carry

        lax.fori_loop(0, tm // rc, chunk, 0)
        ebuf[tm:tm + PHALO, :] = ebuf[0:PHALO, :]

    vec = pl.BlockSpec((1, e), lambda i: (0, 0))
    rev = lambda i: (nt - 1 - i, 0)
    halo = lambda i: (jnp.maximum((nt - 1 - i) * hb - 1, 0), 0)
    wspec = pl.BlockSpec((ng, gc, gc), lambda i: (0, 0, 0))
    return pl.pallas_call(
        body, name="pool_mid_bwd", grid=(nt,),
        in_specs=[pl.BlockSpec((tm, 2 * e), rev), pl.BlockSpec((PHALO, 2 * e), halo),
                  pl.BlockSpec((tm, e), rev), wspec, vec, vec],
        out_specs=[pl.BlockSpec((tm, 2 * e), rev), wspec, vec, vec],
        out_shape=[jax.ShapeDtypeStruct((t, 2 * e), BF16), jax.ShapeDtypeStruct((ng, gc, gc), F32),
                   jax.ShapeDtypeStruct((1, e), F32), jax.ShapeDtypeStruct((1, e), F32)],
        scratch_shapes=[pltpu.VMEM((tm + PHALO, e), F32), pltpu.VMEM((tm, e), F32),
                        pltpu.VMEM((tm + PHALO, e), F32), pltpu.VMEM((tm, e), F32)],
        compiler_params=_params("arbitrary"),
    )(p, p, ds, wg, bg, sc)


def _dhn_rms_bwd(dp, w4, h, g, dh_out, name):
    t, d = h.shape
    nk = w4.shape[-1]
    tm = min(MATMUL_TILE, t)

    def body(dp_ref, w_ref, h_ref, g_ref, dho_ref, dh_ref, dg_ref):
        i = pl.program_id(0)

        @pl.when(i == 0)
        def _():
            dg_ref[...] = jnp.zeros_like(dg_ref)

        dhn = jnp.zeros((tm, d), F32)
        for k in range(N_CHIPS):
            dhn = dhn + lax.dot_general(dp_ref[:, k * nk:(k + 1) * nk], w_ref[k], (((1,), (1,)), ((), ())),
                                        preferred_element_type=F32)
        hh = h_ref[...]
        r = lax.rsqrt(jnp.mean(hh * hh, axis=-1, keepdims=True) + RMS_EPS)
        hhat = hh * r
        tt = dhn * g_ref[...]
        dh_ref[...] = dho_ref[...] + r * (tt - hhat * jnp.mean(tt * hhat, axis=-1, keepdims=True))
        dg_ref[...] += jnp.sum(dhn * hhat, axis=0, keepdims=True)

    return pl.pallas_call(
        body, name=name, grid=(t // tm,),
        in_specs=[pl.BlockSpec((tm, N_CHIPS * nk), lambda i: (i, 0)),
                  _resident((N_CHIPS, d, nk)),
                  pl.BlockSpec((tm, d), lambda i: (i, 0)), pl.BlockSpec((1, d), lambda i: (0, 0)),
                  pl.BlockSpec((tm, d), lambda i: (i, 0))],
        out_specs=[pl.BlockSpec((tm, d), lambda i: (i, 0)), pl.BlockSpec((1, d), lambda i: (0, 0))],
        out_shape=[jax.ShapeDtypeStruct((t, d), F32), jax.ShapeDtypeStruct((1, d), F32)],
        compiler_params=_params("arbitrary"),
    )(dp, w4, h, g, dh_out)


def _wgrad(a, b, nblk, name):
    t, m = a.shape
    nn = b.shape[1] // nblk
    tk = min(WGRAD_TILE, t)
    nk = t // tk

    def body(a_ref, b_ref, o_ref, ob_ref):
        @pl.when(pl.program_id(1) == 0)
        def _():
            o_ref[...] = jnp.zeros_like(o_ref)

        o_ref[...] += lax.dot_general(a_ref[...].astype(BF16), b_ref[...].astype(BF16),
                                      (((0,), (0,)), ((), ())), preferred_element_type=F32)

        @pl.when(pl.program_id(1) == nk - 1)
        def _():
            ob_ref[...] = o_ref[...].astype(BF16)

    ospec = pl.BlockSpec((None, m, nn), lambda j, i: (j, 0, 0))
    return pl.pallas_call(
        body, name=name, grid=(nblk, nk),
        in_specs=[pl.BlockSpec((tk, m), lambda j, i: (i, 0)), pl.BlockSpec((tk, nn), lambda j, i: (i, j))],
        out_specs=[ospec, ospec],
        out_shape=[jax.ShapeDtypeStruct((nblk, m, nn), F32), jax.ShapeDtypeStruct((nblk, m, nn), BF16)],
        compiler_params=_params("parallel", "arbitrary"),
    )(a, b)


def _rows2d(shape):
    rows = 1
    for s in shape[:-1]:
        rows *= s
    return rows, shape[-1]


def _row_tile(rows):
    for cand in (512, 256, 128, 64, 32, 16, 8):
        if rows % cand == 0:
            return cand
    return rows


def _add_eight(own, landed, chip_core, name):
    _, _, rows, cols = own.shape
    tr = _row_tile(rows)

    def body(sel_ref, s_ref, r_ref, o_ref):
        acc = s_ref[...]
        for peer in range(N_DEV - 1):
            acc = acc + r_ref[peer].astype(F32)
        o_ref[...] = acc

    return pl.pallas_call(
        body, name=name,
        grid_spec=pltpu.PrefetchScalarGridSpec(
            num_scalar_prefetch=1, grid=(rows // tr,),
            in_specs=[pl.BlockSpec((None, None, tr, cols), lambda i, s: (s[0], s[1], i, 0)),
                      pl.BlockSpec((N_DEV - 1, tr, cols), lambda i, s: (0, i, 0))],
            out_specs=pl.BlockSpec((None, tr, cols), lambda i, s: (s[1], i, 0))),
        out_shape=jax.ShapeDtypeStruct((2, rows, cols), F32),
        compiler_params=_params("parallel"),
    )(chip_core, own, landed)


def _adamw(w, g, m, v, name):
    rows, cols = w.shape
    tr = _row_tile(rows)

    def body(w_ref, g_ref, m_ref, v_ref, d_ref, m2_ref, v2_ref):
        gg = g_ref[...]
        m2 = ADAM_B1 * m_ref[...] + (1.0 - ADAM_B1) * gg
        v2 = ADAM_B2 * v_ref[...] + (1.0 - ADAM_B2) * (gg * gg)
        m_hat = m2 / (1.0 - ADAM_B1 ** ADAM_STEP)
        v_hat = v2 / (1.0 - ADAM_B2 ** ADAM_STEP)
        d_ref[...] = -ADAM_LR * (m_hat / (jnp.sqrt(v_hat) + ADAM_EPS) + ADAM_WD * w_ref[...])
        m2_ref[...] = m2
        v2_ref[...] = v2

    spec = pl.BlockSpec((tr, cols), lambda i: (i, 0))
    shp = jax.ShapeDtypeStruct((rows, cols), F32)
    return pl.pallas_call(
        body, name=name, grid=(rows // tr,),
        in_specs=[spec, spec, spec, spec], out_specs=[spec, spec, spec], out_shape=[shp, shp, shp],
        compiler_params=_params("parallel"),
    )(w, g, m, v)


ANY = pl.BlockSpec(memory_space=pl.ANY)


def _place():
    x, y, c = lax.axis_index("x"), lax.axis_index("y"), lax.axis_index("c")
    chips = [(1 - x, y), (x, 1 - y), (1 - x, 1 - y)]
    return x, y, c, chips


def _allgather_weights(shards):
    n = len(shards)

    def body(*refs):
        ins, outs = refs[:n], refs[n:2 * n]
        send_ici, recv_ici, send_d2d, recv_d2d = refs[2 * n:]
        x, y, c, chips = _place()
        k0 = 2 * x + y
        sib = (x, y, 1 - c)

        def ici(a, r, src_chip, target):
            return pltpu.make_async_remote_copy(
                src_ref=ins[a].at[c], dst_ref=outs[a].at[src_chip, c],
                send_sem=send_ici.at[a * 3 + r], recv_sem=recv_ici.at[a * 3 + r],
                device_id=target, device_id_type=MESH)

        def d2d(a, r, src_chip, layer):
            return pltpu.make_async_remote_copy(
                src_ref=outs[a].at[src_chip, layer], dst_ref=outs[a].at[src_chip, layer],
                send_sem=send_d2d.at[a * 3 + r], recv_sem=recv_d2d.at[a * 3 + r],
                device_id=sib, device_id_type=MESH)

        first = [ici(a, r, k0, (cx, cy, c)) for a in range(n) for r, (cx, cy) in enumerate(chips)]
        for cp in first:
            cp.start()
        passed = []
        for a in range(n):
            for r, (cx, cy) in enumerate(chips):
                ici(a, r, 2 * cx + cy, (cx, cy, c)).wait_recv()
                cp = d2d(a, r, 2 * cx + cy, c)
                cp.start()
                passed.append(cp)
        for a in range(n):
            for r, (cx, cy) in enumerate(chips):
                d2d(a, r, 2 * cx + cy, 1 - c).wait_recv()
        for cp in first + passed:
            cp.wait_send()

    return pl.pallas_call(
        body, name="allgather_weights",
        in_specs=[ANY] * n, out_specs=[ANY] * n,
        out_shape=[jax.ShapeDtypeStruct((N_CHIPS,) + s.shape, s.dtype) for s in shards],
        scratch_shapes=[pltpu.SemaphoreType.DMA((3 * n,)), pltpu.SemaphoreType.DMA((3 * n,)),
                        pltpu.SemaphoreType.DMA((3 * n,)), pltpu.SemaphoreType.DMA((3 * n,))],
    )(*shards)


def _put_own(gathered, shard, chip):
    return lax.dynamic_update_slice_in_dim(gathered, shard[None], chip, axis=0)


HBM = pl.BlockSpec(memory_space=pltpu.HBM)
SEM = pl.BlockSpec(memory_space=pltpu.SEMAPHORE)
DATAFLOW = pltpu.SideEffectType.DATAFLOW_SIDE_EFFECTING
FLIPS = [(0, 0, 1), (0, 1, 0), (0, 1, 1), (1, 0, 0), (1, 0, 1), (1, 1, 0), (1, 1, 1)]


def _gather_plan(srcs, lands, send_sem, recv_sem):
    x, y, c, chips = _place()
    return [pltpu.make_async_remote_copy(
        src_ref=srcs[a], dst_ref=lands[a].at[2 * x + y],
        send_sem=send_sem.at[a * 3 + r], recv_sem=recv_sem.at[a * 3 + r],
        device_id=(cx, cy, c), device_id_type=MESH)
        for a in range(len(srcs)) for r, (cx, cy) in enumerate(chips)]


def _scatter_plan(srcs, lands, send_sem, recv_sem):
    x, y, c, _ = _place()
    cps = []
    for a in range(len(srcs)):
        for r, (fx, fy, fc) in enumerate(FLIPS):
            tx, ty, tc = (1 - x if fx else x), (1 - y if fy else y), (1 - c if fc else c)
            cps.append(pltpu.make_async_remote_copy(
                src_ref=srcs[a].at[2 * tx + ty, tc], dst_ref=lands[a].at[r],
                send_sem=send_sem.at[a * len(FLIPS) + r], recv_sem=recv_sem.at[a * len(FLIPS) + r],
                device_id=(tx, ty, tc), device_id_type=MESH))
    return cps


def _split_start(name, plan, srcs, lands, n_copies, after):
    n = len(srcs)

    def body(*refs):
        src, land = refs[:n], refs[n:2 * n]
        send_sem, recv_sem = refs[2 * n + 1], refs[2 * n + 2]
        token = refs[-1]
        for cp in plan(src, land, send_sem, recv_sem):
            cp.start()
        token[...] = jnp.zeros_like(token)

    outs = pl.pallas_call(
        body, name=name,
        in_specs=[HBM] * (2 * n) + [ANY],
        out_specs=[SEM, SEM] + [HBM] * (2 * n) + [pl.BlockSpec(memory_space=pltpu.VMEM)],
        out_shape=[pltpu.SemaphoreType.DMA((n_copies,)), pltpu.SemaphoreType.DMA((n_copies,))]
        + [pltpu.HBM(s.shape, s.dtype) for s in srcs] + [pltpu.HBM(l.shape, l.dtype) for l in lands]
        + [jax.ShapeDtypeStruct((8, LANES), F32)],
        input_output_aliases={i: 2 + i for i in range(2 * n)},
        compiler_params=pltpu.CompilerParams(has_side_effects=DATAFLOW),
    )(*[pltpu.with_memory_space_constraint(s, pltpu.HBM) for s in srcs],
      *[pltpu.with_memory_space_constraint(l, pltpu.HBM) for l in lands], after)
    return outs[0], outs[1], list(outs[2:2 + n]), list(outs[2 + n:2 + 2 * n]), outs[-1]


def _split_wait(name, plan, send_sems, recv_sems, srcs, lands, after):
    n = len(srcs)

    def body(*refs):
        src, land = refs[:n], refs[n:2 * n]
        send_sem, recv_sem = refs[2 * n], refs[2 * n + 1]
        for cp in plan(src, land, send_sem, recv_sem):
            cp.wait_send()
            cp.wait_recv()

    outs = pl.pallas_call(
        body, name=name,
        in_specs=[HBM] * (2 * n) + [SEM, SEM, ANY],
        out_specs=[HBM] * (2 * n),
        out_shape=[pltpu.HBM(s.shape, s.dtype) for s in srcs] + [pltpu.HBM(l.shape, l.dtype) for l in lands],
        input_output_aliases={i: i for i in range(2 * n)},
        compiler_params=pltpu.CompilerParams(has_side_effects=DATAFLOW),
    )(*srcs, *lands, send_sems, recv_sems, after)
    return list(outs[:n]), list(outs[n:])


def _share_halves(halves):
    n = len(halves)

    def body(*refs):
        ins, outs = refs[:n], refs[n:2 * n]
        send_sem, recv_sem = refs[2 * n:]
        x, y, c, _ = _place()
        cps = [pltpu.make_async_remote_copy(
            src_ref=outs[a].at[c], dst_ref=outs[a].at[c], send_sem=send_sem.at[a], recv_sem=recv_sem.at[a],
            device_id=(x, y, 1 - c), device_id_type=MESH) for a in range(n)]
        for cp in cps:
            cp.start()
        for cp in cps:
            cp.wait()

    return pl.pallas_call(
        body, name="share_halves",
        in_specs=[ANY] * n, out_specs=[ANY] * n,
        out_shape=[jax.ShapeDtypeStruct(h.shape, h.dtype) for h in halves],
        input_output_aliases={a: a for a in range(n)},
        scratch_shapes=[pltpu.SemaphoreType.DMA((n,)), pltpu.SemaphoreType.DMA((n,))],
    )(*halves)


N_DEV = 8


def _allreduce_small(v):
    m, nc = v.shape

    def body(x_ref, out_ref, gat, send_sems, recv_sems, local_sem):
        x, y, c, chips = _place()
        me, sib = (x, y, c), (x, y, 1 - c)

        def rows(px, py, pc):
            return gat.at[pl.ds((4 * px + 2 * py + pc) * m, m), :]

        def copy(k, block, to, src=None):
            return pltpu.make_async_remote_copy(
                src_ref=rows(*block) if src is None else src, dst_ref=rows(*block),
                send_sem=send_sems.at[k], recv_sem=recv_sems.at[k], device_id=to, device_id_type=MESH)

        mine = pltpu.make_async_copy(x_ref, rows(*me), local_sem)
        mine.start()
        first = [copy(0, me, sib, src=x_ref)]
        first += [copy(1 + j, me, (*chip, c), src=x_ref) for j, chip in enumerate(chips)]
        for cp in first:
            cp.start()
        passed = [copy(4 + j, (*chip, c), sib) for j, chip in enumerate(chips)]
        for j, chip in enumerate(chips):
            copy(1 + j, (*chip, c), me).wait_recv()
            passed[j].start()
        copy(0, sib, me).wait_recv()
        for j, chip in enumerate(chips):
            copy(4 + j, (*chip, 1 - c), me).wait_recv()
        for cp in first + passed:
            cp.wait_send()
        mine.wait()
        acc = gat[0:m, :]
        for dev in range(1, N_DEV):
            acc = acc + gat[dev * m:(dev + 1) * m, :]
        out_ref[...] = acc

    return pl.pallas_call(
        body, name="allreduce_small",
        in_specs=[pl.BlockSpec(memory_space=pltpu.VMEM)],
        out_specs=pl.BlockSpec(memory_space=pltpu.VMEM),
        out_shape=jax.ShapeDtypeStruct((m, nc), F32),
        scratch_shapes=[pltpu.VMEM((N_DEV * m, nc), F32), pltpu.SemaphoreType.DMA((7,)),
                        pltpu.SemaphoreType.DMA((7,)), pltpu.SemaphoreType.DMA],
        compiler_params=pltpu.CompilerParams(vmem_limit_bytes=VMEM_LIMIT),
    )(v)


def _pad_rows(a, rows):
    return jnp.pad(a, ((0, rows - a.shape[0]), (0, 0)))


def kernel(x, norm_g, final_g, conv_w_in, conv_dw, conv_dw_b, conv_ln_g, conv_ln_b, conv_w_out, pool_w_in, pool_w_grp, pool_b_grp, pool_scale, pool_w_out, loss_target, m_norm_g, m_final_g, m_conv_w_in, m_conv_dw, m_conv_dw_b, m_conv_ln_g, m_conv_ln_b, m_conv_w_out, m_pool_w_in, m_pool_w_grp, m_pool_b_grp, m_pool_scale, m_pool_w_out, v_norm_g, v_final_g, v_conv_w_in, v_conv_dw, v_conv_dw_b, v_conv_ln_g, v_conv_ln_b, v_conv_w_out, v_pool_w_in, v_pool_w_grp, v_pool_b_grp, v_pool_scale, v_pool_w_out):
    t, d = x.shape[1], x.shape[2]
    e = conv_w_out.shape[2]
    ng = len(POOL_WINDOWS)
    gc = e // ng
    gcs = pool_w_grp.shape[2]
    ck = conv_dw.shape[1]
    es = conv_dw.shape[2]
    xi, yi, ci = lax.axis_index("x"), lax.axis_index("y"), lax.axis_index("c")
    chip = 2 * xi + yi

    small_rows = ck + 2
    small_pad = -(-small_rows // 8) * 8
    small = jnp.concatenate([conv_dw, pool_b_grp[:, None, :], pool_scale[:, None, :],
                             jnp.zeros((2, small_pad - small_rows, es), F32)], axis=1)
    cwi_b, cwo_b = conv_w_in.astype(BF16), conv_w_out.astype(BF16)

    def halves(a):
        return a.reshape(2, a.shape[0] // 2, a.shape[1])

    first = [halves(cwi_b[0]), halves(cwo_b[0]), small]
    g_cwi0, g_cwo0, g_small = [_put_own(g, q, chip) for g, q in zip(_allgather_weights(first), first)]
    rest = [cwi_b[1], cwo_b[1], pool_w_in.astype(BF16), pool_w_grp.astype(BF16), pool_w_out.astype(BF16)]
    rest_lands = [lax.empty((N_CHIPS,) + r.shape, r.dtype) for r in rest]
    ag_send, ag_recv, rest, rest_lands, ag_token = _split_start(
        "gather_rest_start", _gather_plan, rest, rest_lands, 3 * len(rest), g_small)
    smallf = jnp.transpose(g_small, (1, 2, 0, 3)).reshape(2, small_pad, N_CHIPS * es)

    h = x.reshape(t, d)
    tgt = loss_target.reshape(t, d)
    hs, saved = [], []
    for layer in range(4):
        j = layer // 2
        hs.append(h)
        gvec = norm_g[layer][None, :]
        if layer == 0:
            gvec = gvec + ag_token[0:1, 0:1]
        if layer == 1:
            rest, rest_lands = _split_wait("gather_rest_wait", _gather_plan, ag_send, ag_recv, rest, rest_lands, h)
            g_cwi1, g_cwo1, g_pwi, g_pwg, g_pwo = [_put_own(g, q, chip) for g, q in zip(rest_lands, rest)]
            wg_full = jnp.transpose(g_pwg, (1, 2, 0, 3, 4)).reshape(2, ng, gc, gc)
        if layer % 2 == 0:
            g_in, g_out = (g_cwi0, g_cwo0) if j == 0 else (g_cwi1, g_cwo1)
            w_in4 = g_in.reshape(N_CHIPS, d, -1)
            w_out = g_out.reshape(e, d)
            dw_full = smallf[j, 0:ck]
            p, hn, s, c, h = _conv_fwd(h, gvec, w_in4, w_out, dw_full.reshape(-1, LANES),
                                       conv_dw_b[j].reshape(-1, LANES), conv_ln_g[j][None, :], conv_ln_b[j][None, :])
            saved.append((p, hn, s, c, w_in4, w_out, dw_full))
        else:
            w_in4 = g_pwi[:, j]
            w_out = g_pwo[:, j].reshape(e, d)
            p, hn = _rms_matmul(h, gvec, w_in4, "rms_matmul_pool")
            bg_full = smallf[j, ck:ck + 1]
            sc_full = smallf[j, ck + 1:ck + 2]
            s, h = _pool_mid_out_fwd(p, h, w_out, wg_full[j], bg_full, sc_full)
            saved.append((p, hn, s, None, w_in4, w_out, (wg_full[j], bg_full, sc_full)))

    dh, loss_part, dfg = _loss_head(h, final_g[None, :], tgt)
    loss = lax.psum(loss_part[0, 0], ("x", "y", "c"))

    def by_half(a):
        return a.reshape(N_CHIPS, 2, a.shape[1] // 2, a.shape[2])

    dng = [None] * 4
    g_conv = [None, None]
    g_pool = [None, None]
    flights = {}
    for layer in (3, 2, 1, 0):
        j = layer // 2
        p, hn, s, c, w_in4, w_out, extra = saved[layer]
        gvec = norm_g[layer][None, :]
        dw_out, dw_out_b = [q.reshape(N_CHIPS, e // N_CHIPS, d) for q in _wgrad(s, dh, 1, "wgrad_out")]
        if layer % 2 == 0:
            dp, dlg, dlb, ddwb, ddw3, dw_in = _conv_bwd(p, c, dh, hn, w_out, extra.reshape(-1, LANES),
                                                        conv_ln_g[j][None, :], conv_ln_b[j][None, :])
            ddw = ddw3.reshape(ck, e)
            dw_in_b = dw_in.astype(BF16)
            own, pay = [dw_in, dw_out], [dw_in_b, dw_out_b]
            g_conv[j] = (dlg, dlb, ddwb, ddw)
        else:
            wg, bg_full, sc_full = extra
            ds = _ds_matmul(dh, w_out)
            dp, dwg, dbg, dsc = _pool_mid_bwd(p, ds, wg, bg_full, sc_full)
            dw_in, dw_in_b = _wgrad(hn, dp, N_CHIPS, "wgrad_in_pool")
            dwg4 = jnp.transpose(dwg.reshape(ng, N_CHIPS, gcs, gc), (1, 0, 2, 3)).reshape(N_CHIPS, ng * gcs, gc)
            own, pay = [dw_in, dw_out, dwg4], [dw_in_b, dw_out_b, dwg4.astype(BF16)]
            g_pool[j] = (dbg, dsc)
        pay = [by_half(q) for q in pay]
        lands = [lax.empty((len(FLIPS),) + q.shape[2:], BF16) for q in pay]
        send, recv, pay, lands, token = _split_start(
            "scatter_start_%d" % layer, _scatter_plan, pay, lands, len(FLIPS) * len(pay), own[0])
        flights[layer] = (send, recv, pay, lands, own)
        kind = "dhn_rms_bwd_conv" if layer % 2 == 0 else "dhn_rms_bwd_pool"
        dh, dng[layer] = _dhn_rms_bwd(dp, w_in4, hs[layer], gvec + token[0:1, 0:1], dh, kind)
    grad_x = dh.reshape(x.shape)

    sel_kc = jnp.stack([chip, ci]).astype(jnp.int32)
    summed = {}
    after = dh
    for layer in (3, 2, 1, 0):
        send, recv, pay, lands, own = flights[layer]
        _, lands = _split_wait("scatter_wait_%d" % layer, _scatter_plan, send, recv, pay, lands, after)
        after = lands[0]
        summed[layer] = [_add_eight(by_half(o), l, sel_kc, "add_eight_%d_%d" % (layer % 2, a))
                         for a, (o, l) in enumerate(zip(own, lands))]
    order = [(0, 0), (0, 1), (2, 0), (2, 1), (1, 0), (1, 1), (1, 2), (3, 0), (3, 1), (3, 2)]
    shared = _share_halves([summed[l][a] for l, a in order])
    full = {la: q.reshape(q.shape[0] * q.shape[1], q.shape[2]) for la, q in zip(order, shared)}
    g_cwi_f = jnp.stack([full[(0, 0)], full[(2, 0)]])
    g_cwo_f = jnp.stack([full[(0, 1)], full[(2, 1)]])
    g_pwi_f = jnp.stack([full[(1, 0)], full[(3, 0)]])
    g_pwo_f = jnp.stack([full[(1, 1)], full[(3, 1)]])
    g_pwg_f = jnp.stack([full[(1, 2)], full[(3, 2)]]).reshape(pool_w_grp.shape)

    rows_list = [dng[0], dng[1], dng[2], dng[3], dfg,
                 g_conv[0][2], g_conv[1][2], g_conv[0][0], g_conv[1][0], g_conv[0][1], g_conv[1][1],
                 g_pool[0][0], g_pool[1][0], g_pool[0][1], g_pool[1][1], g_conv[0][3], g_conv[1][3]]
    slab = jnp.concatenate(rows_list, axis=0)
    nrows = slab.shape[0]
    slab = _pad_rows(slab, -(-nrows // 8) * 8)
    tot = _allreduce_small(slab)
    g_norm_g = tot[0:4]
    g_final_g = tot[4]
    g_dwb = tot[5:7]
    g_lng = tot[7:9]
    g_lnb = tot[9:11]
    g_bg = lax.dynamic_slice_in_dim(tot[11:13], chip * es, es, axis=1)
    g_sc = lax.dynamic_slice_in_dim(tot[13:15], chip * es, es, axis=1)
    g_dw = lax.dynamic_slice_in_dim(tot[15:15 + 2 * ck].reshape(2, ck, e), chip * es, es, axis=2)

    def adam_nd(w, g, m, v, nm):
        rows, cols = _rows2d(w.shape)
        outs = _adamw(w.reshape(rows, cols), g.reshape(rows, cols), m.reshape(rows, cols),
                      v.reshape(rows, cols), "adamw_" + nm)
        return [o.reshape(w.shape) for o in outs]

    res = {}
    res["conv_w_in"] = (g_cwi_f, *adam_nd(conv_w_in, g_cwi_f, m_conv_w_in, v_conv_w_in, "cwi"))
    res["conv_w_out"] = (g_cwo_f, *adam_nd(conv_w_out, g_cwo_f, m_conv_w_out, v_conv_w_out, "cwo"))
    res["pool_w_in"] = (g_pwi_f, *adam_nd(pool_w_in, g_pwi_f, m_pool_w_in, v_pool_w_in, "pwi"))
    res["pool_w_grp"] = (g_pwg_f, *adam_nd(pool_w_grp, g_pwg_f, m_pool_w_grp, v_pool_w_grp, "pwg"))
    res["pool_w_out"] = (g_pwo_f, *adam_nd(pool_w_out, g_pwo_f, m_pool_w_out, v_pool_w_out, "pwo"))

    def pack(parts, rows_to):
        return _pad_rows(jnp.concatenate([q.reshape(-1, q.shape[-1]) for q in parts], axis=0), rows_to)

    rep_w = [norm_g, final_g[None, :], conv_dw_b, conv_ln_g, conv_ln_b]
    rep_g = [g_norm_g, g_final_g[None, :], g_dwb, g_lng, g_lnb]
    rep_m = [m_norm_g, m_final_g[None, :], m_conv_dw_b, m_conv_ln_g, m_conv_ln_b]
    rep_v = [v_norm_g, v_final_g[None, :], v_conv_dw_b, v_conv_ln_g, v_conv_ln_b]
    rep = _adamw(pack(rep_w, 16), pack(rep_g, 16), pack(rep_m, 16), pack(rep_v, 16), "adamw_rep")
    rep_names = ["norm_g", "final_g", "conv_dw_b", "conv_ln_g", "conv_ln_b"]
    rep_rows = [(0, 4), (4, 5), (5, 7), (7, 9), (9, 11)]
    for nm, (lo, hi), gq, wq in zip(rep_names, rep_rows, rep_g, rep_w):
        shape = (d,) if nm == "final_g" else wq.shape
        res[nm] = (gq.reshape(shape), *[o[lo:hi].reshape(shape) for o in rep])

    sh_w = [conv_dw, pool_b_grp, pool_scale]
    sh_g = [g_dw, g_bg, g_sc]
    sh_m = [m_conv_dw, m_pool_b_grp, m_pool_scale]
    sh_v = [v_conv_dw, v_pool_b_grp, v_pool_scale]
    sh_total = 2 * ck + 4
    sh_pad = -(-sh_total // 8) * 8
    shd = _adamw(pack(sh_w, sh_pad), pack(sh_g, sh_pad), pack(sh_m, sh_pad), pack(sh_v, sh_pad), "adamw_shard")
    sh_names = ["conv_dw", "pool_b_grp", "pool_scale"]
    sh_rows = [(0, 2 * ck), (2 * ck, 2 * ck + 2), (2 * ck + 2, 2 * ck + 4)]
    for nm, (lo, hi), gq, wq in zip(sh_names, sh_rows, sh_g, sh_w):
        res[nm] = (gq.reshape(wq.shape), *[o[lo:hi].reshape(wq.shape) for o in shd])

    order = ["norm_g", "final_g", "conv_w_in", "conv_dw", "conv_dw_b", "conv_ln_g", "conv_ln_b", "conv_w_out",
             "pool_w_in", "pool_w_grp", "pool_b_grp", "pool_scale", "pool_w_out"]
    outs = [loss, grad_x]
    for part in range(4):
        outs += [res[nm][part] for nm in order]
    return tuple(outs)
```

```python
import functools

import jax
import jax.numpy as jnp
from jax import lax
from jax.experimental import pallas as pl
from jax.experimental.pallas import tpu as pltpu

F32 = jnp.float32
BF16 = jnp.bfloat16
MESH = pl.DeviceIdType.MESH

RMS_EPS = 1e-6
LN_EPS = 1e-5
CONV_K = 31
HALO = 32
PHALO = 16
POOL_WINDOWS = (2, 4, 8, 16)
N_CHIPS = 4
LANES = 128
ROW_CHUNK = 32
CONV_ROW_CHUNK = 32
FIR_BLOCK = 16
TOKEN_TILE = 512
MATMUL_TILE = 1024
WGRAD_TILE = 2048
VMEM_LIMIT = 56 * 1024 * 1024

ADAM_LR = 0.001
ADAM_B1 = 0.9
ADAM_B2 = 0.999
ADAM_EPS = 1e-08
ADAM_WD = 0.01
ADAM_STEP = 10


def _params(*sem):
    return pltpu.CompilerParams(dimension_semantics=sem, vmem_limit_bytes=VMEM_LIMIT)


def _sig(v):
    return 0.5 * jnp.tanh(0.5 * v) + 0.5


def _dsilu(v, sv):
    return sv * (1.0 + v * (1.0 - sv))


def _resident(shape):
    return pl.BlockSpec(shape, lambda *_: (0,) * len(shape), pipeline_mode=pl.Buffered(1))


def _tile(t):
    return min(TOKEN_TILE, t)


def _rms_matmul(h, g, w4, name):
    t, d = h.shape
    nk = w4.shape[-1]
    tm = min(MATMUL_TILE, t)

    def body(h_ref, g_ref, w_ref, p_ref, hn_ref):
        hh = h_ref[...]
        r = lax.rsqrt(jnp.mean(hh * hh, axis=-1, keepdims=True) + RMS_EPS)
        hn = (hh * r * g_ref[...]).astype(BF16)
        hn_ref[...] = hn
        for k in range(N_CHIPS):
            p_ref[:, k * nk:(k + 1) * nk] = jnp.dot(hn, w_ref[k], preferred_element_type=F32).astype(BF16)

    return pl.pallas_call(
        body, name=name, grid=(t // tm,),
        in_specs=[pl.BlockSpec((tm, d), lambda i: (i, 0)),
                  pl.BlockSpec((1, d), lambda i: (0, 0)),
                  _resident((N_CHIPS, d, nk))],
        out_specs=[pl.BlockSpec((tm, N_CHIPS * nk), lambda i: (i, 0)),
                   pl.BlockSpec((tm, d), lambda i: (i, 0))],
        out_shape=[jax.ShapeDtypeStruct((t, N_CHIPS * nk), BF16), jax.ShapeDtypeStruct((t, d), BF16)],
        compiler_params=_params("parallel"),
    )(h, g, w4)


def _to_token_tiles(ref, tok0, rows, val, ng):
    for j in range(ng):
        ref[pl.ds(tok0 * ng + j, rows, stride=ng), :] = val[:, j * LANES:(j + 1) * LANES]


def _from_token_tiles(ref, tok0, rows, ng):
    return jnp.concatenate([ref[pl.ds(tok0 * ng + j, rows, stride=ng), :] for j in range(ng)], axis=1)


def _conv_mid_fwd(p, dw3, dwb3, lg, lb):
    t = p.shape[0]
    e = p.shape[1] // 3
    ng = e // LANES
    tm = _tile(t)
    rc = CONV_ROW_CHUNK
    fb = FIR_BLOCK

    def body(p_ref, dw_ref, dwb_ref, lg_ref, lb_ref, s_ref, c_ref, u3, c3):
        i = pl.program_id(0)

        @pl.when(i == 0)
        def _():
            u3[0:HALO * ng, :] = jnp.zeros((HALO * ng, LANES), F32)

        def glu(rci, carry):
            base = pl.multiple_of(rci * rc, rc)
            a = p_ref[pl.ds(base, rc), 0:e].astype(F32)
            b = p_ref[pl.ds(base, rc), e:2 * e].astype(F32)
            _to_token_tiles(u3, HALO + base, rc, a * _sig(b), ng)
            return carry

        lax.fori_loop(0, tm // rc, glu, 0)

        def fir(bi, carry):
            t0 = bi * fb
            def x(q):
                return u3[pl.ds(pl.multiple_of((t0 + HALO - (CONV_K - 1) + q) * ng, ng), ng), :]

            xs = [x(q) for q in range(fb - 1)]
            accs = [dwb_ref[...]] * fb
            for k in range(CONV_K):
                wk = dw_ref[k * ng:(k + 1) * ng, :]
                xs.append(x(k + fb - 1))
                accs = [accs[q] + wk * xs[q + k] for q in range(fb)]
            for q in range(fb):
                c3[pl.ds(pl.multiple_of((t0 + q) * ng, ng), ng), :] = accs[q]
            return carry

        lax.fori_loop(0, tm // fb, fir, 0)
        u3[0:HALO * ng, :] = u3[tm * ng:(tm + HALO) * ng, :]

        def chunk(rci, carry):
            base = pl.multiple_of(rci * rc, rc)
            c = _from_token_tiles(c3, base, rc, ng)
            mu = jnp.mean(c, axis=-1, keepdims=True)
            cc = c - mu
            var = jnp.mean(cc * cc, axis=-1, keepdims=True)
            ln = cc * lax.rsqrt(var + LN_EPS) * lg_ref[...] + lb_ref[...]
            z = p_ref[pl.ds(base, rc), 2 * e:3 * e].astype(F32)
            s = (ln * _sig(ln)) * (z * _sig(z))
            s_ref[pl.ds(base, rc), :] = s.astype(BF16)
            c_ref[pl.ds(base, rc), :] = c.astype(BF16)
            return carry

        lax.fori_loop(0, tm // rc, chunk, 0, unroll=2)

    vec = pl.BlockSpec((1, e), lambda i: (0, 0))
    return pl.pallas_call(
        body, name="conv_mid_fwd", grid=(t // tm,),
        in_specs=[pl.BlockSpec((tm, 3 * e), lambda i: (i, 0)),
                  pl.BlockSpec((CONV_K * ng, LANES), lambda i: (0, 0)),
                  pl.BlockSpec((ng, LANES), lambda i: (0, 0)), vec, vec],
        out_specs=[pl.BlockSpec((tm, e), lambda i: (i, 0)), pl.BlockSpec((tm, e), lambda i: (i, 0))],
        out_shape=[jax.ShapeDtypeStruct((t, e), BF16), jax.ShapeDtypeStruct((t, e), BF16)],
        scratch_shapes=[pltpu.VMEM(((tm + HALO) * ng, LANES), F32), pltpu.VMEM((tm * ng, LANES), F32)],
        compiler_params=_params("arbitrary"),
    )(p, dw3, dwb3, lg, lb)


def _conv_mid_out_fwd(p, h, w_out, dw3, dwb3, lg, lb):
    t = p.shape[0]
    e = p.shape[1] // 3
    d = h.shape[1]
    ng = e // LANES
    tm = _tile(t)
    nt = t // tm
    rc = CONV_ROW_CHUNK
    fb = FIR_BLOCK

    def body(p_ref, h_ref, w_ref, dw_ref, dwb_ref, lg_ref, lb_ref, s_ref, c_ref, ho_ref, u3, c3, s_prev):
        i = pl.program_id(0)

        def project():
            ho_ref[...] = h_ref[...] + jnp.dot(s_prev[...], w_ref[...], preferred_element_type=F32)

        @pl.when(i == 0)
        def _():
            u3[0:HALO * ng, :] = jnp.zeros((HALO * ng, LANES), F32)
            s_prev[...] = jnp.zeros_like(s_prev)

        @pl.when(i < nt)
        def _():
            def glu(rci, carry):
                base = pl.multiple_of(rci * rc, rc)
                a = p_ref[pl.ds(base, rc), 0:e].astype(F32)
                b = p_ref[pl.ds(base, rc), e:2 * e].astype(F32)
                _to_token_tiles(u3, HALO + base, rc, a * _sig(b), ng)
                return carry

            lax.fori_loop(0, tm // rc, glu, 0)

            project()
            for bi in range(tm // fb):
                t0 = bi * fb
                xs = [u3[(t0 + HALO - (CONV_K - 1) + q) * ng:(t0 + HALO - (CONV_K - 1) + q + 1) * ng, :]
                      for q in range(fb - 1)]
                accs = [dwb_ref[...]] * fb
                for k in range(CONV_K):
                    wk = dw_ref[k * ng:(k + 1) * ng, :]
                    q1 = t0 + HALO - (CONV_K - 1) + k + fb - 1
                    xs.append(u3[q1 * ng:(q1 + 1) * ng, :])
                    accs = [accs[q] + wk * xs[q + k] for q in range(fb)]
                for q in range(fb):
                    c3[(t0 + q) * ng:(t0 + q + 1) * ng, :] = accs[q]
            u3[0:HALO * ng, :] = u3[tm * ng:(tm + HALO) * ng, :]

            def chunk(rci, carry):
                base = pl.multiple_of(rci * rc, rc)
                c = _from_token_tiles(c3, base, rc, ng)
                mu = jnp.mean(c, axis=-1, keepdims=True)
                cc = c - mu
                var = jnp.mean(cc * cc, axis=-1, keepdims=True)
                ln = cc * lax.rsqrt(var + LN_EPS) * lg_ref[...] + lb_ref[...]
                z = p_ref[pl.ds(base, rc), 2 * e:3 * e].astype(F32)
                s = ((ln * _sig(ln)) * (z * _sig(z))).astype(BF16)
                s_ref[pl.ds(base, rc), :] = s
                s_prev[pl.ds(base, rc), :] = s
                c_ref[pl.ds(base, rc), :] = c.astype(BF16)
                return carry

            lax.fori_loop(0, tm // rc, chunk, 0, unroll=2)

        @pl.when(i == nt)
        def _():
            project()

    vec = pl.BlockSpec((1, e), lambda i: (0, 0))
    cur = lambda i: (jnp.minimum(i, nt - 1), 0)
    lag = lambda i: (jnp.maximum(i - 1, 0), 0)
    return pl.pallas_call(
        body, name="conv_mid_out_fwd", grid=(nt + 1,),
        in_specs=[pl.BlockSpec((tm, 3 * e), cur), pl.BlockSpec((tm, d), lag), _resident((e, d)),
                  pl.BlockSpec((CONV_K * ng, LANES), lambda i: (0, 0)),
                  pl.BlockSpec((ng, LANES), lambda i: (0, 0)), vec, vec],
        out_specs=[pl.BlockSpec((tm, e), cur), pl.BlockSpec((tm, e), cur), pl.BlockSpec((tm, d), lag)],
        out_shape=[jax.ShapeDtypeStruct((t, e), BF16), jax.ShapeDtypeStruct((t, e), BF16),
                   jax.ShapeDtypeStruct((t, d), F32)],
        scratch_shapes=[pltpu.VMEM(((tm + HALO) * ng, LANES), F32), pltpu.VMEM((tm * ng, LANES), F32),
                        pltpu.VMEM((tm, e), BF16)],
        compiler_params=_params("arbitrary"),
    )(p, h, w_out, dw3, dwb3, lg, lb)


def _conv_fwd(h, g, w_in4, w_out, dw3, dwb3, lg, lb):
    t, d = h.shape
    nk = w_in4.shape[-1]
    e = N_CHIPS * nk // 3
    ng = e // LANES
    tm = _tile(t)
    nt = t // tm
    rc = CONV_ROW_CHUNK
    fb = FIR_BLOCK

    def body(hl_ref, hg_ref, g_ref, wi_ref, wo_ref, dw_ref, dwb_ref, lg_ref, lb_ref,
             p_ref, hn_ref, s_ref, c_ref, ho_ref, u3, c3, s_prev, p_s, hn_s):
        q = pl.program_id(0)
        nxt = q % 2
        cur = 1 - nxt

        def normed():
            hh = hl_ref[...]
            r = lax.rsqrt(jnp.mean(hh * hh, axis=-1, keepdims=True) + RMS_EPS)
            hn = (hh * r * g_ref[...]).astype(BF16)
            hn_s[nxt] = hn
            return hn

        def in_project(hn, k):
            p_s[nxt, :, k * nk:(k + 1) * nk] = jnp.dot(hn, wi_ref[k], preferred_element_type=F32).astype(BF16)

        def out_project():
            ho_ref[...] = hg_ref[...] + jnp.dot(s_prev[...], wo_ref[...], preferred_element_type=F32)

        @pl.when(q == 0)
        def _():
            u3[0:HALO * ng, :] = jnp.zeros((HALO * ng, LANES), F32)
            s_prev[...] = jnp.zeros_like(s_prev)
            hn0 = normed()
            for k in range(N_CHIPS):
                in_project(hn0, k)

        @pl.when(jnp.logical_and(q >= 1, q <= nt))
        def _():
            def glu(rci, carry):
                base = pl.multiple_of(rci * rc, rc)
                rows = pl.ds(base, rc)
                pa = p_s[cur, rows, 0:e]
                pb = p_s[cur, rows, e:2 * e]
                p_ref[rows, 0:e] = pa
                p_ref[rows, e:2 * e] = pb
                hn_ref[rows, :] = hn_s[cur, rows, :]
                _to_token_tiles(u3, HALO + base, rc, pa.astype(F32) * _sig(pb.astype(F32)), ng)
                return carry

            lax.fori_loop(0, tm // rc, glu, 0)

            nfir = tm // fb
            spots = {(m + 1) * nfir // (N_CHIPS + 2): m for m in range(N_CHIPS + 1)}
            hn = normed()
            for bi in range(nfir):
                if bi in spots:
                    if spots[bi] < N_CHIPS:
                        in_project(hn, spots[bi])
                    else:
                        out_project()
                t0 = bi * fb
                xs = [u3[(t0 + HALO - (CONV_K - 1) + j) * ng:(t0 + HALO - (CONV_K - 1) + j + 1) * ng, :]
                      for j in range(fb - 1)]
                accs = [dwb_ref[...]] * fb
                for k in range(CONV_K):
                    wk = dw_ref[k * ng:(k + 1) * ng, :]
                    j1 = t0 + HALO - (CONV_K - 1) + k + fb - 1
                    xs.append(u3[j1 * ng:(j1 + 1) * ng, :])
                    accs = [accs[j] + wk * xs[j + k] for j in range(fb)]
                for j in range(fb):
                    c3[(t0 + j) * ng:(t0 + j + 1) * ng, :] = accs[j]
            u3[0:HALO * ng, :] = u3[tm * ng:(tm + HALO) * ng, :]

            def chunk(rci, carry):
                base = pl.multiple_of(rci * rc, rc)
                rows = pl.ds(base, rc)
                c = _from_token_tiles(c3, base, rc, ng)
                mu = jnp.mean(c, axis=-1, keepdims=True)
                cc = c - mu
                var = jnp.mean(cc * cc, axis=-1, keepdims=True)
                ln = cc * lax.rsqrt(var + LN_EPS) * lg_ref[...] + lb_ref[...]
                pz = p_s[cur, rows, 2 * e:3 * e]
                p_ref[rows, 2 * e:3 * e] = pz
                z = pz.astype(F32)
                s = ((ln * _sig(ln)) * (z * _sig(z))).astype(BF16)
                s_ref[rows, :] = s
                s_prev[rows, :] = s
                c_ref[rows, :] = c.astype(BF16)
                return carry

            lax.fori_loop(0, tm // rc, chunk, 0, unroll=2)

        @pl.when(q == nt + 1)
        def _():
            out_project()

    vec = pl.BlockSpec((1, e), lambda q: (0, 0))
    lead = lambda q: (jnp.clip(q, 0, nt - 1), 0)
    cur_map = lambda q: (jnp.clip(q - 1, 0, nt - 1), 0)
    lag = lambda q: (jnp.clip(q - 2, 0, nt - 1), 0)
    return pl.pallas_call(
        body, name="conv_fwd", grid=(nt + 2,),
        in_specs=[pl.BlockSpec((tm, d), lead), pl.BlockSpec((tm, d), lag), pl.BlockSpec((1, d), lambda q: (0, 0)),
                  _resident((N_CHIPS, d, nk)), _resident((e, d)),
                  pl.BlockSpec((CONV_K * ng, LANES), lambda q: (0, 0)),
                  pl.BlockSpec((ng, LANES), lambda q: (0, 0)), vec, vec],
        out_specs=[pl.BlockSpec((tm, 3 * e), cur_map), pl.BlockSpec((tm, d), cur_map),
                   pl.BlockSpec((tm, e), cur_map), pl.BlockSpec((tm, e), cur_map), pl.BlockSpec((tm, d), lag)],
        out_shape=[jax.ShapeDtypeStruct((t, 3 * e), BF16), jax.ShapeDtypeStruct((t, d), BF16),
                   jax.ShapeDtypeStruct((t, e), BF16), jax.ShapeDtypeStruct((t, e), BF16),
                   jax.ShapeDtypeStruct((t, d), F32)],
        scratch_shapes=[pltpu.VMEM(((tm + HALO) * ng, LANES), F32), pltpu.VMEM((tm * ng, LANES), F32),
                        pltpu.VMEM((tm, e), BF16), pltpu.VMEM((2, tm, 3 * e), BF16),
                        pltpu.VMEM((2, tm, d), BF16)],
        compiler_params=_params("arbitrary"),
    )(h, h, g, w_in4, w_out, dw3, dwb3, lg, lb)


def _pool_group(lc, e):
    return (lc * LANES) // (e // len(POOL_WINDOWS))


def _pool_inv_count(row0, rows, w):
    tpos = row0 + lax.broadcasted_iota(jnp.int32, (rows, 1), 0)
    return 1.0 / jnp.minimum(tpos + 1, w).astype(F32)


def _pool_window_dev(ubuf, base, rc, e, row0, dbuf):
    n = rc + PHALO
    for lc in range(e // LANES):
        lanes = slice(lc * LANES, (lc + 1) * LANES)
        g = _pool_group(lc, e)
        w = POOL_WINDOWS[g]
        blk = ubuf[pl.ds(base, n), lanes]
        acc = blk
        step = 1
        while step < w:
            acc = acc + pltpu.roll(acc, step, 0)
            step *= 2
        win = acc[PHALO:n]
        tok = blk[PHALO:n]
        dbuf[pl.ds(base, rc), lanes] = win * _pool_inv_count(row0 + base, rc, w) - tok


def _pool_mid_out_fwd(p, h, w_out, wg, bg, sc):
    t = p.shape[0]
    e = p.shape[1] // 2
    d = h.shape[1]
    gc = e // len(POOL_WINDOWS)
    tm = _tile(t)
    nt = t // tm
    rc = ROW_CHUNK

    def body(p_ref, h_ref, w_ref, wg_ref, bg_ref, sc_ref, s_ref, ho_ref, ubuf, dbuf, s_prev):
        i = pl.program_id(0)

        def project():
            ho_ref[...] = h_ref[...] + jnp.dot(s_prev[...], w_ref[...], preferred_element_type=F32)

        @pl.when(i == 0)
        def _():
            ubuf[0:PHALO, :] = jnp.zeros((PHALO, e), F32)
            s_prev[...] = jnp.zeros_like(s_prev)

        @pl.when(i < nt)
        def _():
            ubuf[PHALO:PHALO + tm, :] = p_ref[:, 0:e].astype(F32)

            def chunk(rci, carry):
                base = pl.multiple_of(rci * rc, rc)
                _pool_window_dev(ubuf, base, rc, e, i * tm, dbuf)
                return carry

            lax.fori_loop(0, tm // rc, chunk, 0)
            ubuf[0:PHALO, :] = ubuf[tm:tm + PHALO, :]

            project()
            for g in range(len(POOL_WINDOWS)):
                cols = slice(g * gc, (g + 1) * gc)
                yg = jnp.dot(dbuf[:, cols].astype(BF16), wg_ref[g], preferred_element_type=F32)
                z = p_ref[:, e + g * gc:e + (g + 1) * gc].astype(F32)
                s = (((yg + bg_ref[:, cols]) * sc_ref[:, cols]) * (z * _sig(z))).astype(BF16)
                s_ref[:, cols] = s
                s_prev[:, cols] = s

        @pl.when(i == nt)
        def _():
            project()

    vec = pl.BlockSpec((1, e), lambda i: (0, 0))
    cur = lambda i: (jnp.minimum(i, nt - 1), 0)
    lag = lambda i: (jnp.maximum(i - 1, 0), 0)
    return pl.pallas_call(
        body, name="pool_mid_out_fwd", grid=(nt + 1,),
        in_specs=[pl.BlockSpec((tm, 2 * e), cur), pl.BlockSpec((tm, d), lag), _resident((e, d)),
                  pl.BlockSpec((len(POOL_WINDOWS), gc, gc), lambda i: (0, 0, 0)), vec, vec],
        out_specs=[pl.BlockSpec((tm, e), cur), pl.BlockSpec((tm, d), lag)],
        out_shape=[jax.ShapeDtypeStruct((t, e), BF16), jax.ShapeDtypeStruct((t, d), F32)],
        scratch_shapes=[pltpu.VMEM((tm + PHALO, e), F32), pltpu.VMEM((tm, e), F32), pltpu.VMEM((tm, e), BF16)],
        compiler_params=_params("arbitrary"),
    )(p, h, w_out, wg, bg, sc)


def _matmul_res(h, s, w):
    t, d = h.shape
    e = s.shape[1]
    tm = min(MATMUL_TILE, t)

    def body(h_ref, s_ref, w_ref, o_ref):
        o_ref[...] = h_ref[...] + jnp.dot(s_ref[...], w_ref[...], preferred_element_type=F32)

    return pl.pallas_call(
        body, name="matmul_res", grid=(t // tm,),
        in_specs=[pl.BlockSpec((tm, d), lambda i: (i, 0)), pl.BlockSpec((tm, e), lambda i: (i, 0)),
                  _resident((e, d))],
        out_specs=pl.BlockSpec((tm, d), lambda i: (i, 0)),
        out_shape=jax.ShapeDtypeStruct((t, d), F32),
        compiler_params=_params("parallel"),
    )(h, s, w)


def _loss_head(h, fg, tgt):
    t, d = h.shape
    tm = min(MATMUL_TILE, t)

    def body(h_ref, g_ref, t_ref, dh_ref, loss_ref, dg_ref):
        i = pl.program_id(0)

        @pl.when(i == 0)
        def _():
            loss_ref[...] = jnp.zeros_like(loss_ref)
            dg_ref[...] = jnp.zeros_like(dg_ref)

        hh = h_ref[...]
        r = lax.rsqrt(jnp.mean(hh * hh, axis=-1, keepdims=True) + RMS_EPS)
        hhat = hh * r
        err = hhat * g_ref[...] - t_ref[...]
        per_tok = jnp.mean(err * err, axis=-1, keepdims=True)
        loss_ref[...] += 0.5 * jnp.sum(per_tok, axis=0, keepdims=True)
        dy = err * (1.0 / d)
        tt = dy * g_ref[...]
        dh_ref[...] = r * (tt - hhat * jnp.mean(tt * hhat, axis=-1, keepdims=True))
        dg_ref[...] += jnp.sum(dy * hhat, axis=0, keepdims=True)

    return pl.pallas_call(
        body, name="loss_head", grid=(t // tm,),
        in_specs=[pl.BlockSpec((tm, d), lambda i: (i, 0)), pl.BlockSpec((1, d), lambda i: (0, 0)),
                  pl.BlockSpec((tm, d), lambda i: (i, 0))],
        out_specs=[pl.BlockSpec((tm, d), lambda i: (i, 0)), pl.BlockSpec((1, LANES), lambda i: (0, 0)),
                   pl.BlockSpec((1, d), lambda i: (0, 0))],
        out_shape=[jax.ShapeDtypeStruct((t, d), F32), jax.ShapeDtypeStruct((1, LANES), F32),
                   jax.ShapeDtypeStruct((1, d), F32)],
        compiler_params=_params("arbitrary"),
    )(h, fg, tgt)


def _ds_matmul(dy, w):
    t, d = dy.shape
    e = w.shape[0]
    tm = min(MATMUL_TILE, t)

    def body(dy_ref, w_ref, ds_ref):
        ds_ref[...] = lax.dot_general(dy_ref[...].astype(BF16), w_ref[...], (((1,), (1,)), ((), ())),
                                      preferred_element_type=F32).astype(BF16)

    return pl.pallas_call(
        body, name="ds_matmul", grid=(t // tm,),
        in_specs=[pl.BlockSpec((tm, d), lambda i: (i, 0)), _resident((e, d))],
        out_specs=pl.BlockSpec((tm, e), lambda i: (i, 0)),
        out_shape=jax.ShapeDtypeStruct((t, e), BF16),
        compiler_params=_params("parallel"),
    )(dy, w)


def _conv_mid_bwd(p, c, ds, dw3, lg, lb):
    t = p.shape[0]
    e = p.shape[1] // 3
    ng = e // LANES
    tm = _tile(t)
    nt = t // tm
    rc = CONV_ROW_CHUNK
    fb = FIR_BLOCK
    hb = tm // HALO

    def body(p_ref, ph_ref, c_ref, ds_ref, dw_ref, lg_ref, lb_ref,
             dp_ref, dlg_ref, dlb_ref, ddwb_ref, ddw_ref, u3, dc3, du3):
        i = pl.program_id(0)
        ti = nt - 1 - i

        @pl.when(i == 0)
        def _():
            dc3[tm * ng:(tm + HALO) * ng, :] = jnp.zeros((HALO * ng, LANES), F32)
            dlg_ref[...] = jnp.zeros_like(dlg_ref)
            dlb_ref[...] = jnp.zeros_like(dlb_ref)
            ddwb_ref[...] = jnp.zeros_like(ddwb_ref)
            ddw_ref[...] = jnp.zeros_like(ddw_ref)

        ha = ph_ref[:, 0:e].astype(F32)
        hbb = ph_ref[:, e:2 * e].astype(F32)
        _to_token_tiles(u3, 0, HALO, jnp.where(ti > 0, ha * _sig(hbb), 0.0), ng)

        def front(rci, carry):
            slg, slb, sdwb = carry
            base = pl.multiple_of(rci * rc, rc)
            rows = pl.ds(base, rc)
            a = p_ref[rows, 0:e].astype(F32)
            b = p_ref[rows, e:2 * e].astype(F32)
            _to_token_tiles(u3, HALO + base, rc, a * _sig(b), ng)
            cv = c_ref[rows, :].astype(F32)
            mu = jnp.mean(cv, axis=-1, keepdims=True)
            cc = cv - mu
            var = jnp.mean(cc * cc, axis=-1, keepdims=True)
            rs = lax.rsqrt(var + LN_EPS)
            nn = cc * rs
            ln = nn * lg_ref[...] + lb_ref[...]
            z = p_ref[rows, 2 * e:3 * e].astype(F32)
            sz = _sig(z)
            sl = _sig(ln)
            dsv = ds_ref[rows, :].astype(F32)
            dln = dsv * (z * sz) * _dsilu(ln, sl)
            dz = dsv * (ln * sl) * _dsilu(z, sz)
            dp_ref[rows, 2 * e:3 * e] = dz.astype(BF16)
            dn = dln * lg_ref[...]
            dc = rs * (dn - jnp.mean(dn, axis=-1, keepdims=True)
                       - nn * jnp.mean(dn * nn, axis=-1, keepdims=True))
            _to_token_tiles(dc3, base, rc, dc, ng)
            return (slg + jnp.sum(dln * nn, axis=0, keepdims=True),
                    slb + jnp.sum(dln, axis=0, keepdims=True),
                    sdwb + jnp.sum(dc, axis=0, keepdims=True))

        zero = jnp.zeros((1, e), F32)
        slg, slb, sdwb = lax.fori_loop(0, tm // rc, front, (zero, zero, zero), unroll=2)
        dlg_ref[...] += slg
        dlb_ref[...] += slb
        ddwb_ref[...] += sdwb

        def fir(bi, carry):
            t0 = bi * fb

            def dcs(q):
                return dc3[pl.ds(pl.multiple_of((t0 + q) * ng, ng), ng), :]

            def us(q):
                return u3[pl.ds(pl.multiple_of((t0 + HALO - (CONV_K - 1) + q) * ng, ng), ng), :]

            xs = [dcs(q) for q in range(fb - 1)]
            accs = [None] * fb
            for j in range(CONV_K):
                wk = dw_ref[(CONV_K - 1 - j) * ng:(CONV_K - j) * ng, :]
                xs.append(dcs(j + fb - 1))
                accs = [wk * xs[q + j] if accs[q] is None else accs[q] + wk * xs[q + j] for q in range(fb)]
            for q in range(fb):
                du3[pl.ds(pl.multiple_of((t0 + q) * ng, ng), ng), :] = accs[q]
            own = xs[0:fb]
            ys = [us(q) for q in range(fb - 1)]
            for k in range(CONV_K):
                ys.append(us(k + fb - 1))
                prods = [own[q] * ys[q + k] for q in range(fb)]
                while len(prods) > 1:
                    prods = [prods[2 * v] + prods[2 * v + 1] for v in range(len(prods) // 2)]
                ddw_ref[k * ng:(k + 1) * ng, :] += prods[0]
            return carry

        lax.fori_loop(0, tm // fb, fir, 0)
        dc3[tm * ng:(tm + HALO) * ng, :] = dc3[0:HALO * ng, :]

        def back(rci, carry):
            base = pl.multiple_of(rci * rc, rc)
            rows = pl.ds(base, rc)
            a = p_ref[rows, 0:e].astype(F32)
            b = p_ref[rows, e:2 * e].astype(F32)
            sb = _sig(b)
            duv = _from_token_tiles(du3, base, rc, ng)
            dp_ref[rows, 0:e] = (duv * sb).astype(BF16)
            dp_ref[rows, e:2 * e] = (duv * a * sb * (1.0 - sb)).astype(BF16)
            return carry

        lax.fori_loop(0, tm // rc, back, 0, unroll=2)

    vec = pl.BlockSpec((1, e), lambda i: (0, 0))
    taps = pl.BlockSpec((CONV_K * ng, LANES), lambda i: (0, 0))
    rev = lambda i: (nt - 1 - i, 0)
    halo = lambda i: (jnp.maximum((nt - 1 - i) * hb - 1, 0), 0)
    return pl.pallas_call(
        body, name="conv_mid_bwd", grid=(nt,),
        in_specs=[pl.BlockSpec((tm, 3 * e), rev), pl.BlockSpec((HALO, 3 * e), halo),
                  pl.BlockSpec((tm, e), rev), pl.BlockSpec((tm, e), rev), taps, vec, vec],
        out_specs=[pl.BlockSpec((tm, 3 * e), rev), vec, vec, vec, taps],
        out_shape=[jax.ShapeDtypeStruct((t, 3 * e), BF16), jax.ShapeDtypeStruct((1, e), F32),
                   jax.ShapeDtypeStruct((1, e), F32), jax.ShapeDtypeStruct((1, e), F32),
                   jax.ShapeDtypeStruct((CONV_K * ng, LANES), F32)],
        scratch_shapes=[pltpu.VMEM(((tm + HALO) * ng, LANES), F32), pltpu.VMEM(((tm + HALO) * ng, LANES), F32),
                        pltpu.VMEM((tm * ng, LANES), F32)],
        compiler_params=_params("arbitrary"),
    )(p, p, c, ds, dw3, lg, lb)


def _conv_bwd(p, c, dy, hn, w_out, dw3, lg, lb):
    t = p.shape[0]
    e = p.shape[1] // 3
    d = hn.shape[1]
    nk = 3 * e // N_CHIPS
    ng = e // LANES
    tm = _tile(t)
    nt = t // tm
    rc = CONV_ROW_CHUNK
    fb = FIR_BLOCK
    hb = tm // HALO

    def body(p_ref, ph_ref, c_ref, dyn_ref, hnp_ref, w_ref, dw_ref, lg_ref, lb_ref,
             dp_ref, dlg_ref, dlb_ref, ddwb_ref, ddw_ref, dwi_ref, u3, dc3, du3, ds_s, dp_s):
        g = pl.program_id(0)
        ti = nt - g

        def next_ds():
            ds_s[...] = lax.dot_general(dyn_ref[...].astype(BF16), w_ref[...], (((1,), (1,)), ((), ())),
                                        preferred_element_type=F32).astype(BF16)

        def prev_wgrad(slot):
            for k in range(N_CHIPS):
                dwi_ref[k] += lax.dot_general(hnp_ref[...], dp_s[slot, :, k * nk:(k + 1) * nk],
                                              (((0,), (0,)), ((), ())), preferred_element_type=F32)

        @pl.when(g == 0)
        def _():
            dc3[tm * ng:(tm + HALO) * ng, :] = jnp.zeros((HALO * ng, LANES), F32)
            dlg_ref[...] = jnp.zeros_like(dlg_ref)
            dlb_ref[...] = jnp.zeros_like(dlb_ref)
            ddwb_ref[...] = jnp.zeros_like(ddwb_ref)
            ddw_ref[...] = jnp.zeros_like(ddw_ref)
            dwi_ref[...] = jnp.zeros_like(dwi_ref)
            dp_s[0] = jnp.zeros_like(dp_s[0])
            next_ds()

        @pl.when(jnp.logical_and(g >= 1, g <= nt))
        def _():
            ha = ph_ref[:, 0:e].astype(F32)
            hbb = ph_ref[:, e:2 * e].astype(F32)
            _to_token_tiles(u3, 0, HALO, jnp.where(ti > 0, ha * _sig(hbb), 0.0), ng)
            cur = g % 2

            def front(rci, carry):
                slg, slb, sdwb = carry
                base = pl.multiple_of(rci * rc, rc)
                rows = pl.ds(base, rc)
                a = p_ref[rows, 0:e].astype(F32)
                b = p_ref[rows, e:2 * e].astype(F32)
                _to_token_tiles(u3, HALO + base, rc, a * _sig(b), ng)
                cv = c_ref[rows, :].astype(F32)
                mu = jnp.mean(cv, axis=-1, keepdims=True)
                cc = cv - mu
                var = jnp.mean(cc * cc, axis=-1, keepdims=True)
                rs = lax.rsqrt(var + LN_EPS)
                nn = cc * rs
                ln = nn * lg_ref[...] + lb_ref[...]
                z = p_ref[rows, 2 * e:3 * e].astype(F32)
                sz = _sig(z)
                sl = _sig(ln)
                dsv = ds_s[rows, :].astype(F32)
                dln = dsv * (z * sz) * _dsilu(ln, sl)
                dzb = (dsv * (ln * sl) * _dsilu(z, sz)).astype(BF16)
                dp_ref[rows, 2 * e:3 * e] = dzb
                dp_s[cur, rows, 2 * e:3 * e] = dzb
                dn = dln * lg_ref[...]
                dc = rs * (dn - jnp.mean(dn, axis=-1, keepdims=True)
                           - nn * jnp.mean(dn * nn, axis=-1, keepdims=True))
                _to_token_tiles(dc3, base, rc, dc, ng)
                return (slg + jnp.sum(dln * nn, axis=0, keepdims=True),
                        slb + jnp.sum(dln, axis=0, keepdims=True),
                        sdwb + jnp.sum(dc, axis=0, keepdims=True))

            zero = jnp.zeros((1, e), F32)
            slg, slb, sdwb = lax.fori_loop(0, tm // rc, front, (zero, zero, zero), unroll=2)
            dlg_ref[...] += slg
            dlb_ref[...] += slb
            ddwb_ref[...] += sdwb

            next_ds()
            prev_wgrad(1 - cur)
            for bi in range(tm // fb):
                t0 = bi * fb

                def dcs(q):
                    return dc3[(t0 + q) * ng:(t0 + q + 1) * ng, :]

                def us(q):
                    r0 = t0 + HALO - (CONV_K - 1) + q
                    return u3[r0 * ng:(r0 + 1) * ng, :]

                xs = [dcs(q) for q in range(fb - 1)]
                accs = [None] * fb
                for j in range(CONV_K):
                    wk = dw_ref[(CONV_K - 1 - j) * ng:(CONV_K - j) * ng, :]
                    xs.append(dcs(j + fb - 1))
                    accs = [wk * xs[q + j] if accs[q] is None else accs[q] + wk * xs[q + j] for q in range(fb)]
                for q in range(fb):
                    du3[(t0 + q) * ng:(t0 + q + 1) * ng, :] = accs[q]
                own = xs[0:fb]
                ys = [us(q) for q in range(fb - 1)]
                for k in range(CONV_K):
                    ys.append(us(k + fb - 1))
                    prods = [own[q] * ys[q + k] for q in range(fb)]
                    while len(prods) > 1:
                        prods = [prods[2 * v] + prods[2 * v + 1] for v in range(len(prods) // 2)]
                    ddw_ref[k * ng:(k + 1) * ng, :] += prods[0]
            dc3[tm * ng:(tm + HALO) * ng, :] = dc3[0:HALO * ng, :]

            def back(rci, carry):
                base = pl.multiple_of(rci * rc, rc)
                rows = pl.ds(base, rc)
                a = p_ref[rows, 0:e].astype(F32)
                b = p_ref[rows, e:2 * e].astype(F32)
                sb = _sig(b)
                duv = _from_token_tiles(du3, base, rc, ng)
                dab = (duv * sb).astype(BF16)
                dbb = (duv * a * sb * (1.0 - sb)).astype(BF16)
                dp_ref[rows, 0:e] = dab
                dp_ref[rows, e:2 * e] = dbb
                dp_s[cur, rows, 0:e] = dab
                dp_s[cur, rows, e:2 * e] = dbb
                return carry

            lax.fori_loop(0, tm // rc, back, 0, unroll=2)

        @pl.when(g == nt + 1)
        def _():
            prev_wgrad(nt % 2)

    vec = pl.BlockSpec((1, e), lambda g: (0, 0))
    taps = pl.BlockSpec((CONV_K * ng, LANES), lambda g: (0, 0))
    tile_of = lambda g: jnp.clip(nt - g, 0, nt - 1)
    cur_map = lambda g: (tile_of(g), 0)
    halo = lambda g: (jnp.maximum(tile_of(g) * hb - 1, 0), 0)
    nxt_map = lambda g: (jnp.clip(nt - 1 - g, 0, nt - 1), 0)
    prv_map = lambda g: (jnp.clip(nt + 1 - g, 0, nt - 1), 0)
    return pl.pallas_call(
        body, name="conv_bwd", grid=(nt + 2,),
        in_specs=[pl.BlockSpec((tm, 3 * e), cur_map), pl.BlockSpec((HALO, 3 * e), halo),
                  pl.BlockSpec((tm, e), cur_map), pl.BlockSpec((tm, d), nxt_map),
                  pl.BlockSpec((tm, d), prv_map), _resident((e, d)), taps, vec, vec],
        out_specs=[pl.BlockSpec((tm, 3 * e), cur_map), vec, vec, vec, taps,
                   pl.BlockSpec((N_CHIPS, d, nk), lambda g: (0, 0, 0))],
        out_shape=[jax.ShapeDtypeStruct((t, 3 * e), BF16), jax.ShapeDtypeStruct((1, e), F32),
                   jax.ShapeDtypeStruct((1, e), F32), jax.ShapeDtypeStruct((1, e), F32),
                   jax.ShapeDtypeStruct((CONV_K * ng, LANES), F32),
                   jax.ShapeDtypeStruct((N_CHIPS, d, nk), F32)],
        scratch_shapes=[pltpu.VMEM(((tm + HALO) * ng, LANES), F32), pltpu.VMEM(((tm + HALO) * ng, LANES), F32),
                        pltpu.VMEM((tm * ng, LANES), F32), pltpu.VMEM((tm, e), BF16),
                        pltpu.VMEM((2, tm, 3 * e), BF16)],
        compiler_params=_params("arbitrary"),
    )(p, p, c, dy, hn, w_out, dw3, lg, lb)


def _pool_mid_bwd(p, dy, w_out, wg, bg, sc):
    t = p.shape[0]
    e = p.shape[1] // 2
    d = dy.shape[1]
    ng = len(POOL_WINDOWS)
    gc = e // ng
    tm = _tile(t)
    nt = t // tm
    rc = ROW_CHUNK
    hb = tm // PHALO

    def body(p_ref, ph_ref, dyc_ref, dyn_ref, w_ref, wg_ref, bg_ref, sc_ref,
             dp_ref, dwg_ref, dbg_ref, dsc_ref, ubuf, dbuf, ebuf, ddbuf, ds_s):
        i = pl.program_id(0)
        ti = nt - 1 - i

        def project(dy_ref):
            return lax.dot_general(dy_ref[...].astype(BF16), w_ref[...], (((1,), (1,)), ((), ())),
                                   preferred_element_type=F32).astype(BF16)

        @pl.when(i == 0)
        def _():
            ebuf[tm:tm + PHALO, :] = jnp.zeros((PHALO, e), F32)
            dwg_ref[...] = jnp.zeros_like(dwg_ref)
            dbg_ref[...] = jnp.zeros_like(dbg_ref)
            dsc_ref[...] = jnp.zeros_like(dsc_ref)
            ds_s[...] = project(dyc_ref)

        ubuf[0:PHALO, :] = jnp.where(ti > 0, ph_ref[:, 0:e].astype(F32), 0.0)
        ubuf[PHALO:PHALO + tm, :] = p_ref[:, 0:e].astype(F32)

        def recompute(rci, carry):
            base = pl.multiple_of(rci * rc, rc)
            _pool_window_dev(ubuf, base, rc, e, ti * tm, dbuf)
            return carry

        lax.fori_loop(0, tm // rc, recompute, 0)

        for g in range(ng):
            cols = slice(g * gc, (g + 1) * gc)
            dg = dbuf[:, cols].astype(BF16)
            q = jnp.dot(dg, wg_ref[g], preferred_element_type=F32) + bg_ref[:, cols]
            z = p_ref[:, e + g * gc:e + (g + 1) * gc].astype(F32)
            sz = _sig(z)
            dsv = ds_s[:, cols].astype(F32)
            dz = dsv * (q * sc_ref[:, cols]) * _dsilu(z, sz)
            dp_ref[:, e + g * gc:e + (g + 1) * gc] = dz.astype(BF16)
            dy2 = dsv * (z * sz)
            dsc_ref[:, cols] += jnp.sum(dy2 * q, axis=0, keepdims=True)
            dq = dy2 * sc_ref[:, cols]
            dbg_ref[:, cols] += jnp.sum(dq, axis=0, keepdims=True)
            dqb = dq.astype(BF16)
            dwg_ref[g] += lax.dot_general(dg, dqb, (((0,), (0,)), ((), ())), preferred_element_type=F32)
            ddbuf[:, cols] = lax.dot_general(dqb, wg_ref[g], (((1,), (1,)), ((), ())),
                                             preferred_element_type=F32)
        ds_s[...] = project(dyn_ref)

        def scale(rci, carry):
            base = pl.multiple_of(rci * rc, rc)
            for lc in range(e // LANES):
                lanes = slice(lc * LANES, (lc + 1) * LANES)
                w = POOL_WINDOWS[_pool_group(lc, e)]
                ebuf[pl.ds(base, rc), lanes] = (ddbuf[pl.ds(base, rc), lanes]
                                                * _pool_inv_count(ti * tm + base, rc, w))
            return carry

        lax.fori_loop(0, tm // rc, scale, 0)

        def chunk(rci, carry):
            base = pl.multiple_of(rci * rc, rc)
            n = rc + PHALO
            for lc in range(e // LANES):
                lanes = slice(lc * LANES, (lc + 1) * LANES)
                w = POOL_WINDOWS[_pool_group(lc, e)]
                acc = ebuf[pl.ds(base, n), lanes]
                step = 1
                while step < w:
                    acc = acc + pltpu.roll(acc, n - step, 0)
                    step *= 2
                du = acc[0:rc] - ddbuf[pl.ds(base, rc), lanes]
                dp_ref[pl.ds(base, rc), lanes] = du.astype(BF16)
            return carry

        lax.fori_loop(0, tm // rc, chunk, 0)
        ebuf[tm:tm + PHALO, :] = ebuf[0:PHALO, :]

    vec = pl.BlockSpec((1, e), lambda i: (0, 0))
    rev = lambda i: (nt - 1 - i, 0)
    halo = lambda i: (jnp.maximum((nt - 1 - i) * hb - 1, 0), 0)
    wspec = pl.BlockSpec((ng, gc, gc), lambda i: (0, 0, 0))
    nxt = lambda i: (jnp.maximum(nt - 2 - i, 0), 0)
    return pl.pallas_call(
        body, name="pool_mid_bwd", grid=(nt,),
        in_specs=[pl.BlockSpec((tm, 2 * e), rev), pl.BlockSpec((PHALO, 2 * e), halo),
                  pl.BlockSpec((tm, d), rev), pl.BlockSpec((tm, d), nxt), _resident((e, d)), wspec, vec, vec],
        out_specs=[pl.BlockSpec((tm, 2 * e), rev), wspec, vec, vec],
        out_shape=[jax.ShapeDtypeStruct((t, 2 * e), BF16), jax.ShapeDtypeStruct((ng, gc, gc), F32),
                   jax.ShapeDtypeStruct((1, e), F32), jax.ShapeDtypeStruct((1, e), F32)],
        scratch_shapes=[pltpu.VMEM((tm + PHALO, e), F32), pltpu.VMEM((tm, e), F32),
                        pltpu.VMEM((tm + PHALO, e), F32), pltpu.VMEM((tm, e), F32), pltpu.VMEM((tm, e), BF16)],
        compiler_params=_params("arbitrary"),
    )(p, p, dy, dy, w_out, wg, bg, sc)


def _dhn_rms_bwd(dp, w4, h, g, dh_out, name):
    t, d = h.shape
    nk = w4.shape[-1]
    tm = min(MATMUL_TILE, t)

    def body(dp_ref, w_ref, h_ref, g_ref, dho_ref, dh_ref, dg_ref):
        i = pl.program_id(0)

        @pl.when(i == 0)
        def _():
            dg_ref[...] = jnp.zeros_like(dg_ref)

        dhn = jnp.zeros((tm, d), F32)
        for k in range(N_CHIPS):
            dhn = dhn + lax.dot_general(dp_ref[:, k * nk:(k + 1) * nk], w_ref[k], (((1,), (1,)), ((), ())),
                                        preferred_element_type=F32)
        hh = h_ref[...]
        r = lax.rsqrt(jnp.mean(hh * hh, axis=-1, keepdims=True) + RMS_EPS)
        hhat = hh * r
        tt = dhn * g_ref[...]
        dh_ref[...] = dho_ref[...] + r * (tt - hhat * jnp.mean(tt * hhat, axis=-1, keepdims=True))
        dg_ref[...] += jnp.sum(dhn * hhat, axis=0, keepdims=True)

    return pl.pallas_call(
        body, name=name, grid=(t // tm,),
        in_specs=[pl.BlockSpec((tm, N_CHIPS * nk), lambda i: (i, 0)),
                  _resident((N_CHIPS, d, nk)),
                  pl.BlockSpec((tm, d), lambda i: (i, 0)), pl.BlockSpec((1, d), lambda i: (0, 0)),
                  pl.BlockSpec((tm, d), lambda i: (i, 0))],
        out_specs=[pl.BlockSpec((tm, d), lambda i: (i, 0)), pl.BlockSpec((1, d), lambda i: (0, 0))],
        out_shape=[jax.ShapeDtypeStruct((t, d), F32), jax.ShapeDtypeStruct((1, d), F32)],
        compiler_params=_params("arbitrary"),
    )(dp, w4, h, g, dh_out)


def _wgrad(a, b, nblk, name):
    t, m = a.shape
    nn = b.shape[1] // nblk
    tk = min(WGRAD_TILE, t)
    nk = t // tk

    def body(a_ref, b_ref, o_ref, ob_ref):
        @pl.when(pl.program_id(1) == 0)
        def _():
            o_ref[...] = jnp.zeros_like(o_ref)

        o_ref[...] += lax.dot_general(a_ref[...].astype(BF16), b_ref[...].astype(BF16),
                                      (((0,), (0,)), ((), ())), preferred_element_type=F32)

        @pl.when(pl.program_id(1) == nk - 1)
        def _():
            ob_ref[...] = o_ref[...].astype(BF16)

    ospec = pl.BlockSpec((None, m, nn), lambda j, i: (j, 0, 0))
    return pl.pallas_call(
        body, name=name, grid=(nblk, nk),
        in_specs=[pl.BlockSpec((tk, m), lambda j, i: (i, 0)), pl.BlockSpec((tk, nn), lambda j, i: (i, j))],
        out_specs=[ospec, ospec],
        out_shape=[jax.ShapeDtypeStruct((nblk, m, nn), F32), jax.ShapeDtypeStruct((nblk, m, nn), BF16)],
        compiler_params=_params("parallel", "arbitrary"),
    )(a, b)


def _rows2d(shape):
    rows = 1
    for s in shape[:-1]:
        rows *= s
    return rows, shape[-1]


def _row_tile(rows):
    for cand in (512, 256, 128, 64, 32, 16, 8):
        if rows % cand == 0:
            return cand
    return rows


def _add_eight(own, landed, chip_core, name):
    _, _, rows, cols = own.shape
    tr = _row_tile(rows)

    def body(sel_ref, s_ref, r_ref, o_ref):
        acc = s_ref[...]
        for peer in range(N_DEV - 1):
            acc = acc + r_ref[peer].astype(F32)
        o_ref[...] = acc

    return pl.pallas_call(
        body, name=name,
        grid_spec=pltpu.PrefetchScalarGridSpec(
            num_scalar_prefetch=1, grid=(rows // tr,),
            in_specs=[pl.BlockSpec((None, None, tr, cols), lambda i, s: (s[0], s[1], i, 0)),
                      pl.BlockSpec((N_DEV - 1, tr, cols), lambda i, s: (0, i, 0))],
            out_specs=pl.BlockSpec((None, tr, cols), lambda i, s: (s[1], i, 0))),
        out_shape=jax.ShapeDtypeStruct((2, rows, cols), F32),
        compiler_params=_params("parallel"),
    )(chip_core, own, landed)


def _adamw(w, g, m, v, name):
    rows, cols = w.shape
    tr = _row_tile(rows)

    def body(w_ref, g_ref, m_ref, v_ref, d_ref, m2_ref, v2_ref):
        gg = g_ref[...]
        m2 = ADAM_B1 * m_ref[...] + (1.0 - ADAM_B1) * gg
        v2 = ADAM_B2 * v_ref[...] + (1.0 - ADAM_B2) * (gg * gg)
        m_hat = m2 / (1.0 - ADAM_B1 ** ADAM_STEP)
        v_hat = v2 / (1.0 - ADAM_B2 ** ADAM_STEP)
        d_ref[...] = -ADAM_LR * (m_hat / (jnp.sqrt(v_hat) + ADAM_EPS) + ADAM_WD * w_ref[...])
        m2_ref[...] = m2
        v2_ref[...] = v2

    spec = pl.BlockSpec((tr, cols), lambda i: (i, 0))
    shp = jax.ShapeDtypeStruct((rows, cols), F32)
    return pl.pallas_call(
        body, name=name, grid=(rows // tr,),
        in_specs=[spec, spec, spec, spec], out_specs=[spec, spec, spec], out_shape=[shp, shp, shp],
        compiler_params=_params("parallel"),
    )(w, g, m, v)


ANY = pl.BlockSpec(memory_space=pl.ANY)


def _place():
    x, y, c = lax.axis_index("x"), lax.axis_index("y"), lax.axis_index("c")
    chips = [(1 - x, y), (x, 1 - y), (1 - x, 1 - y)]
    return x, y, c, chips


def _allgather_weights(shards):
    n = len(shards)

    def body(*refs):
        ins, outs = refs[:n], refs[n:2 * n]
        send_ici, recv_ici, send_d2d, recv_d2d = refs[2 * n:]
        x, y, c, chips = _place()
        k0 = 2 * x + y
        sib = (x, y, 1 - c)

        def ici(a, r, src_chip, target):
            return pltpu.make_async_remote_copy(
                src_ref=ins[a].at[c], dst_ref=outs[a].at[src_chip, c],
                send_sem=send_ici.at[a * 3 + r], recv_sem=recv_ici.at[a * 3 + r],
                device_id=target, device_id_type=MESH)

        def d2d(a, r, src_chip, layer):
            return pltpu.make_async_remote_copy(
                src_ref=outs[a].at[src_chip, layer], dst_ref=outs[a].at[src_chip, layer],
                send_sem=send_d2d.at[a * 3 + r], recv_sem=recv_d2d.at[a * 3 + r],
                device_id=sib, device_id_type=MESH)

        first = [ici(a, r, k0, (cx, cy, c)) for a in range(n) for r, (cx, cy) in enumerate(chips)]
        for cp in first:
            cp.start()
        passed = []
        for a in range(n):
            for r, (cx, cy) in enumerate(chips):
                ici(a, r, 2 * cx + cy, (cx, cy, c)).wait_recv()
                cp = d2d(a, r, 2 * cx + cy, c)
                cp.start()
                passed.append(cp)
        for a in range(n):
            for r, (cx, cy) in enumerate(chips):
                d2d(a, r, 2 * cx + cy, 1 - c).wait_recv()
        for cp in first + passed:
            cp.wait_send()

    return pl.pallas_call(
        body, name="allgather_weights",
        in_specs=[ANY] * n, out_specs=[ANY] * n,
        out_shape=[jax.ShapeDtypeStruct((N_CHIPS,) + s.shape, s.dtype) for s in shards],
        scratch_shapes=[pltpu.SemaphoreType.DMA((3 * n,)), pltpu.SemaphoreType.DMA((3 * n,)),
                        pltpu.SemaphoreType.DMA((3 * n,)), pltpu.SemaphoreType.DMA((3 * n,))],
    )(*shards)


def _put_own(gathered, shard, chip):
    return lax.dynamic_update_slice_in_dim(gathered, shard[None], chip, axis=0)


HBM = pl.BlockSpec(memory_space=pltpu.HBM)
SEM = pl.BlockSpec(memory_space=pltpu.SEMAPHORE)
DATAFLOW = pltpu.SideEffectType.DATAFLOW_SIDE_EFFECTING
FLIPS = [(0, 0, 1), (0, 1, 0), (0, 1, 1), (1, 0, 0), (1, 0, 1), (1, 1, 0), (1, 1, 1)]


def _gather_plan(srcs, lands, send_sem, recv_sem):
    x, y, c, chips = _place()
    return [pltpu.make_async_remote_copy(
        src_ref=srcs[a], dst_ref=lands[a].at[2 * x + y],
        send_sem=send_sem.at[a * 3 + r], recv_sem=recv_sem.at[a * 3 + r],
        device_id=(cx, cy, c), device_id_type=MESH)
        for a in range(len(srcs)) for r, (cx, cy) in enumerate(chips)]


def _scatter_plan(srcs, lands, send_sem, recv_sem):
    x, y, c, _ = _place()
    cps = []
    for a in range(len(srcs)):
        for r, (fx, fy, fc) in enumerate(FLIPS):
            tx, ty, tc = (1 - x if fx else x), (1 - y if fy else y), (1 - c if fc else c)
            cps.append(pltpu.make_async_remote_copy(
                src_ref=srcs[a].at[2 * tx + ty, tc], dst_ref=lands[a].at[r],
                send_sem=send_sem.at[a * len(FLIPS) + r], recv_sem=recv_sem.at[a * len(FLIPS) + r],
                device_id=(tx, ty, tc), device_id_type=MESH))
    return cps


def _split_start(name, plan, srcs, lands, n_copies, after):
    n = len(srcs)

    def body(*refs):
        src, land = refs[:n], refs[n:2 * n]
        send_sem, recv_sem = refs[2 * n + 1], refs[2 * n + 2]
        token = refs[-1]
        for cp in plan(src, land, send_sem, recv_sem):
            cp.start()
        token[...] = jnp.zeros_like(token)

    outs = pl.pallas_call(
        body, name=name,
        in_specs=[HBM] * (2 * n) + [ANY],
        out_specs=[SEM, SEM] + [HBM] * (2 * n) + [pl.BlockSpec(memory_space=pltpu.VMEM)],
        out_shape=[pltpu.SemaphoreType.DMA((n_copies,)), pltpu.SemaphoreType.DMA((n_copies,))]
        + [pltpu.HBM(s.shape, s.dtype) for s in srcs] + [pltpu.HBM(l.shape, l.dtype) for l in lands]
        + [jax.ShapeDtypeStruct((8, LANES), F32)],
        input_output_aliases={i: 2 + i for i in range(2 * n)},
        compiler_params=pltpu.CompilerParams(has_side_effects=DATAFLOW),
    )(*[pltpu.with_memory_space_constraint(s, pltpu.HBM) for s in srcs],
      *[pltpu.with_memory_space_constraint(l, pltpu.HBM) for l in lands], after)
    return outs[0], outs[1], list(outs[2:2 + n]), list(outs[2 + n:2 + 2 * n]), outs[-1]


def _split_wait(name, plan, send_sems, recv_sems, srcs, lands, after):
    n = len(srcs)

    def body(*refs):
        src, land = refs[:n], refs[n:2 * n]
        send_sem, recv_sem = refs[2 * n], refs[2 * n + 1]
        for cp in plan(src, land, send_sem, recv_sem):
            cp.wait_send()
            cp.wait_recv()

    outs = pl.pallas_call(
        body, name=name,
        in_specs=[HBM] * (2 * n) + [SEM, SEM, ANY],
        out_specs=[HBM] * (2 * n),
        out_shape=[pltpu.HBM(s.shape, s.dtype) for s in srcs] + [pltpu.HBM(l.shape, l.dtype) for l in lands],
        input_output_aliases={i: i for i in range(2 * n)},
        compiler_params=pltpu.CompilerParams(has_side_effects=DATAFLOW),
    )(*srcs, *lands, send_sems, recv_sems, after)
    return list(outs[:n]), list(outs[n:])


def _share_halves(halves):
    n = len(halves)

    def body(*refs):
        ins, outs = refs[:n], refs[n:2 * n]
        send_sem, recv_sem = refs[2 * n:]
        x, y, c, _ = _place()
        cps = [pltpu.make_async_remote_copy(
            src_ref=outs[a].at[c], dst_ref=outs[a].at[c], send_sem=send_sem.at[a], recv_sem=recv_sem.at[a],
            device_id=(x, y, 1 - c), device_id_type=MESH) for a in range(n)]
        for cp in cps:
            cp.start()
        for cp in cps:
            cp.wait()

    return pl.pallas_call(
        body, name="share_halves",
        in_specs=[ANY] * n, out_specs=[ANY] * n,
        out_shape=[jax.ShapeDtypeStruct(h.shape, h.dtype) for h in halves],
        input_output_aliases={a: a for a in range(n)},
        scratch_shapes=[pltpu.SemaphoreType.DMA((n,)), pltpu.SemaphoreType.DMA((n,))],
    )(*halves)


N_DEV = 8


def _allreduce_small(v):
    m, nc = v.shape

    def body(x_ref, out_ref, gat, send_sems, recv_sems, local_sem):
        x, y, c, chips = _place()
        me, sib = (x, y, c), (x, y, 1 - c)

        def rows(px, py, pc):
            return gat.at[pl.ds((4 * px + 2 * py + pc) * m, m), :]

        def copy(k, block, to, src=None):
            return pltpu.make_async_remote_copy(
                src_ref=rows(*block) if src is None else src, dst_ref=rows(*block),
                send_sem=send_sems.at[k], recv_sem=recv_sems.at[k], device_id=to, device_id_type=MESH)

        mine = pltpu.make_async_copy(x_ref, rows(*me), local_sem)
        mine.start()
        first = [copy(0, me, sib, src=x_ref)]
        first += [copy(1 + j, me, (*chip, c), src=x_ref) for j, chip in enumerate(chips)]
        for cp in first:
            cp.start()
        passed = [copy(4 + j, (*chip, c), sib) for j, chip in enumerate(chips)]
        for j, chip in enumerate(chips):
            copy(1 + j, (*chip, c), me).wait_recv()
            passed[j].start()
        copy(0, sib, me).wait_recv()
        for j, chip in enumerate(chips):
            copy(4 + j, (*chip, 1 - c), me).wait_recv()
        for cp in first + passed:
            cp.wait_send()
        mine.wait()
        acc = gat[0:m, :]
        for dev in range(1, N_DEV):
            acc = acc + gat[dev * m:(dev + 1) * m, :]
        out_ref[...] = acc

    return pl.pallas_call(
        body, name="allreduce_small",
        in_specs=[pl.BlockSpec(memory_space=pltpu.VMEM)],
        out_specs=pl.BlockSpec(memory_space=pltpu.VMEM),
        out_shape=jax.ShapeDtypeStruct((m, nc), F32),
        scratch_shapes=[pltpu.VMEM((N_DEV * m, nc), F32), pltpu.SemaphoreType.DMA((7,)),
                        pltpu.SemaphoreType.DMA((7,)), pltpu.SemaphoreType.DMA],
        compiler_params=pltpu.CompilerParams(vmem_limit_bytes=VMEM_LIMIT),
    )(v)


def _pad_rows(a, rows):
    return jnp.pad(a, ((0, rows - a.shape[0]), (0, 0)))


def kernel(x, norm_g, final_g, conv_w_in, conv_dw, conv_dw_b, conv_ln_g, conv_ln_b, conv_w_out, pool_w_in, pool_w_grp, pool_b_grp, pool_scale, pool_w_out, loss_target, m_norm_g, m_final_g, m_conv_w_in, m_conv_dw, m_conv_dw_b, m_conv_ln_g, m_conv_ln_b, m_conv_w_out, m_pool_w_in, m_pool_w_grp, m_pool_b_grp, m_pool_scale, m_pool_w_out, v_norm_g, v_final_g, v_conv_w_in, v_conv_dw, v_conv_dw_b, v_conv_ln_g, v_conv_ln_b, v_conv_w_out, v_pool_w_in, v_pool_w_grp, v_pool_b_grp, v_pool_scale, v_pool_w_out):
    t, d = x.shape[1], x.shape[2]
    e = conv_w_out.shape[2]
    ng = len(POOL_WINDOWS)
    gc = e // ng
    gcs = pool_w_grp.shape[2]
    ck = conv_dw.shape[1]
    es = conv_dw.shape[2]
    xi, yi, ci = lax.axis_index("x"), lax.axis_index("y"), lax.axis_index("c")
    chip = 2 * xi + yi

    small_rows = ck + 2
    small_pad = -(-small_rows // 8) * 8
    small = jnp.concatenate([conv_dw, pool_b_grp[:, None, :], pool_scale[:, None, :],
                             jnp.zeros((2, small_pad - small_rows, es), F32)], axis=1)
    cwi_b, cwo_b = conv_w_in.astype(BF16), conv_w_out.astype(BF16)

    def halves(a):
        return a.reshape(2, a.shape[0] // 2, a.shape[1])

    first = [halves(cwi_b[0]), halves(cwo_b[0]), small]
    g_cwi0, g_cwo0, g_small = [_put_own(g, q, chip) for g, q in zip(_allgather_weights(first), first)]
    rest = [cwi_b[1], cwo_b[1], pool_w_in.astype(BF16), pool_w_grp.astype(BF16), pool_w_out.astype(BF16)]
    rest_lands = [lax.empty((N_CHIPS,) + r.shape, r.dtype) for r in rest]
    ag_send, ag_recv, rest, rest_lands, ag_token = _split_start(
        "gather_rest_start", _gather_plan, rest, rest_lands, 3 * len(rest), g_small)
    smallf = jnp.transpose(g_small, (1, 2, 0, 3)).reshape(2, small_pad, N_CHIPS * es)

    h = x.reshape(t, d)
    tgt = loss_target.reshape(t, d)
    hs, saved = [], []
    for layer in range(4):
        j = layer // 2
        hs.append(h)
        gvec = norm_g[layer][None, :]
        if layer == 0:
            gvec = gvec + ag_token[0:1, 0:1]
        if layer == 1:
            rest, rest_lands = _split_wait("gather_rest_wait", _gather_plan, ag_send, ag_recv, rest, rest_lands, h)
            g_cwi1, g_cwo1, g_pwi, g_pwg, g_pwo = [_put_own(g, q, chip) for g, q in zip(rest_lands, rest)]
            wg_full = jnp.transpose(g_pwg, (1, 2, 0, 3, 4)).reshape(2, ng, gc, gc)
        if layer % 2 == 0:
            g_in, g_out = (g_cwi0, g_cwo0) if j == 0 else (g_cwi1, g_cwo1)
            w_in4 = g_in.reshape(N_CHIPS, d, -1)
            w_out = g_out.reshape(e, d)
            dw_full = smallf[j, 0:ck]
            p, hn, s, c, h = _conv_fwd(h, gvec, w_in4, w_out, dw_full.reshape(-1, LANES),
                                       conv_dw_b[j].reshape(-1, LANES), conv_ln_g[j][None, :], conv_ln_b[j][None, :])
            saved.append((p, hn, s, c, w_in4, w_out, dw_full))
        else:
            w_in4 = g_pwi[:, j]
            w_out = g_pwo[:, j].reshape(e, d)
            p, hn = _rms_matmul(h, gvec, w_in4, "rms_matmul_pool")
            bg_full = smallf[j, ck:ck + 1]
            sc_full = smallf[j, ck + 1:ck + 2]
            s, h = _pool_mid_out_fwd(p, h, w_out, wg_full[j], bg_full, sc_full)
            saved.append((p, hn, s, None, w_in4, w_out, (wg_full[j], bg_full, sc_full)))

    dh, loss_part, dfg = _loss_head(h, final_g[None, :], tgt)
    loss = lax.psum(loss_part[0, 0], ("x", "y", "c"))

    def by_half(a):
        return a.reshape(N_CHIPS, 2, a.shape[1] // 2, a.shape[2])

    dng = [None] * 4
    g_conv = [None, None]
    g_pool = [None, None]
    flights = {}
    for layer in (3, 2, 1, 0):
        j = layer // 2
        p, hn, s, c, w_in4, w_out, extra = saved[layer]
        gvec = norm_g[layer][None, :]
        dw_out, dw_out_b = [q.reshape(N_CHIPS, e // N_CHIPS, d) for q in _wgrad(s, dh, 1, "wgrad_out")]
        if layer % 2 == 0:
            dp, dlg, dlb, ddwb, ddw3, dw_in = _conv_bwd(p, c, dh, hn, w_out, extra.reshape(-1, LANES),
                                                        conv_ln_g[j][None, :], conv_ln_b[j][None, :])
            ddw = ddw3.reshape(ck, e)
            dw_in_b = dw_in.astype(BF16)
            own, pay = [dw_in, dw_out], [dw_in_b, dw_out_b]
            g_conv[j] = (dlg, dlb, ddwb, ddw)
        else:
            wg, bg_full, sc_full = extra
            dp, dwg, dbg, dsc = _pool_mid_bwd(p, dh, w_out, wg, bg_full, sc_full)
            dw_in, dw_in_b = _wgrad(hn, dp, N_CHIPS, "wgrad_in_pool")
            dwg4 = jnp.transpose(dwg.reshape(ng, N_CHIPS, gcs, gc), (1, 0, 2, 3)).reshape(N_CHIPS, ng * gcs, gc)
            own, pay = [dw_in, dw_out, dwg4], [dw_in_b, dw_out_b, dwg4.astype(BF16)]
            g_pool[j] = (dbg, dsc)
        pay = [by_half(q) for q in pay]
        lands = [lax.empty((len(FLIPS),) + q.shape[2:], BF16) for q in pay]
        send, recv, pay, lands, token = _split_start(
            "scatter_start_%d" % layer, _scatter_plan, pay, lands, len(FLIPS) * len(pay), own[0])
        flights[layer] = (send, recv, pay, lands, own)
        kind = "dhn_rms_bwd_conv" if layer % 2 == 0 else "dhn_rms_bwd_pool"
        dh, dng[layer] = _dhn_rms_bwd(dp, w_in4, hs[layer], gvec + token[0:1, 0:1], dh, kind)
    grad_x = dh.reshape(x.shape)

    sel_kc = jnp.stack([chip, ci]).astype(jnp.int32)
    summed = {}
    after = dh
    for layer in (3, 2, 1, 0):
        send, recv, pay, lands, own = flights[layer]
        _, lands = _split_wait("scatter_wait_%d" % layer, _scatter_plan, send, recv, pay, lands, after)
        after = lands[0]
        summed[layer] = [_add_eight(by_half(o), l, sel_kc, "add_eight_%d_%d" % (layer % 2, a))
                         for a, (o, l) in enumerate(zip(own, lands))]
    order = [(0, 0), (0, 1), (2, 0), (2, 1), (1, 0), (1, 1), (1, 2), (3, 0), (3, 1), (3, 2)]
    shared = _share_halves([summed[l][a] for l, a in order])
    full = {la: q.reshape(q.shape[0] * q.shape[1], q.shape[2]) for la, q in zip(order, shared)}
    g_cwi_f = jnp.stack([full[(0, 0)], full[(2, 0)]])
    g_cwo_f = jnp.stack([full[(0, 1)], full[(2, 1)]])
    g_pwi_f = jnp.stack([full[(1, 0)], full[(3, 0)]])
    g_pwo_f = jnp.stack([full[(1, 1)], full[(3, 1)]])
    g_pwg_f = jnp.stack([full[(1, 2)], full[(3, 2)]]).reshape(pool_w_grp.shape)

    rows_list = [dng[0], dng[1], dng[2], dng[3], dfg,
                 g_conv[0][2], g_conv[1][2], g_conv[0][0], g_conv[1][0], g_conv[0][1], g_conv[1][1],
                 g_pool[0][0], g_pool[1][0], g_pool[0][1], g_pool[1][1], g_conv[0][3], g_conv[1][3]]
    slab = jnp.concatenate(rows_list, axis=0)
    nrows = slab.shape[0]
    slab = _pad_rows(slab, -(-nrows // 8) * 8)
    tot = _allreduce_small(slab)
    g_norm_g = tot[0:4]
    g_final_g = tot[4]
    g_dwb = tot[5:7]
    g_lng = tot[7:9]
    g_lnb = tot[9:11]
    g_bg = lax.dynamic_slice_in_dim(tot[11:13], chip * es, es, axis=1)
    g_sc = lax.dynamic_slice_in_dim(tot[13:15], chip * es, es, axis=1)
    g_dw = lax.dynamic_slice_in_dim(tot[15:15 + 2 * ck].reshape(2, ck, e), chip * es, es, axis=2)

    def adam_nd(w, g, m, v, nm):
        rows, cols = _rows2d(w.shape)
        outs = _adamw(w.reshape(rows, cols), g.reshape(rows, cols), m.reshape(rows, cols),
                      v.reshape(rows, cols), "adamw_" + nm)
        return [o.reshape(w.shape) for o in outs]

    res = {}
    res["conv_w_in"] = (g_cwi_f, *adam_nd(conv_w_in, g_cwi_f, m_conv_w_in, v_conv_w_in, "cwi"))
    res["conv_w_out"] = (g_cwo_f, *adam_nd(conv_w_out, g_cwo_f, m_conv_w_out, v_conv_w_out, "cwo"))
    res["pool_w_in"] = (g_pwi_f, *adam_nd(pool_w_in, g_pwi_f, m_pool_w_in, v_pool_w_in, "pwi"))
    res["pool_w_grp"] = (g_pwg_f, *adam_nd(pool_w_grp, g_pwg_f, m_pool_w_grp, v_pool_w_grp, "pwg"))
    res["pool_w_out"] = (g_pwo_f, *adam_nd(pool_w_out, g_pwo_f, m_pool_w_out, v_pool_w_out, "pwo"))

    def pack(parts, rows_to):
        return _pad_rows(jnp.concatenate([q.reshape(-1, q.shape[-1]) for q in parts], axis=0), rows_to)

    rep_w = [norm_g, final_g[None, :], conv_dw_b, conv_ln_g, conv_ln_b]
    rep_g = [g_norm_g, g_final_g[None, :], g_dwb, g_lng, g_lnb]
    rep_m = [m_norm_g, m_final_g[None, :], m_conv_dw_b, m_conv_ln_g, m_conv_ln_b]
    rep_v = [v_norm_g, v_final_g[None, :], v_conv_dw_b, v_conv_ln_g, v_conv_ln_b]
    rep = _adamw(pack(rep_w, 16), pack(rep_g, 16), pack(rep_m, 16), pack(rep_v, 16), "adamw_rep")
    rep_names = ["norm_g", "final_g", "conv_dw_b", "conv_ln_g", "conv_ln_b"]
    rep_rows = [(0, 4), (4, 5), (5, 7), (7, 9), (9, 11)]
    for nm, (lo, hi), gq, wq in zip(rep_names, rep_rows, rep_g, rep_w):
        shape = (d,) if nm == "final_g" else wq.shape
        res[nm] = (gq.reshape(shape), *[o[lo:hi].reshape(shape) for o in rep])

    sh_w = [conv_dw, pool_b_grp, pool_scale]
    sh_g = [g_dw, g_bg, g_sc]
    sh_m = [m_conv_dw, m_pool_b_grp, m_pool_scale]
    sh_v = [v_conv_dw, v_pool_b_grp, v_pool_scale]
    sh_total = 2 * ck + 4
    sh_pad = -(-sh_total // 8) * 8
    shd = _adamw(pack(sh_w, sh_pad), pack(sh_g, sh_pad), pack(sh_m, sh_pad), pack(sh_v, sh_pad), "adamw_shard")
    sh_names = ["conv_dw", "pool_b_grp", "pool_scale"]
    sh_rows = [(0, 2 * ck), (2 * ck, 2 * ck + 2), (2 * ck + 2, 2 * ck + 4)]
    for nm, (lo, hi), gq, wq in zip(sh_names, sh_rows, sh_g, sh_w):
        res[nm] = (gq.reshape(wq.shape), *[o[lo:hi].reshape(wq.shape) for o in shd])

    order = ["norm_g", "final_g", "conv_w_in", "conv_dw", "conv_dw_b", "conv_ln_g", "conv_ln_b", "conv_w_out",
             "pool_w_in", "pool_w_grp", "pool_b_grp", "pool_scale", "pool_w_out"]
    outs = [loss, grad_x]
    for part in range(4):
        outs += [res[nm][part] for nm in order]
    return tuple(outs)
```

```python
import functools

import jax
import jax.numpy as jnp
from jax import lax
from jax.experimental import pallas as pl
from jax.experimental.pallas import tpu as pltpu

F32 = jnp.float32
BF16 = jnp.bfloat16
MESH = pl.DeviceIdType.MESH

RMS_EPS = 1e-6
LN_EPS = 1e-5
CONV_K = 31
HALO = 32
PHALO = 16
POOL_WINDOWS = (2, 4, 8, 16)
N_CHIPS = 4
LANES = 128
ROW_CHUNK = 32
CONV_ROW_CHUNK = 32
FIR_BLOCK = 16
TOKEN_TILE = 512
MATMUL_TILE = 1024
WGRAD_TILE = 2048
VMEM_LIMIT = 56 * 1024 * 1024

ADAM_LR = 0.001
ADAM_B1 = 0.9
ADAM_B2 = 0.999
ADAM_EPS = 1e-08
ADAM_WD = 0.01
ADAM_STEP = 10


def _params(*sem):
    return pltpu.CompilerParams(dimension_semantics=sem, vmem_limit_bytes=VMEM_LIMIT)


def _sig(v):
    return 0.5 * jnp.tanh(0.5 * v) + 0.5


def _dsilu(v, sv):
    return sv * (1.0 + v * (1.0 - sv))


def _resident(shape):
    return pl.BlockSpec(shape, lambda *_: (0,) * len(shape), pipeline_mode=pl.Buffered(1))


def _tile(t):
    return min(TOKEN_TILE, t)


def _rms_matmul(h, g, w4, name):
    t, d = h.shape
    nk = w4.shape[-1]
    tm = min(MATMUL_TILE, t)

    def body(h_ref, g_ref, w_ref, p_ref, hn_ref):
        hh = h_ref[...]
        r = lax.rsqrt(jnp.mean(hh * hh, axis=-1, keepdims=True) + RMS_EPS)
        hn = (hh * r * g_ref[...]).astype(BF16)
        hn_ref[...] = hn
        for k in range(N_CHIPS):
            p_ref[:, k * nk:(k + 1) * nk] = jnp.dot(hn, w_ref[k], preferred_element_type=F32).astype(BF16)

    return pl.pallas_call(
        body, name=name, grid=(t // tm,),
        in_specs=[pl.BlockSpec((tm, d), lambda i: (i, 0)),
                  pl.BlockSpec((1, d), lambda i: (0, 0)),
                  _resident((N_CHIPS, d, nk))],
        out_specs=[pl.BlockSpec((tm, N_CHIPS * nk), lambda i: (i, 0)),
                   pl.BlockSpec((tm, d), lambda i: (i, 0))],
        out_shape=[jax.ShapeDtypeStruct((t, N_CHIPS * nk), BF16), jax.ShapeDtypeStruct((t, d), BF16)],
        compiler_params=_params("parallel"),
    )(h, g, w4)


def _to_token_tiles(ref, tok0, rows, val, ng):
    for j in range(ng):
        ref[pl.ds(tok0 * ng + j, rows, stride=ng), :] = val[:, j * LANES:(j + 1) * LANES]


def _from_token_tiles(ref, tok0, rows, ng):
    return jnp.concatenate([ref[pl.ds(tok0 * ng + j, rows, stride=ng), :] for j in range(ng)], axis=1)


def _conv_mid_fwd(p, dw3, dwb3, lg, lb):
    t = p.shape[0]
    e = p.shape[1] // 3
    ng = e // LANES
    tm = _tile(t)
    rc = CONV_ROW_CHUNK
    fb = FIR_BLOCK

    def body(p_ref, dw_ref, dwb_ref, lg_ref, lb_ref, s_ref, c_ref, u3, c3):
        i = pl.program_id(0)

        @pl.when(i == 0)
        def _():
            u3[0:HALO * ng, :] = jnp.zeros((HALO * ng, LANES), F32)

        def glu(rci, carry):
            base = pl.multiple_of(rci * rc, rc)
            a = p_ref[pl.ds(base, rc), 0:e].astype(F32)
            b = p_ref[pl.ds(base, rc), e:2 * e].astype(F32)
            _to_token_tiles(u3, HALO + base, rc, a * _sig(b), ng)
            return carry

        lax.fori_loop(0, tm // rc, glu, 0)

        def fir(bi, carry):
            t0 = bi * fb
            def x(q):
                return u3[pl.ds(pl.multiple_of((t0 + HALO - (CONV_K - 1) + q) * ng, ng), ng), :]

            xs = [x(q) for q in range(fb - 1)]
            accs = [dwb_ref[...]] * fb
            for k in range(CONV_K):
                wk = dw_ref[k * ng:(k + 1) * ng, :]
                xs.append(x(k + fb - 1))
                accs = [accs[q] + wk * xs[q + k] for q in range(fb)]
            for q in range(fb):
                c3[pl.ds(pl.multiple_of((t0 + q) * ng, ng), ng), :] = accs[q]
            return carry

        lax.fori_loop(0, tm // fb, fir, 0)
        u3[0:HALO * ng, :] = u3[tm * ng:(tm + HALO) * ng, :]

        def chunk(rci, carry):
            base = pl.multiple_of(rci * rc, rc)
            c = _from_token_tiles(c3, base, rc, ng)
            mu = jnp.mean(c, axis=-1, keepdims=True)
            cc = c - mu
            var = jnp.mean(cc * cc, axis=-1, keepdims=True)
            ln = cc * lax.rsqrt(var + LN_EPS) * lg_ref[...] + lb_ref[...]
            z = p_ref[pl.ds(base, rc), 2 * e:3 * e].astype(F32)
            s = (ln * _sig(ln)) * (z * _sig(z))
            s_ref[pl.ds(base, rc), :] = s.astype(BF16)
            c_ref[pl.ds(base, rc), :] = c.astype(BF16)
            return carry

        lax.fori_loop(0, tm // rc, chunk, 0, unroll=2)

    vec = pl.BlockSpec((1, e), lambda i: (0, 0))
    return pl.pallas_call(
        body, name="conv_mid_fwd", grid=(t // tm,),
        in_specs=[pl.BlockSpec((tm, 3 * e), lambda i: (i, 0)),
                  pl.BlockSpec((CONV_K * ng, LANES), lambda i: (0, 0)),
                  pl.BlockSpec((ng, LANES), lambda i: (0, 0)), vec, vec],
        out_specs=[pl.BlockSpec((tm, e), lambda i: (i, 0)), pl.BlockSpec((tm, e), lambda i: (i, 0))],
        out_shape=[jax.ShapeDtypeStruct((t, e), BF16), jax.ShapeDtypeStruct((t, e), BF16)],
        scratch_shapes=[pltpu.VMEM(((tm + HALO) * ng, LANES), F32), pltpu.VMEM((tm * ng, LANES), F32)],
        compiler_params=_params("arbitrary"),
    )(p, dw3, dwb3, lg, lb)


def _conv_mid_out_fwd(p, h, w_out, dw3, dwb3, lg, lb):
    t = p.shape[0]
    e = p.shape[1] // 3
    d = h.shape[1]
    ng = e // LANES
    tm = _tile(t)
    nt = t // tm
    rc = CONV_ROW_CHUNK
    fb = FIR_BLOCK

    def body(p_ref, h_ref, w_ref, dw_ref, dwb_ref, lg_ref, lb_ref, s_ref, c_ref, ho_ref, u3, c3, s_prev):
        i = pl.program_id(0)

        def project():
            ho_ref[...] = h_ref[...] + jnp.dot(s_prev[...], w_ref[...], preferred_element_type=F32)

        @pl.when(i == 0)
        def _():
            u3[0:HALO * ng, :] = jnp.zeros((HALO * ng, LANES), F32)
            s_prev[...] = jnp.zeros_like(s_prev)

        @pl.when(i < nt)
        def _():
            def glu(rci, carry):
                base = pl.multiple_of(rci * rc, rc)
                a = p_ref[pl.ds(base, rc), 0:e].astype(F32)
                b = p_ref[pl.ds(base, rc), e:2 * e].astype(F32)
                _to_token_tiles(u3, HALO + base, rc, a * _sig(b), ng)
                return carry

            lax.fori_loop(0, tm // rc, glu, 0)

            project()
            for bi in range(tm // fb):
                t0 = bi * fb
                xs = [u3[(t0 + HALO - (CONV_K - 1) + q) * ng:(t0 + HALO - (CONV_K - 1) + q + 1) * ng, :]
                      for q in range(fb - 1)]
                accs = [dwb_ref[...]] * fb
                for k in range(CONV_K):
                    wk = dw_ref[k * ng:(k + 1) * ng, :]
                    q1 = t0 + HALO - (CONV_K - 1) + k + fb - 1
                    xs.append(u3[q1 * ng:(q1 + 1) * ng, :])
                    accs = [accs[q] + wk * xs[q + k] for q in range(fb)]
                for q in range(fb):
                    c3[(t0 + q) * ng:(t0 + q + 1) * ng, :] = accs[q]
            u3[0:HALO * ng, :] = u3[tm * ng:(tm + HALO) * ng, :]

            def chunk(rci, carry):
                base = pl.multiple_of(rci * rc, rc)
                c = _from_token_tiles(c3, base, rc, ng)
                mu = jnp.mean(c, axis=-1, keepdims=True)
                cc = c - mu
                var = jnp.mean(cc * cc, axis=-1, keepdims=True)
                ln = cc * lax.rsqrt(var + LN_EPS) * lg_ref[...] + lb_ref[...]
                z = p_ref[pl.ds(base, rc), 2 * e:3 * e].astype(F32)
                s = ((ln * _sig(ln)) * (z * _sig(z))).astype(BF16)
                s_ref[pl.ds(base, rc), :] = s
                s_prev[pl.ds(base, rc), :] = s
                c_ref[pl.ds(base, rc), :] = c.astype(BF16)
                return carry

            lax.fori_loop(0, tm // rc, chunk, 0, unroll=2)

        @pl.when(i == nt)
        def _():
            project()

    vec = pl.BlockSpec((1, e), lambda i: (0, 0))
    cur = lambda i: (jnp.minimum(i, nt - 1), 0)
    lag = lambda i: (jnp.maximum(i - 1, 0), 0)
    return pl.pallas_call(
        body, name="conv_mid_out_fwd", grid=(nt + 1,),
        in_specs=[pl.BlockSpec((tm, 3 * e), cur), pl.BlockSpec((tm, d), lag), _resident((e, d)),
                  pl.BlockSpec((CONV_K * ng, LANES), lambda i: (0, 0)),
                  pl.BlockSpec((ng, LANES), lambda i: (0, 0)), vec, vec],
        out_specs=[pl.BlockSpec((tm, e), cur), pl.BlockSpec((tm, e), cur), pl.BlockSpec((tm, d), lag)],
        out_shape=[jax.ShapeDtypeStruct((t, e), BF16), jax.ShapeDtypeStruct((t, e), BF16),
                   jax.ShapeDtypeStruct((t, d), F32)],
        scratch_shapes=[pltpu.VMEM(((tm + HALO) * ng, LANES), F32), pltpu.VMEM((tm * ng, LANES), F32),
                        pltpu.VMEM((tm, e), BF16)],
        compiler_params=_params("arbitrary"),
    )(p, h, w_out, dw3, dwb3, lg, lb)


def _conv_fwd(h, g, w_in4, w_out, dw3, dwb3, lg, lb):
    t, d = h.shape
    nk = w_in4.shape[-1]
    e = N_CHIPS * nk // 3
    ng = e // LANES
    tm = _tile(t)
    nt = t // tm
    rc = CONV_ROW_CHUNK
    fb = FIR_BLOCK

    def body(hl_ref, hg_ref, g_ref, wi_ref, wo_ref, dw_ref, dwb_ref, lg_ref, lb_ref,
             p_ref, hn_ref, s_ref, c_ref, ho_ref, u3, c3, s_prev, p_s, hn_s):
        q = pl.program_id(0)
        nxt = q % 2
        cur = 1 - nxt

        def normed():
            hh = hl_ref[...]
            r = lax.rsqrt(jnp.mean(hh * hh, axis=-1, keepdims=True) + RMS_EPS)
            hn = (hh * r * g_ref[...]).astype(BF16)
            hn_s[nxt] = hn
            return hn

        def in_project(hn, k):
            p_s[nxt, :, k * nk:(k + 1) * nk] = jnp.dot(hn, wi_ref[k], preferred_element_type=F32).astype(BF16)

        def out_project():
            ho_ref[...] = hg_ref[...] + jnp.dot(s_prev[...], wo_ref[...], preferred_element_type=F32)

        @pl.when(q == 0)
        def _():
            u3[0:HALO * ng, :] = jnp.zeros((HALO * ng, LANES), F32)
            s_prev[...] = jnp.zeros_like(s_prev)
            hn0 = normed()
            for k in range(N_CHIPS):
                in_project(hn0, k)

        @pl.when(jnp.logical_and(q >= 1, q <= nt))
        def _():
            def glu(rci, carry):
                base = pl.multiple_of(rci * rc, rc)
                rows = pl.ds(base, rc)
                pa = p_s[cur, rows, 0:e]
                pb = p_s[cur, rows, e:2 * e]
                p_ref[rows, 0:e] = pa
                p_ref[rows, e:2 * e] = pb
                hn_ref[rows, :] = hn_s[cur, rows, :]
                _to_token_tiles(u3, HALO + base, rc, pa.astype(F32) * _sig(pb.astype(F32)), ng)
                return carry

            lax.fori_loop(0, tm // rc, glu, 0)

            nfir = tm // fb
            spots = {(m + 1) * nfir // (N_CHIPS + 2): m for m in range(N_CHIPS + 1)}
            hn = normed()
            for bi in range(nfir):
                if bi in spots:
                    if spots[bi] < N_CHIPS:
                        in_project(hn, spots[bi])
                    else:
                        out_project()
                t0 = bi * fb
                xs = [u3[(t0 + HALO - (CONV_K - 1) + j) * ng:(t0 + HALO - (CONV_K - 1) + j + 1) * ng, :]
                      for j in range(fb - 1)]
                accs = [dwb_ref[...]] * fb
                for k in range(CONV_K):
                    wk = dw_ref[k * ng:(k + 1) * ng, :]
                    j1 = t0 + HALO - (CONV_K - 1) + k + fb - 1
                    xs.append(u3[j1 * ng:(j1 + 1) * ng, :])
                    accs = [accs[j] + wk * xs[j + k] for j in range(fb)]
                for j in range(fb):
                    c3[(t0 + j) * ng:(t0 + j + 1) * ng, :] = accs[j]
            u3[0:HALO * ng, :] = u3[tm * ng:(tm + HALO) * ng, :]

            def chunk(rci, carry):
                base = pl.multiple_of(rci * rc, rc)
                rows = pl.ds(base, rc)
                c = _from_token_tiles(c3, base, rc, ng)
                mu = jnp.mean(c, axis=-1, keepdims=True)
                cc = c - mu
                var = jnp.mean(cc * cc, axis=-1, keepdims=True)
                ln = cc * lax.rsqrt(var + LN_EPS) * lg_ref[...] + lb_ref[...]
                pz = p_s[cur, rows, 2 * e:3 * e]
                p_ref[rows, 2 * e:3 * e] = pz
                z = pz.astype(F32)
                s = ((ln * _sig(ln)) * (z * _sig(z))).astype(BF16)
                s_ref[rows, :] = s
                s_prev[rows, :] = s
                c_ref[rows, :] = c.astype(BF16)
                return carry

            lax.fori_loop(0, tm // rc, chunk, 0, unroll=2)

        @pl.when(q == nt + 1)
        def _():
            out_project()

    vec = pl.BlockSpec((1, e), lambda q: (0, 0))
    lead = lambda q: (jnp.clip(q, 0, nt - 1), 0)
    cur_map = lambda q: (jnp.clip(q - 1, 0, nt - 1), 0)
    lag = lambda q: (jnp.clip(q - 2, 0, nt - 1), 0)
    return pl.pallas_call(
        body, name="conv_fwd", grid=(nt + 2,),
        in_specs=[pl.BlockSpec((tm, d), lead), pl.BlockSpec((tm, d), lag), pl.BlockSpec((1, d), lambda q: (0, 0)),
                  _resident((N_CHIPS, d, nk)), _resident((e, d)),
                  pl.BlockSpec((CONV_K * ng, LANES), lambda q: (0, 0)),
                  pl.BlockSpec((ng, LANES), lambda q: (0, 0)), vec, vec],
        out_specs=[pl.BlockSpec((tm, 3 * e), cur_map), pl.BlockSpec((tm, d), cur_map),
                   pl.BlockSpec((tm, e), cur_map), pl.BlockSpec((tm, e), cur_map), pl.BlockSpec((tm, d), lag)],
        out_shape=[jax.ShapeDtypeStruct((t, 3 * e), BF16), jax.ShapeDtypeStruct((t, d), BF16),
                   jax.ShapeDtypeStruct((t, e), BF16), jax.ShapeDtypeStruct((t, e), BF16),
                   jax.ShapeDtypeStruct((t, d), F32)],
        scratch_shapes=[pltpu.VMEM(((tm + HALO) * ng, LANES), F32), pltpu.VMEM((tm * ng, LANES), F32),
                        pltpu.VMEM((tm, e), BF16), pltpu.VMEM((2, tm, 3 * e), BF16),
                        pltpu.VMEM((2, tm, d), BF16)],
        compiler_params=_params("arbitrary"),
    )(h, h, g, w_in4, w_out, dw3, dwb3, lg, lb)


def _pool_group(lc, e):
    return (lc * LANES) // (e // len(POOL_WINDOWS))


def _pool_inv_count(row0, rows, w):
    tpos = row0 + lax.broadcasted_iota(jnp.int32, (rows, 1), 0)
    return 1.0 / jnp.minimum(tpos + 1, w).astype(F32)


def _pool_window_dev(ubuf, base, rc, e, row0, dbuf):
    n = rc + PHALO
    for lc in range(e // LANES):
        lanes = slice(lc * LANES, (lc + 1) * LANES)
        g = _pool_group(lc, e)
        w = POOL_WINDOWS[g]
        blk = ubuf[pl.ds(base, n), lanes]
        acc = blk
        step = 1
        while step < w:
            acc = acc + pltpu.roll(acc, step, 0)
            step *= 2
        win = acc[PHALO:n]
        tok = blk[PHALO:n]
        dbuf[pl.ds(base, rc), lanes] = win * _pool_inv_count(row0 + base, rc, w) - tok


def _pool_mid_out_fwd(p, h, w_out, wg, bg, sc):
    t = p.shape[0]
    e = p.shape[1] // 2
    d = h.shape[1]
    gc = e // len(POOL_WINDOWS)
    tm = _tile(t)
    nt = t // tm
    rc = ROW_CHUNK

    def body(p_ref, h_ref, w_ref, wg_ref, bg_ref, sc_ref, s_ref, ho_ref, ubuf, dbuf, s_prev):
        i = pl.program_id(0)

        def project():
            ho_ref[...] = h_ref[...] + jnp.dot(s_prev[...], w_ref[...], preferred_element_type=F32)

        @pl.when(i == 0)
        def _():
            ubuf[0:PHALO, :] = jnp.zeros((PHALO, e), F32)
            s_prev[...] = jnp.zeros_like(s_prev)

        @pl.when(i < nt)
        def _():
            ubuf[PHALO:PHALO + tm, :] = p_ref[:, 0:e].astype(F32)

            def chunk(rci, carry):
                base = pl.multiple_of(rci * rc, rc)
                _pool_window_dev(ubuf, base, rc, e, i * tm, dbuf)
                return carry

            lax.fori_loop(0, tm // rc, chunk, 0)
            ubuf[0:PHALO, :] = ubuf[tm:tm + PHALO, :]

            project()
            for g in range(len(POOL_WINDOWS)):
                cols = slice(g * gc, (g + 1) * gc)
                yg = jnp.dot(dbuf[:, cols].astype(BF16), wg_ref[g], preferred_element_type=F32)
                z = p_ref[:, e + g * gc:e + (g + 1) * gc].astype(F32)
                s = (((yg + bg_ref[:, cols]) * sc_ref[:, cols]) * (z * _sig(z))).astype(BF16)
                s_ref[:, cols] = s
                s_prev[:, cols] = s

        @pl.when(i == nt)
        def _():
            project()

    vec = pl.BlockSpec((1, e), lambda i: (0, 0))
    cur = lambda i: (jnp.minimum(i, nt - 1), 0)
    lag = lambda i: (jnp.maximum(i - 1, 0), 0)
    return pl.pallas_call(
        body, name="pool_mid_out_fwd", grid=(nt + 1,),
        in_specs=[pl.BlockSpec((tm, 2 * e), cur), pl.BlockSpec((tm, d), lag), _resident((e, d)),
                  pl.BlockSpec((len(POOL_WINDOWS), gc, gc), lambda i: (0, 0, 0)), vec, vec],
        out_specs=[pl.BlockSpec((tm, e), cur), pl.BlockSpec((tm, d), lag)],
        out_shape=[jax.ShapeDtypeStruct((t, e), BF16), jax.ShapeDtypeStruct((t, d), F32)],
        scratch_shapes=[pltpu.VMEM((tm + PHALO, e), F32), pltpu.VMEM((tm, e), F32), pltpu.VMEM((tm, e), BF16)],
        compiler_params=_params("arbitrary"),
    )(p, h, w_out, wg, bg, sc)


def _matmul_res(h, s, w):
    t, d = h.shape
    e = s.shape[1]
    tm = min(MATMUL_TILE, t)

    def body(h_ref, s_ref, w_ref, o_ref):
        o_ref[...] = h_ref[...] + jnp.dot(s_ref[...], w_ref[...], preferred_element_type=F32)

    return pl.pallas_call(
        body, name="matmul_res", grid=(t // tm,),
        in_specs=[pl.BlockSpec((tm, d), lambda i: (i, 0)), pl.BlockSpec((tm, e), lambda i: (i, 0)),
                  _resident((e, d))],
        out_specs=pl.BlockSpec((tm, d), lambda i: (i, 0)),
        out_shape=jax.ShapeDtypeStruct((t, d), F32),
        compiler_params=_params("parallel"),
    )(h, s, w)


def _loss_head(h, fg, tgt):
    t, d = h.shape
    tm = min(MATMUL_TILE, t)

    def body(h_ref, g_ref, t_ref, dh_ref, loss_ref, dg_ref):
        i = pl.program_id(0)

        @pl.when(i == 0)
        def _():
            loss_ref[...] = jnp.zeros_like(loss_ref)
            dg_ref[...] = jnp.zeros_like(dg_ref)

        hh = h_ref[...]
        r = lax.rsqrt(jnp.mean(hh * hh, axis=-1, keepdims=True) + RMS_EPS)
        hhat = hh * r
        err = hhat * g_ref[...] - t_ref[...]
        per_tok = jnp.mean(err * err, axis=-1, keepdims=True)
        loss_ref[...] += 0.5 * jnp.sum(per_tok, axis=0, keepdims=True)
        dy = err * (1.0 / d)
        tt = dy * g_ref[...]
        dh_ref[...] = r * (tt - hhat * jnp.mean(tt * hhat, axis=-1, keepdims=True))
        dg_ref[...] += jnp.sum(dy * hhat, axis=0, keepdims=True)

    return pl.pallas_call(
        body, name="loss_head", grid=(t // tm,),
        in_specs=[pl.BlockSpec((tm, d), lambda i: (i, 0)), pl.BlockSpec((1, d), lambda i: (0, 0)),
                  pl.BlockSpec((tm, d), lambda i: (i, 0))],
        out_specs=[pl.BlockSpec((tm, d), lambda i: (i, 0)), pl.BlockSpec((1, LANES), lambda i: (0, 0)),
                   pl.BlockSpec((1, d), lambda i: (0, 0))],
        out_shape=[jax.ShapeDtypeStruct((t, d), F32), jax.ShapeDtypeStruct((1, LANES), F32),
                   jax.ShapeDtypeStruct((1, d), F32)],
        compiler_params=_params("arbitrary"),
    )(h, fg, tgt)


def _ds_matmul(dy, w):
    t, d = dy.shape
    e = w.shape[0]
    tm = min(MATMUL_TILE, t)

    def body(dy_ref, w_ref, ds_ref):
        ds_ref[...] = lax.dot_general(dy_ref[...].astype(BF16), w_ref[...], (((1,), (1,)), ((), ())),
                                      preferred_element_type=F32).astype(BF16)

    return pl.pallas_call(
        body, name="ds_matmul", grid=(t // tm,),
        in_specs=[pl.BlockSpec((tm, d), lambda i: (i, 0)), _resident((e, d))],
        out_specs=pl.BlockSpec((tm, e), lambda i: (i, 0)),
        out_shape=jax.ShapeDtypeStruct((t, e), BF16),
        compiler_params=_params("parallel"),
    )(dy, w)


def _conv_mid_bwd(p, c, ds, dw3, lg, lb):
    t = p.shape[0]
    e = p.shape[1] // 3
    ng = e // LANES
    tm = _tile(t)
    nt = t // tm
    rc = CONV_ROW_CHUNK
    fb = FIR_BLOCK
    hb = tm // HALO

    def body(p_ref, ph_ref, c_ref, ds_ref, dw_ref, lg_ref, lb_ref,
             dp_ref, dlg_ref, dlb_ref, ddwb_ref, ddw_ref, u3, dc3, du3):
        i = pl.program_id(0)
        ti = nt - 1 - i

        @pl.when(i == 0)
        def _():
            dc3[tm * ng:(tm + HALO) * ng, :] = jnp.zeros((HALO * ng, LANES), F32)
            dlg_ref[...] = jnp.zeros_like(dlg_ref)
            dlb_ref[...] = jnp.zeros_like(dlb_ref)
            ddwb_ref[...] = jnp.zeros_like(ddwb_ref)
            ddw_ref[...] = jnp.zeros_like(ddw_ref)

        ha = ph_ref[:, 0:e].astype(F32)
        hbb = ph_ref[:, e:2 * e].astype(F32)
        _to_token_tiles(u3, 0, HALO, jnp.where(ti > 0, ha * _sig(hbb), 0.0), ng)

        def front(rci, carry):
            slg, slb, sdwb = carry
            base = pl.multiple_of(rci * rc, rc)
            rows = pl.ds(base, rc)
            a = p_ref[rows, 0:e].astype(F32)
            b = p_ref[rows, e:2 * e].astype(F32)
            _to_token_tiles(u3, HALO + base, rc, a * _sig(b), ng)
            cv = c_ref[rows, :].astype(F32)
            mu = jnp.mean(cv, axis=-1, keepdims=True)
            cc = cv - mu
            var = jnp.mean(cc * cc, axis=-1, keepdims=True)
            rs = lax.rsqrt(var + LN_EPS)
            nn = cc * rs
            ln = nn * lg_ref[...] + lb_ref[...]
            z = p_ref[rows, 2 * e:3 * e].astype(F32)
            sz = _sig(z)
            sl = _sig(ln)
            dsv = ds_ref[rows, :].astype(F32)
            dln = dsv * (z * sz) * _dsilu(ln, sl)
            dz = dsv * (ln * sl) * _dsilu(z, sz)
            dp_ref[rows, 2 * e:3 * e] = dz.astype(BF16)
            dn = dln * lg_ref[...]
            dc = rs * (dn - jnp.mean(dn, axis=-1, keepdims=True)
                       - nn * jnp.mean(dn * nn, axis=-1, keepdims=True))
            _to_token_tiles(dc3, base, rc, dc, ng)
            return (slg + jnp.sum(dln * nn, axis=0, keepdims=True),
                    slb + jnp.sum(dln, axis=0, keepdims=True),
                    sdwb + jnp.sum(dc, axis=0, keepdims=True))

        zero = jnp.zeros((1, e), F32)
        slg, slb, sdwb = lax.fori_loop(0, tm // rc, front, (zero, zero, zero), unroll=2)
        dlg_ref[...] += slg
        dlb_ref[...] += slb
        ddwb_ref[...] += sdwb

        def fir(bi, carry):
            t0 = bi * fb

            def dcs(q):
                return dc3[pl.ds(pl.multiple_of((t0 + q) * ng, ng), ng), :]

            def us(q):
                return u3[pl.ds(pl.multiple_of((t0 + HALO - (CONV_K - 1) + q) * ng, ng), ng), :]

            xs = [dcs(q) for q in range(fb - 1)]
            accs = [None] * fb
            for j in range(CONV_K):
                wk = dw_ref[(CONV_K - 1 - j) * ng:(CONV_K - j) * ng, :]
                xs.append(dcs(j + fb - 1))
                accs = [wk * xs[q + j] if accs[q] is None else accs[q] + wk * xs[q + j] for q in range(fb)]
            for q in range(fb):
                du3[pl.ds(pl.multiple_of((t0 + q) * ng, ng), ng), :] = accs[q]
            own = xs[0:fb]
            ys = [us(q) for q in range(fb - 1)]
            for k in range(CONV_K):
                ys.append(us(k + fb - 1))
                prods = [own[q] * ys[q + k] for q in range(fb)]
                while len(prods) > 1:
                    prods = [prods[2 * v] + prods[2 * v + 1] for v in range(len(prods) // 2)]
                ddw_ref[k * ng:(k + 1) * ng, :] += prods[0]
            return carry

        lax.fori_loop(0, tm // fb, fir, 0)
        dc3[tm * ng:(tm + HALO) * ng, :] = dc3[0:HALO * ng, :]

        def back(rci, carry):
            base = pl.multiple_of(rci * rc, rc)
            rows = pl.ds(base, rc)
            a = p_ref[rows, 0:e].astype(F32)
            b = p_ref[rows, e:2 * e].astype(F32)
            sb = _sig(b)
            duv = _from_token_tiles(du3, base, rc, ng)
            dp_ref[rows, 0:e] = (duv * sb).astype(BF16)
            dp_ref[rows, e:2 * e] = (duv * a * sb * (1.0 - sb)).astype(BF16)
            return carry

        lax.fori_loop(0, tm // rc, back, 0, unroll=2)

    vec = pl.BlockSpec((1, e), lambda i: (0, 0))
    taps = pl.BlockSpec((CONV_K * ng, LANES), lambda i: (0, 0))
    rev = lambda i: (nt - 1 - i, 0)
    halo = lambda i: (jnp.maximum((nt - 1 - i) * hb - 1, 0), 0)
    return pl.pallas_call(
        body, name="conv_mid_bwd", grid=(nt,),
        in_specs=[pl.BlockSpec((tm, 3 * e), rev), pl.BlockSpec((HALO, 3 * e), halo),
                  pl.BlockSpec((tm, e), rev), pl.BlockSpec((tm, e), rev), taps, vec, vec],
        out_specs=[pl.BlockSpec((tm, 3 * e), rev), vec, vec, vec, taps],
        out_shape=[jax.ShapeDtypeStruct((t, 3 * e), BF16), jax.ShapeDtypeStruct((1, e), F32),
                   jax.ShapeDtypeStruct((1, e), F32), jax.ShapeDtypeStruct((1, e), F32),
                   jax.ShapeDtypeStruct((CONV_K * ng, LANES), F32)],
        scratch_shapes=[pltpu.VMEM(((tm + HALO) * ng, LANES), F32), pltpu.VMEM(((tm + HALO) * ng, LANES), F32),
                        pltpu.VMEM((tm * ng, LANES), F32)],
        compiler_params=_params("arbitrary"),
    )(p, p, c, ds, dw3, lg, lb)


def _conv_bwd(p, c, dy, hn, w_out, dw3, lg, lb):
    t = p.shape[0]
    e = p.shape[1] // 3
    d = hn.shape[1]
    nk = 3 * e // N_CHIPS
    ng = e // LANES
    tm = _tile(t)
    nt = t // tm
    rc = CONV_ROW_CHUNK
    fb = FIR_BLOCK
    hb = tm // HALO

    def body(p_ref, ph_ref, c_ref, dyn_ref, hnp_ref, w_ref, dw_ref, lg_ref, lb_ref,
             dp_ref, dlg_ref, dlb_ref, ddwb_ref, ddw_ref, dwi_ref, u3, dc3, du3, ds_s, dp_s):
        g = pl.program_id(0)
        ti = nt - g

        def next_ds():
            ds_s[...] = lax.dot_general(dyn_ref[...].astype(BF16), w_ref[...], (((1,), (1,)), ((), ())),
                                        preferred_element_type=F32).astype(BF16)

        def prev_wgrad(slot):
            for k in range(N_CHIPS):
                dwi_ref[k] += lax.dot_general(hnp_ref[...], dp_s[slot, :, k * nk:(k + 1) * nk],
                                              (((0,), (0,)), ((), ())), preferred_element_type=F32)

        @pl.when(g == 0)
        def _():
            dc3[tm * ng:(tm + HALO) * ng, :] = jnp.zeros((HALO * ng, LANES), F32)
            dlg_ref[...] = jnp.zeros_like(dlg_ref)
            dlb_ref[...] = jnp.zeros_like(dlb_ref)
            ddwb_ref[...] = jnp.zeros_like(ddwb_ref)
            ddw_ref[...] = jnp.zeros_like(ddw_ref)
            dwi_ref[...] = jnp.zeros_like(dwi_ref)
            dp_s[0] = jnp.zeros_like(dp_s[0])
            next_ds()

        @pl.when(jnp.logical_and(g >= 1, g <= nt))
        def _():
            ha = ph_ref[:, 0:e].astype(F32)
            hbb = ph_ref[:, e:2 * e].astype(F32)
            _to_token_tiles(u3, 0, HALO, jnp.where(ti > 0, ha * _sig(hbb), 0.0), ng)
            cur = g % 2

            def front(rci, carry):
                slg, slb, sdwb = carry
                base = pl.multiple_of(rci * rc, rc)
                rows = pl.ds(base, rc)
                a = p_ref[rows, 0:e].astype(F32)
                b = p_ref[rows, e:2 * e].astype(F32)
                _to_token_tiles(u3, HALO + base, rc, a * _sig(b), ng)
                cv = c_ref[rows, :].astype(F32)
                mu = jnp.mean(cv, axis=-1, keepdims=True)
                cc = cv - mu
                var = jnp.mean(cc * cc, axis=-1, keepdims=True)
                rs = lax.rsqrt(var + LN_EPS)
                nn = cc * rs
                ln = nn * lg_ref[...] + lb_ref[...]
                z = p_ref[rows, 2 * e:3 * e].astype(F32)
                sz = _sig(z)
                sl = _sig(ln)
                dsv = ds_s[rows, :].astype(F32)
                dln = dsv * (z * sz) * _dsilu(ln, sl)
                dzb = (dsv * (ln * sl) * _dsilu(z, sz)).astype(BF16)
                dp_ref[rows, 2 * e:3 * e] = dzb
                dp_s[cur, rows, 2 * e:3 * e] = dzb
                dn = dln * lg_ref[...]
                dc = rs * (dn - jnp.mean(dn, axis=-1, keepdims=True)
                           - nn * jnp.mean(dn * nn, axis=-1, keepdims=True))
                _to_token_tiles(dc3, base, rc, dc, ng)
                return (slg + jnp.sum(dln * nn, axis=0, keepdims=True),
                        slb + jnp.sum(dln, axis=0, keepdims=True),
                        sdwb + jnp.sum(dc, axis=0, keepdims=True))

            zero = jnp.zeros((1, e), F32)
            slg, slb, sdwb = lax.fori_loop(0, tm // rc, front, (zero, zero, zero), unroll=2)
            dlg_ref[...] += slg
            dlb_ref[...] += slb
            ddwb_ref[...] += sdwb

            next_ds()
            prev_wgrad(1 - cur)
            for bi in range(tm // fb):
                t0 = bi * fb

                def dcs(q):
                    return dc3[(t0 + q) * ng:(t0 + q + 1) * ng, :]

                def us(q):
                    r0 = t0 + HALO - (CONV_K - 1) + q
                    return u3[r0 * ng:(r0 + 1) * ng, :]

                xs = [dcs(q) for q in range(fb - 1)]
                accs = [None] * fb
                for j in range(CONV_K):
                    wk = dw_ref[(CONV_K - 1 - j) * ng:(CONV_K - j) * ng, :]
                    xs.append(dcs(j + fb - 1))
                    accs = [wk * xs[q + j] if accs[q] is None else accs[q] + wk * xs[q + j] for q in range(fb)]
                for q in range(fb):
                    du3[(t0 + q) * ng:(t0 + q + 1) * ng, :] = accs[q]
                own = xs[0:fb]
                ys = [us(q) for q in range(fb - 1)]
                for k in range(CONV_K):
                    ys.append(us(k + fb - 1))
                    prods = [own[q] * ys[q + k] for q in range(fb)]
                    while len(prods) > 1:
                        prods = [prods[2 * v] + prods[2 * v + 1] for v in range(len(prods) // 2)]
                    ddw_ref[k * ng:(k + 1) * ng, :] += prods[0]
            dc3[tm * ng:(tm + HALO) * ng, :] = dc3[0:HALO * ng, :]

            def back(rci, carry):
                base = pl.multiple_of(rci * rc, rc)
                rows = pl.ds(base, rc)
                a = p_ref[rows, 0:e].astype(F32)
                b = p_ref[rows, e:2 * e].astype(F32)
                sb = _sig(b)
                duv = _from_token_tiles(du3, base, rc, ng)
                dab = (duv * sb).astype(BF16)
                dbb = (duv * a * sb * (1.0 - sb)).astype(BF16)
                dp_ref[rows, 0:e] = dab
                dp_ref[rows, e:2 * e] = dbb
                dp_s[cur, rows, 0:e] = dab
                dp_s[cur, rows, e:2 * e] = dbb
                return carry

            lax.fori_loop(0, tm // rc, back, 0, unroll=2)

        @pl.when(g == nt + 1)
        def _():
            prev_wgrad(nt % 2)

    vec = pl.BlockSpec((1, e), lambda g: (0, 0))
    taps = pl.BlockSpec((CONV_K * ng, LANES), lambda g: (0, 0))
    tile_of = lambda g: jnp.clip(nt - g, 0, nt - 1)
    cur_map = lambda g: (tile_of(g), 0)
    halo = lambda g: (jnp.maximum(tile_of(g) * hb - 1, 0), 0)
    nxt_map = lambda g: (jnp.clip(nt - 1 - g, 0, nt - 1), 0)
    prv_map = lambda g: (jnp.clip(nt + 1 - g, 0, nt - 1), 0)
    return pl.pallas_call(
        body, name="conv_bwd", grid=(nt + 2,),
        in_specs=[pl.BlockSpec((tm, 3 * e), cur_map), pl.BlockSpec((HALO, 3 * e), halo),
                  pl.BlockSpec((tm, e), cur_map), pl.BlockSpec((tm, d), nxt_map),
                  pl.BlockSpec((tm, d), prv_map), _resident((e, d)), taps, vec, vec],
        out_specs=[pl.BlockSpec((tm, 3 * e), cur_map), vec, vec, vec, taps,
                   pl.BlockSpec((N_CHIPS, d, nk), lambda g: (0, 0, 0))],
        out_shape=[jax.ShapeDtypeStruct((t, 3 * e), BF16), jax.ShapeDtypeStruct((1, e), F32),
                   jax.ShapeDtypeStruct((1, e), F32), jax.ShapeDtypeStruct((1, e), F32),
                   jax.ShapeDtypeStruct((CONV_K * ng, LANES), F32),
                   jax.ShapeDtypeStruct((N_CHIPS, d, nk), F32)],
        scratch_shapes=[pltpu.VMEM(((tm + HALO) * ng, LANES), F32), pltpu.VMEM(((tm + HALO) * ng, LANES), F32),
                        pltpu.VMEM((tm * ng, LANES), F32), pltpu.VMEM((tm, e), BF16),
                        pltpu.VMEM((2, tm, 3 * e), BF16)],
        compiler_params=_params("arbitrary"),
    )(p, p, c, dy, hn, w_out, dw3, lg, lb)


def _pool_mid_bwd(p, dy, w_out, wg, bg, sc):
    t = p.shape[0]
    e = p.shape[1] // 2
    d = dy.shape[1]
    ng = len(POOL_WINDOWS)
    gc = e // ng
    tm = _tile(t)
    nt = t // tm
    rc = ROW_CHUNK
    hb = tm // PHALO

    def body(p_ref, ph_ref, dyc_ref, dyn_ref, w_ref, wg_ref, bg_ref, sc_ref,
             dp_ref, dwg_ref, dbg_ref, dsc_ref, ubuf, dbuf, ebuf, ddbuf, ds_s):
        i = pl.program_id(0)
        ti = nt - 1 - i

        def project(dy_ref):
            return lax.dot_general(dy_ref[...].astype(BF16), w_ref[...], (((1,), (1,)), ((), ())),
                                   preferred_element_type=F32).astype(BF16)

        @pl.when(i == 0)
        def _():
            ebuf[tm:tm + PHALO, :] = jnp.zeros((PHALO, e), F32)
            dwg_ref[...] = jnp.zeros_like(dwg_ref)
            dbg_ref[...] = jnp.zeros_like(dbg_ref)
            dsc_ref[...] = jnp.zeros_like(dsc_ref)
            ds_s[...] = project(dyc_ref)

        ubuf[0:PHALO, :] = jnp.where(ti > 0, ph_ref[:, 0:e].astype(F32), 0.0)
        ubuf[PHALO:PHALO + tm, :] = p_ref[:, 0:e].astype(F32)

        def recompute(rci, carry):
            base = pl.multiple_of(rci * rc, rc)
            _pool_window_dev(ubuf, base, rc, e, ti * tm, dbuf)
            return carry

        lax.fori_loop(0, tm // rc, recompute, 0)

        for g in range(ng):
            cols = slice(g * gc, (g + 1) * gc)
            dg = dbuf[:, cols].astype(BF16)
            q = jnp.dot(dg, wg_ref[g], preferred_element_type=F32) + bg_ref[:, cols]
            z = p_ref[:, e + g * gc:e + (g + 1) * gc].astype(F32)
            sz = _sig(z)
            dsv = ds_s[:, cols].astype(F32)
            dz = dsv * (q * sc_ref[:, cols]) * _dsilu(z, sz)
            dp_ref[:, e + g * gc:e + (g + 1) * gc] = dz.astype(BF16)
            dy2 = dsv * (z * sz)
            dsc_ref[:, cols] += jnp.sum(dy2 * q, axis=0, keepdims=True)
            dq = dy2 * sc_ref[:, cols]
            dbg_ref[:, cols] += jnp.sum(dq, axis=0, keepdims=True)
            dqb = dq.astype(BF16)
            dwg_ref[g] += lax.dot_general(dg, dqb, (((0,), (0,)), ((), ())), preferred_element_type=F32)
            ddbuf[:, cols] = lax.dot_general(dqb, wg_ref[g], (((1,), (1,)), ((), ())),
                                             preferred_element_type=F32)
        ds_s[...] = project(dyn_ref)

        def scale(rci, carry):
            base = pl.multiple_of(rci * rc, rc)
            for lc in range(e // LANES):
                lanes = slice(lc * LANES, (lc + 1) * LANES)
                w = POOL_WINDOWS[_pool_group(lc, e)]
                ebuf[pl.ds(base, rc), lanes] = (ddbuf[pl.ds(base, rc), lanes]
                                                * _pool_inv_count(ti * tm + base, rc, w))
            return carry

        lax.fori_loop(0, tm // rc, scale, 0)

        def chunk(rci, carry):
            base = pl.multiple_of(rci * rc, rc)
            n = rc + PHALO
            for lc in range(e // LANES):
                lanes = slice(lc * LANES, (lc + 1) * LANES)
                w = POOL_WINDOWS[_pool_group(lc, e)]
                acc = ebuf[pl.ds(base, n), lanes]
                step = 1
                while step < w:
                    acc = acc + pltpu.roll(acc, n - step, 0)
                    step *= 2
                du = acc[0:rc] - ddbuf[pl.ds(base, rc), lanes]
                dp_ref[pl.ds(base, rc), lanes] = du.astype(BF16)
            return carry

        lax.fori_loop(0, tm // rc, chunk, 0)
        ebuf[tm:tm + PHALO, :] = ebuf[0:PHALO, :]

    vec = pl.BlockSpec((1, e), lambda i: (0, 0))
    rev = lambda i: (nt - 1 - i, 0)
    halo = lambda i: (jnp.maximum((nt - 1 - i) * hb - 1, 0), 0)
    wspec = pl.BlockSpec((ng, gc, gc), lambda i: (0, 0, 0))
    nxt = lambda i: (jnp.maximum(nt - 2 - i, 0), 0)
    return pl.pallas_call(
        body, name="pool_mid_bwd", grid=(nt,),
        in_specs=[pl.BlockSpec((tm, 2 * e), rev), pl.BlockSpec((PHALO, 2 * e), halo),
                  pl.BlockSpec((tm, d), rev), pl.BlockSpec((tm, d), nxt), _resident((e, d)), wspec, vec, vec],
        out_specs=[pl.BlockSpec((tm, 2 * e), rev), wspec, vec, vec],
        out_shape=[jax.ShapeDtypeStruct((t, 2 * e), BF16), jax.ShapeDtypeStruct((ng, gc, gc), F32),
                   jax.ShapeDtypeStruct((1, e), F32), jax.ShapeDtypeStruct((1, e), F32)],
        scratch_shapes=[pltpu.VMEM((tm + PHALO, e), F32), pltpu.VMEM((tm, e), F32),
                        pltpu.VMEM((tm + PHALO, e), F32), pltpu.VMEM((tm, e), F32), pltpu.VMEM((tm, e), BF16)],
        compiler_params=_params("arbitrary"),
    )(p, p, dy, dy, w_out, wg, bg, sc)


def _dhn_rms_bwd(dp, w4, h, g, dh_out, name):
    t, d = h.shape
    nk = w4.shape[-1]
    tm = min(MATMUL_TILE, t)
    nt = t // tm

    def body(dp_ref, w_ref, h_ref, g_ref, dho_ref, dh_ref, dg_ref, dhn_s):
        i = pl.program_id(0)

        @pl.when(i == 0)
        def _():
            dg_ref[...] = jnp.zeros_like(dg_ref)
            dhn_s[...] = jnp.zeros_like(dhn_s)

        def epilogue():
            dhn = dhn_s[...]
            hh = h_ref[...]
            r = lax.rsqrt(jnp.mean(hh * hh, axis=-1, keepdims=True) + RMS_EPS)
            hhat = hh * r
            tt = dhn * g_ref[...]
            dh_ref[...] = dho_ref[...] + r * (tt - hhat * jnp.mean(tt * hhat, axis=-1, keepdims=True))
            dg_ref[...] += jnp.sum(dhn * hhat, axis=0, keepdims=True)

        @pl.when(i < nt)
        def _():
            epilogue()
            dhn = jnp.zeros((tm, d), F32)
            for k in range(N_CHIPS):
                dhn = dhn + lax.dot_general(dp_ref[:, k * nk:(k + 1) * nk], w_ref[k], (((1,), (1,)), ((), ())),
                                            preferred_element_type=F32)
            dhn_s[...] = dhn

        @pl.when(i == nt)
        def _():
            epilogue()

    cur = lambda i: (jnp.minimum(i, nt - 1), 0)
    lag = lambda i: (jnp.maximum(i - 1, 0), 0)
    return pl.pallas_call(
        body, name=name, grid=(nt + 1,),
        in_specs=[pl.BlockSpec((tm, N_CHIPS * nk), cur),
                  _resident((N_CHIPS, d, nk)),
                  pl.BlockSpec((tm, d), lag), pl.BlockSpec((1, d), lambda i: (0, 0)),
                  pl.BlockSpec((tm, d), lag)],
        out_specs=[pl.BlockSpec((tm, d), lag), pl.BlockSpec((1, d), lambda i: (0, 0))],
        out_shape=[jax.ShapeDtypeStruct((t, d), F32), jax.ShapeDtypeStruct((1, d), F32)],
        scratch_shapes=[pltpu.VMEM((tm, d), F32)],
        compiler_params=_params("arbitrary"),
    )(dp, w4, h, g, dh_out)


def _wgrad(a, b, nblk, name):
    t, m = a.shape
    nn = b.shape[1] // nblk
    tk = min(WGRAD_TILE, t)
    nk = t // tk

    def body(a_ref, b_ref, o_ref, ob_ref):
        @pl.when(pl.program_id(1) == 0)
        def _():
            o_ref[...] = jnp.zeros_like(o_ref)

        o_ref[...] += lax.dot_general(a_ref[...].astype(BF16), b_ref[...].astype(BF16),
                                      (((0,), (0,)), ((), ())), preferred_element_type=F32)

        @pl.when(pl.program_id(1) == nk - 1)
        def _():
            ob_ref[...] = o_ref[...].astype(BF16)

    ospec = pl.BlockSpec((None, m, nn), lambda j, i: (j, 0, 0))
    return pl.pallas_call(
        body, name=name, grid=(nblk, nk),
        in_specs=[pl.BlockSpec((tk, m), lambda j, i: (i, 0)), pl.BlockSpec((tk, nn), lambda j, i: (i, j))],
        out_specs=[ospec, ospec],
        out_shape=[jax.ShapeDtypeStruct((nblk, m, nn), F32), jax.ShapeDtypeStruct((nblk, m, nn), BF16)],
        compiler_params=_params("parallel", "arbitrary"),
    )(a, b)


def _rows2d(shape):
    rows = 1
    for s in shape[:-1]:
        rows *= s
    return rows, shape[-1]


def _row_tile(rows):
    for cand in (512, 256, 128, 64, 32, 16, 8):
        if rows % cand == 0:
            return cand
    return rows


def _add_eight(own, landed, chip_core, name):
    _, _, rows, cols = own.shape
    tr = _row_tile(rows)

    def body(sel_ref, s_ref, r_ref, o_ref):
        acc = s_ref[...]
        for peer in range(N_DEV - 1):
            acc = acc + r_ref[peer].astype(F32)
        o_ref[...] = acc

    return pl.pallas_call(
        body, name=name,
        grid_spec=pltpu.PrefetchScalarGridSpec(
            num_scalar_prefetch=1, grid=(rows // tr,),
            in_specs=[pl.BlockSpec((None, None, tr, cols), lambda i, s: (s[0], s[1], i, 0)),
                      pl.BlockSpec((N_DEV - 1, tr, cols), lambda i, s: (0, i, 0))],
            out_specs=pl.BlockSpec((None, tr, cols), lambda i, s: (s[1], i, 0))),
        out_shape=jax.ShapeDtypeStruct((2, rows, cols), F32),
        compiler_params=_params("parallel"),
    )(chip_core, own, landed)


def _adamw(w, g, m, v, name):
    rows, cols = w.shape
    tr = _row_tile(rows)

    def body(w_ref, g_ref, m_ref, v_ref, d_ref, m2_ref, v2_ref):
        gg = g_ref[...]
        m2 = ADAM_B1 * m_ref[...] + (1.0 - ADAM_B1) * gg
        v2 = ADAM_B2 * v_ref[...] + (1.0 - ADAM_B2) * (gg * gg)
        m_hat = m2 / (1.0 - ADAM_B1 ** ADAM_STEP)
        v_hat = v2 / (1.0 - ADAM_B2 ** ADAM_STEP)
        d_ref[...] = -ADAM_LR * (m_hat / (jnp.sqrt(v_hat) + ADAM_EPS) + ADAM_WD * w_ref[...])
        m2_ref[...] = m2
        v2_ref[...] = v2

    spec = pl.BlockSpec((tr, cols), lambda i: (i, 0))
    shp = jax.ShapeDtypeStruct((rows, cols), F32)
    return pl.pallas_call(
        body, name=name, grid=(rows // tr,),
        in_specs=[spec, spec, spec, spec], out_specs=[spec, spec, spec], out_shape=[shp, shp, shp],
        compiler_params=_params("parallel"),
    )(w, g, m, v)


ANY = pl.BlockSpec(memory_space=pl.ANY)


def _place():
    x, y, c = lax.axis_index("x"), lax.axis_index("y"), lax.axis_index("c")
    chips = [(1 - x, y), (x, 1 - y), (1 - x, 1 - y)]
    return x, y, c, chips


def _allgather_weights(shards):
    n = len(shards)

    def body(*refs):
        ins, outs = refs[:n], refs[n:2 * n]
        send_ici, recv_ici, send_d2d, recv_d2d = refs[2 * n:]
        x, y, c, chips = _place()
        k0 = 2 * x + y
        sib = (x, y, 1 - c)

        def ici(a, r, src_chip, target):
            return pltpu.make_async_remote_copy(
                src_ref=ins[a].at[c], dst_ref=outs[a].at[src_chip, c],
                send_sem=send_ici.at[a * 3 + r], recv_sem=recv_ici.at[a * 3 + r],
                device_id=target, device_id_type=MESH)

        def d2d(a, r, src_chip, layer):
            return pltpu.make_async_remote_copy(
                src_ref=outs[a].at[src_chip, layer], dst_ref=outs[a].at[src_chip, layer],
                send_sem=send_d2d.at[a * 3 + r], recv_sem=recv_d2d.at[a * 3 + r],
                device_id=sib, device_id_type=MESH)

        first = [ici(a, r, k0, (cx, cy, c)) for a in range(n) for r, (cx, cy) in enumerate(chips)]
        for cp in first:
            cp.start()
        passed = []
        for a in range(n):
            for r, (cx, cy) in enumerate(chips):
                ici(a, r, 2 * cx + cy, (cx, cy, c)).wait_recv()
                cp = d2d(a, r, 2 * cx + cy, c)
                cp.start()
                passed.append(cp)
        for a in range(n):
            for r, (cx, cy) in enumerate(chips):
                d2d(a, r, 2 * cx + cy, 1 - c).wait_recv()
        for cp in first + passed:
            cp.wait_send()

    return pl.pallas_call(
        body, name="allgather_weights",
        in_specs=[ANY] * n, out_specs=[ANY] * n,
        out_shape=[jax.ShapeDtypeStruct((N_CHIPS,) + s.shape, s.dtype) for s in shards],
        scratch_shapes=[pltpu.SemaphoreType.DMA((3 * n,)), pltpu.SemaphoreType.DMA((3 * n,)),
                        pltpu.SemaphoreType.DMA((3 * n,)), pltpu.SemaphoreType.DMA((3 * n,))],
    )(*shards)


def _put_own(gathered, shard, chip):
    return lax.dynamic_update_slice_in_dim(gathered, shard[None], chip, axis=0)


HBM = pl.BlockSpec(memory_space=pltpu.HBM)
SEM = pl.BlockSpec(memory_space=pltpu.SEMAPHORE)
DATAFLOW = pltpu.SideEffectType.DATAFLOW_SIDE_EFFECTING
FLIPS = [(0, 0, 1), (0, 1, 0), (0, 1, 1), (1, 0, 0), (1, 0, 1), (1, 1, 0), (1, 1, 1)]


def _gather_plan(srcs, lands, send_sem, recv_sem):
    x, y, c, chips = _place()
    return [pltpu.make_async_remote_copy(
        src_ref=srcs[a], dst_ref=lands[a].at[2 * x + y],
        send_sem=send_sem.at[a * 3 + r], recv_sem=recv_sem.at[a * 3 + r],
        device_id=(cx, cy, c), device_id_type=MESH)
        for a in range(len(srcs)) for r, (cx, cy) in enumerate(chips)]


def _scatter_plan(srcs, lands, send_sem, recv_sem):
    x, y, c, _ = _place()
    cps = []
    for a in range(len(srcs)):
        for r, (fx, fy, fc) in enumerate(FLIPS):
            tx, ty, tc = (1 - x if fx else x), (1 - y if fy else y), (1 - c if fc else c)
            cps.append(pltpu.make_async_remote_copy(
                src_ref=srcs[a].at[2 * tx + ty, tc], dst_ref=lands[a].at[r],
                send_sem=send_sem.at[a * len(FLIPS) + r], recv_sem=recv_sem.at[a * len(FLIPS) + r],
                device_id=(tx, ty, tc), device_id_type=MESH))
    return cps


def _split_start(name, plan, srcs, lands, n_copies, after):
    n = len(srcs)

    def body(*refs):
        src, land = refs[:n], refs[n:2 * n]
        send_sem, recv_sem = refs[2 * n + 1], refs[2 * n + 2]
        token = refs[-1]
        for cp in plan(src, land, send_sem, recv_sem):
            cp.start()
        token[...] = jnp.zeros_like(token)

    outs = pl.pallas_call(
        body, name=name,
        in_specs=[HBM] * (2 * n) + [ANY],
        out_specs=[SEM, SEM] + [HBM] * (2 * n) + [pl.BlockSpec(memory_space=pltpu.VMEM)],
        out_shape=[pltpu.SemaphoreType.DMA((n_copies,)), pltpu.SemaphoreType.DMA((n_copies,))]
        + [pltpu.HBM(s.shape, s.dtype) for s in srcs] + [pltpu.HBM(l.shape, l.dtype) for l in lands]
        + [jax.ShapeDtypeStruct((8, LANES), F32)],
        input_output_aliases={i: 2 + i for i in range(2 * n)},
        compiler_params=pltpu.CompilerParams(has_side_effects=DATAFLOW),
    )(*[pltpu.with_memory_space_constraint(s, pltpu.HBM) for s in srcs],
      *[pltpu.with_memory_space_constraint(l, pltpu.HBM) for l in lands], after)
    return outs[0], outs[1], list(outs[2:2 + n]), list(outs[2 + n:2 + 2 * n]), outs[-1]


def _split_wait(name, plan, send_sems, recv_sems, srcs, lands, after):
    n = len(srcs)

    def body(*refs):
        src, land = refs[:n], refs[n:2 * n]
        send_sem, recv_sem = refs[2 * n], refs[2 * n + 1]
        for cp in plan(src, land, send_sem, recv_sem):
            cp.wait_send()
            cp.wait_recv()

    outs = pl.pallas_call(
        body, name=name,
        in_specs=[HBM] * (2 * n) + [SEM, SEM, ANY],
        out_specs=[HBM] * (2 * n),
        out_shape=[pltpu.HBM(s.shape, s.dtype) for s in srcs] + [pltpu.HBM(l.shape, l.dtype) for l in lands],
        input_output_aliases={i: i for i in range(2 * n)},
        compiler_params=pltpu.CompilerParams(has_side_effects=DATAFLOW),
    )(*srcs, *lands, send_sems, recv_sems, after)
    return list(outs[:n]), list(outs[n:])


def _share_halves(halves):
    n = len(halves)

    def body(*refs):
        ins, outs = refs[:n], refs[n:2 * n]
        send_sem, recv_sem = refs[2 * n:]
        x, y, c, _ = _place()
        cps = [pltpu.make_async_remote_copy(
            src_ref=outs[a].at[c], dst_ref=outs[a].at[c], send_sem=send_sem.at[a], recv_sem=recv_sem.at[a],
            device_id=(x, y, 1 - c), device_id_type=MESH) for a in range(n)]
        for cp in cps:
            cp.start()
        for cp in cps:
            cp.wait()

    return pl.pallas_call(
        body, name="share_halves",
        in_specs=[ANY] * n, out_specs=[ANY] * n,
        out_shape=[jax.ShapeDtypeStruct(h.shape, h.dtype) for h in halves],
        input_output_aliases={a: a for a in range(n)},
        scratch_shapes=[pltpu.SemaphoreType.DMA((n,)), pltpu.SemaphoreType.DMA((n,))],
    )(*halves)


N_DEV = 8


def _allreduce_small(v):
    m, nc = v.shape

    def body(x_ref, out_ref, gat, send_sems, recv_sems, local_sem):
        x, y, c, chips = _place()
        me, sib = (x, y, c), (x, y, 1 - c)

        def rows(px, py, pc):
            return gat.at[pl.ds((4 * px + 2 * py + pc) * m, m), :]

        def copy(k, block, to, src=None):
            return pltpu.make_async_remote_copy(
                src_ref=rows(*block) if src is None else src, dst_ref=rows(*block),
                send_sem=send_sems.at[k], recv_sem=recv_sems.at[k], device_id=to, device_id_type=MESH)

        mine = pltpu.make_async_copy(x_ref, rows(*me), local_sem)
        mine.start()
        first = [copy(0, me, sib, src=x_ref)]
        first += [copy(1 + j, me, (*chip, c), src=x_ref) for j, chip in enumerate(chips)]
        for cp in first:
            cp.start()
        passed = [copy(4 + j, (*chip, c), sib) for j, chip in enumerate(chips)]
        for j, chip in enumerate(chips):
            copy(1 + j, (*chip, c), me).wait_recv()
            passed[j].start()
        copy(0, sib, me).wait_recv()
        for j, chip in enumerate(chips):
            copy(4 + j, (*chip, 1 - c), me).wait_recv()
        for cp in first + passed:
            cp.wait_send()
        mine.wait()
        acc = gat[0:m, :]
        for dev in range(1, N_DEV):
            acc = acc + gat[dev * m:(dev + 1) * m, :]
        out_ref[...] = acc

    return pl.pallas_call(
        body, name="allreduce_small",
        in_specs=[pl.BlockSpec(memory_space=pltpu.VMEM)],
        out_specs=pl.BlockSpec(memory_space=pltpu.VMEM),
        out_shape=jax.ShapeDtypeStruct((m, nc), F32),
        scratch_shapes=[pltpu.VMEM((N_DEV * m, nc), F32), pltpu.SemaphoreType.DMA((7,)),
                        pltpu.SemaphoreType.DMA((7,)), pltpu.SemaphoreType.DMA],
        compiler_params=pltpu.CompilerParams(vmem_limit_bytes=VMEM_LIMIT),
    )(v)


def _pad_rows(a, rows):
    return jnp.pad(a, ((0, rows - a.shape[0]), (0, 0)))


def kernel(x, norm_g, final_g, conv_w_in, conv_dw, conv_dw_b, conv_ln_g, conv_ln_b, conv_w_out, pool_w_in, pool_w_grp, pool_b_grp, pool_scale, pool_w_out, loss_target, m_norm_g, m_final_g, m_conv_w_in, m_conv_dw, m_conv_dw_b, m_conv_ln_g, m_conv_ln_b, m_conv_w_out, m_pool_w_in, m_pool_w_grp, m_pool_b_grp, m_pool_scale, m_pool_w_out, v_norm_g, v_final_g, v_conv_w_in, v_conv_dw, v_conv_dw_b, v_conv_ln_g, v_conv_ln_b, v_conv_w_out, v_pool_w_in, v_pool_w_grp, v_pool_b_grp, v_pool_scale, v_pool_w_out):
    t, d = x.shape[1], x.shape[2]
    e = conv_w_out.shape[2]
    ng = len(POOL_WINDOWS)
    gc = e // ng
    gcs = pool_w_grp.shape[2]
    ck = conv_dw.shape[1]
    es = conv_dw.shape[2]
    xi, yi, ci = lax.axis_index("x"), lax.axis_index("y"), lax.axis_index("c")
    chip = 2 * xi + yi

    small_rows = ck + 2
    small_pad = -(-small_rows // 8) * 8
    small = jnp.concatenate([conv_dw, pool_b_grp[:, None, :], pool_scale[:, None, :],
                             jnp.zeros((2, small_pad - small_rows, es), F32)], axis=1)
    cwi_b, cwo_b = conv_w_in.astype(BF16), conv_w_out.astype(BF16)

    def halves(a):
        return a.reshape(2, a.shape[0] // 2, a.shape[1])

    first = [halves(cwi_b[0]), halves(cwo_b[0]), small]
    g_cwi0, g_cwo0, g_small = [_put_own(g, q, chip) for g, q in zip(_allgather_weights(first), first)]
    rest = [cwi_b[1], cwo_b[1], pool_w_in.astype(BF16), pool_w_grp.astype(BF16), pool_w_out.astype(BF16)]
    rest_lands = [lax.empty((N_CHIPS,) + r.shape, r.dtype) for r in rest]
    ag_send, ag_recv, rest, rest_lands, ag_token = _split_start(
        "gather_rest_start", _gather_plan, rest, rest_lands, 3 * len(rest), g_small)
    smallf = jnp.transpose(g_small, (1, 2, 0, 3)).reshape(2, small_pad, N_CHIPS * es)

    h = x.reshape(t, d)
    tgt = loss_target.reshape(t, d)
    hs, saved = [], []
    for layer in range(4):
        j = layer // 2
        hs.append(h)
        gvec = norm_g[layer][None, :]
        if layer == 0:
            gvec = gvec + ag_token[0:1, 0:1]
        if layer == 1:
            rest, rest_lands = _split_wait("gather_rest_wait", _gather_plan, ag_send, ag_recv, rest, rest_lands, h)
            g_cwi1, g_cwo1, g_pwi, g_pwg, g_pwo = [_put_own(g, q, chip) for g, q in zip(rest_lands, rest)]
            wg_full = jnp.transpose(g_pwg, (1, 2, 0, 3, 4)).reshape(2, ng, gc, gc)
        if layer % 2 == 0:
            g_in, g_out = (g_cwi0, g_cwo0) if j == 0 else (g_cwi1, g_cwo1)
            w_in4 = g_in.reshape(N_CHIPS, d, -1)
            w_out = g_out.reshape(e, d)
            dw_full = smallf[j, 0:ck]
            p, hn, s, c, h = _conv_fwd(h, gvec, w_in4, w_out, dw_full.reshape(-1, LANES),
                                       conv_dw_b[j].reshape(-1, LANES), conv_ln_g[j][None, :], conv_ln_b[j][None, :])
            saved.append((p, hn, s, c, w_in4, w_out, dw_full))
        else:
            w_in4 = g_pwi[:, j]
            w_out = g_pwo[:, j].reshape(e, d)
            p, hn = _rms_matmul(h, gvec, w_in4, "rms_matmul_pool")
            bg_full = smallf[j, ck:ck + 1]
            sc_full = smallf[j, ck + 1:ck + 2]
            s, h = _pool_mid_out_fwd(p, h, w_out, wg_full[j], bg_full, sc_full)
            saved.append((p, hn, s, None, w_in4, w_out, (wg_full[j], bg_full, sc_full)))

    dh, loss_part, dfg = _loss_head(h, final_g[None, :], tgt)
    loss = lax.psum(loss_part[0, 0], ("x", "y", "c"))

    def by_half(a):
        return a.reshape(N_CHIPS, 2, a.shape[1] // 2, a.shape[2])

    dng = [None] * 4
    g_conv = [None, None]
    g_pool = [None, None]
    flights = {}
    for layer in (3, 2, 1, 0):
        j = layer // 2
        p, hn, s, c, w_in4, w_out, extra = saved[layer]
        gvec = norm_g[layer][None, :]
        dw_out, dw_out_b = [q.reshape(N_CHIPS, e // N_CHIPS, d) for q in _wgrad(s, dh, 1, "wgrad_out")]
        if layer % 2 == 0:
            dp, dlg, dlb, ddwb, ddw3, dw_in = _conv_bwd(p, c, dh, hn, w_out, extra.reshape(-1, LANES),
                                                        conv_ln_g[j][None, :], conv_ln_b[j][None, :])
            ddw = ddw3.reshape(ck, e)
            dw_in_b = dw_in.astype(BF16)
            own, pay = [dw_in, dw_out], [dw_in_b, dw_out_b]
            g_conv[j] = (dlg, dlb, ddwb, ddw)
        else:
            wg, bg_full, sc_full = extra
            dp, dwg, dbg, dsc = _pool_mid_bwd(p, dh, w_out, wg, bg_full, sc_full)
            dw_in, dw_in_b = _wgrad(hn, dp, N_CHIPS, "wgrad_in_pool")
            dwg4 = jnp.transpose(dwg.reshape(ng, N_CHIPS, gcs, gc), (1, 0, 2, 3)).reshape(N_CHIPS, ng * gcs, gc)
            own, pay = [dw_in, dw_out, dwg4], [dw_in_b, dw_out_b, dwg4.astype(BF16)]
            g_pool[j] = (dbg, dsc)
        pay = [by_half(q) for q in pay]
        lands = [lax.empty((len(FLIPS),) + q.shape[2:], BF16) for q in pay]
        send, recv, pay, lands, token = _split_start(
            "scatter_start_%d" % layer, _scatter_plan, pay, lands, len(FLIPS) * len(pay), own[0])
        flights[layer] = (send, recv, pay, lands, own)
        kind = "dhn_rms_bwd_conv" if layer % 2 == 0 else "dhn_rms_bwd_pool"
        dh, dng[layer] = _dhn_rms_bwd(dp, w_in4, hs[layer], gvec + token[0:1, 0:1], dh, kind)
    grad_x = dh.reshape(x.shape)

    sel_kc = jnp.stack([chip, ci]).astype(jnp.int32)
    summed = {}
    after = dh
    for layer in (3, 2, 1, 0):
        send, recv, pay, lands, own = flights[layer]
        _, lands = _split_wait("scatter_wait_%d" % layer, _scatter_plan, send, recv, pay, lands, after)
        after = lands[0]
        summed[layer] = [_add_eight(by_half(o), l, sel_kc, "add_eight_%d_%d" % (layer % 2, a))
                         for a, (o, l) in enumerate(zip(own, lands))]
    order = [(0, 0), (0, 1), (2, 0), (2, 1), (1, 0), (1, 1), (1, 2), (3, 0), (3, 1), (3, 2)]
    shared = _share_halves([summed[l][a] for l, a in order])
    full = {la: q.reshape(q.shape[0] * q.shape[1], q.shape[2]) for la, q in zip(order, shared)}
    g_cwi_f = jnp.stack([full[(0, 0)], full[(2, 0)]])
    g_cwo_f = jnp.stack([full[(0, 1)], full[(2, 1)]])
    g_pwi_f = jnp.stack([full[(1, 0)], full[(3, 0)]])
    g_pwo_f = jnp.stack([full[(1, 1)], full[(3, 1)]])
    g_pwg_f = jnp.stack([full[(1, 2)], full[(3, 2)]]).reshape(pool_w_grp.shape)

    rows_list = [dng[0], dng[1], dng[2], dng[3], dfg,
                 g_conv[0][2], g_conv[1][2], g_conv[0][0], g_conv[1][0], g_conv[0][1], g_conv[1][1],
                 g_pool[0][0], g_pool[1][0], g_pool[0][1], g_pool[1][1], g_conv[0][3], g_conv[1][3]]
    slab = jnp.concatenate(rows_list, axis=0)
    nrows = slab.shape[0]
    slab = _pad_rows(slab, -(-nrows // 8) * 8)
    tot = _allreduce_small(slab)
    g_norm_g = tot[0:4]
    g_final_g = tot[4]
    g_dwb = tot[5:7]
    g_lng = tot[7:9]
    g_lnb = tot[9:11]
    g_bg = lax.dynamic_slice_in_dim(tot[11:13], chip * es, es, axis=1)
    g_sc = lax.dynamic_slice_in_dim(tot[13:15], chip * es, es, axis=1)
    g_dw = lax.dynamic_slice_in_dim(tot[15:15 + 2 * ck].reshape(2, ck, e), chip * es, es, axis=2)

    def adam_nd(w, g, m, v, nm):
        rows, cols = _rows2d(w.shape)
        outs = _adamw(w.reshape(rows, cols), g.reshape(rows, cols), m.reshape(rows, cols),
                      v.reshape(rows, cols), "adamw_" + nm)
        return [o.reshape(w.shape) for o in outs]

    res = {}
    res["conv_w_in"] = (g_cwi_f, *adam_nd(conv_w_in, g_cwi_f, m_conv_w_in, v_conv_w_in, "cwi"))
    res["conv_w_out"] = (g_cwo_f, *adam_nd(conv_w_out, g_cwo_f, m_conv_w_out, v_conv_w_out, "cwo"))
    res["pool_w_in"] = (g_pwi_f, *adam_nd(pool_w_in, g_pwi_f, m_pool_w_in, v_pool_w_in, "pwi"))
    res["pool_w_grp"] = (g_pwg_f, *adam_nd(pool_w_grp, g_pwg_f, m_pool_w_grp, v_pool_w_grp, "pwg"))
    res["pool_w_out"] = (g_pwo_f, *adam_nd(pool_w_out, g_pwo_f, m_pool_w_out, v_pool_w_out, "pwo"))

    def pack(parts, rows_to):
        return _pad_rows(jnp.concatenate([q.reshape(-1, q.shape[-1]) for q in parts], axis=0), rows_to)

    rep_w = [norm_g, final_g[None, :], conv_dw_b, conv_ln_g, conv_ln_b]
    rep_g = [g_norm_g, g_final_g[None, :], g_dwb, g_lng, g_lnb]
    rep_m = [m_norm_g, m_final_g[None, :], m_conv_dw_b, m_conv_ln_g, m_conv_ln_b]
    rep_v = [v_norm_g, v_final_g[None, :], v_conv_dw_b, v_conv_ln_g, v_conv_ln_b]
    rep = _adamw(pack(rep_w, 16), pack(rep_g, 16), pack(rep_m, 16), pack(rep_v, 16), "adamw_rep")
    rep_names = ["norm_g", "final_g", "conv_dw_b", "conv_ln_g", "conv_ln_b"]
    rep_rows = [(0, 4), (4, 5), (5, 7), (7, 9), (9, 11)]
    for nm, (lo, hi), gq, wq in zip(rep_names, rep_rows, rep_g, rep_w):
        shape = (d,) if nm == "final_g" else wq.shape
        res[nm] = (gq.reshape(shape), *[o[lo:hi].reshape(shape) for o in rep])

    sh_w = [conv_dw, pool_b_grp, pool_scale]
    sh_g = [g_dw, g_bg, g_sc]
    sh_m = [m_conv_dw, m_pool_b_grp, m_pool_scale]
    sh_v = [v_conv_dw, v_pool_b_grp, v_pool_scale]
    sh_total = 2 * ck + 4
    sh_pad = -(-sh_total // 8) * 8
    shd = _adamw(pack(sh_w, sh_pad), pack(sh_g, sh_pad), pack(sh_m, sh_pad), pack(sh_v, sh_pad), "adamw_shard")
    sh_names = ["conv_dw", "pool_b_grp", "pool_scale"]
    sh_rows = [(0, 2 * ck), (2 * ck, 2 * ck + 2), (2 * ck + 2, 2 * ck + 4)]
    for nm, (lo, hi), gq, wq in zip(sh_names, sh_rows, sh_g, sh_w):
        res[nm] = (gq.reshape(wq.shape), *[o[lo:hi].reshape(wq.shape) for o in shd])

    order = ["norm_g", "final_g", "conv_w_in", "conv_dw", "conv_dw_b", "conv_ln_g", "conv_ln_b", "conv_w_out",
             "pool_w_in", "pool_w_grp", "pool_b_grp", "pool_scale", "pool_w_out"]
    outs = [loss, grad_x]
    for part in range(4):
        outs += [res[nm][part] for nm in order]
    return tuple(outs)
```
